```python
import math
import jax, jax.numpy as jnp
from jax import lax
import numpy as np

D_MODEL = 1024
BATCH = 8
SEQ = 2048
DEPTH = 2
DEC_BATCH = 128
DEC_SEQ = 1
PAST_LEN = 16384
PAGE_SIZE = 128

H_RET = 4
DK_RET = 128
DV_RET = 256
H_GDN = 4
DK_GDN = 128
DV_GDN = 256
CONV_W = 4
H_X = 4
HD_X = 256
N_MEM = 256
D_FF = 2816
CHUNK = 64
ROPE_BASE = 10000.0
EPS = 1e-6

RET_QK = H_RET * DK_RET
RET_V = H_RET * DV_RET
GDN_QK = H_GDN * DK_GDN
GDN_V = H_GDN * DV_GDN
CONV_DIM = 2 * GDN_QK + GDN_V
X_W = H_X * HD_X
_W_IN_SIZES = (RET_QK, RET_QK, RET_V, RET_V, CONV_DIM, GDN_V, H_GDN, H_GDN, X_W, 3 * D_MODEL)
W_IN_DIM = sum(_W_IN_SIZES)
W_IN_SPLITS = tuple(sum(_W_IN_SIZES[:i]) for i in range(1, len(_W_IN_SIZES)))

kernel_name = 'hybrid_retention_gdn_memory_decoder_step'


def rms_norm(x, gain=None):
    xf = x.astype(jnp.float32)
    y = xf * lax.rsqrt(jnp.mean(xf * xf, axis=-1, keepdims=True) + EPS)
    if gain is not None:
        y = y * gain.astype(jnp.float32)
    return y.astype(x.dtype)


def l2_normalize(x):
    return x * lax.rsqrt(jnp.sum(x * x, axis=-1, keepdims=True) + EPS)


def swiglu(x, w_in, w_out):
    gate, up = jnp.split(x @ w_in, 2, axis=-1)
    return (jax.nn.silu(gate) * up) @ w_out


def rotary(x, pos):
    half = x.shape[-1] // 2
    inv_freq = ROPE_BASE ** (-jnp.arange(half, dtype=jnp.float32) / half)
    ang = pos.astype(jnp.float32)[:, None] * inv_freq[None, :]
    cos = jnp.cos(ang)[:, None, :]
    sin = jnp.sin(ang)[:, None, :]
    x1, x2 = x[..., :half], x[..., half:]
    return jnp.concatenate([x1 * cos - x2 * sin, x1 * sin + x2 * cos], axis=-1)


def heads_first(t):
    return jnp.swapaxes(t, 1, 2)


def chunk_decay_scan(q, k, v, g, beta, s0):
    b_, h_, L, dk = q.shape
    dv = v.shape[-1]
    c = min(CHUNK, L)
    n = -(-L // c)
    pad = n * c - L
    if pad:
        q, k, v = (jnp.pad(t, ((0, 0), (0, 0), (0, pad), (0, 0))) for t in (q, k, v))
        g = jnp.pad(g, ((0, 0), (0, 0), (0, pad)))
        if beta is not None:
            beta = jnp.pad(beta, ((0, 0), (0, 0), (0, pad)))
    q, k, v = (t.reshape(b_, h_, n, c, t.shape[-1]) for t in (q, k, v))
    b = jnp.cumsum(g.reshape(b_, h_, n, c), axis=-1)
    causal = jnp.tril(jnp.ones((c, c), dtype=bool))
    decay = jnp.exp(jnp.where(causal, b[..., :, None] - b[..., None, :], -jnp.inf))
    attn = jnp.einsum('bhncd,bhnmd->bhncm', q, k) * decay
    q_dec = q * jnp.exp(b)[..., None]
    k_dec = k * jnp.exp(b[..., -1:] - b)[..., None]
    chunk_decay = jnp.exp(b[..., -1])
    if beta is None:
        xs = (q_dec, k_dec, v, attn, chunk_decay)
    else:
        beta = beta.reshape(b_, h_, n, c)[..., None]
        k_beta = k * beta
        strict = jnp.tril(jnp.ones((c, c), dtype=bool), -1)
        a = jnp.where(strict, jnp.einsum('bhncd,bhnmd->bhncm', k_beta, k) * decay, 0.0)
        rhs = jnp.concatenate([v * beta, k_beta * jnp.exp(b)[..., None]], axis=-1)
        sol = lax.linalg.triangular_solve(a, rhs, left_side=True, lower=True, unit_diagonal=True)
        xs = (q_dec, k_dec, sol[..., :dv], attn, chunk_decay, sol[..., dv:])
    xs = tuple(jnp.moveaxis(t, 2, 0) for t in xs)

    def step(s, inp):
        q_c, k_c, u_c, a_c, d_c = inp[:5]
        v_new = u_c if beta is None else u_c - jnp.einsum('bhcd,bhde->bhce', inp[5], s)
        o_c = jnp.einsum('bhcd,bhde->bhce', q_c, s) + jnp.einsum('bhcm,bhme->bhce', a_c, v_new)
        s = s * d_c[..., None, None] + jnp.einsum('bhcd,bhce->bhde', k_c, v_new)
        return s, o_c

    s_fin, o = lax.scan(step, s0.astype(jnp.float32), xs)
    o = jnp.moveaxis(o, 0, 2).reshape(b_, h_, n * c, dv)[:, :, :L]
    return o, s_fin


def retention_branch(q_raw, k_raw, v_raw, g_raw, pos, s0):
    bsz, L, _ = q_raw.shape
    f32 = jnp.float32
    q = rotary(q_raw.astype(f32).reshape(bsz, L, H_RET, DK_RET), pos)
    k = rotary(k_raw.astype(f32).reshape(bsz, L, H_RET, DK_RET), pos) * DK_RET ** -0.5
    v = v_raw.astype(f32).reshape(bsz, L, H_RET, DV_RET)
    log_gamma = jnp.log1p(-jnp.exp2(-5.0 - jnp.arange(H_RET, dtype=f32)))
    g = jnp.broadcast_to(log_gamma[None, :, None], (bsz, H_RET, L))
    o, s = chunk_decay_scan(heads_first(q), heads_first(k), heads_first(v), g, None, s0)
    o = heads_first(rms_norm(o)).reshape(bsz, L, RET_V)
    return (jax.nn.silu(g_raw.astype(f32)) * o).astype(q_raw.dtype), s


def gated_delta_branch(qkv_raw, z_raw, a_raw, b_raw, conv_buf, conv_w, a_log, dt_bias, norm_w, s0):
    bsz, L, _ = qkv_raw.shape
    f32 = jnp.float32
    full = jnp.concatenate([conv_buf.astype(qkv_raw.dtype), qkv_raw], axis=1)
    conv = sum(full[:, j:j + L] * conv_w[j] for j in range(CONV_W))
    new_buf = full[:, L:]
    qkv = jax.nn.silu(conv.astype(f32))
    q, k, v = jnp.split(qkv, [GDN_QK, 2 * GDN_QK], axis=-1)
    q = l2_normalize(q.reshape(bsz, L, H_GDN, DK_GDN)) * DK_GDN ** -0.5
    k = l2_normalize(k.reshape(bsz, L, H_GDN, DK_GDN))
    v = v.reshape(bsz, L, H_GDN, DV_GDN)
    beta = jax.nn.sigmoid(b_raw.astype(f32))
    g = -jnp.exp(a_log.astype(f32)) * jax.nn.softplus(a_raw.astype(f32) + dt_bias.astype(f32))
    o, s = chunk_decay_scan(heads_first(q), heads_first(k), heads_first(v),
                            jnp.swapaxes(g, 1, 2), jnp.swapaxes(beta, 1, 2), s0)
    o = rms_norm(heads_first(o), norm_w)
    z = z_raw.astype(f32).reshape(bsz, L, H_GDN, DV_GDN)
    o = (o * jax.nn.silu(z)).reshape(bsz, L, GDN_V)
    return o.astype(qkv_raw.dtype), s, new_buf


def memory_attention(q_raw, mem_k, mem_v):
    bsz, L, _ = q_raw.shape
    q = q_raw.reshape(bsz, L, H_X, HD_X)
    s = jnp.einsum('blhd,bmhd->bhlm', q, mem_k.astype(q.dtype)).astype(jnp.float32) * HD_X ** -0.5
    prob = jax.nn.softmax(s, axis=-1).astype(q.dtype)
    o = jnp.einsum('bhlm,bmhd->blhd', prob, mem_v.astype(q.dtype))
    return o.reshape(bsz, L, X_W)


def token_mixing(xn, pos, mem_k, mem_v, s_ret, s_gdn, conv_buf, p):
    bsz, L, _ = xn.shape
    r_q, r_k, r_v, r_g, g_qkv, g_z, g_a, g_b, m_q, gates = jnp.split(xn @ p['w_in'], W_IN_SPLITS, axis=-1)
    o_ret, s_ret = retention_branch(r_q, r_k, r_v, r_g, pos, s_ret)
    o_gdn, s_gdn, conv_buf = gated_delta_branch(g_qkv, g_z, g_a, g_b, conv_buf, p['gdn_conv_w'],
                                                p['gdn_a_log'], p['gdn_dt_bias'], p['gdn_norm'], s_gdn)
    o_mem = memory_attention(m_q, mem_k, mem_v)
    gate = jax.nn.sigmoid(gates.astype(jnp.float32)).astype(xn.dtype).reshape(bsz, L, 3, D_MODEL)
    merged = (gate[:, :, 0] * (o_ret @ p['w_branch_ret'])
              + gate[:, :, 1] * (o_gdn @ p['w_branch_gdn'])
              + gate[:, :, 2] * (o_mem @ p['w_branch_mem']))
    return merged @ p['w_out'], s_ret, s_gdn, conv_buf


def decoder_layer(x, pos, mem_k, mem_v, s_ret, s_gdn, conv_buf, p):
    h = x + 0.5 * rms_norm(swiglu(rms_norm(x, p['n_ffn1_pre']), p['ffn1_w_in'], p['ffn1_w_out']), p['n_ffn1_post'])
    m, s_ret, s_gdn, conv_buf = token_mixing(rms_norm(h, p['n_mix_pre']), pos, mem_k, mem_v,
                                             s_ret, s_gdn, conv_buf, p)
    h = h + rms_norm(m, p['n_mix_post'])
    h = h + 0.5 * rms_norm(swiglu(rms_norm(h, p['n_ffn2_pre']), p['ffn2_w_in'], p['ffn2_w_out']), p['n_ffn2_post'])
    return h, s_ret, s_gdn, conv_buf


def setup_inputs(seed: int = 0) -> dict:
    key = jax.random.key(seed)
    keys = iter(jax.random.split(key, 40))

    def nrm(shape, scale):
        return scale * jax.random.normal(next(keys), shape, jnp.float32)

    def gain(width=D_MODEL):
        return 1.0 + nrm((DEPTH, width), 0.02)

    a_log = jnp.log(jax.random.uniform(next(keys), (DEPTH, H_GDN), jnp.float32, 1.0, 16.0))
    dt = jnp.exp(jax.random.uniform(next(keys), (DEPTH, H_GDN), jnp.float32, math.log(1e-3), math.log(1e-1)))
    dt_bias = dt + jnp.log(-jnp.expm1(-dt))
    return {
        'x_prompt': nrm((BATCH, SEQ, D_MODEL), 1.0),
        'x_sample': nrm((DEC_BATCH, DEC_SEQ, D_MODEL), 1.0),
        'mem_prompt': nrm((BATCH, N_MEM, D_MODEL), 1.0),
        'state_ret': nrm((DEPTH, DEC_BATCH, H_RET, DK_RET, DV_RET), 0.5),
        'state_gdn': nrm((DEPTH, DEC_BATCH, H_GDN, DK_GDN, DV_GDN), 0.5),
        'state_conv': nrm((DEPTH, DEC_BATCH, CONV_W - 1, CONV_DIM), 1.0),
        'cache_mem_k': nrm((DEPTH, DEC_BATCH, N_MEM, H_X, HD_X), 1.0),
        'cache_mem_v': nrm((DEPTH, DEC_BATCH, N_MEM, H_X, HD_X), 1.0),
        'norm_ffn1_pre': gain(),
        'norm_ffn1_post': gain(),
        'ffn1_w_in': nrm((DEPTH, D_MODEL, 2 * D_FF), D_MODEL ** -0.5),
        'ffn1_w_out': nrm((DEPTH, D_FF, D_MODEL), D_FF ** -0.5),
        'norm_mix_pre': gain(),
        'norm_mix_post': gain(),
        'w_in': nrm((DEPTH, D_MODEL, W_IN_DIM), D_MODEL ** -0.5),
        'gdn_conv_w': nrm((DEPTH, CONV_W, CONV_DIM), 0.5),
        'gdn_a_log': a_log,
        'gdn_dt_bias': dt_bias,
        'gdn_norm': gain(DV_GDN),
        'norm_mem': gain(),
        'w_mem_k': nrm((DEPTH, D_MODEL, X_W), D_MODEL ** -0.5),
        'w_mem_v': nrm((DEPTH, D_MODEL, X_W), D_MODEL ** -0.5),
        'w_branch_ret': nrm((DEPTH, RET_V, D_MODEL), RET_V ** -0.5),
        'w_branch_gdn': nrm((DEPTH, GDN_V, D_MODEL), GDN_V ** -0.5),
        'w_branch_mem': nrm((DEPTH, X_W, D_MODEL), X_W ** -0.5),
        'w_out': nrm((DEPTH, D_MODEL, D_MODEL), D_MODEL ** -0.5),
        'norm_ffn2_pre': gain(),
        'norm_ffn2_post': gain(),
        'ffn2_w_in': nrm((DEPTH, D_MODEL, 2 * D_FF), D_MODEL ** -0.5),
        'ffn2_w_out': nrm((DEPTH, D_FF, D_MODEL), D_FF ** -0.5),
    }


def reference(x_prompt, x_sample, mem_prompt, state_ret, state_gdn, state_conv, cache_mem_k, cache_mem_v,
              norm_ffn1_pre, norm_ffn1_post, ffn1_w_in, ffn1_w_out, norm_mix_pre, norm_mix_post, w_in,
              gdn_conv_w, gdn_a_log, gdn_dt_bias, gdn_norm, norm_mem, w_mem_k, w_mem_v,
              w_branch_ret, w_branch_gdn, w_branch_mem, w_out, norm_ffn2_pre, norm_ffn2_post,
              ffn2_w_in, ffn2_w_out):
    f32 = jnp.float32
    dt = x_prompt.dtype
    bp, sp = x_prompt.shape[0], x_prompt.shape[1]
    ss = x_sample.shape[1]
    pos_prompt = jnp.arange(sp)
    pos_sample = PAST_LEN + jnp.arange(ss)
    hp, hs = x_prompt, x_sample
    ret_p, gdn_p, conv_p, mk_p, mv_p, ret_s, gdn_s, conv_s = ([] for _ in range(8))
    for l in range(DEPTH):
        p = dict(n_ffn1_pre=norm_ffn1_pre[l], n_ffn1_post=norm_ffn1_post[l],
                 ffn1_w_in=ffn1_w_in[l], ffn1_w_out=ffn1_w_out[l],
                 n_mix_pre=norm_mix_pre[l], n_mix_post=norm_mix_post[l], w_in=w_in[l],
                 gdn_conv_w=gdn_conv_w[l], gdn_a_log=gdn_a_log[l], gdn_dt_bias=gdn_dt_bias[l],
                 gdn_norm=gdn_norm[l], w_branch_ret=w_branch_ret[l], w_branch_gdn=w_branch_gdn[l],
                 w_branch_mem=w_branch_mem[l], w_out=w_out[l],
                 n_ffn2_pre=norm_ffn2_pre[l], n_ffn2_post=norm_ffn2_post[l],
                 ffn2_w_in=ffn2_w_in[l], ffn2_w_out=ffn2_w_out[l])
        mem_n = rms_norm(mem_prompt, norm_mem[l])
        mk = (mem_n @ w_mem_k[l]).reshape(bp, N_MEM, H_X, HD_X)
        mv = (mem_n @ w_mem_v[l]).reshape(bp, N_MEM, H_X, HD_X)
        hp, sr, sg, cb = decoder_layer(hp, pos_prompt, mk, mv,
                                       jnp.zeros((bp, H_RET, DK_RET, DV_RET), f32),
                                       jnp.zeros((bp, H_GDN, DK_GDN, DV_GDN), f32),
                                       jnp.zeros((bp, CONV_W - 1, CONV_DIM), dt), p)
        ret_p.append(sr)
        gdn_p.append(sg)
        conv_p.append(cb)
        mk_p.append(mk)
        mv_p.append(mv)
        hs, sr, sg, cb = decoder_layer(hs, pos_sample, cache_mem_k[l], cache_mem_v[l],
                                       state_ret[l], state_gdn[l], state_conv[l], p)
        ret_s.append(sr)
        gdn_s.append(sg)
        conv_s.append(cb)
    return (hp, hs,
            jnp.stack(ret_p).astype(dt), jnp.stack(gdn_p).astype(dt), jnp.stack(conv_p).astype(dt),
            jnp.stack(mk_p).astype(dt), jnp.stack(mv_p).astype(dt),
            jnp.stack(ret_s).astype(dt), jnp.stack(gdn_s).astype(dt), jnp.stack(conv_s).astype(dt))
```

```python
import math

import jax
import jax.numpy as jnp
from jax import lax
from jax.experimental import pallas as pl
from jax.experimental.pallas import tpu as pltpu

f32, bf16 = jnp.float32, jnp.bfloat16

D_MODEL = 1024
N_HEADS = 4
DK = 128
DV = 256
QK_W = N_HEADS * DK
V_W = N_HEADS * DV
CONV_W = 4
CONV_DIM = 2 * QK_W + V_W
HD_X = 256
N_MEM = 256
D_FF = 2816
EPS = 1e-6
ROPE_BASE = 10000.0
PAST_LEN = 16384
CHUNK = 64
LANES = 128
SUBLANES = 8
VMEM_LIMIT = 56 * 1024 * 1024
LOG_GAMMA = tuple(math.log1p(-2.0 ** (-5.0 - h)) for h in range(N_HEADS))
HIGHEST = lax.Precision.HIGHEST
NT = (((1,), (1,)), ((), ()))
TN = (((0,), (0,)), ((), ()))


def _params(*sem):
    return pltpu.CompilerParams(dimension_semantics=sem, vmem_limit_bytes=VMEM_LIMIT)


def _resident(shape):
    return pl.BlockSpec(shape, lambda *_: (0,) * len(shape), pipeline_mode=pl.Buffered(1))


def _rms(x, gain=None):
    y = x * lax.rsqrt(jnp.mean(x * x, axis=-1, keepdims=True) + EPS)
    return y if gain is None else y * gain


def _silu(x):
    return x * jax.nn.sigmoid(x)


def _softplus(x):
    return jnp.maximum(x, 0.0) + jnp.log1p(jnp.exp(-jnp.abs(x)))


def _dot(a, b):
    return jnp.dot(a, b, preferred_element_type=f32)


def _iota(shape, dim):
    return lax.broadcasted_iota(jnp.int32, shape, dim)


def _ffn_body(x_ref, gpre_ref, wgu_ref, wo_ref, gpost_ref, o_ref):
    x = x_ref[...]
    xn = _rms(x, gpre_ref[...]).astype(bf16)
    h = _dot(xn, wgu_ref[...])
    act = (_silu(h[:, :D_FF]) * h[:, D_FF:]).astype(bf16)
    y = _dot(act, wo_ref[...])
    o_ref[...] = x + 0.5 * _rms(y, gpost_ref[...])


def _ffn(x2d, gpre, wgu, wo, gpost, tm):
    t = x2d.shape[0]
    row = pl.BlockSpec((tm, D_MODEL), lambda i: (i, 0))
    return pl.pallas_call(
        _ffn_body, grid=(t // tm,),
        in_specs=[row, _resident((1, D_MODEL)), _resident((D_MODEL, 2 * D_FF)),
                  _resident((D_FF, D_MODEL)), _resident((1, D_MODEL))],
        out_specs=row, out_shape=jax.ShapeDtypeStruct((t, D_MODEL), f32),
        compiler_params=_params("parallel"), name="ffn",
    )(x2d, gpre, wgu, wo, gpost)


def _memkv_body(m_ref, g_ref, wk_ref, wv_ref, k_ref, v_ref, kb_ref, vb_ref):
    mn = _rms(m_ref[...], g_ref[...]).astype(bf16)
    k = _dot(mn, wk_ref[...])
    v = _dot(mn, wv_ref[...])
    k_ref[...] = k
    v_ref[...] = v
    kb_ref[...] = k.astype(bf16)
    vb_ref[...] = v.astype(bf16)


def _memkv(mem, g, wk, wv):
    b = mem.shape[0]
    blk = pl.BlockSpec((None, N_MEM, D_MODEL), lambda i: (i, 0, 0))
    sds = lambda dt: jax.ShapeDtypeStruct((b, N_MEM, D_MODEL), dt)
    return pl.pallas_call(
        _memkv_body, grid=(b,),
        in_specs=[blk, _resident((1, D_MODEL)), _resident((D_MODEL, D_MODEL)), _resident((D_MODEL, D_MODEL))],
        out_specs=[blk, blk, blk, blk], out_shape=[sds(f32), sds(f32), sds(bf16), sds(bf16)],
        compiler_params=_params("parallel"), name="memkv",
    )(mem, g, wk, wv)


_RQ, _RK, _RV, _RG, _GQKV, _GZ, _MAIN_W = 0, 512, 1024, 2048, 3072, 5120, 6144


def _rotary(x, cos, sin_signed):
    return x * cos + pltpu.roll(x, DK // 2, axis=1) * sin_signed


def _decay_beta(ab, a_log, dt_bias, is_decay):
    g = -jnp.exp(a_log) * _softplus(ab + dt_bias)
    return jnp.where(is_decay, g, jax.nn.sigmoid(ab))


def _gdn_qkv(conv):
    a = _silu(conv)
    qs, ks = [], []
    for h in range(N_HEADS):
        q = a[:, h * DK:(h + 1) * DK]
        k = a[:, QK_W + h * DK:QK_W + (h + 1) * DK]
        qs.append(q * (lax.rsqrt(jnp.sum(q * q, axis=-1, keepdims=True) + EPS) * DK ** -0.5))
        ks.append(k * lax.rsqrt(jnp.sum(k * k, axis=-1, keepdims=True) + EPS))
    return qs, ks, a[:, 2 * QK_W:]


def _proj_body(h_ref, gpre_ref, wm_ref, wab_ref, wabt_ref, cw_ref, alog_ref, dtb_ref, alogt_ref, dtbt_ref,
               cos_ref, sin_ref,
               rq_ref, rk_ref, rv_ref, rg_ref, gq_ref, gk_ref, gv_ref, gz_ref, gcol_ref, grow_ref, tail_ref,
               carry_ref):
    j = pl.program_id(1)
    tm = h_ref.shape[0]

    @pl.when(j == 0)
    def _():
        carry_ref[...] = jnp.zeros_like(carry_ref)

    xn = _rms(h_ref[...], gpre_ref[...]).astype(bf16)
    p = _dot(xn, wm_ref[...])
    cos, sin = cos_ref[...], sin_ref[...]
    for h in range(N_HEADS):
        sl = slice(h * DK, (h + 1) * DK)
        rq_ref[:, sl] = _rotary(p[:, _RQ + h * DK:_RQ + (h + 1) * DK], cos, sin).astype(bf16)
        rk_ref[:, sl] = (_rotary(p[:, _RK + h * DK:_RK + (h + 1) * DK], cos, sin) * DK ** -0.5).astype(bf16)
    rv_ref[...] = p[:, _RV:_RG].astype(bf16)
    rg_ref[...] = _silu(p[:, _RG:_GQKV]).astype(bf16)
    gz_ref[...] = _silu(p[:, _GZ:_MAIN_W]).astype(bf16)

    x = p[:, _GQKV:_GZ]
    carry = carry_ref[...]
    cw = cw_ref[...]
    top = x[:SUBLANES]
    conv = x * cw[CONV_W - 1:CONV_W]
    conv_top = top * cw[CONV_W - 1:CONV_W]
    first = _iota((SUBLANES, CONV_DIM), 0)
    for s in range(1, CONV_W):
        w = cw[CONV_W - 1 - s:CONV_W - s]
        xs = pltpu.roll(x, s, axis=0)
        conv = conv + xs * w
        ts = jnp.where(first < s, pltpu.roll(carry, s, axis=0), xs[:SUBLANES])
        conv_top = conv_top + ts * w
    carry_ref[...] = x[tm - SUBLANES:]
    tail_ref[...] = x[tm - SUBLANES:]

    for rows, c in ((slice(None), conv), (slice(0, SUBLANES), conv_top)):
        qs, ks, v = _gdn_qkv(c)
        for h in range(N_HEADS):
            gq_ref[rows, h * DK:(h + 1) * DK] = qs[h].astype(bf16)
            gk_ref[rows, h * DK:(h + 1) * DK] = ks[h].astype(bf16)
        gv_ref[rows, :] = v.astype(bf16)

    ab = _dot(xn, wab_ref[...])
    gcol_ref[...] = _decay_beta(ab, alog_ref[...], dtb_ref[...], _iota(ab.shape, 1) < N_HEADS)
    abt = lax.dot_general(wabt_ref[...], xn, NT, preferred_element_type=f32)
    grow_ref[...] = _decay_beta(abt, alogt_ref[...], dtbt_ref[...], _iota(abt.shape, 0) < N_HEADS)


def _proj(h3d, gpre, wm, wab, wabt, cw, alog, dtb, alogt, dtbt, cos, sin, tm):
    b, l, _ = h3d.shape
    tok = lambda w: pl.BlockSpec((None, tm, w), lambda i, j: (i, j, 0))
    tab = pl.BlockSpec((tm, DK), lambda i, j: (j, 0))
    sds = lambda w, dt=bf16: jax.ShapeDtypeStruct((b, l, w), dt)
    return pl.pallas_call(
        _proj_body, grid=(b, l // tm),
        in_specs=[tok(D_MODEL), _resident((1, D_MODEL)), _resident((D_MODEL, _MAIN_W)), _resident((D_MODEL, LANES)),
                  _resident((SUBLANES, D_MODEL)), _resident((CONV_W, CONV_DIM)), _resident((1, LANES)),
                  _resident((1, LANES)), _resident((SUBLANES, 1)), _resident((SUBLANES, 1)), tab, tab],
        out_specs=[tok(QK_W), tok(QK_W), tok(V_W), tok(V_W), tok(QK_W), tok(QK_W), tok(V_W), tok(V_W),
                   tok(LANES), pl.BlockSpec((None, SUBLANES, tm), lambda i, j: (i, 0, j)),
                   pl.BlockSpec((None, SUBLANES, CONV_DIM), lambda i, j: (i, 0, 0))],
        out_shape=[sds(QK_W), sds(QK_W), sds(V_W), sds(V_W), sds(QK_W), sds(QK_W), sds(V_W), sds(V_W),
                   sds(LANES, f32), jax.ShapeDtypeStruct((b, SUBLANES, l), f32),
                   jax.ShapeDtypeStruct((b, SUBLANES, CONV_DIM), f32)],
        scratch_shapes=[pltpu.VMEM((SUBLANES, CONV_DIM), f32)],
        compiler_params=_params("parallel", "arbitrary"), name="proj",
    )(h3d, gpre, wm, wab, wabt, cw, alog, dtb, alogt, dtbt, cos, sin)


def _ret_body(q_ref, k_ref, v_ref, sg_ref, o_ref, sout_ref, s_ref):
    j = pl.program_id(1)
    c = q_ref.shape[0]

    @pl.when(j == 0)
    def _():
        s_ref[...] = jnp.zeros_like(s_ref)

    row, col = _iota((c, c), 0), _iota((c, c), 1)
    dist = (row - col).astype(f32)
    pos = _iota((c, 1), 0).astype(f32)
    for h in range(N_HEADS):
        lg = LOG_GAMMA[h]
        qh = q_ref[:, h * DK:(h + 1) * DK]
        kh = k_ref[:, h * DK:(h + 1) * DK]
        vh = v_ref[:, h * DV:(h + 1) * DV]
        decay = jnp.exp(jnp.where(row >= col, dist * lg, -jnp.inf))
        attn = lax.dot_general(qh, kh, NT, preferred_element_type=f32) * decay
        q_dec = (qh.astype(f32) * jnp.exp((pos + 1.0) * lg)).astype(bf16)
        k_dec = (kh.astype(f32) * jnp.exp((c - 1.0 - pos) * lg)).astype(bf16)
        s = s_ref[h]
        o = _dot(q_dec, s.astype(bf16)) + _dot(attn.astype(bf16), vh)
        s_ref[h] = s * math.exp(c * lg) + lax.dot_general(k_dec, vh, TN, preferred_element_type=f32)
        o_ref[:, h * DV:(h + 1) * DV] = (sg_ref[:, h * DV:(h + 1) * DV].astype(f32) * _rms(o)).astype(bf16)

    @pl.when(j == pl.num_programs(1) - 1)
    def _():
        sout_ref[...] = s_ref[...]


def _retention(rq, rk, rv, sg):
    b, l, _ = rq.shape
    tok = lambda w: pl.BlockSpec((None, CHUNK, w), lambda i, j: (i, j, 0))
    st = pl.BlockSpec((None, N_HEADS, DK, DV), lambda i, j: (i, 0, 0, 0))
    return pl.pallas_call(
        _ret_body, grid=(b, l // CHUNK),
        in_specs=[tok(QK_W), tok(QK_W), tok(V_W), tok(V_W)],
        out_specs=[tok(V_W), st],
        out_shape=[jax.ShapeDtypeStruct((b, l, V_W), bf16), jax.ShapeDtypeStruct((b, N_HEADS, DK, DV), f32)],
        scratch_shapes=[pltpu.VMEM((N_HEADS, DK, DV), f32)],
        compiler_params=_params("parallel", "arbitrary"), name="retention",
    )(rq, rk, rv, sg)


def _unit_lower_inverse(a):
    c = a.shape[0]
    eye = (_iota((c, c), 0) == _iota((c, c), 1)).astype(f32)
    x = -a
    inv = eye + x
    for _ in range(int(math.log2(c)) - 1):
        x = jnp.dot(x, x, preferred_element_type=f32, precision=HIGHEST)
        inv = inv + jnp.dot(inv, x, preferred_element_type=f32, precision=HIGHEST)
    return inv


def _gdn_body(q_ref, k_ref, v_ref, sz_ref, gcol_ref, grow_ref, nw_ref, o_ref, sout_ref, s_ref):
    j = pl.program_id(1)
    c = q_ref.shape[0]

    @pl.when(j == 0)
    def _():
        s_ref[...] = jnp.zeros_like(s_ref)

    row, col = _iota((c, c), 0), _iota((c, c), 1)
    lower = (row >= col).astype(f32)
    gcol = gcol_ref[...]
    bcol = jnp.dot(lower, gcol, preferred_element_type=f32, precision=HIGHEST)
    brow = jnp.dot(grow_ref[...], (row <= col).astype(f32), preferred_element_type=f32, precision=HIGHEST)
    nw = nw_ref[...]
    for h in range(N_HEADS):
        qh = q_ref[:, h * DK:(h + 1) * DK]
        kh = k_ref[:, h * DK:(h + 1) * DK]
        vh = v_ref[:, h * DV:(h + 1) * DV].astype(f32)
        beta = gcol[:, N_HEADS + h:N_HEADS + h + 1]
        bc = bcol[:, h:h + 1]
        b_last = bc[c - 1:c]
        decay = jnp.exp(jnp.where(row >= col, bc - brow[h:h + 1], -jnp.inf))
        attn = lax.dot_general(qh, kh, NT, preferred_element_type=f32) * decay
        kk = lax.dot_general(kh, kh, NT, preferred_element_type=f32)
        a = jnp.where(row > col, kk * beta * decay, 0.0)
        inv = _unit_lower_inverse(a)
        kf = kh.astype(f32)
        e_b = jnp.exp(bc)
        u = jnp.dot(inv, vh * beta, preferred_element_type=f32, precision=HIGHEST)
        w = jnp.dot(inv, kf * (beta * e_b), preferred_element_type=f32, precision=HIGHEST)
        s = s_ref[h]
        sb = s.astype(bf16)
        v_new = u - _dot(w.astype(bf16), sb)
        vb = v_new.astype(bf16)
        o = _dot((qh.astype(f32) * e_b).astype(bf16), sb) + _dot(attn.astype(bf16), vb)
        k_dec = (kf * jnp.exp(b_last - bc)).astype(bf16)
        s_ref[h] = s * jnp.exp(b_last) + lax.dot_general(k_dec, vb, TN, preferred_element_type=f32)
        o_ref[:, h * DV:(h + 1) * DV] = (_rms(o, nw) * sz_ref[:, h * DV:(h + 1) * DV].astype(f32)).astype(bf16)

    @pl.when(j == pl.num_programs(1) - 1)
    def _():
        sout_ref[...] = s_ref[...]


def _gated_delta(gq, gk, gv, sz, gcol, grow4, nw):
    b, l, _ = gq.shape
    tok = lambda w: pl.BlockSpec((None, CHUNK, w), lambda i, j: (i, j, 0))
    st = pl.BlockSpec((None, N_HEADS, DK, DV), lambda i, j: (i, 0, 0, 0))
    return pl.pallas_call(
        _gdn_body, grid=(b, l // CHUNK),
        in_specs=[tok(QK_W), tok(QK_W), tok(V_W), tok(V_W), tok(LANES),
                  pl.BlockSpec((None, None, SUBLANES, CHUNK), lambda i, j: (i, j, 0, 0)), _resident((1, DV))],
        out_specs=[tok(V_W), st],
        out_shape=[jax.ShapeDtypeStruct((b, l, V_W), bf16), jax.ShapeDtypeStruct((b, N_HEADS, DK, DV), f32)],
        scratch_shapes=[pltpu.VMEM((N_HEADS, DK, DV), f32)],
        compiler_params=_params("parallel", "arbitrary"), name="gated_delta",
    )(gq, gk, gv, sz, gcol, grow4, nw)


def _softmax(s):
    e = jnp.exp(s - jnp.max(s, axis=-1, keepdims=True))
    return e / jnp.sum(e, axis=-1, keepdims=True)


def _merge_tail(x, xn, o_ret, o_gdn, mem_proj, wg_ref, wr_ref, wd_ref, wo_ref, gpost):
    gates = jax.nn.sigmoid(_dot(xn, wg_ref[...]))
    merged = (gates[:, :D_MODEL] * _dot(o_ret, wr_ref[...])
              + gates[:, D_MODEL:2 * D_MODEL] * _dot(o_gdn, wd_ref[...])
              + gates[:, 2 * D_MODEL:] * mem_proj)
    y = _dot(merged.astype(bf16), wo_ref[...])
    return x + _rms(y, gpost)


def _merge_body(h_ref, gpre_ref, wq_ref, wg_ref, mk_ref, mv_ref, oret_ref, ogdn_ref,
                wr_ref, wd_ref, wm_ref, wo_ref, gpost_ref, o_ref):
    x = h_ref[...]
    xn = _rms(x, gpre_ref[...]).astype(bf16)
    mq = _dot(xn, wq_ref[...]).astype(bf16)
    mem_proj = None
    for h in range(N_HEADS):
        sl = slice(h * HD_X, (h + 1) * HD_X)
        s = lax.dot_general(mq[:, sl], mk_ref[:, sl], NT, preferred_element_type=f32) * HD_X ** -0.5
        o = _dot(_softmax(s).astype(bf16), mv_ref[:, sl])
        part = _dot(o.astype(bf16), wm_ref[sl, :])
        mem_proj = part if mem_proj is None else mem_proj + part
    o_ref[...] = _merge_tail(x, xn, oret_ref[...], ogdn_ref[...], mem_proj, wg_ref, wr_ref, wd_ref, wo_ref,
                             gpost_ref[...])


def _merge(h3d, gpre, wq, wg, mk, mv, o_ret, o_gdn, wr, wd, wm, wo, gpost, tm):
    b, l, _ = h3d.shape
    tok = pl.BlockSpec((None, tm, D_MODEL), lambda i, j: (i, j, 0))
    mem = pl.BlockSpec((None, N_MEM, D_MODEL), lambda i, j: (i, 0, 0))
    sq = _resident((D_MODEL, D_MODEL))
    return pl.pallas_call(
        _merge_body, grid=(b, l // tm),
        in_specs=[tok, _resident((1, D_MODEL)), sq, _resident((D_MODEL, 3 * D_MODEL)), mem, mem, tok, tok,
                  sq, sq, sq, sq, _resident((1, D_MODEL))],
        out_specs=tok, out_shape=jax.ShapeDtypeStruct((b, l, D_MODEL), f32),
        compiler_params=_params("parallel", "parallel"), name="merge",
    )(h3d, gpre, wq, wg, mk, mv, o_ret, o_gdn, wr, wd, wm, wo, gpost)


def _sproj_body(h_ref, gpre_ref, wm_ref, wab_ref, wq_ref, cw_ref, alog_ref, dtb_ref, cos_ref, sin_ref, buf_ref,
                rq_ref, rk_ref, rv_ref, rg_ref, gq_ref, gk_ref, gv_ref, gz_ref, gcol_ref, mq_ref, nbuf_ref):
    xn = _rms(h_ref[...], gpre_ref[...]).astype(bf16)
    p = _dot(xn, wm_ref[...])
    cos, sin = cos_ref[...], sin_ref[...]
    for h in range(N_HEADS):
        sl = slice(h * DK, (h + 1) * DK)
        rq_ref[:, sl] = _rotary(p[:, _RQ + h * DK:_RQ + (h + 1) * DK], cos, sin)
        rk_ref[:, sl] = _rotary(p[:, _RK + h * DK:_RK + (h + 1) * DK], cos, sin) * DK ** -0.5
    rv_ref[...] = p[:, _RV:_RG]
    rg_ref[...] = _silu(p[:, _RG:_GQKV])
    gz_ref[...] = _silu(p[:, _GZ:_MAIN_W])
    x = p[:, _GQKV:_GZ]
    cw = cw_ref[...]
    conv = x * cw[CONV_W - 1:CONV_W]
    for s in range(CONV_W - 1):
        prev = buf_ref[:, s * CONV_DIM:(s + 1) * CONV_DIM]
        conv = conv + prev * cw[s:s + 1]
        if s > 0:
            nbuf_ref[:, (s - 1) * CONV_DIM:s * CONV_DIM] = prev
    nbuf_ref[:, (CONV_W - 2) * CONV_DIM:] = x
    qs, ks, v = _gdn_qkv(conv)
    for h in range(N_HEADS):
        gq_ref[:, h * DK:(h + 1) * DK] = qs[h]
        gk_ref[:, h * DK:(h + 1) * DK] = ks[h]
    gv_ref[...] = v
    ab = _dot(xn, wab_ref[...])
    gcol_ref[...] = _decay_beta(ab, alog_ref[...], dtb_ref[...], _iota(ab.shape, 1) < N_HEADS)
    mq_ref[...] = _dot(xn, wq_ref[...])


def _sample_proj(h2d, gpre, wm, wab, wq, cw, alog, dtb, cos, sin, buf2d):
    n = h2d.shape[0]
    sds = lambda w: jax.ShapeDtypeStruct((n, w), f32)
    return pl.pallas_call(
        _sproj_body,
        out_shape=[sds(QK_W), sds(QK_W), sds(V_W), sds(V_W), sds(QK_W), sds(QK_W), sds(V_W), sds(V_W),
                   sds(LANES), sds(D_MODEL), sds((CONV_W - 1) * CONV_DIM)],
        compiler_params=pltpu.CompilerParams(vmem_limit_bytes=VMEM_LIMIT), name="sample_proj",
    )(h2d, gpre, wm, wab, wq, cw, alog, dtb, cos, sin, buf2d)


_STATE_BLOCK = 8


def _sstate_body(cols_ref, rv_ref, gv_ref, gcol_ref, sret_ref, sgdn_ref, oret_ref, ogdn_ref, nret_ref, ngdn_ref):
    for j in range(_STATE_BLOCK):
        for h in range(N_HEADS):
            vs = slice(h * DV, (h + 1) * DV)
            q = cols_ref[0, h, :, j:j + 1]
            k = cols_ref[0, N_HEADS + h, :, j:j + 1]
            s_new = sret_ref[j, h] * math.exp(LOG_GAMMA[h]) + k * rv_ref[j:j + 1, vs]
            nret_ref[j, h] = s_new
            oret_ref[j:j + 1, vs] = jnp.sum(q * s_new, axis=0, keepdims=True)
            q = cols_ref[0, 2 * N_HEADS + h, :, j:j + 1]
            k = cols_ref[0, 3 * N_HEADS + h, :, j:j + 1]
            eg = jnp.exp(gcol_ref[j:j + 1, h:h + 1])
            beta = gcol_ref[j:j + 1, N_HEADS + h:N_HEADS + h + 1]
            s = sgdn_ref[j, h]
            ks = jnp.sum(k * s, axis=0, keepdims=True)
            v_new = beta * (gv_ref[j:j + 1, vs] - eg * ks)
            s_new = s * eg + k * v_new
            ngdn_ref[j, h] = s_new
            ogdn_ref[j:j + 1, vs] = jnp.sum(q * s_new, axis=0, keepdims=True)


def _sample_state(cols, rv, gv, gcol, s_ret, s_gdn):
    n = rv.shape[0]
    bt = _STATE_BLOCK
    rowb = lambda w: pl.BlockSpec((bt, w), lambda i: (i, 0))
    st = pl.BlockSpec((bt, N_HEADS, DK, DV), lambda i: (i, 0, 0, 0))
    return pl.pallas_call(
        _sstate_body, grid=(n // bt,),
        in_specs=[pl.BlockSpec((1, 4 * N_HEADS, DK, bt), lambda i: (i, 0, 0, 0)), rowb(V_W), rowb(V_W), rowb(LANES),
                  st, st],
        out_specs=[rowb(V_W), rowb(V_W), st, st],
        out_shape=[jax.ShapeDtypeStruct((n, V_W), f32), jax.ShapeDtypeStruct((n, V_W), f32),
                   jax.ShapeDtypeStruct(s_ret.shape, f32), jax.ShapeDtypeStruct(s_gdn.shape, f32)],
        compiler_params=_params("parallel"), name="sample_state",
    )(cols, rv, gv, gcol, s_ret, s_gdn)


_ATTN_BLOCK = 8


def _sattn_body(mq_ref, k_ref, v_ref, o_ref):
    head_of_lane = _iota((SUBLANES, D_MODEL), 1) // HD_X
    own = head_of_lane == _iota((SUBLANES, D_MODEL), 0)
    for j in range(_ATTN_BLOCK):
        q = jnp.where(own, mq_ref[j:j + 1, :], 0.0).astype(bf16)
        s = lax.dot_general(q, k_ref[j].astype(bf16), NT, preferred_element_type=f32) * HD_X ** -0.5
        o = _dot(_softmax(s).astype(bf16), v_ref[j].astype(bf16))
        o_ref[j:j + 1, :] = jnp.sum(jnp.where(own, o, 0.0), axis=0, keepdims=True)


def _sample_attn(mq, mem_k, mem_v):
    n = mq.shape[0]
    bt = _ATTN_BLOCK
    rowb = pl.BlockSpec((bt, D_MODEL), lambda i: (i, 0))
    mem = pl.BlockSpec((bt, N_MEM, D_MODEL), lambda i: (i, 0, 0))
    return pl.pallas_call(
        _sattn_body, grid=(n // bt,), in_specs=[rowb, mem, mem], out_specs=rowb,
        out_shape=jax.ShapeDtypeStruct((n, D_MODEL), f32),
        compiler_params=_params("parallel"), name="sample_attn",
    )(mq, mem_k, mem_v)


def _smerge_body(h_ref, gpre_ref, wg_ref, oret_ref, rg_ref, ogdn_ref, gz_ref, nw_ref, omem_ref,
                 wr_ref, wd_ref, wm_ref, wo_ref, gpost_ref, o_ref):
    x = h_ref[...]
    xn = _rms(x, gpre_ref[...]).astype(bf16)
    nw = nw_ref[...]
    rets, gdns = [], []
    for h in range(N_HEADS):
        vs = slice(h * DV, (h + 1) * DV)
        rets.append(rg_ref[:, vs] * _rms(oret_ref[:, vs]))
        gdns.append(_rms(ogdn_ref[:, vs], nw) * gz_ref[:, vs])
    o_ret = jnp.concatenate(rets, axis=-1).astype(bf16)
    o_gdn = jnp.concatenate(gdns, axis=-1).astype(bf16)
    mem_proj = _dot(omem_ref[...].astype(bf16), wm_ref[...])
    o_ref[...] = _merge_tail(x, xn, o_ret, o_gdn, mem_proj, wg_ref, wr_ref, wd_ref, wo_ref, gpost_ref[...])


def _sample_merge(h2d, gpre, wg, o_ret, rg, o_gdn, gz, nw, o_mem, wr, wd, wm, wo, gpost):
    return pl.pallas_call(
        _smerge_body, out_shape=jax.ShapeDtypeStruct(h2d.shape, f32),
        compiler_params=pltpu.CompilerParams(vmem_limit_bytes=VMEM_LIMIT), name="sample_merge",
    )(h2d, gpre, wg, o_ret, rg, o_gdn, gz, nw, o_mem, wr, wd, wm, wo, gpost)


def _rope_tables(pos):
    half = DK // 2
    inv_freq = ROPE_BASE ** (-jnp.arange(half, dtype=f32) / half)
    ang = pos.astype(f32)[:, None] * inv_freq[None, :]
    cos, sin = jnp.cos(ang), jnp.sin(ang)
    return jnp.concatenate([cos, cos], axis=-1), jnp.concatenate([-sin, sin], axis=-1)


def _row_tile(n, want):
    t = min(n, want)
    assert n % t == 0, (n, t)
    return t


def kernel(x_prompt, x_sample, mem_prompt, state_ret, state_gdn, state_conv, cache_mem_k, cache_mem_v, norm_ffn1_pre, norm_ffn1_post, ffn1_w_in, ffn1_w_out, norm_mix_pre, norm_mix_post, w_in, gdn_conv_w, gdn_a_log, gdn_dt_bias, gdn_norm, norm_mem, w_mem_k, w_mem_v, w_branch_ret, w_branch_gdn, w_branch_mem, w_out, norm_ffn2_pre, norm_ffn2_post, ffn2_w_in, ffn2_w_out):
    bp, sp, _ = x_prompt.shape
    ns, ss, _ = x_sample.shape
    depth = w_in.shape[0]
    assert ss == 1 and sp % CHUNK == 0 and ns % _STATE_BLOCK == 0 and ns % _ATTN_BLOCK == 0
    tm_ffn = _row_tile(bp * sp, 512)
    tm_seq = _row_tile(sp, 512)

    cos_p, sin_p = _rope_tables(jnp.arange(sp))
    cos_s, sin_s = _rope_tables(PAST_LEN + jnp.arange(ss))
    row = lambda v: v.reshape(1, -1).astype(f32)
    pad_lanes = lambda v: jnp.pad(v.astype(f32), (0, LANES - v.shape[0])).reshape(1, LANES)
    pad_rows = lambda v: jnp.pad(v.astype(f32), (0, SUBLANES - v.shape[0])).reshape(SUBLANES, 1)

    hp = x_prompt
    hs = x_sample.reshape(ns, D_MODEL)
    outs = [[] for _ in range(8)]
    for l in range(depth):
        wl = w_in[l]
        w_main = wl[:, :_MAIN_W].astype(bf16)
        w_ab = wl[:, _MAIN_W:_MAIN_W + 2 * N_HEADS]
        w_ab_col = jnp.pad(w_ab, ((0, 0), (0, LANES - 2 * N_HEADS))).astype(bf16)
        w_ab_row = w_ab.T.astype(bf16)
        w_mq = wl[:, _MAIN_W + 2 * N_HEADS:_MAIN_W + 2 * N_HEADS + D_MODEL].astype(bf16)
        w_gates = wl[:, _MAIN_W + 2 * N_HEADS + D_MODEL:].astype(bf16)
        f1_in, f1_out = ffn1_w_in[l].astype(bf16), ffn1_w_out[l].astype(bf16)
        f2_in, f2_out = ffn2_w_in[l].astype(bf16), ffn2_w_out[l].astype(bf16)
        wr, wd, wm, wo = (w[l].astype(bf16) for w in (w_branch_ret, w_branch_gdn, w_branch_mem, w_out))
        cw = gdn_conv_w[l].astype(f32)
        alog, dtb = pad_lanes(gdn_a_log[l]), pad_lanes(gdn_dt_bias[l])
        alog_t, dtb_t = pad_rows(gdn_a_log[l]), pad_rows(gdn_dt_bias[l])
        nw = row(gdn_norm[l])
        g_f1pre, g_f1post = row(norm_ffn1_pre[l]), row(norm_ffn1_post[l])
        g_f2pre, g_f2post = row(norm_ffn2_pre[l]), row(norm_ffn2_post[l])
        g_mpre, g_mpost = row(norm_mix_pre[l]), row(norm_mix_post[l])

        mk, mv, mk_b, mv_b = _memkv(mem_prompt, row(norm_mem[l]), w_mem_k[l].astype(bf16), w_mem_v[l].astype(bf16))
        h1 = _ffn(hp.reshape(bp * sp, D_MODEL), g_f1pre, f1_in, f1_out, g_f1post, tm_ffn).reshape(bp, sp, D_MODEL)
        rq, rk, rv, rg, gq, gk, gv, gz, gcol, grow, tail = _proj(
            h1, g_mpre, w_main, w_ab_col, w_ab_row, cw, alog, dtb, alog_t, dtb_t, cos_p, sin_p, tm_seq)
        o_ret, s_ret = _retention(rq, rk, rv, rg)
        grow4 = grow.reshape(bp, SUBLANES, sp // CHUNK, CHUNK).transpose(0, 2, 1, 3)
        o_gdn, s_gdn = _gated_delta(gq, gk, gv, gz, gcol, grow4, nw)
        h2 = _merge(h1, g_mpre, w_mq, w_gates, mk_b, mv_b, o_ret, o_gdn, wr, wd, wm, wo, g_mpost, tm_seq)
        hp = _ffn(h2.reshape(bp * sp, D_MODEL), g_f2pre, f2_in, f2_out, g_f2post, tm_ffn).reshape(bp, sp, D_MODEL)
        outs[0].append(s_ret)
        outs[1].append(s_gdn)
        outs[2].append(tail[:, SUBLANES - (CONV_W - 1):])
        outs[3].append(mk.reshape(bp, N_MEM, N_HEADS, HD_X))
        outs[4].append(mv.reshape(bp, N_MEM, N_HEADS, HD_X))

        h1s = _ffn(hs, g_f1pre, f1_in, f1_out, g_f1post, ns)
        rq, rk, rv, rg, gq, gk, gv, gz, gcol, mq, nbuf = _sample_proj(
            h1s, g_mpre, w_main, w_ab_col, w_mq, cw, alog, dtb, cos_s, sin_s,
            state_conv[l].reshape(ns, (CONV_W - 1) * CONV_DIM))
        cols = jnp.stack([rq, rk, gq, gk]).reshape(4, ns // _STATE_BLOCK, _STATE_BLOCK, N_HEADS, DK)
        cols = cols.transpose(1, 0, 3, 4, 2).reshape(ns // _STATE_BLOCK, 4 * N_HEADS, DK, _STATE_BLOCK)
        o_ret, o_gdn, n_ret, n_gdn = _sample_state(cols, rv, gv, gcol, state_ret[l], state_gdn[l])
        o_mem = _sample_attn(mq, cache_mem_k[l].reshape(ns, N_MEM, D_MODEL), cache_mem_v[l].reshape(ns, N_MEM, D_MODEL))
        h2s = _sample_merge(h1s, g_mpre, w_gates, o_ret, rg, o_gdn, gz, nw, o_mem, wr, wd, wm, wo, g_mpost)
        hs = _ffn(h2s, g_f2pre, f2_in, f2_out, g_f2post, ns)
        outs[5].append(n_ret)
        outs[6].append(n_gdn)
        outs[7].append(nbuf.reshape(ns, CONV_W - 1, CONV_DIM))

    return (hp, hs.reshape(ns, ss, D_MODEL)) + tuple(jnp.stack(o) for o in outs)
```

```python
import functools
import math

import jax
import jax.numpy as jnp
from jax import lax
from jax.experimental import pallas as pl
from jax.experimental.pallas import tpu as pltpu

f32, bf16 = jnp.float32, jnp.bfloat16

D_MODEL = 1024
N_HEADS = 4
DK = 128
DV = 256
QK_W = N_HEADS * DK
V_W = N_HEADS * DV
CONV_W = 4
CONV_DIM = 2 * QK_W + V_W
HD_X = 256
N_MEM = 256
D_FF = 2816
EPS = 1e-6
ROPE_BASE = 10000.0
PAST_LEN = 16384
CHUNK = 64
LANES = 128
SUBLANES = 8
VMEM_LIMIT = 56 * 1024 * 1024
LOG_GAMMA = tuple(math.log1p(-2.0 ** (-5.0 - h)) for h in range(N_HEADS))
HIGHEST = lax.Precision.HIGHEST
NT = (((1,), (1,)), ((), ()))
TN = (((0,), (0,)), ((), ()))


def _params(*sem):
    return pltpu.CompilerParams(dimension_semantics=sem, vmem_limit_bytes=VMEM_LIMIT)


def _resident(shape):
    return pl.BlockSpec(shape, lambda *_: (0,) * len(shape), pipeline_mode=pl.Buffered(1))


def _rms(x, gain=None):
    y = x * lax.rsqrt(jnp.mean(x * x, axis=-1, keepdims=True) + EPS)
    return y if gain is None else y * gain


def _silu(x):
    return x * jax.nn.sigmoid(x)


def _softplus(x):
    return jnp.maximum(x, 0.0) + jnp.log1p(jnp.exp(-jnp.abs(x)))


def _dot(a, b):
    return jnp.dot(a, b, preferred_element_type=f32)


def _iota(shape, dim):
    return lax.broadcasted_iota(jnp.int32, shape, dim)


def _ffn_body(x_ref, gpre_ref, wgu_ref, wo_ref, gpost_ref, o_ref):
    x = x_ref[...]
    xn = _rms(x, gpre_ref[...]).astype(bf16)
    h = _dot(xn, wgu_ref[...])
    act = (_silu(h[:, :D_FF]) * h[:, D_FF:]).astype(bf16)
    y = _dot(act, wo_ref[...])
    o_ref[...] = x + 0.5 * _rms(y, gpost_ref[...])


def _ffn(x2d, gpre, wgu, wo, gpost, tm):
    t = x2d.shape[0]
    row = pl.BlockSpec((tm, D_MODEL), lambda i: (i, 0))
    return pl.pallas_call(
        _ffn_body, grid=(t // tm,),
        in_specs=[row, _resident((1, D_MODEL)), _resident((D_MODEL, 2 * D_FF)),
                  _resident((D_FF, D_MODEL)), _resident((1, D_MODEL))],
        out_specs=row, out_shape=jax.ShapeDtypeStruct((t, D_MODEL), f32),
        compiler_params=_params("parallel"), name="ffn",
    )(x2d, gpre, wgu, wo, gpost)


def _stacked_out(layer, depth, shape, dtype, index_map):
    spec = pl.BlockSpec((None,) + shape, lambda *i: (layer,) + index_map(*i))
    return spec, lambda full: jax.ShapeDtypeStruct((depth,) + full, dtype)


def _alias_prev(n_inputs, prev):
    prev = [] if prev is None else list(prev)
    return prev, [pl.BlockSpec(memory_space=pl.ANY)] * len(prev), {n_inputs + i: i for i in range(len(prev))}


def _memkv_body(m_ref, g_ref, wk_ref, wv_ref, *refs):
    k_ref, v_ref, kb_ref, vb_ref = refs[-4:]
    mn = _rms(m_ref[...], g_ref[...]).astype(bf16)
    k = _dot(mn, wk_ref[...])
    v = _dot(mn, wv_ref[...])
    for h in range(N_HEADS):
        k_ref[:, h, :] = k[:, h * HD_X:(h + 1) * HD_X]
        v_ref[:, h, :] = v[:, h * HD_X:(h + 1) * HD_X]
    kb_ref[...] = k.astype(bf16)
    vb_ref[...] = v.astype(bf16)


def _memkv(layer, depth, mem, g, wk, wv, prev):
    b = mem.shape[0]
    blk = pl.BlockSpec((None, N_MEM, D_MODEL), lambda i: (i, 0, 0))
    out5, sds5 = _stacked_out(layer, depth, (None, N_MEM, N_HEADS, HD_X), f32, lambda i: (i, 0, 0, 0))
    sds5 = sds5((b, N_MEM, N_HEADS, HD_X))
    sdsb = jax.ShapeDtypeStruct((b, N_MEM, D_MODEL), bf16)
    prev, prev_specs, aliases = _alias_prev(4, prev)
    return pl.pallas_call(
        _memkv_body, grid=(b,),
        in_specs=[blk, _resident((1, D_MODEL)), _resident((D_MODEL, D_MODEL)), _resident((D_MODEL, D_MODEL))]
        + prev_specs,
        out_specs=[out5, out5, blk, blk], out_shape=[sds5, sds5, sdsb, sdsb],
        input_output_aliases=aliases, compiler_params=_params("parallel"), name="memkv",
    )(mem, g, wk, wv, *prev)


_RQ, _RK, _RV, _RG, _GQKV, _GZ, _MAIN_W = 0, 512, 1024, 2048, 3072, 5120, 6144


def _rotary(x, cos, sin_signed):
    return x * cos + pltpu.roll(x, DK // 2, axis=1) * sin_signed


def _decay_beta(ab, a_log, dt_bias, is_decay):
    g = -jnp.exp(a_log) * _softplus(ab + dt_bias)
    return jnp.where(is_decay, g, jax.nn.sigmoid(ab))


def _gdn_qkv(conv):
    a = _silu(conv)
    qs, ks = [], []
    for h in range(N_HEADS):
        q = a[:, h * DK:(h + 1) * DK]
        k = a[:, QK_W + h * DK:QK_W + (h + 1) * DK]
        qs.append(q * (lax.rsqrt(jnp.sum(q * q, axis=-1, keepdims=True) + EPS) * DK ** -0.5))
        ks.append(k * lax.rsqrt(jnp.sum(k * k, axis=-1, keepdims=True) + EPS))
    return qs, ks, a[:, 2 * QK_W:]


def _proj_body(h_ref, gpre_ref, wm_ref, wab_ref, wabt_ref, cw_ref, alog_ref, dtb_ref, alogt_ref, dtbt_ref,
               cos_ref, sin_ref,
               rq_ref, rk_ref, rv_ref, rg_ref, gq_ref, gk_ref, gv_ref, gz_ref, gcol_ref, grow_ref, tail_ref,
               carry_ref):
    j = pl.program_id(1)
    tm = h_ref.shape[0]

    @pl.when(j == 0)
    def _():
        carry_ref[...] = jnp.zeros_like(carry_ref)

    xn = _rms(h_ref[...], gpre_ref[...]).astype(bf16)
    p = _dot(xn, wm_ref[...])
    cos, sin = cos_ref[...], sin_ref[...]
    for h in range(N_HEADS):
        sl = slice(h * DK, (h + 1) * DK)
        rq_ref[:, sl] = _rotary(p[:, _RQ + h * DK:_RQ + (h + 1) * DK], cos, sin).astype(bf16)
        rk_ref[:, sl] = (_rotary(p[:, _RK + h * DK:_RK + (h + 1) * DK], cos, sin) * DK ** -0.5).astype(bf16)
    rv_ref[...] = p[:, _RV:_RG].astype(bf16)
    rg_ref[...] = _silu(p[:, _RG:_GQKV]).astype(bf16)
    gz_ref[...] = _silu(p[:, _GZ:_MAIN_W]).astype(bf16)

    x = p[:, _GQKV:_GZ]
    carry = carry_ref[...]
    cw = cw_ref[...]
    top = x[:SUBLANES]
    conv = x * cw[CONV_W - 1:CONV_W]
    conv_top = top * cw[CONV_W - 1:CONV_W]
    first = _iota((SUBLANES, CONV_DIM), 0)
    for s in range(1, CONV_W):
        w = cw[CONV_W - 1 - s:CONV_W - s]
        xs = pltpu.roll(x, s, axis=0)
        conv = conv + xs * w
        ts = jnp.where(first < s, pltpu.roll(carry, s, axis=0), xs[:SUBLANES])
        conv_top = conv_top + ts * w
    carry_ref[...] = x[tm - SUBLANES:]
    tail_ref[...] = x[tm - SUBLANES:]

    for rows, c in ((slice(None), conv), (slice(0, SUBLANES), conv_top)):
        qs, ks, v = _gdn_qkv(c)
        for h in range(N_HEADS):
            gq_ref[rows, h * DK:(h + 1) * DK] = qs[h].astype(bf16)
            gk_ref[rows, h * DK:(h + 1) * DK] = ks[h].astype(bf16)
        gv_ref[rows, :] = v.astype(bf16)

    ab = _dot(xn, wab_ref[...])
    gcol_ref[...] = _decay_beta(ab, alog_ref[...], dtb_ref[...], _iota(ab.shape, 1) < N_HEADS)
    abt = lax.dot_general(wabt_ref[...], xn, NT, preferred_element_type=f32)
    grow_ref[...] = _decay_beta(abt, alogt_ref[...], dtbt_ref[...], _iota(abt.shape, 0) < N_HEADS)


def _proj(h3d, gpre, wm, wab, wabt, cw, alog, dtb, alogt, dtbt, cos, sin, tm):
    b, l, _ = h3d.shape
    tok = lambda w: pl.BlockSpec((None, tm, w), lambda i, j: (i, j, 0))
    tab = pl.BlockSpec((tm, DK), lambda i, j: (j, 0))
    sds = lambda w, dt=bf16: jax.ShapeDtypeStruct((b, l, w), dt)
    return pl.pallas_call(
        _proj_body, grid=(b, l // tm),
        in_specs=[tok(D_MODEL), _resident((1, D_MODEL)), _resident((D_MODEL, _MAIN_W)), _resident((D_MODEL, LANES)),
                  _resident((SUBLANES, D_MODEL)), _resident((CONV_W, CONV_DIM)), _resident((1, LANES)),
                  _resident((1, LANES)), _resident((SUBLANES, 1)), _resident((SUBLANES, 1)), tab, tab],
        out_specs=[tok(QK_W), tok(QK_W), tok(V_W), tok(V_W), tok(QK_W), tok(QK_W), tok(V_W), tok(V_W),
                   tok(LANES), pl.BlockSpec((None, SUBLANES, tm), lambda i, j: (i, 0, j)),
                   pl.BlockSpec((None, SUBLANES, CONV_DIM), lambda i, j: (i, 0, 0))],
        out_shape=[sds(QK_W), sds(QK_W), sds(V_W), sds(V_W), sds(QK_W), sds(QK_W), sds(V_W), sds(V_W),
                   sds(LANES, f32), jax.ShapeDtypeStruct((b, SUBLANES, l), f32),
                   jax.ShapeDtypeStruct((b, SUBLANES, CONV_DIM), f32)],
        scratch_shapes=[pltpu.VMEM((SUBLANES, CONV_DIM), f32)],
        compiler_params=_params("parallel", "arbitrary"), name="proj",
    )(h3d, gpre, wm, wab, wabt, cw, alog, dtb, alogt, dtbt, cos, sin)


def _ret_body(q_ref, k_ref, v_ref, sg_ref, o_ref, sout_ref, s_ref, *, chunk):
    j = pl.program_id(1)
    c = chunk

    @pl.when(j == 0)
    def _():
        s_ref[...] = jnp.zeros_like(s_ref)

    row, col = _iota((c, c), 0), _iota((c, c), 1)
    dist = (row - col).astype(f32)
    pos = _iota((c, 1), 0).astype(f32)
    for h in range(N_HEADS):
        lg = LOG_GAMMA[h]
        decay = jnp.exp(jnp.where(row >= col, dist * lg, -jnp.inf))
        q_scale = jnp.exp((pos + 1.0) * lg)
        k_scale = jnp.exp((c - 1.0 - pos) * lg)
        for t in range(q_ref.shape[0] // c):
            rows = slice(t * c, (t + 1) * c)
            qh = q_ref[rows, h * DK:(h + 1) * DK]
            kh = k_ref[rows, h * DK:(h + 1) * DK]
            vh = v_ref[rows, h * DV:(h + 1) * DV]
            attn = lax.dot_general(qh, kh, NT, preferred_element_type=f32) * decay
            q_dec = (qh.astype(f32) * q_scale).astype(bf16)
            k_dec = (kh.astype(f32) * k_scale).astype(bf16)
            s = s_ref[h]
            o = _dot(q_dec, s.astype(bf16)) + _dot(attn.astype(bf16), vh)
            s_ref[h] = s * math.exp(c * lg) + lax.dot_general(k_dec, vh, TN, preferred_element_type=f32)
            o_ref[rows, h * DV:(h + 1) * DV] = (sg_ref[rows, h * DV:(h + 1) * DV].astype(f32) * _rms(o)).astype(bf16)

    @pl.when(j == pl.num_programs(1) - 1)
    def _():
        sout_ref[...] = s_ref[...]


def _retention(rq, rk, rv, sg, tile, chunk):
    b, l, _ = rq.shape
    tok = lambda w: pl.BlockSpec((None, tile, w), lambda i, j: (i, j, 0))
    st = pl.BlockSpec((None, N_HEADS, DK, DV), lambda i, j: (i, 0, 0, 0))
    return pl.pallas_call(
        functools.partial(_ret_body, chunk=chunk), grid=(b, l // tile),
        in_specs=[tok(QK_W), tok(QK_W), tok(V_W), tok(V_W)],
        out_specs=[tok(V_W), st],
        out_shape=[jax.ShapeDtypeStruct((b, l, V_W), bf16), jax.ShapeDtypeStruct((b, N_HEADS, DK, DV), f32)],
        scratch_shapes=[pltpu.VMEM((N_HEADS, DK, DV), f32)],
        compiler_params=_params("parallel", "arbitrary"), name="retention",
    )(rq, rk, rv, sg)


def _unit_lower_correction(x):
    n = x
    for _ in range(int(math.log2(CHUNK)) - 1):
        xb = x.astype(bf16)
        x = _dot(xb, xb)
        n = n + x + _dot(n.astype(bf16), x.astype(bf16))
    return n


def _gdn_body(q_ref, k_ref, v_ref, sz_ref, gcol_ref, grow_ref, nw_ref, o_ref, sout_ref, s_ref):
    j = pl.program_id(1)
    c = CHUNK
    hc = N_HEADS * c

    @pl.when(j == 0)
    def _():
        s_ref[...] = jnp.zeros_like(s_ref)

    row, col = _iota((hc, hc), 0), _iota((hc, hc), 1)
    same_head = (row // c) == (col // c)
    lower = same_head & (row >= col)
    strict = same_head & (row > col)
    cum_rows = ((_iota((hc, c), 0) % c) >= _iota((hc, c), 1)).astype(f32)
    cum_cols = (_iota((c, hc), 0) <= (_iota((c, hc), 1) % c)).astype(f32)
    lane, lane_head = _iota((hc, LANES), 1), _iota((hc, LANES), 0) // c
    own_g, own_beta = lane == lane_head, lane == lane_head + N_HEADS
    own_row = _iota((SUBLANES, hc), 0) == _iota((SUBLANES, hc), 1) // c
    nw = nw_ref[...]
    pick = lambda mask, x: jnp.sum(jnp.where(mask, x, 0.0), axis=1, keepdims=True)

    for t in range(q_ref.shape[0] // c):
        rows = slice(t * c, (t + 1) * c)
        stack = lambda ref, w: jnp.concatenate([ref[rows, h * w:(h + 1) * w] for h in range(N_HEADS)], axis=0)
        qst, kst = stack(q_ref, DK), stack(k_ref, DK)
        vst = stack(v_ref, DV).astype(f32)
        gcol = gcol_ref[rows, :]
        b_col = pick(own_g, jnp.dot(cum_rows, gcol, preferred_element_type=f32, precision=HIGHEST))
        beta = pick(own_beta, jnp.concatenate([gcol] * N_HEADS, axis=0))
        b_row = jnp.sum(jnp.where(own_row, jnp.dot(grow_ref[t], cum_cols, preferred_element_type=f32,
                                                   precision=HIGHEST), 0.0), axis=0, keepdims=True)
        decay = jnp.exp(jnp.where(lower, b_col - b_row, -jnp.inf))
        attn = lax.dot_general(qst, kst, NT, preferred_element_type=f32) * decay
        kk = lax.dot_general(kst, kst, NT, preferred_element_type=f32)
        n = _unit_lower_correction(jnp.where(strict, kk * (-beta) * decay, 0.0))
        nb = n.astype(bf16)
        kf = kst.astype(f32)
        e_b = jnp.exp(b_col)
        ru, rw = vst * beta, kf * (beta * e_b)
        u = ru + _dot(nb, ru.astype(bf16))
        w = (rw + _dot(nb, rw.astype(bf16))).astype(bf16)
        q_dec = (qst.astype(f32) * e_b).astype(bf16)
        v_new, o_inter = [], []
        for h in range(N_HEADS):
            hs = slice(h * c, (h + 1) * c)
            r = _dot(jnp.concatenate([w[hs], q_dec[hs]], axis=0), s_ref[h].astype(bf16))
            v_new.append(u[hs] - r[:c])
            o_inter.append(r[c:])
        vb = jnp.concatenate(v_new, axis=0).astype(bf16)
        o = jnp.concatenate(o_inter, axis=0) + _dot(attn.astype(bf16), vb)
        for h in range(N_HEADS):
            hs = slice(h * c, (h + 1) * c)
            b_last = b_col[(h + 1) * c - 1:(h + 1) * c]
            k_dec = (kf[hs] * jnp.exp(b_last - b_col[hs])).astype(bf16)
            s_ref[h] = s_ref[h] * jnp.exp(b_last) + lax.dot_general(k_dec, vb[hs], TN, preferred_element_type=f32)
            o_ref[rows, h * DV:(h + 1) * DV] = (
                _rms(o[hs], nw) * sz_ref[rows, h * DV:(h + 1) * DV].astype(f32)).astype(bf16)

    @pl.when(j == pl.num_programs(1) - 1)
    def _():
        sout_ref[...] = s_ref[...]


def _gated_delta(gq, gk, gv, sz, gcol, grow4, nw, tile):
    b, l, _ = gq.shape
    tok = lambda w: pl.BlockSpec((None, tile, w), lambda i, j: (i, j, 0))
    st = pl.BlockSpec((None, N_HEADS, DK, DV), lambda i, j: (i, 0, 0, 0))
    return pl.pallas_call(
        _gdn_body, grid=(b, l // tile),
        in_specs=[tok(QK_W), tok(QK_W), tok(V_W), tok(V_W), tok(LANES),
                  pl.BlockSpec((None, tile // CHUNK, SUBLANES, CHUNK), lambda i, j: (i, j, 0, 0)), _resident((1, DV))],
        out_specs=[tok(V_W), st],
        out_shape=[jax.ShapeDtypeStruct((b, l, V_W), bf16), jax.ShapeDtypeStruct((b, N_HEADS, DK, DV), f32)],
        scratch_shapes=[pltpu.VMEM((N_HEADS, DK, DV), f32)],
        compiler_params=_params("parallel", "arbitrary"), name="gated_delta",
    )(gq, gk, gv, sz, gcol, grow4, nw)


def _softmax(s):
    e = jnp.exp(s - jnp.max(s, axis=-1, keepdims=True))
    return e / jnp.sum(e, axis=-1, keepdims=True)


def _merge_tail(x, xn, o_ret, o_gdn, mem_proj, wg_ref, wr_ref, wd_ref, wo_ref, gpost):
    gates = jax.nn.sigmoid(_dot(xn, wg_ref[...]))
    merged = (gates[:, :D_MODEL] * _dot(o_ret, wr_ref[...])
              + gates[:, D_MODEL:2 * D_MODEL] * _dot(o_gdn, wd_ref[...])
              + gates[:, 2 * D_MODEL:] * mem_proj)
    y = _dot(merged.astype(bf16), wo_ref[...])
    return x + _rms(y, gpost)


def _merge_body(h_ref, gpre_ref, wq_ref, wg_ref, mk_ref, mv_ref, oret_ref, ogdn_ref,
                wr_ref, wd_ref, wm_ref, wo_ref, gpost_ref, o_ref):
    x = h_ref[...]
    xn = _rms(x, gpre_ref[...]).astype(bf16)
    mq = _dot(xn, wq_ref[...]).astype(bf16)
    mem_proj = None
    for h in range(N_HEADS):
        sl = slice(h * HD_X, (h + 1) * HD_X)
        s = lax.dot_general(mq[:, sl], mk_ref[:, sl], NT, preferred_element_type=f32) * HD_X ** -0.5
        o = _dot(_softmax(s).astype(bf16), mv_ref[:, sl])
        part = _dot(o.astype(bf16), wm_ref[sl, :])
        mem_proj = part if mem_proj is None else mem_proj + part
    o_ref[...] = _merge_tail(x, xn, oret_ref[...], ogdn_ref[...], mem_proj, wg_ref, wr_ref, wd_ref, wo_ref,
                             gpost_ref[...])


def _merge(h3d, gpre, wq, wg, mk, mv, o_ret, o_gdn, wr, wd, wm, wo, gpost, tm):
    b, l, _ = h3d.shape
    tok = pl.BlockSpec((None, tm, D_MODEL), lambda i, j: (i, j, 0))
    mem = pl.BlockSpec((None, N_MEM, D_MODEL), lambda i, j: (i, 0, 0))
    sq = _resident((D_MODEL, D_MODEL))
    return pl.pallas_call(
        _merge_body, grid=(b, l // tm),
        in_specs=[tok, _resident((1, D_MODEL)), sq, _resident((D_MODEL, 3 * D_MODEL)), mem, mem, tok, tok,
                  sq, sq, sq, sq, _resident((1, D_MODEL))],
        out_specs=tok, out_shape=jax.ShapeDtypeStruct((b, l, D_MODEL), f32),
        compiler_params=_params("parallel", "parallel"), name="merge",
    )(h3d, gpre, wq, wg, mk, mv, o_ret, o_gdn, wr, wd, wm, wo, gpost)


def _sproj_body(h_ref, gpre_ref, wm_ref, wab_ref, wq_ref, cw_ref, alog_ref, dtb_ref, cos_ref, sin_ref, buf_ref,
                rq_ref, rk_ref, rv_ref, rg_ref, gq_ref, gk_ref, gv_ref, gz_ref, gcol_ref, mq_ref, nbuf_ref):
    xn = _rms(h_ref[...], gpre_ref[...]).astype(bf16)
    p = _dot(xn, wm_ref[...])
    cos, sin = cos_ref[...], sin_ref[...]
    for h in range(N_HEADS):
        sl = slice(h * DK, (h + 1) * DK)
        rq_ref[:, sl] = _rotary(p[:, _RQ + h * DK:_RQ + (h + 1) * DK], cos, sin)
        rk_ref[:, sl] = _rotary(p[:, _RK + h * DK:_RK + (h + 1) * DK], cos, sin) * DK ** -0.5
    rv_ref[...] = p[:, _RV:_RG]
    rg_ref[...] = _silu(p[:, _RG:_GQKV])
    gz_ref[...] = _silu(p[:, _GZ:_MAIN_W])
    x = p[:, _GQKV:_GZ]
    cw = cw_ref[...]
    conv = x * cw[CONV_W - 1:CONV_W]
    for s in range(CONV_W - 1):
        prev = buf_ref[:, s * CONV_DIM:(s + 1) * CONV_DIM]
        conv = conv + prev * cw[s:s + 1]
        if s > 0:
            nbuf_ref[:, (s - 1) * CONV_DIM:s * CONV_DIM] = prev
    nbuf_ref[:, (CONV_W - 2) * CONV_DIM:] = x
    qs, ks, v = _gdn_qkv(conv)
    for h in range(N_HEADS):
        gq_ref[:, h * DK:(h + 1) * DK] = qs[h]
        gk_ref[:, h * DK:(h + 1) * DK] = ks[h]
    gv_ref[...] = v
    ab = _dot(xn, wab_ref[...])
    gcol_ref[...] = _decay_beta(ab, alog_ref[...], dtb_ref[...], _iota(ab.shape, 1) < N_HEADS)
    mq_ref[...] = _dot(xn, wq_ref[...])


def _sample_proj(h2d, gpre, wm, wab, wq, cw, alog, dtb, cos, sin, buf2d):
    n = h2d.shape[0]
    sds = lambda w: jax.ShapeDtypeStruct((n, w), f32)
    return pl.pallas_call(
        _sproj_body,
        out_shape=[sds(QK_W), sds(QK_W), sds(V_W), sds(V_W), sds(QK_W), sds(QK_W), sds(V_W), sds(V_W),
                   sds(LANES), sds(D_MODEL), sds((CONV_W - 1) * CONV_DIM)],
        compiler_params=pltpu.CompilerParams(vmem_limit_bytes=VMEM_LIMIT), name="sample_proj",
    )(h2d, gpre, wm, wab, wq, cw, alog, dtb, cos, sin, buf2d)


_STATE_BLOCK = 8


def _sstate_body(cols_ref, rv_ref, gv_ref, gcol_ref, sret_ref, sgdn_ref, *refs):
    nret_ref, ngdn_ref, oret_ref, ogdn_ref = refs[-4:]
    for j in range(_STATE_BLOCK):
        for h in range(N_HEADS):
            vs = slice(h * DV, (h + 1) * DV)
            q = cols_ref[0, h, :, j:j + 1]
            k = cols_ref[0, N_HEADS + h, :, j:j + 1]
            s_new = sret_ref[j, h] * math.exp(LOG_GAMMA[h]) + k * rv_ref[j:j + 1, vs]
            nret_ref[j, h] = s_new
            oret_ref[j:j + 1, vs] = jnp.sum(q * s_new, axis=0, keepdims=True)
            q = cols_ref[0, 2 * N_HEADS + h, :, j:j + 1]
            k = cols_ref[0, 3 * N_HEADS + h, :, j:j + 1]
            eg = jnp.exp(gcol_ref[j:j + 1, h:h + 1])
            beta = gcol_ref[j:j + 1, N_HEADS + h:N_HEADS + h + 1]
            s = sgdn_ref[j, h]
            ks = jnp.sum(k * s, axis=0, keepdims=True)
            v_new = beta * (gv_ref[j:j + 1, vs] - eg * ks)
            s_new = s * eg + k * v_new
            ngdn_ref[j, h] = s_new
            ogdn_ref[j:j + 1, vs] = jnp.sum(q * s_new, axis=0, keepdims=True)


def _sample_state(layer, cols, rv, gv, gcol, s_ret, s_gdn, prev):
    depth, n = s_ret.shape[:2]
    bt = _STATE_BLOCK
    rowb = lambda w: pl.BlockSpec((bt, w), lambda i: (i, 0))
    st, sds = _stacked_out(layer, depth, (bt, N_HEADS, DK, DV), f32, lambda i: (i, 0, 0, 0))
    sds = sds((n, N_HEADS, DK, DV))
    prev, prev_specs, aliases = _alias_prev(6, prev)
    return pl.pallas_call(
        _sstate_body, grid=(n // bt,),
        in_specs=[pl.BlockSpec((1, 4 * N_HEADS, DK, bt), lambda i: (i, 0, 0, 0)), rowb(V_W), rowb(V_W), rowb(LANES),
                  st, st] + prev_specs,
        out_specs=[st, st, rowb(V_W), rowb(V_W)],
        out_shape=[sds, sds, jax.ShapeDtypeStruct((n, V_W), f32), jax.ShapeDtypeStruct((n, V_W), f32)],
        input_output_aliases=aliases, compiler_params=_params("parallel"), name="sample_state",
    )(cols, rv, gv, gcol, s_ret, s_gdn, *prev)


_ATTN_BLOCK = 8


def _sattn_body(mq_ref, k_ref, v_ref, o_ref):
    for j in range(_ATTN_BLOCK):
        for h in range(N_HEADS):
            sl = slice(h * HD_X, (h + 1) * HD_X)
            q = jnp.broadcast_to(mq_ref[j:j + 1, sl], (SUBLANES, HD_X)).astype(bf16)
            s = lax.dot_general(q, k_ref[j, :, h, :].astype(bf16), NT, preferred_element_type=f32) * HD_X ** -0.5
            o = _dot(_softmax(s).astype(bf16), v_ref[j, :, h, :].astype(bf16))
            o_ref[j:j + 1, sl] = o[:1]


def _sample_attn(layer, mq, mem_k, mem_v):
    n = mq.shape[0]
    bt = _ATTN_BLOCK
    rowb = pl.BlockSpec((bt, D_MODEL), lambda i: (i, 0))
    mem = pl.BlockSpec((None, bt, N_MEM, N_HEADS, HD_X), lambda i: (layer, i, 0, 0, 0))
    return pl.pallas_call(
        _sattn_body, grid=(n // bt,), in_specs=[rowb, mem, mem], out_specs=rowb,
        out_shape=jax.ShapeDtypeStruct((n, D_MODEL), f32),
        compiler_params=_params("parallel"), name="sample_attn",
    )(mq, mem_k, mem_v)


def _smerge_body(h_ref, gpre_ref, wg_ref, oret_ref, rg_ref, ogdn_ref, gz_ref, nw_ref, omem_ref,
                 wr_ref, wd_ref, wm_ref, wo_ref, gpost_ref, o_ref):
    x = h_ref[...]
    xn = _rms(x, gpre_ref[...]).astype(bf16)
    nw = nw_ref[...]
    rets, gdns = [], []
    for h in range(N_HEADS):
        vs = slice(h * DV, (h + 1) * DV)
        rets.append(rg_ref[:, vs] * _rms(oret_ref[:, vs]))
        gdns.append(_rms(ogdn_ref[:, vs], nw) * gz_ref[:, vs])
    o_ret = jnp.concatenate(rets, axis=-1).astype(bf16)
    o_gdn = jnp.concatenate(gdns, axis=-1).astype(bf16)
    mem_proj = _dot(omem_ref[...].astype(bf16), wm_ref[...])
    o_ref[...] = _merge_tail(x, xn, o_ret, o_gdn, mem_proj, wg_ref, wr_ref, wd_ref, wo_ref, gpost_ref[...])


def _sample_merge(h2d, gpre, wg, o_ret, rg, o_gdn, gz, nw, o_mem, wr, wd, wm, wo, gpost):
    return pl.pallas_call(
        _smerge_body, out_shape=jax.ShapeDtypeStruct(h2d.shape, f32),
        compiler_params=pltpu.CompilerParams(vmem_limit_bytes=VMEM_LIMIT), name="sample_merge",
    )(h2d, gpre, wg, o_ret, rg, o_gdn, gz, nw, o_mem, wr, wd, wm, wo, gpost)


def _rope_tables(pos):
    half = DK // 2
    inv_freq = ROPE_BASE ** (-jnp.arange(half, dtype=f32) / half)
    ang = pos.astype(f32)[:, None] * inv_freq[None, :]
    cos, sin = jnp.cos(ang), jnp.sin(ang)
    return jnp.concatenate([cos, cos], axis=-1), jnp.concatenate([-sin, sin], axis=-1)


def _row_tile(n, want):
    t = min(n, want)
    assert n % t == 0, (n, t)
    return t


def kernel(x_prompt, x_sample, mem_prompt, state_ret, state_gdn, state_conv, cache_mem_k, cache_mem_v, norm_ffn1_pre, norm_ffn1_post, ffn1_w_in, ffn1_w_out, norm_mix_pre, norm_mix_post, w_in, gdn_conv_w, gdn_a_log, gdn_dt_bias, gdn_norm, norm_mem, w_mem_k, w_mem_v, w_branch_ret, w_branch_gdn, w_branch_mem, w_out, norm_ffn2_pre, norm_ffn2_post, ffn2_w_in, ffn2_w_out):
    bp, sp, _ = x_prompt.shape
    ns, ss, _ = x_sample.shape
    depth = w_in.shape[0]
    assert ss == 1 and sp % CHUNK == 0 and ns % _STATE_BLOCK == 0 and ns % _ATTN_BLOCK == 0
    tm_ffn = _row_tile(bp * sp, 512)
    tm_seq = _row_tile(sp, 512)

    cos_p, sin_p = _rope_tables(jnp.arange(sp))
    cos_s, sin_s = _rope_tables(PAST_LEN + jnp.arange(ss))
    row = lambda v: v.reshape(1, -1).astype(f32)
    pad_lanes = lambda v: jnp.pad(v.astype(f32), (0, LANES - v.shape[0])).reshape(1, LANES)
    pad_rows = lambda v: jnp.pad(v.astype(f32), (0, SUBLANES - v.shape[0])).reshape(SUBLANES, 1)

    hp = x_prompt
    hs = x_sample.reshape(ns, D_MODEL)
    outs = [[] for _ in range(4)]
    mem_kv = new_states = None
    ret_chunk = _row_tile(tm_seq, 256)
    for l in range(depth):
        wl = w_in[l]
        w_main = wl[:, :_MAIN_W].astype(bf16)
        w_ab = wl[:, _MAIN_W:_MAIN_W + 2 * N_HEADS]
        w_ab_col = jnp.pad(w_ab, ((0, 0), (0, LANES - 2 * N_HEADS))).astype(bf16)
        w_ab_row = w_ab.T.astype(bf16)
        w_mq = wl[:, _MAIN_W + 2 * N_HEADS:_MAIN_W + 2 * N_HEADS + D_MODEL].astype(bf16)
        w_gates = wl[:, _MAIN_W + 2 * N_HEADS + D_MODEL:].astype(bf16)
        f1_in, f1_out = ffn1_w_in[l].astype(bf16), ffn1_w_out[l].astype(bf16)
        f2_in, f2_out = ffn2_w_in[l].astype(bf16), ffn2_w_out[l].astype(bf16)
        wr, wd, wm, wo = (w[l].astype(bf16) for w in (w_branch_ret, w_branch_gdn, w_branch_mem, w_out))
        cw = gdn_conv_w[l].astype(f32)
        alog, dtb = pad_lanes(gdn_a_log[l]), pad_lanes(gdn_dt_bias[l])
        alog_t, dtb_t = pad_rows(gdn_a_log[l]), pad_rows(gdn_dt_bias[l])
        nw = row(gdn_norm[l])
        g_f1pre, g_f1post = row(norm_ffn1_pre[l]), row(norm_ffn1_post[l])
        g_f2pre, g_f2post = row(norm_ffn2_pre[l]), row(norm_ffn2_post[l])
        g_mpre, g_mpost = row(norm_mix_pre[l]), row(norm_mix_post[l])

        mk, mv, mk_b, mv_b = _memkv(l, depth, mem_prompt, row(norm_mem[l]), w_mem_k[l].astype(bf16),
                                    w_mem_v[l].astype(bf16), mem_kv)
        mem_kv = (mk, mv)
        h1 = _ffn(hp.reshape(bp * sp, D_MODEL), g_f1pre, f1_in, f1_out, g_f1post, tm_ffn).reshape(bp, sp, D_MODEL)
        rq, rk, rv, rg, gq, gk, gv, gz, gcol, grow, tail = _proj(
            h1, g_mpre, w_main, w_ab_col, w_ab_row, cw, alog, dtb, alog_t, dtb_t, cos_p, sin_p, tm_seq)
        o_ret, s_ret = _retention(rq, rk, rv, rg, tm_seq, ret_chunk)
        grow4 = grow.reshape(bp, SUBLANES, sp // CHUNK, CHUNK).transpose(0, 2, 1, 3)
        o_gdn, s_gdn = _gated_delta(gq, gk, gv, gz, gcol, grow4, nw, tm_seq)
        h2 = _merge(h1, g_mpre, w_mq, w_gates, mk_b, mv_b, o_ret, o_gdn, wr, wd, wm, wo, g_mpost, tm_seq)
        hp = _ffn(h2.reshape(bp * sp, D_MODEL), g_f2pre, f2_in, f2_out, g_f2post, tm_ffn).reshape(bp, sp, D_MODEL)
        outs[0].append(s_ret)
        outs[1].append(s_gdn)
        outs[2].append(tail[:, SUBLANES - (CONV_W - 1):])

        h1s = _ffn(hs, g_f1pre, f1_in, f1_out, g_f1post, ns)
        rq, rk, rv, rg, gq, gk, gv, gz, gcol, mq, nbuf = _sample_proj(
            h1s, g_mpre, w_main, w_ab_col, w_mq, cw, alog, dtb, cos_s, sin_s,
            state_conv[l].reshape(ns, (CONV_W - 1) * CONV_DIM))
        cols = jnp.stack([rq, rk, gq, gk]).reshape(4, ns // _STATE_BLOCK, _STATE_BLOCK, N_HEADS, DK)
        cols = cols.transpose(1, 0, 3, 4, 2).reshape(ns // _STATE_BLOCK, 4 * N_HEADS, DK, _STATE_BLOCK)
        n_ret, n_gdn, o_ret, o_gdn = _sample_state(l, cols, rv, gv, gcol, state_ret, state_gdn, new_states)
        new_states = (n_ret, n_gdn)
        o_mem = _sample_attn(l, mq, cache_mem_k, cache_mem_v)
        h2s = _sample_merge(h1s, g_mpre, w_gates, o_ret, rg, o_gdn, gz, nw, o_mem, wr, wd, wm, wo, g_mpost)
        hs = _ffn(h2s, g_f2pre, f2_in, f2_out, g_f2post, ns)
        outs[3].append(nbuf.reshape(ns, CONV_W - 1, CONV_DIM))

    ret_p, gdn_p, conv_p, conv_s = (jnp.stack(o) for o in outs)
    return (hp, hs.reshape(ns, ss, D_MODEL), ret_p, gdn_p, conv_p) + mem_kv + new_states + (conv_s,)
```

```python
import functools
import math

import jax
import jax.numpy as jnp
from jax import lax
from jax.experimental import pallas as pl
from jax.experimental.pallas import tpu as pltpu

f32, bf16 = jnp.float32, jnp.bfloat16

D_MODEL = 1024
N_HEADS = 4
DK = 128
DV = 256
QK_W = N_HEADS * DK
V_W = N_HEADS * DV
CONV_W = 4
CONV_DIM = 2 * QK_W + V_W
HD_X = 256
N_MEM = 256
D_FF = 2816
EPS = 1e-6
ROPE_BASE = 10000.0
PAST_LEN = 16384
CHUNK = 64
LANES = 128
SUBLANES = 8
VMEM_LIMIT = 56 * 1024 * 1024
LOG_GAMMA = tuple(math.log1p(-2.0 ** (-5.0 - h)) for h in range(N_HEADS))
HIGHEST = lax.Precision.HIGHEST
NT = (((1,), (1,)), ((), ()))
TN = (((0,), (0,)), ((), ()))


def _params(*sem):
    return pltpu.CompilerParams(dimension_semantics=sem, vmem_limit_bytes=VMEM_LIMIT)


def _resident(shape):
    return pl.BlockSpec(shape, lambda *_: (0,) * len(shape), pipeline_mode=pl.Buffered(1))


def _rms(x, gain=None):
    y = x * lax.rsqrt(jnp.mean(x * x, axis=-1, keepdims=True) + EPS)
    return y if gain is None else y * gain


def _silu(x):
    return x * jax.nn.sigmoid(x)


def _softplus(x):
    return jnp.maximum(x, 0.0) + jnp.log1p(jnp.exp(-jnp.abs(x)))


def _dot(a, b):
    return jnp.dot(a, b, preferred_element_type=f32)


def _iota(shape, dim):
    return lax.broadcasted_iota(jnp.int32, shape, dim)


def _ffn_body(x_ref, gpre_ref, wgu_ref, wo_ref, gpost_ref, o_ref):
    x = x_ref[...]
    xn = _rms(x, gpre_ref[...]).astype(bf16)
    h = _dot(xn, wgu_ref[...])
    act = (_silu(h[:, :D_FF]) * h[:, D_FF:]).astype(bf16)
    y = _dot(act, wo_ref[...])
    o_ref[...] = x + 0.5 * _rms(y, gpost_ref[...])


def _ffn(x2d, gpre, wgu, wo, gpost, tm):
    t = x2d.shape[0]
    row = pl.BlockSpec((tm, D_MODEL), lambda i: (i, 0))
    return pl.pallas_call(
        _ffn_body, grid=(t // tm,),
        in_specs=[row, _resident((1, D_MODEL)), _resident((D_MODEL, 2 * D_FF)),
                  _resident((D_FF, D_MODEL)), _resident((1, D_MODEL))],
        out_specs=row, out_shape=jax.ShapeDtypeStruct((t, D_MODEL), f32),
        compiler_params=_params("parallel"), name="ffn",
    )(x2d, gpre, wgu, wo, gpost)


def _stacked_out(layer, depth, shape, dtype, index_map):
    spec = pl.BlockSpec((None,) + shape, lambda *i: (layer,) + index_map(*i))
    return spec, lambda full: jax.ShapeDtypeStruct((depth,) + full, dtype)


def _alias_prev(n_inputs, prev):
    prev = [] if prev is None else list(prev)
    return prev, [pl.BlockSpec(memory_space=pl.ANY)] * len(prev), {n_inputs + i: i for i in range(len(prev))}


def _memkv_body(m_ref, g_ref, wk_ref, wv_ref, *refs):
    k_ref, v_ref, kb_ref, vb_ref = refs[-4:]
    mn = _rms(m_ref[...], g_ref[...]).astype(bf16)
    k = _dot(mn, wk_ref[...])
    v = _dot(mn, wv_ref[...])
    for h in range(N_HEADS):
        k_ref[:, h, :] = k[:, h * HD_X:(h + 1) * HD_X]
        v_ref[:, h, :] = v[:, h * HD_X:(h + 1) * HD_X]
    kb_ref[...] = k.astype(bf16)
    vb_ref[...] = v.astype(bf16)


def _memkv(layer, depth, mem, g, wk, wv, prev):
    b = mem.shape[0]
    blk = pl.BlockSpec((None, N_MEM, D_MODEL), lambda i: (i, 0, 0))
    out5, sds5 = _stacked_out(layer, depth, (None, N_MEM, N_HEADS, HD_X), f32, lambda i: (i, 0, 0, 0))
    sds5 = sds5((b, N_MEM, N_HEADS, HD_X))
    sdsb = jax.ShapeDtypeStruct((b, N_MEM, D_MODEL), bf16)
    prev, prev_specs, aliases = _alias_prev(4, prev)
    return pl.pallas_call(
        _memkv_body, grid=(b,),
        in_specs=[blk, _resident((1, D_MODEL)), _resident((D_MODEL, D_MODEL)), _resident((D_MODEL, D_MODEL))]
        + prev_specs,
        out_specs=[out5, out5, blk, blk], out_shape=[sds5, sds5, sdsb, sdsb],
        input_output_aliases=aliases, compiler_params=_params("parallel"), name="memkv",
    )(mem, g, wk, wv, *prev)


_RQ, _RK, _RV, _RG, _GQKV, _GZ, _MAIN_W = 0, 512, 1024, 2048, 3072, 5120, 6144


def _rotary(x, cos, sin_signed):
    return x * cos + pltpu.roll(x, DK // 2, axis=1) * sin_signed


def _decay_beta(ab, a_log, dt_bias, is_decay):
    g = -jnp.exp(a_log) * _softplus(ab + dt_bias)
    return jnp.where(is_decay, g, jax.nn.sigmoid(ab))


def _gdn_qkv(conv):
    a = _silu(conv)
    qs, ks = [], []
    for h in range(N_HEADS):
        q = a[:, h * DK:(h + 1) * DK]
        k = a[:, QK_W + h * DK:QK_W + (h + 1) * DK]
        qs.append(q * (lax.rsqrt(jnp.sum(q * q, axis=-1, keepdims=True) + EPS) * DK ** -0.5))
        ks.append(k * lax.rsqrt(jnp.sum(k * k, axis=-1, keepdims=True) + EPS))
    return qs, ks, a[:, 2 * QK_W:]


def _proj_body(h_ref, gpre_ref, wm_ref, wab_ref, wabt_ref, cw_ref, alog_ref, dtb_ref, alogt_ref, dtbt_ref,
               cos_ref, sin_ref,
               rq_ref, rk_ref, rv_ref, rg_ref, gq_ref, gk_ref, gv_ref, gz_ref, gcol_ref, grow_ref, tail_ref,
               carry_ref):
    j = pl.program_id(1)
    tm = h_ref.shape[0]

    @pl.when(j == 0)
    def _():
        carry_ref[...] = jnp.zeros_like(carry_ref)

    xn = _rms(h_ref[...], gpre_ref[...]).astype(bf16)
    p = _dot(xn, wm_ref[...])
    cos, sin = cos_ref[...], sin_ref[...]
    for h in range(N_HEADS):
        sl = slice(h * DK, (h + 1) * DK)
        rq_ref[:, sl] = _rotary(p[:, _RQ + h * DK:_RQ + (h + 1) * DK], cos, sin).astype(bf16)
        rk_ref[:, sl] = (_rotary(p[:, _RK + h * DK:_RK + (h + 1) * DK], cos, sin) * DK ** -0.5).astype(bf16)
    rv_ref[...] = p[:, _RV:_RG].astype(bf16)
    rg_ref[...] = _silu(p[:, _RG:_GQKV]).astype(bf16)
    gz_ref[...] = _silu(p[:, _GZ:_MAIN_W]).astype(bf16)

    x = p[:, _GQKV:_GZ]
    carry = carry_ref[...]
    cw = cw_ref[...]
    top = x[:SUBLANES]
    conv = x * cw[CONV_W - 1:CONV_W]
    conv_top = top * cw[CONV_W - 1:CONV_W]
    first = _iota((SUBLANES, CONV_DIM), 0)
    for s in range(1, CONV_W):
        w = cw[CONV_W - 1 - s:CONV_W - s]
        xs = pltpu.roll(x, s, axis=0)
        conv = conv + xs * w
        ts = jnp.where(first < s, pltpu.roll(carry, s, axis=0), xs[:SUBLANES])
        conv_top = conv_top + ts * w
    carry_ref[...] = x[tm - SUBLANES:]
    tail_ref[...] = x[tm - SUBLANES:]

    for rows, c in ((slice(None), conv), (slice(0, SUBLANES), conv_top)):
        qs, ks, v = _gdn_qkv(c)
        for h in range(N_HEADS):
            gq_ref[rows, h * DK:(h + 1) * DK] = qs[h].astype(bf16)
            gk_ref[rows, h * DK:(h + 1) * DK] = ks[h].astype(bf16)
        gv_ref[rows, :] = v.astype(bf16)

    ab = _dot(xn, wab_ref[...])
    gcol_ref[...] = _decay_beta(ab, alog_ref[...], dtb_ref[...], _iota(ab.shape, 1) < N_HEADS)
    abt = lax.dot_general(wabt_ref[...], xn, NT, preferred_element_type=f32)
    grow_ref[...] = _decay_beta(abt, alogt_ref[...], dtbt_ref[...], _iota(abt.shape, 0) < N_HEADS)


def _proj(h3d, gpre, wm, wab, wabt, cw, alog, dtb, alogt, dtbt, cos, sin, tm):
    b, l, _ = h3d.shape
    tok = lambda w: pl.BlockSpec((None, tm, w), lambda i, j: (i, j, 0))
    tab = pl.BlockSpec((tm, DK), lambda i, j: (j, 0))
    sds = lambda w, dt=bf16: jax.ShapeDtypeStruct((b, l, w), dt)
    return pl.pallas_call(
        _proj_body, grid=(b, l // tm),
        in_specs=[tok(D_MODEL), _resident((1, D_MODEL)), _resident((D_MODEL, _MAIN_W)), _resident((D_MODEL, LANES)),
                  _resident((SUBLANES, D_MODEL)), _resident((CONV_W, CONV_DIM)), _resident((1, LANES)),
                  _resident((1, LANES)), _resident((SUBLANES, 1)), _resident((SUBLANES, 1)), tab, tab],
        out_specs=[tok(QK_W), tok(QK_W), tok(V_W), tok(V_W), tok(QK_W), tok(QK_W), tok(V_W), tok(V_W),
                   tok(LANES), pl.BlockSpec((None, SUBLANES, tm), lambda i, j: (i, 0, j)),
                   pl.BlockSpec((None, SUBLANES, CONV_DIM), lambda i, j: (i, 0, 0))],
        out_shape=[sds(QK_W), sds(QK_W), sds(V_W), sds(V_W), sds(QK_W), sds(QK_W), sds(V_W), sds(V_W),
                   sds(LANES, f32), jax.ShapeDtypeStruct((b, SUBLANES, l), f32),
                   jax.ShapeDtypeStruct((b, SUBLANES, CONV_DIM), f32)],
        scratch_shapes=[pltpu.VMEM((SUBLANES, CONV_DIM), f32)],
        compiler_params=_params("parallel", "arbitrary"), name="proj",
    )(h3d, gpre, wm, wab, wabt, cw, alog, dtb, alogt, dtbt, cos, sin)


def _ret_body(q_ref, k_ref, v_ref, sg_ref, o_ref, sout_ref, s_ref, *, chunk):
    j = pl.program_id(1)
    c = chunk

    @pl.when(j == 0)
    def _():
        s_ref[...] = jnp.zeros_like(s_ref)

    row, col = _iota((c, c), 0), _iota((c, c), 1)
    dist = (row - col).astype(f32)
    pos = _iota((c, 1), 0).astype(f32)
    for h in range(N_HEADS):
        lg = LOG_GAMMA[h]
        decay = jnp.exp(jnp.where(row >= col, dist * lg, -jnp.inf))
        q_scale = jnp.exp((pos + 1.0) * lg)
        k_scale = jnp.exp((c - 1.0 - pos) * lg)
        for t in range(q_ref.shape[0] // c):
            rows = slice(t * c, (t + 1) * c)
            qh = q_ref[rows, h * DK:(h + 1) * DK]
            kh = k_ref[rows, h * DK:(h + 1) * DK]
            vh = v_ref[rows, h * DV:(h + 1) * DV]
            attn = lax.dot_general(qh, kh, NT, preferred_element_type=f32) * decay
            q_dec = (qh.astype(f32) * q_scale).astype(bf16)
            k_dec = (kh.astype(f32) * k_scale).astype(bf16)
            s = s_ref[h]
            o = _dot(q_dec, s.astype(bf16)) + _dot(attn.astype(bf16), vh)
            s_ref[h] = s * math.exp(c * lg) + lax.dot_general(k_dec, vh, TN, preferred_element_type=f32)
            o_ref[rows, h * DV:(h + 1) * DV] = (sg_ref[rows, h * DV:(h + 1) * DV].astype(f32) * _rms(o)).astype(bf16)

    @pl.when(j == pl.num_programs(1) - 1)
    def _():
        sout_ref[...] = s_ref[...]


def _retention(rq, rk, rv, sg, tile, chunk):
    b, l, _ = rq.shape
    tok = lambda w: pl.BlockSpec((None, tile, w), lambda i, j: (i, j, 0))
    st = pl.BlockSpec((None, N_HEADS, DK, DV), lambda i, j: (i, 0, 0, 0))
    return pl.pallas_call(
        functools.partial(_ret_body, chunk=chunk), grid=(b, l // tile),
        in_specs=[tok(QK_W), tok(QK_W), tok(V_W), tok(V_W)],
        out_specs=[tok(V_W), st],
        out_shape=[jax.ShapeDtypeStruct((b, l, V_W), bf16), jax.ShapeDtypeStruct((b, N_HEADS, DK, DV), f32)],
        scratch_shapes=[pltpu.VMEM((N_HEADS, DK, DV), f32)],
        compiler_params=_params("parallel", "arbitrary"), name="retention",
    )(rq, rk, rv, sg)


def _gdn_body(q_ref, k_ref, v_ref, sz_ref, gcol_ref, grow_ref, nw_ref, o_ref, sout_ref, s_ref):
    j = pl.program_id(1)
    c = CHUNK
    hc = N_HEADS * c

    @pl.when(j == 0)
    def _():
        s_ref[...] = jnp.zeros_like(s_ref)

    row, lane = _iota((c, hc), 0), _iota((c, hc), 1)
    lane_head, col = lane // c, lane % c
    lower, strict = row >= col, row > col
    head_mask = [(lane_head == h).astype(bf16) for h in range(N_HEADS)]
    block_diag = lambda x: jnp.concatenate([x * m for m in head_mask], axis=0)
    spread = lambda cols: functools.reduce(
        lambda acc, h: jnp.where(lane_head == h, cols[h], acc), range(N_HEADS - 1), cols[N_HEADS - 1])
    cum_rows = (_iota((c, c), 0) >= _iota((c, c), 1)).astype(f32)
    cum_cols = (_iota((c, hc), 0) <= (_iota((c, hc), 1) % c)).astype(f32)
    own_row = _iota((SUBLANES, hc), 0) == _iota((SUBLANES, hc), 1) // c
    zeros_k = jnp.zeros((c, DK), bf16)
    nw = nw_ref[...]

    chunks = range(q_ref.shape[0] // c)
    tok = lambda t: slice(t * c, (t + 1) * c)

    def chunk_setup(t):
        rows = tok(t)
        k_heads = [k_ref[rows, h * DK:(h + 1) * DK] for h in range(N_HEADS)]
        k_bd = jnp.concatenate([jnp.concatenate([k_heads[h] if g == h else zeros_k for g in range(N_HEADS)], axis=1)
                                for h in range(N_HEADS)], axis=0)
        qk = lax.dot_general(jnp.concatenate([q_ref[rows, :], k_ref[rows, :]], axis=0), k_bd, NT,
                             preferred_element_type=f32)
        gcol = gcol_ref[rows, :]
        cum = jnp.dot(cum_rows, gcol, preferred_element_type=f32, precision=HIGHEST)
        b_cols = [cum[:, h:h + 1] for h in range(N_HEADS)]
        beta_cols = [gcol[:, N_HEADS + h:N_HEADS + h + 1] for h in range(N_HEADS)]
        b_row = jnp.sum(jnp.where(own_row, jnp.dot(grow_ref[t], cum_cols, preferred_element_type=f32,
                                                   precision=HIGHEST), 0.0), axis=0, keepdims=True)
        decay = jnp.exp(jnp.where(lower, spread(b_cols) - b_row, -jnp.inf))
        x = jnp.where(strict, qk[c:] * spread(beta_cols) * decay, 0.0) * -1.0
        return (qk[:c] * decay, x, jnp.concatenate(b_cols, axis=0), jnp.concatenate(beta_cols, axis=0))

    attns, xs, b_colv, betav = zip(*[chunk_setup(t) for t in chunks])
    ns = xs
    x_bds = [block_diag(x.astype(bf16)) for x in xs]
    for _ in range(int(math.log2(c)) - 1):
        xs = [_dot(x.astype(bf16), x_bd) for x, x_bd in zip(xs, x_bds)]
        x_bds = [block_diag(x.astype(bf16)) for x in xs]
        ns = [n + x + _dot(n.astype(bf16), x_bd) for n, x, x_bd in zip(ns, xs, x_bds)]

    def chunk_wy(t):
        rows = tok(t)
        b_col, beta = b_colv[t], betav[t]
        kst = jnp.concatenate([k_ref[rows, h * DK:(h + 1) * DK] for h in range(N_HEADS)], axis=0)
        qst = jnp.concatenate([q_ref[rows, h * DK:(h + 1) * DK] for h in range(N_HEADS)], axis=0)
        vst = jnp.concatenate([v_ref[rows, h * DV:(h + 1) * DV] for h in range(N_HEADS)], axis=0).astype(f32)
        n_bd = block_diag(ns[t].astype(bf16))
        kf = kst.astype(f32)
        e_b = jnp.exp(b_col)
        ru, rw = vst * beta, kf * (beta * e_b)
        u = ru + _dot(n_bd, ru.astype(bf16))
        w = (rw + _dot(n_bd, rw.astype(bf16))).astype(bf16)
        q_dec = (qst.astype(f32) * e_b).astype(bf16)
        return u, w, q_dec, kf, block_diag(attns[t].astype(bf16))

    wy = [chunk_wy(t) for t in chunks]
    for t in chunks:
        rows = tok(t)
        u, w, q_dec, kf, attn_bd = wy[t]
        b_col = b_colv[t]
        v_new, o_inter = [], []
        for h in range(N_HEADS):
            hs = slice(h * c, (h + 1) * c)
            r = _dot(jnp.concatenate([w[hs], q_dec[hs]], axis=0), s_ref[h].astype(bf16))
            v_new.append(u[hs] - r[:c])
            o_inter.append(r[c:])
        vb = jnp.concatenate(v_new, axis=0).astype(bf16)
        o = jnp.concatenate(o_inter, axis=0) + _dot(attn_bd, vb)
        for h in range(N_HEADS):
            hs = slice(h * c, (h + 1) * c)
            b_last = b_col[(h + 1) * c - 1:(h + 1) * c]
            k_dec = (kf[hs] * jnp.exp(b_last - b_col[hs])).astype(bf16)
            s_ref[h] = s_ref[h] * jnp.exp(b_last) + lax.dot_general(k_dec, vb[hs], TN, preferred_element_type=f32)
            o_ref[rows, h * DV:(h + 1) * DV] = (
                _rms(o[hs], nw) * sz_ref[rows, h * DV:(h + 1) * DV].astype(f32)).astype(bf16)

    @pl.when(j == pl.num_programs(1) - 1)
    def _():
        sout_ref[...] = s_ref[...]


def _gated_delta(gq, gk, gv, sz, gcol, grow4, nw, tile):
    b, l, _ = gq.shape
    tok = lambda w: pl.BlockSpec((None, tile, w), lambda i, j: (i, j, 0))
    st = pl.BlockSpec((None, N_HEADS, DK, DV), lambda i, j: (i, 0, 0, 0))
    return pl.pallas_call(
        _gdn_body, grid=(b, l // tile),
        in_specs=[tok(QK_W), tok(QK_W), tok(V_W), tok(V_W), tok(LANES),
                  pl.BlockSpec((None, tile // CHUNK, SUBLANES, CHUNK), lambda i, j: (i, j, 0, 0)), _resident((1, DV))],
        out_specs=[tok(V_W), st],
        out_shape=[jax.ShapeDtypeStruct((b, l, V_W), bf16), jax.ShapeDtypeStruct((b, N_HEADS, DK, DV), f32)],
        scratch_shapes=[pltpu.VMEM((N_HEADS, DK, DV), f32)],
        compiler_params=_params("parallel", "arbitrary"), name="gated_delta",
    )(gq, gk, gv, sz, gcol, grow4, nw)


def _softmax(s):
    e = jnp.exp(s - jnp.max(s, axis=-1, keepdims=True))
    return e / jnp.sum(e, axis=-1, keepdims=True)


def _merge_tail(x, xn, o_ret, o_gdn, mem_proj, wg_ref, wr_ref, wd_ref, wo_ref, gpost):
    gates = jax.nn.sigmoid(_dot(xn, wg_ref[...]))
    merged = (gates[:, :D_MODEL] * _dot(o_ret, wr_ref[...])
              + gates[:, D_MODEL:2 * D_MODEL] * _dot(o_gdn, wd_ref[...])
              + gates[:, 2 * D_MODEL:] * mem_proj)
    y = _dot(merged.astype(bf16), wo_ref[...])
    return x + _rms(y, gpost)


def _merge_body(h_ref, gpre_ref, wq_ref, wg_ref, mk_ref, mv_ref, oret_ref, ogdn_ref,
                wr_ref, wd_ref, wm_ref, wo_ref, gpost_ref, o_ref):
    x = h_ref[...]
    xn = _rms(x, gpre_ref[...]).astype(bf16)
    mq = _dot(xn, wq_ref[...]).astype(bf16)
    mem_proj = None
    for h in range(N_HEADS):
        sl = slice(h * HD_X, (h + 1) * HD_X)
        s = lax.dot_general(mq[:, sl], mk_ref[:, sl], NT, preferred_element_type=f32) * HD_X ** -0.5
        o = _dot(_softmax(s).astype(bf16), mv_ref[:, sl])
        part = _dot(o.astype(bf16), wm_ref[sl, :])
        mem_proj = part if mem_proj is None else mem_proj + part
    o_ref[...] = _merge_tail(x, xn, oret_ref[...], ogdn_ref[...], mem_proj, wg_ref, wr_ref, wd_ref, wo_ref,
                             gpost_ref[...])


def _merge(h3d, gpre, wq, wg, mk, mv, o_ret, o_gdn, wr, wd, wm, wo, gpost, tm):
    b, l, _ = h3d.shape
    tok = pl.BlockSpec((None, tm, D_MODEL), lambda i, j: (i, j, 0))
    mem = pl.BlockSpec((None, N_MEM, D_MODEL), lambda i, j: (i, 0, 0))
    sq = _resident((D_MODEL, D_MODEL))
    return pl.pallas_call(
        _merge_body, grid=(b, l // tm),
        in_specs=[tok, _resident((1, D_MODEL)), sq, _resident((D_MODEL, 3 * D_MODEL)), mem, mem, tok, tok,
                  sq, sq, sq, sq, _resident((1, D_MODEL))],
        out_specs=tok, out_shape=jax.ShapeDtypeStruct((b, l, D_MODEL), f32),
        compiler_params=_params("parallel", "parallel"), name="merge",
    )(h3d, gpre, wq, wg, mk, mv, o_ret, o_gdn, wr, wd, wm, wo, gpost)


def _sproj_body(h_ref, gpre_ref, wm_ref, wab_ref, wq_ref, cw_ref, alog_ref, dtb_ref, cos_ref, sin_ref, buf_ref,
                rq_ref, rk_ref, rv_ref, rg_ref, gq_ref, gk_ref, gv_ref, gz_ref, gcol_ref, mq_ref, nbuf_ref):
    xn = _rms(h_ref[...], gpre_ref[...]).astype(bf16)
    p = _dot(xn, wm_ref[...])
    cos, sin = cos_ref[...], sin_ref[...]
    for h in range(N_HEADS):
        sl = slice(h * DK, (h + 1) * DK)
        rq_ref[:, sl] = _rotary(p[:, _RQ + h * DK:_RQ + (h + 1) * DK], cos, sin)
        rk_ref[:, sl] = _rotary(p[:, _RK + h * DK:_RK + (h + 1) * DK], cos, sin) * DK ** -0.5
    rv_ref[...] = p[:, _RV:_RG]
    rg_ref[...] = _silu(p[:, _RG:_GQKV])
    gz_ref[...] = _silu(p[:, _GZ:_MAIN_W])
    x = p[:, _GQKV:_GZ]
    cw = cw_ref[...]
    conv = x * cw[CONV_W - 1:CONV_W]
    for s in range(CONV_W - 1):
        prev = buf_ref[:, s * CONV_DIM:(s + 1) * CONV_DIM]
        conv = conv + prev * cw[s:s + 1]
        if s > 0:
            nbuf_ref[:, (s - 1) * CONV_DIM:s * CONV_DIM] = prev
    nbuf_ref[:, (CONV_W - 2) * CONV_DIM:] = x
    qs, ks, v = _gdn_qkv(conv)
    for h in range(N_HEADS):
        gq_ref[:, h * DK:(h + 1) * DK] = qs[h]
        gk_ref[:, h * DK:(h + 1) * DK] = ks[h]
    gv_ref[...] = v
    ab = _dot(xn, wab_ref[...])
    gcol_ref[...] = _decay_beta(ab, alog_ref[...], dtb_ref[...], _iota(ab.shape, 1) < N_HEADS)
    mq_ref[...] = _dot(xn, wq_ref[...])


def _sample_proj(h2d, gpre, wm, wab, wq, cw, alog, dtb, cos, sin, buf2d):
    n = h2d.shape[0]
    sds = lambda w: jax.ShapeDtypeStruct((n, w), f32)
    return pl.pallas_call(
        _sproj_body,
        out_shape=[sds(QK_W), sds(QK_W), sds(V_W), sds(V_W), sds(QK_W), sds(QK_W), sds(V_W), sds(V_W),
                   sds(LANES), sds(D_MODEL), sds((CONV_W - 1) * CONV_DIM)],
        compiler_params=pltpu.CompilerParams(vmem_limit_bytes=VMEM_LIMIT), name="sample_proj",
    )(h2d, gpre, wm, wab, wq, cw, alog, dtb, cos, sin, buf2d)


_STATE_BLOCK = 8


def _sstate_body(cols_ref, rv_ref, gv_ref, gcol_ref, sret_ref, sgdn_ref, *refs):
    nret_ref, ngdn_ref, oret_ref, ogdn_ref = refs[-4:]
    for j in range(_STATE_BLOCK):
        for h in range(N_HEADS):
            vs = slice(h * DV, (h + 1) * DV)
            q = cols_ref[0, h, :, j:j + 1]
            k = cols_ref[0, N_HEADS + h, :, j:j + 1]
            s_new = sret_ref[j, h] * math.exp(LOG_GAMMA[h]) + k * rv_ref[j:j + 1, vs]
            nret_ref[j, h] = s_new
            oret_ref[j:j + 1, vs] = jnp.sum(q * s_new, axis=0, keepdims=True)
            q = cols_ref[0, 2 * N_HEADS + h, :, j:j + 1]
            k = cols_ref[0, 3 * N_HEADS + h, :, j:j + 1]
            eg = jnp.exp(gcol_ref[j:j + 1, h:h + 1])
            beta = gcol_ref[j:j + 1, N_HEADS + h:N_HEADS + h + 1]
            s = sgdn_ref[j, h]
            ks = jnp.sum(k * s, axis=0, keepdims=True)
            v_new = beta * (gv_ref[j:j + 1, vs] - eg * ks)
            s_new = s * eg + k * v_new
            ngdn_ref[j, h] = s_new
            ogdn_ref[j:j + 1, vs] = jnp.sum(q * s_new, axis=0, keepdims=True)


def _sample_state(layer, cols, rv, gv, gcol, s_ret, s_gdn, prev):
    depth, n = s_ret.shape[:2]
    bt = _STATE_BLOCK
    rowb = lambda w: pl.BlockSpec((bt, w), lambda i: (i, 0))
    st, sds = _stacked_out(layer, depth, (bt, N_HEADS, DK, DV), f32, lambda i: (i, 0, 0, 0))
    sds = sds((n, N_HEADS, DK, DV))
    prev, prev_specs, aliases = _alias_prev(6, prev)
    return pl.pallas_call(
        _sstate_body, grid=(n // bt,),
        in_specs=[pl.BlockSpec((1, 4 * N_HEADS, DK, bt), lambda i: (i, 0, 0, 0)), rowb(V_W), rowb(V_W), rowb(LANES),
                  st, st] + prev_specs,
        out_specs=[st, st, rowb(V_W), rowb(V_W)],
        out_shape=[sds, sds, jax.ShapeDtypeStruct((n, V_W), f32), jax.ShapeDtypeStruct((n, V_W), f32)],
        input_output_aliases=aliases, compiler_params=_params("parallel"), name="sample_state",
    )(cols, rv, gv, gcol, s_ret, s_gdn, *prev)


_ATTN_BLOCK = 8


def _sattn_body(mq_ref, k_ref, v_ref, o_ref):
    for j in range(_ATTN_BLOCK):
        s = jnp.sum(k_ref[j] * mq_ref[j][None], axis=-1, keepdims=True) * HD_X ** -0.5
        e = jnp.exp(s - jnp.max(s, axis=0, keepdims=True))
        p = e / jnp.sum(e, axis=0, keepdims=True)
        o_ref[j] = jnp.sum(p * v_ref[j], axis=0)


def _sample_attn(layer, mq, mem_k, mem_v):
    n = mq.shape[0]
    bt = _ATTN_BLOCK
    rowb = pl.BlockSpec((bt, N_HEADS, HD_X), lambda i: (i, 0, 0))
    mem = pl.BlockSpec((None, bt, N_MEM, N_HEADS, HD_X), lambda i: (layer, i, 0, 0, 0))
    return pl.pallas_call(
        _sattn_body, grid=(n // bt,), in_specs=[rowb, mem, mem], out_specs=rowb,
        out_shape=jax.ShapeDtypeStruct((n, N_HEADS, HD_X), f32),
        compiler_params=_params("parallel"), name="sample_attn",
    )(mq.reshape(n, N_HEADS, HD_X), mem_k, mem_v).reshape(n, D_MODEL)


def _smerge_body(h_ref, gpre_ref, wg_ref, oret_ref, rg_ref, ogdn_ref, gz_ref, nw_ref, omem_ref,
                 wr_ref, wd_ref, wm_ref, wo_ref, gpost_ref, o_ref):
    x = h_ref[...]
    xn = _rms(x, gpre_ref[...]).astype(bf16)
    nw = nw_ref[...]
    rets, gdns = [], []
    for h in range(N_HEADS):
        vs = slice(h * DV, (h + 1) * DV)
        rets.append(rg_ref[:, vs] * _rms(oret_ref[:, vs]))
        gdns.append(_rms(ogdn_ref[:, vs], nw) * gz_ref[:, vs])
    o_ret = jnp.concatenate(rets, axis=-1).astype(bf16)
    o_gdn = jnp.concatenate(gdns, axis=-1).astype(bf16)
    mem_proj = _dot(omem_ref[...].astype(bf16), wm_ref[...])
    o_ref[...] = _merge_tail(x, xn, o_ret, o_gdn, mem_proj, wg_ref, wr_ref, wd_ref, wo_ref, gpost_ref[...])


def _sample_merge(h2d, gpre, wg, o_ret, rg, o_gdn, gz, nw, o_mem, wr, wd, wm, wo, gpost):
    return pl.pallas_call(
        _smerge_body, out_shape=jax.ShapeDtypeStruct(h2d.shape, f32),
        compiler_params=pltpu.CompilerParams(vmem_limit_bytes=VMEM_LIMIT), name="sample_merge",
    )(h2d, gpre, wg, o_ret, rg, o_gdn, gz, nw, o_mem, wr, wd, wm, wo, gpost)


def _rope_tables(pos):
    half = DK // 2
    inv_freq = ROPE_BASE ** (-jnp.arange(half, dtype=f32) / half)
    ang = pos.astype(f32)[:, None] * inv_freq[None, :]
    cos, sin = jnp.cos(ang), jnp.sin(ang)
    return jnp.concatenate([cos, cos], axis=-1), jnp.concatenate([-sin, sin], axis=-1)


def _row_tile(n, want):
    t = min(n, want)
    assert n % t == 0, (n, t)
    return t


def kernel(x_prompt, x_sample, mem_prompt, state_ret, state_gdn, state_conv, cache_mem_k, cache_mem_v, norm_ffn1_pre, norm_ffn1_post, ffn1_w_in, ffn1_w_out, norm_mix_pre, norm_mix_post, w_in, gdn_conv_w, gdn_a_log, gdn_dt_bias, gdn_norm, norm_mem, w_mem_k, w_mem_v, w_branch_ret, w_branch_gdn, w_branch_mem, w_out, norm_ffn2_pre, norm_ffn2_post, ffn2_w_in, ffn2_w_out):
    bp, sp, _ = x_prompt.shape
    ns, ss, _ = x_sample.shape
    depth = w_in.shape[0]
    assert ss == 1 and sp % CHUNK == 0 and ns % _STATE_BLOCK == 0 and ns % _ATTN_BLOCK == 0
    tm_ffn = _row_tile(bp * sp, 512)
    tm_seq = _row_tile(sp, 512)

    cos_p, sin_p = _rope_tables(jnp.arange(sp))
    cos_s, sin_s = _rope_tables(PAST_LEN + jnp.arange(ss))
    row = lambda v: v.reshape(1, -1).astype(f32)
    pad_lanes = lambda v: jnp.pad(v.astype(f32), (0, LANES - v.shape[0])).reshape(1, LANES)
    pad_rows = lambda v: jnp.pad(v.astype(f32), (0, SUBLANES - v.shape[0])).reshape(SUBLANES, 1)

    hp = x_prompt
    hs = x_sample.reshape(ns, D_MODEL)
    outs = [[] for _ in range(4)]
    mem_kv = new_states = None
    ret_chunk = _row_tile(tm_seq, 256)
    for l in range(depth):
        wl = w_in[l]
        w_main = wl[:, :_MAIN_W].astype(bf16)
        w_ab = wl[:, _MAIN_W:_MAIN_W + 2 * N_HEADS]
        w_ab_col = jnp.pad(w_ab, ((0, 0), (0, LANES - 2 * N_HEADS))).astype(bf16)
        w_ab_row = w_ab.T.astype(bf16)
        w_mq = wl[:, _MAIN_W + 2 * N_HEADS:_MAIN_W + 2 * N_HEADS + D_MODEL].astype(bf16)
        w_gates = wl[:, _MAIN_W + 2 * N_HEADS + D_MODEL:].astype(bf16)
        f1_in, f1_out = ffn1_w_in[l].astype(bf16), ffn1_w_out[l].astype(bf16)
        f2_in, f2_out = ffn2_w_in[l].astype(bf16), ffn2_w_out[l].astype(bf16)
        wr, wd, wm, wo = (w[l].astype(bf16) for w in (w_branch_ret, w_branch_gdn, w_branch_mem, w_out))
        cw = gdn_conv_w[l].astype(f32)
        alog, dtb = pad_lanes(gdn_a_log[l]), pad_lanes(gdn_dt_bias[l])
        alog_t, dtb_t = pad_rows(gdn_a_log[l]), pad_rows(gdn_dt_bias[l])
        nw = row(gdn_norm[l])
        g_f1pre, g_f1post = row(norm_ffn1_pre[l]), row(norm_ffn1_post[l])
        g_f2pre, g_f2post = row(norm_ffn2_pre[l]), row(norm_ffn2_post[l])
        g_mpre, g_mpost = row(norm_mix_pre[l]), row(norm_mix_post[l])

        mk, mv, mk_b, mv_b = _memkv(l, depth, mem_prompt, row(norm_mem[l]), w_mem_k[l].astype(bf16),
                                    w_mem_v[l].astype(bf16), mem_kv)
        mem_kv = (mk, mv)
        h1 = _ffn(hp.reshape(bp * sp, D_MODEL), g_f1pre, f1_in, f1_out, g_f1post, tm_ffn).reshape(bp, sp, D_MODEL)
        rq, rk, rv, rg, gq, gk, gv, gz, gcol, grow, tail = _proj(
            h1, g_mpre, w_main, w_ab_col, w_ab_row, cw, alog, dtb, alog_t, dtb_t, cos_p, sin_p, tm_seq)
        o_ret, s_ret = _retention(rq, rk, rv, rg, tm_seq, ret_chunk)
        grow4 = grow.reshape(bp, SUBLANES, sp // CHUNK, CHUNK).transpose(0, 2, 1, 3)
        o_gdn, s_gdn = _gated_delta(gq, gk, gv, gz, gcol, grow4, nw, tm_seq)
        h2 = _merge(h1, g_mpre, w_mq, w_gates, mk_b, mv_b, o_ret, o_gdn, wr, wd, wm, wo, g_mpost, tm_seq)
        hp = _ffn(h2.reshape(bp * sp, D_MODEL), g_f2pre, f2_in, f2_out, g_f2post, tm_ffn).reshape(bp, sp, D_MODEL)
        outs[0].append(s_ret)
        outs[1].append(s_gdn)
        outs[2].append(tail[:, SUBLANES - (CONV_W - 1):])

        h1s = _ffn(hs, g_f1pre, f1_in, f1_out, g_f1post, ns)
        rq, rk, rv, rg, gq, gk, gv, gz, gcol, mq, nbuf = _sample_proj(
            h1s, g_mpre, w_main, w_ab_col, w_mq, cw, alog, dtb, cos_s, sin_s,
            state_conv[l].reshape(ns, (CONV_W - 1) * CONV_DIM))
        cols = jnp.stack([rq, rk, gq, gk]).reshape(4, ns // _STATE_BLOCK, _STATE_BLOCK, N_HEADS, DK)
        cols = cols.transpose(1, 0, 3, 4, 2).reshape(ns // _STATE_BLOCK, 4 * N_HEADS, DK, _STATE_BLOCK)
        n_ret, n_gdn, o_ret, o_gdn = _sample_state(l, cols, rv, gv, gcol, state_ret, state_gdn, new_states)
        new_states = (n_ret, n_gdn)
        o_mem = _sample_attn(l, mq, cache_mem_k, cache_mem_v)
        h2s = _sample_merge(h1s, g_mpre, w_gates, o_ret, rg, o_gdn, gz, nw, o_mem, wr, wd, wm, wo, g_mpost)
        hs = _ffn(h2s, g_f2pre, f2_in, f2_out, g_f2post, ns)
        outs[3].append(nbuf.reshape(ns, CONV_W - 1, CONV_DIM))

    ret_p, gdn_p, conv_p, conv_s = (jnp.stack(o) for o in outs)
    return (hp, hs.reshape(ns, ss, D_MODEL), ret_p, gdn_p, conv_p) + mem_kv + new_states + (conv_s,)
```

```python
import functools
import math

import jax
import jax.numpy as jnp
from jax import lax
from jax.experimental import pallas as pl
from jax.experimental.pallas import tpu as pltpu

f32, bf16 = jnp.float32, jnp.bfloat16

D_MODEL = 1024
N_HEADS = 4
DK = 128
DV = 256
QK_W = N_HEADS * DK
V_W = N_HEADS * DV
CONV_W = 4
CONV_DIM = 2 * QK_W + V_W
HD_X = 256
N_MEM = 256
D_FF = 2816
EPS = 1e-6
ROPE_BASE = 10000.0
PAST_LEN = 16384
CHUNK = 64
LANES = 128
SUBLANES = 8
VMEM_LIMIT = 56 * 1024 * 1024
LOG_GAMMA = tuple(math.log1p(-2.0 ** (-5.0 - h)) for h in range(N_HEADS))
HIGHEST = lax.Precision.HIGHEST
NT = (((1,), (1,)), ((), ()))
TN = (((0,), (0,)), ((), ()))


def _params(*sem):
    return pltpu.CompilerParams(dimension_semantics=sem, vmem_limit_bytes=VMEM_LIMIT)


def _resident(shape, layer=None):
    if layer is None:
        return pl.BlockSpec(shape, lambda *_: (0,) * len(shape), pipeline_mode=pl.Buffered(1))
    return pl.BlockSpec((None,) + shape, lambda *_: (layer,) + (0,) * len(shape), pipeline_mode=pl.Buffered(1))


def _rms(x, gain=None):
    y = x * lax.rsqrt(jnp.mean(x * x, axis=-1, keepdims=True) + EPS)
    return y if gain is None else y * gain


def _silu(x):
    return x * jax.nn.sigmoid(x)


def _softplus(x):
    return jnp.maximum(x, 0.0) + jnp.log1p(jnp.exp(-jnp.abs(x)))


def _dot(a, b):
    return jnp.dot(a, b, preferred_element_type=f32)


def _iota(shape, dim):
    return lax.broadcasted_iota(jnp.int32, shape, dim)


def _ffn_body(x_ref, gpre_ref, wgu_ref, wo_ref, gpost_ref, o_ref):
    x = x_ref[...]
    xn = _rms(x, gpre_ref[...]).astype(bf16)
    h = _dot(xn, wgu_ref[...])
    act = (_silu(h[:, :D_FF]) * h[:, D_FF:]).astype(bf16)
    y = _dot(act, wo_ref[...])
    o_ref[...] = x + 0.5 * _rms(y, gpost_ref[...])


def _ffn(layer, x2d, gpre, wgu, wo, gpost, tm):
    t = x2d.shape[0]
    row = pl.BlockSpec((tm, D_MODEL), lambda i: (i, 0))
    return pl.pallas_call(
        _ffn_body, grid=(t // tm,),
        in_specs=[row, _resident((1, D_MODEL), layer), _resident((D_MODEL, 2 * D_FF), layer),
                  _resident((D_FF, D_MODEL), layer), _resident((1, D_MODEL), layer)],
        out_specs=row, out_shape=jax.ShapeDtypeStruct((t, D_MODEL), f32),
        compiler_params=_params("parallel"), name="ffn",
    )(x2d, gpre, wgu, wo, gpost)


def _stacked_out(layer, depth, shape, dtype, index_map):
    spec = pl.BlockSpec((None,) + shape, lambda *i: (layer,) + index_map(*i))
    return spec, lambda full: jax.ShapeDtypeStruct((depth,) + full, dtype)


def _alias_prev(n_inputs, prev):
    prev = [] if prev is None else list(prev)
    return prev, [pl.BlockSpec(memory_space=pl.ANY)] * len(prev), {n_inputs + i: i for i in range(len(prev))}


def _memkv_body(m_ref, g_ref, wk_ref, wv_ref, k_ref, v_ref, kb_ref, vb_ref):
    mn = _rms(m_ref[...], g_ref[...]).astype(bf16)
    k = _dot(mn, wk_ref[...])
    v = _dot(mn, wv_ref[...])
    for h in range(N_HEADS):
        k_ref[:, h, :] = k[:, h * HD_X:(h + 1) * HD_X]
        v_ref[:, h, :] = v[:, h * HD_X:(h + 1) * HD_X]
    kb_ref[...] = k.astype(bf16)
    vb_ref[...] = v.astype(bf16)


def _memkv(mem, g, wk, wv):
    b = mem.shape[0]
    depth = wk.shape[0]
    per_layer = lambda shape: pl.BlockSpec((None,) + shape, lambda l, i: (l,) + (0,) * len(shape))
    out5 = pl.BlockSpec((None, None, N_MEM, N_HEADS, HD_X), lambda l, i: (l, i, 0, 0, 0))
    outb = pl.BlockSpec((None, None, N_MEM, D_MODEL), lambda l, i: (l, i, 0, 0))
    sds5 = jax.ShapeDtypeStruct((depth, b, N_MEM, N_HEADS, HD_X), f32)
    sdsb = jax.ShapeDtypeStruct((depth, b, N_MEM, D_MODEL), bf16)
    return pl.pallas_call(
        _memkv_body, grid=(depth, b),
        in_specs=[pl.BlockSpec((None, N_MEM, D_MODEL), lambda l, i: (i, 0, 0)), per_layer((1, D_MODEL)),
                  per_layer((D_MODEL, D_MODEL)), per_layer((D_MODEL, D_MODEL))],
        out_specs=[out5, out5, outb, outb], out_shape=[sds5, sds5, sdsb, sdsb],
        compiler_params=_params("parallel", "parallel"), name="memkv",
    )(mem, g, wk, wv)


_RQ, _RK, _RV, _RG, _GQKV, _GZ, _MAIN_W = 0, 512, 1024, 2048, 3072, 5120, 6144


def _rotary(x, cos, sin_signed):
    return x * cos + pltpu.roll(x, DK // 2, axis=1) * sin_signed


def _decay_beta(ab, a_log, dt_bias, is_decay):
    g = -jnp.exp(a_log) * _softplus(ab + dt_bias)
    return jnp.where(is_decay, g, jax.nn.sigmoid(ab))


def _gdn_qkv(conv):
    a = _silu(conv)
    qs, ks = [], []
    for h in range(N_HEADS):
        q = a[:, h * DK:(h + 1) * DK]
        k = a[:, QK_W + h * DK:QK_W + (h + 1) * DK]
        qs.append(q * (lax.rsqrt(jnp.sum(q * q, axis=-1, keepdims=True) + EPS) * DK ** -0.5))
        ks.append(k * lax.rsqrt(jnp.sum(k * k, axis=-1, keepdims=True) + EPS))
    return qs, ks, a[:, 2 * QK_W:]


def _proj_body(h_ref, gpre_ref, wm_ref, wab_ref, wabt_ref, cw_ref, alog_ref, dtb_ref, alogt_ref, dtbt_ref,
               cos_ref, sin_ref,
               rq_ref, rk_ref, rv_ref, rg_ref, gq_ref, gk_ref, gv_ref, gz_ref, gcol_ref, grow_ref, tail_ref,
               carry_ref):
    j = pl.program_id(1)
    tm = h_ref.shape[0]

    @pl.when(j == 0)
    def _():
        carry_ref[...] = jnp.zeros_like(carry_ref)

    xn = _rms(h_ref[...], gpre_ref[...]).astype(bf16)
    cos, sin = cos_ref[...], sin_ref[...]

    def ret_qk(p):
        for h in range(N_HEADS):
            sl = slice(h * DK, (h + 1) * DK)
            rq_ref[:, sl] = _rotary(p[:, h * DK:(h + 1) * DK], cos, sin).astype(bf16)
            rk_ref[:, sl] = (_rotary(p[:, QK_W + h * DK:QK_W + (h + 1) * DK], cos, sin) * DK ** -0.5).astype(bf16)

    def ret_v(p):
        rv_ref[...] = p.astype(bf16)

    def ret_gate(p):
        rg_ref[...] = _silu(p).astype(bf16)

    def gdn_z(p):
        gz_ref[...] = _silu(p).astype(bf16)

    def conv_silu(x, cols):
        carry = carry_ref[:, cols]
        cw = cw_ref[:, cols]
        conv = x * cw[CONV_W - 1:CONV_W]
        conv_top = x[:SUBLANES] * cw[CONV_W - 1:CONV_W]
        first = _iota(conv_top.shape, 0)
        for s in range(1, CONV_W):
            w = cw[CONV_W - 1 - s:CONV_W - s]
            xs = pltpu.roll(x, s, axis=0)
            conv = conv + xs * w
            conv_top = conv_top + jnp.where(first < s, pltpu.roll(carry, s, axis=0), xs[:SUBLANES]) * w
        carry_ref[:, cols] = x[tm - SUBLANES:]
        tail_ref[:, cols] = x[tm - SUBLANES:]
        return (slice(None), _silu(conv)), (slice(0, SUBLANES), _silu(conv_top))

    def gdn_qk(p):
        for rows, a in conv_silu(p, slice(0, 2 * QK_W)):
            for h in range(N_HEADS):
                q = a[:, h * DK:(h + 1) * DK]
                k = a[:, QK_W + h * DK:QK_W + (h + 1) * DK]
                q = q * (lax.rsqrt(jnp.sum(q * q, axis=-1, keepdims=True) + EPS) * DK ** -0.5)
                k = k * lax.rsqrt(jnp.sum(k * k, axis=-1, keepdims=True) + EPS)
                gq_ref[rows, h * DK:(h + 1) * DK] = q.astype(bf16)
                gk_ref[rows, h * DK:(h + 1) * DK] = k.astype(bf16)

    def gdn_v(p):
        for rows, a in conv_silu(p, slice(2 * QK_W, CONV_DIM)):
            gv_ref[rows, :] = a.astype(bf16)

    groups = ((_RQ, _RV, ret_qk), (_RV, _RG, ret_v), (_RG, _GQKV, ret_gate), (_GQKV, _GQKV + 2 * QK_W, gdn_qk),
              (_GQKV + 2 * QK_W, _GZ, gdn_v), (_GZ, _MAIN_W, gdn_z))
    pending = None
    for lo, hi, epilogue in groups:
        p = _dot(xn, wm_ref[:, lo:hi])
        if pending is not None:
            pending()
        pending = functools.partial(epilogue, p)
    pending()

    ab = _dot(xn, wab_ref[...])
    gcol_ref[...] = _decay_beta(ab, alog_ref[...], dtb_ref[...], _iota(ab.shape, 1) < N_HEADS)
    abt = lax.dot_general(wabt_ref[...], xn, NT, preferred_element_type=f32)
    grow_ref[...] = _decay_beta(abt, alogt_ref[...], dtbt_ref[...], _iota(abt.shape, 0) < N_HEADS)


def _proj(layer, h3d, gpre, wm, wab, wabt, cw, alog, dtb, alogt, dtbt, cos, sin, tm):
    b, l, _ = h3d.shape
    tok = lambda w: pl.BlockSpec((None, tm, w), lambda i, j: (i, j, 0))
    tab = pl.BlockSpec((tm, DK), lambda i, j: (j, 0))
    sds = lambda w, dt=bf16: jax.ShapeDtypeStruct((b, l, w), dt)
    res = lambda *shape: _resident(shape, layer)
    return pl.pallas_call(
        _proj_body, grid=(b, l // tm),
        in_specs=[tok(D_MODEL), res(1, D_MODEL), res(D_MODEL, _MAIN_W), res(D_MODEL, LANES),
                  res(SUBLANES, D_MODEL), res(CONV_W, CONV_DIM), res(1, LANES),
                  res(1, LANES), res(SUBLANES, 1), res(SUBLANES, 1), tab, tab],
        out_specs=[tok(QK_W), tok(QK_W), tok(V_W), tok(V_W), tok(QK_W), tok(QK_W), tok(V_W), tok(V_W),
                   tok(LANES), pl.BlockSpec((None, SUBLANES, tm), lambda i, j: (i, 0, j)),
                   pl.BlockSpec((None, SUBLANES, CONV_DIM), lambda i, j: (i, 0, 0))],
        out_shape=[sds(QK_W), sds(QK_W), sds(V_W), sds(V_W), sds(QK_W), sds(QK_W), sds(V_W), sds(V_W),
                   sds(LANES, f32), jax.ShapeDtypeStruct((b, SUBLANES, l), f32),
                   jax.ShapeDtypeStruct((b, SUBLANES, CONV_DIM), f32)],
        scratch_shapes=[pltpu.VMEM((SUBLANES, CONV_DIM), f32)],
        compiler_params=_params("parallel", "arbitrary"), name="proj",
    )(h3d, gpre, wm, wab, wabt, cw, alog, dtb, alogt, dtbt, cos, sin)


def _ret_body(q_ref, k_ref, v_ref, sg_ref, o_ref, sout_ref, s_ref, *, chunk):
    j = pl.program_id(1)
    c = chunk

    @pl.when(j == 0)
    def _():
        s_ref[...] = jnp.zeros_like(s_ref)

    row, col = _iota((c, c), 0), _iota((c, c), 1)
    dist = (row - col).astype(f32)
    pos = _iota((c, 1), 0).astype(f32)
    for h in range(N_HEADS):
        lg = LOG_GAMMA[h]
        decay = jnp.exp(jnp.where(row >= col, dist * lg, -jnp.inf))
        q_scale = jnp.exp((pos + 1.0) * lg)
        k_scale = jnp.exp((c - 1.0 - pos) * lg)
        for t in range(q_ref.shape[0] // c):
            rows = slice(t * c, (t + 1) * c)
            qh = q_ref[rows, h * DK:(h + 1) * DK]
            kh = k_ref[rows, h * DK:(h + 1) * DK]
            vh = v_ref[rows, h * DV:(h + 1) * DV]
            attn = lax.dot_general(qh, kh, NT, preferred_element_type=f32) * decay
            q_dec = (qh.astype(f32) * q_scale).astype(bf16)
            k_dec = (kh.astype(f32) * k_scale).astype(bf16)
            s = s_ref[h]
            o = _dot(q_dec, s.astype(bf16)) + _dot(attn.astype(bf16), vh)
            s_ref[h] = s * math.exp(c * lg) + lax.dot_general(k_dec, vh, TN, preferred_element_type=f32)
            o_ref[rows, h * DV:(h + 1) * DV] = (sg_ref[rows, h * DV:(h + 1) * DV].astype(f32) * _rms(o)).astype(bf16)

    @pl.when(j == pl.num_programs(1) - 1)
    def _():
        sout_ref[...] = s_ref[...]


def _retention(rq, rk, rv, sg, tile, chunk):
    b, l, _ = rq.shape
    tok = lambda w: pl.BlockSpec((None, tile, w), lambda i, j: (i, j, 0))
    st = pl.BlockSpec((None, N_HEADS, DK, DV), lambda i, j: (i, 0, 0, 0))
    return pl.pallas_call(
        functools.partial(_ret_body, chunk=chunk), grid=(b, l // tile),
        in_specs=[tok(QK_W), tok(QK_W), tok(V_W), tok(V_W)],
        out_specs=[tok(V_W), st],
        out_shape=[jax.ShapeDtypeStruct((b, l, V_W), bf16), jax.ShapeDtypeStruct((b, N_HEADS, DK, DV), f32)],
        scratch_shapes=[pltpu.VMEM((N_HEADS, DK, DV), f32)],
        compiler_params=_params("parallel", "arbitrary"), name="retention",
    )(rq, rk, rv, sg)


def _gdn_body(q_ref, k_ref, v_ref, sz_ref, gcol_ref, grow_ref, nw_ref, o_ref, sout_ref, s_ref):
    j = pl.program_id(1)
    c = CHUNK
    hc = N_HEADS * c

    @pl.when(j == 0)
    def _():
        s_ref[...] = jnp.zeros_like(s_ref)

    row, lane = _iota((c, hc), 0), _iota((c, hc), 1)
    lane_head, col = lane // c, lane % c
    lower, strict = row >= col, row > col
    head_mask = [(lane_head == h).astype(bf16) for h in range(N_HEADS)]
    block_diag = lambda x: jnp.concatenate([x * m for m in head_mask], axis=0)
    spread = lambda cols: functools.reduce(
        lambda acc, h: jnp.where(lane_head == h, cols[h], acc), range(N_HEADS - 1), cols[N_HEADS - 1])
    cum_rows = (_iota((c, c), 0) >= _iota((c, c), 1)).astype(f32)
    cum_cols = (_iota((c, hc), 0) <= (_iota((c, hc), 1) % c)).astype(f32)
    own_row = _iota((SUBLANES, hc), 0) == _iota((SUBLANES, hc), 1) // c
    zeros_k = jnp.zeros((c, DK), bf16)
    nw = nw_ref[...]

    chunks = range(q_ref.shape[0] // c)
    tok = lambda t: slice(t * c, (t + 1) * c)

    def chunk_setup(t):
        rows = tok(t)
        k_heads = [k_ref[rows, h * DK:(h + 1) * DK] for h in range(N_HEADS)]
        k_bd = jnp.concatenate([jnp.concatenate([k_heads[h] if g == h else zeros_k for g in range(N_HEADS)], axis=1)
                                for h in range(N_HEADS)], axis=0)
        qk = lax.dot_general(jnp.concatenate([q_ref[rows, :], k_ref[rows, :]], axis=0), k_bd, NT,
                             preferred_element_type=f32)
        gcol = gcol_ref[rows, :]
        cum = jnp.dot(cum_rows, gcol, preferred_element_type=f32, precision=HIGHEST)
        b_cols = [cum[:, h:h + 1] for h in range(N_HEADS)]
        beta_cols = [gcol[:, N_HEADS + h:N_HEADS + h + 1] for h in range(N_HEADS)]
        b_row = jnp.sum(jnp.where(own_row, jnp.dot(grow_ref[t], cum_cols, preferred_element_type=f32,
                                                   precision=HIGHEST), 0.0), axis=0, keepdims=True)
        decay = jnp.exp(jnp.where(lower, spread(b_cols) - b_row, -jnp.inf))
        x = jnp.where(strict, qk[c:] * spread(beta_cols) * decay, 0.0) * -1.0
        return (qk[:c] * decay, x, jnp.concatenate(b_cols, axis=0), jnp.concatenate(beta_cols, axis=0))

    attns, xs, b_colv, betav = zip(*[chunk_setup(t) for t in chunks])
    ns = xs
    x_bds = [block_diag(x.astype(bf16)) for x in xs]
    for _ in range(int(math.log2(c)) - 1):
        xs = [_dot(x.astype(bf16), x_bd) for x, x_bd in zip(xs, x_bds)]
        x_bds = [block_diag(x.astype(bf16)) for x in xs]
        ns = [n + x + _dot(n.astype(bf16), x_bd) for n, x, x_bd in zip(ns, xs, x_bds)]

    def chunk_wy(t):
        rows = tok(t)
        b_col, beta = b_colv[t], betav[t]
        kst = jnp.concatenate([k_ref[rows, h * DK:(h + 1) * DK] for h in range(N_HEADS)], axis=0)
        qst = jnp.concatenate([q_ref[rows, h * DK:(h + 1) * DK] for h in range(N_HEADS)], axis=0)
        vst = jnp.concatenate([v_ref[rows, h * DV:(h + 1) * DV] for h in range(N_HEADS)], axis=0).astype(f32)
        n_bd = block_diag(ns[t].astype(bf16))
        kf = kst.astype(f32)
        e_b = jnp.exp(b_col)
        ru, rw = vst * beta, kf * (beta * e_b)
        u = ru + _dot(n_bd, ru.astype(bf16))
        w = (rw + _dot(n_bd, rw.astype(bf16))).astype(bf16)
        q_dec = (qst.astype(f32) * e_b).astype(bf16)
        return u, w, q_dec, kf, block_diag(attns[t].astype(bf16))

    wy = [chunk_wy(t) for t in chunks]
    for t in chunks:
        rows = tok(t)
        u, w, q_dec, kf, attn_bd = wy[t]
        b_col = b_colv[t]
        v_new, o_inter = [], []
        for h in range(N_HEADS):
            hs = slice(h * c, (h + 1) * c)
            r = _dot(jnp.concatenate([w[hs], q_dec[hs]], axis=0), s_ref[h].astype(bf16))
            v_new.append(u[hs] - r[:c])
            o_inter.append(r[c:])
        vb = jnp.concatenate(v_new, axis=0).astype(bf16)
        o = jnp.concatenate(o_inter, axis=0) + _dot(attn_bd, vb)
        for h in range(N_HEADS):
            hs = slice(h * c, (h + 1) * c)
            b_last = b_col[(h + 1) * c - 1:(h + 1) * c]
            k_dec = (kf[hs] * jnp.exp(b_last - b_col[hs])).astype(bf16)
            s_ref[h] = s_ref[h] * jnp.exp(b_last) + lax.dot_general(k_dec, vb[hs], TN, preferred_element_type=f32)
            o_ref[rows, h * DV:(h + 1) * DV] = (
                _rms(o[hs], nw) * sz_ref[rows, h * DV:(h + 1) * DV].astype(f32)).astype(bf16)

    @pl.when(j == pl.num_programs(1) - 1)
    def _():
        sout_ref[...] = s_ref[...]


def _gated_delta(layer, gq, gk, gv, sz, gcol, grow4, nw, tile):
    b, l, _ = gq.shape
    tok = lambda w: pl.BlockSpec((None, tile, w), lambda i, j: (i, j, 0))
    st = pl.BlockSpec((None, N_HEADS, DK, DV), lambda i, j: (i, 0, 0, 0))
    return pl.pallas_call(
        _gdn_body, grid=(b, l // tile),
        in_specs=[tok(QK_W), tok(QK_W), tok(V_W), tok(V_W), tok(LANES),
                  pl.BlockSpec((None, tile // CHUNK, SUBLANES, CHUNK), lambda i, j: (i, j, 0, 0)),
                  _resident((1, DV), layer)],
        out_specs=[tok(V_W), st],
        out_shape=[jax.ShapeDtypeStruct((b, l, V_W), bf16), jax.ShapeDtypeStruct((b, N_HEADS, DK, DV), f32)],
        scratch_shapes=[pltpu.VMEM((N_HEADS, DK, DV), f32)],
        compiler_params=_params("parallel", "arbitrary"), name="gated_delta",
    )(gq, gk, gv, sz, gcol, grow4, nw)


def _softmax(s):
    e = jnp.exp(s - jnp.max(s, axis=-1, keepdims=True))
    return e / jnp.sum(e, axis=-1, keepdims=True)


def _merge_tail(x, xn, o_ret, o_gdn, mem_proj, wg_ref, wr_ref, wd_ref, wo_ref, gpost):
    gates = jax.nn.sigmoid(_dot(xn, wg_ref[...]))
    merged = (gates[:, :D_MODEL] * _dot(o_ret, wr_ref[...])
              + gates[:, D_MODEL:2 * D_MODEL] * _dot(o_gdn, wd_ref[...])
              + gates[:, 2 * D_MODEL:] * mem_proj)
    y = _dot(merged.astype(bf16), wo_ref[...])
    return x + _rms(y, gpost)


def _merge_body(h_ref, gpre_ref, wq_ref, wg_ref, mk_ref, mv_ref, oret_ref, ogdn_ref,
                wr_ref, wd_ref, wm_ref, wo_ref, gpost_ref, o_ref):
    x = h_ref[...]
    xn = _rms(x, gpre_ref[...]).astype(bf16)
    mq = _dot(xn, wq_ref[...]).astype(bf16)
    mem_proj = None
    for h in range(N_HEADS):
        sl = slice(h * HD_X, (h + 1) * HD_X)
        s = lax.dot_general(mq[:, sl], mk_ref[:, sl], NT, preferred_element_type=f32) * HD_X ** -0.5
        o = _dot(_softmax(s).astype(bf16), mv_ref[:, sl])
        part = _dot(o.astype(bf16), wm_ref[sl, :])
        mem_proj = part if mem_proj is None else mem_proj + part
    o_ref[...] = _merge_tail(x, xn, oret_ref[...], ogdn_ref[...], mem_proj, wg_ref, wr_ref, wd_ref, wo_ref,
                             gpost_ref[...])


def _merge(layer, h3d, gpre, wq, wg, mk, mv, o_ret, o_gdn, wr, wd, wm, wo, gpost, tm):
    b, l, _ = h3d.shape
    tok = pl.BlockSpec((None, tm, D_MODEL), lambda i, j: (i, j, 0))
    mem = pl.BlockSpec((None, None, N_MEM, D_MODEL), lambda i, j: (layer, i, 0, 0))
    sq = _resident((D_MODEL, D_MODEL), layer)
    return pl.pallas_call(
        _merge_body, grid=(b, l // tm),
        in_specs=[tok, _resident((1, D_MODEL), layer), sq, _resident((D_MODEL, 3 * D_MODEL), layer), mem, mem,
                  tok, tok, sq, sq, sq, sq, _resident((1, D_MODEL), layer)],
        out_specs=tok, out_shape=jax.ShapeDtypeStruct((b, l, D_MODEL), f32),
        compiler_params=_params("parallel", "parallel"), name="merge",
    )(h3d, gpre, wq, wg, mk, mv, o_ret, o_gdn, wr, wd, wm, wo, gpost)


def _sproj_body(h_ref, gpre_ref, wm_ref, wab_ref, wq_ref, cw_ref, alog_ref, dtb_ref, cos_ref, sin_ref, buf_ref,
                rq_ref, rk_ref, rv_ref, rg_ref, gq_ref, gk_ref, gv_ref, gz_ref, gcol_ref, mq_ref, nbuf_ref):
    xn = _rms(h_ref[...], gpre_ref[...]).astype(bf16)
    p = _dot(xn, wm_ref[...])
    cos, sin = cos_ref[...], sin_ref[...]
    for h in range(N_HEADS):
        sl = slice(h * DK, (h + 1) * DK)
        rq_ref[:, sl] = _rotary(p[:, _RQ + h * DK:_RQ + (h + 1) * DK], cos, sin)
        rk_ref[:, sl] = _rotary(p[:, _RK + h * DK:_RK + (h + 1) * DK], cos, sin) * DK ** -0.5
    rv_ref[...] = p[:, _RV:_RG]
    rg_ref[...] = _silu(p[:, _RG:_GQKV])
    gz_ref[...] = _silu(p[:, _GZ:_MAIN_W])
    x = p[:, _GQKV:_GZ]
    cw = cw_ref[...]
    conv = x * cw[CONV_W - 1:CONV_W]
    for s in range(CONV_W - 1):
        prev = buf_ref[:, s * CONV_DIM:(s + 1) * CONV_DIM]
        conv = conv + prev * cw[s:s + 1]
        if s > 0:
            nbuf_ref[:, (s - 1) * CONV_DIM:s * CONV_DIM] = prev
    nbuf_ref[:, (CONV_W - 2) * CONV_DIM:] = x
    qs, ks, v = _gdn_qkv(conv)
    for h in range(N_HEADS):
        gq_ref[:, h * DK:(h + 1) * DK] = qs[h]
        gk_ref[:, h * DK:(h + 1) * DK] = ks[h]
    gv_ref[...] = v
    ab = _dot(xn, wab_ref[...])
    gcol_ref[...] = _decay_beta(ab, alog_ref[...], dtb_ref[...], _iota(ab.shape, 1) < N_HEADS)
    mq_ref[...] = _dot(xn, wq_ref[...])


def _sample_proj(layer, h2d, gpre, wm, wab, wq, cw, alog, dtb, cos, sin, buf2d):
    n = h2d.shape[0]
    sds = lambda w: jax.ShapeDtypeStruct((n, w), f32)
    res = lambda *shape: _resident(shape, layer)
    widths = (QK_W, QK_W, V_W, V_W, QK_W, QK_W, V_W, V_W, LANES, D_MODEL, (CONV_W - 1) * CONV_DIM)
    return pl.pallas_call(
        _sproj_body, grid=(1,),
        in_specs=[_resident((n, D_MODEL)), res(1, D_MODEL), res(D_MODEL, _MAIN_W), res(D_MODEL, LANES),
                  res(D_MODEL, D_MODEL), res(CONV_W, CONV_DIM), res(1, LANES), res(1, LANES),
                  _resident((1, DK)), _resident((1, DK)), _resident((n, (CONV_W - 1) * CONV_DIM))],
        out_specs=[pl.BlockSpec((n, w), lambda i: (0, 0)) for w in widths],
        out_shape=[sds(w) for w in widths],
        compiler_params=_params("arbitrary"), name="sample_proj",
    )(h2d, gpre, wm, wab, wq, cw, alog, dtb, cos, sin, buf2d)


_STATE_BLOCK = 8


def _sstate_body(cols_ref, rv_ref, gv_ref, gcol_ref, sret_ref, sgdn_ref, *refs):
    nret_ref, ngdn_ref, oret_ref, ogdn_ref = refs[-4:]
    for j in range(_STATE_BLOCK):
        for h in range(N_HEADS):
            vs = slice(h * DV, (h + 1) * DV)
            q = cols_ref[0, h, :, j:j + 1]
            k = cols_ref[0, N_HEADS + h, :, j:j + 1]
            s_new = sret_ref[j, h] * math.exp(LOG_GAMMA[h]) + k * rv_ref[j:j + 1, vs]
            nret_ref[j, h] = s_new
            oret_ref[j:j + 1, vs] = jnp.sum(q * s_new, axis=0, keepdims=True)
            q = cols_ref[0, 2 * N_HEADS + h, :, j:j + 1]
            k = cols_ref[0, 3 * N_HEADS + h, :, j:j + 1]
            eg = jnp.exp(gcol_ref[j:j + 1, h:h + 1])
            beta = gcol_ref[j:j + 1, N_HEADS + h:N_HEADS + h + 1]
            s = sgdn_ref[j, h]
            ks = jnp.sum(k * s, axis=0, keepdims=True)
            v_new = beta * (gv_ref[j:j + 1, vs] - eg * ks)
            s_new = s * eg + k * v_new
            ngdn_ref[j, h] = s_new
            ogdn_ref[j:j + 1, vs] = jnp.sum(q * s_new, axis=0, keepdims=True)


def _sample_state(layer, cols, rv, gv, gcol, s_ret, s_gdn, prev):
    depth, n = s_ret.shape[:2]
    bt = _STATE_BLOCK
    rowb = lambda w: pl.BlockSpec((bt, w), lambda i: (i, 0))
    st, sds = _stacked_out(layer, depth, (bt, N_HEADS, DK, DV), f32, lambda i: (i, 0, 0, 0))
    sds = sds((n, N_HEADS, DK, DV))
    prev, prev_specs, aliases = _alias_prev(6, prev)
    return pl.pallas_call(
        _sstate_body, grid=(n // bt,),
        in_specs=[pl.BlockSpec((1, 4 * N_HEADS, DK, bt), lambda i: (i, 0, 0, 0)), rowb(V_W), rowb(V_W), rowb(LANES),
                  st, st] + prev_specs,
        out_specs=[st, st, rowb(V_W), rowb(V_W)],
        out_shape=[sds, sds, jax.ShapeDtypeStruct((n, V_W), f32), jax.ShapeDtypeStruct((n, V_W), f32)],
        input_output_aliases=aliases, compiler_params=_params("parallel"), name="sample_state",
    )(cols, rv, gv, gcol, s_ret, s_gdn, *prev)


_ATTN_BLOCK = 8


def _sattn_body(mq_ref, k_ref, v_ref, o_ref):
    for j in range(_ATTN_BLOCK):
        s = jnp.sum(k_ref[j] * mq_ref[j][None], axis=-1, keepdims=True) * HD_X ** -0.5
        e = jnp.exp(s - jnp.max(s, axis=0, keepdims=True))
        p = e / jnp.sum(e, axis=0, keepdims=True)
        o_ref[j] = jnp.sum(p * v_ref[j], axis=0)


def _sample_attn(layer, mq, mem_k, mem_v):
    n = mq.shape[0]
    bt = _ATTN_BLOCK
    rowb = pl.BlockSpec((bt, N_HEADS, HD_X), lambda i: (i, 0, 0))
    mem = pl.BlockSpec((None, bt, N_MEM, N_HEADS, HD_X), lambda i: (layer, i, 0, 0, 0))
    return pl.pallas_call(
        _sattn_body, grid=(n // bt,), in_specs=[rowb, mem, mem], out_specs=rowb,
        out_shape=jax.ShapeDtypeStruct((n, N_HEADS, HD_X), f32),
        compiler_params=_params("parallel"), name="sample_attn",
    )(mq.reshape(n, N_HEADS, HD_X), mem_k, mem_v).reshape(n, D_MODEL)


def _smerge_body(h_ref, gpre_ref, wg_ref, oret_ref, rg_ref, ogdn_ref, gz_ref, nw_ref, omem_ref,
                 wr_ref, wd_ref, wm_ref, wo_ref, gpost_ref, o_ref):
    x = h_ref[...]
    xn = _rms(x, gpre_ref[...]).astype(bf16)
    nw = nw_ref[...]
    rets, gdns = [], []
    for h in range(N_HEADS):
        vs = slice(h * DV, (h + 1) * DV)
        rets.append(rg_ref[:, vs] * _rms(oret_ref[:, vs]))
        gdns.append(_rms(ogdn_ref[:, vs], nw) * gz_ref[:, vs])
    o_ret = jnp.concatenate(rets, axis=-1).astype(bf16)
    o_gdn = jnp.concatenate(gdns, axis=-1).astype(bf16)
    mem_proj = _dot(omem_ref[...].astype(bf16), wm_ref[...])
    o_ref[...] = _merge_tail(x, xn, o_ret, o_gdn, mem_proj, wg_ref, wr_ref, wd_ref, wo_ref, gpost_ref[...])


def _sample_merge(layer, h2d, gpre, wg, o_ret, rg, o_gdn, gz, nw, o_mem, wr, wd, wm, wo, gpost):
    n = h2d.shape[0]
    act = _resident((n, D_MODEL))
    sq = _resident((D_MODEL, D_MODEL), layer)
    return pl.pallas_call(
        _smerge_body, grid=(1,),
        in_specs=[act, _resident((1, D_MODEL), layer), _resident((D_MODEL, 3 * D_MODEL), layer), act, act, act, act,
                  _resident((1, DV), layer), act, sq, sq, sq, sq, _resident((1, D_MODEL), layer)],
        out_specs=pl.BlockSpec((n, D_MODEL), lambda i: (0, 0)), out_shape=jax.ShapeDtypeStruct(h2d.shape, f32),
        compiler_params=_params("arbitrary"), name="sample_merge",
    )(h2d, gpre, wg, o_ret, rg, o_gdn, gz, nw, o_mem, wr, wd, wm, wo, gpost)


def _rope_tables(pos):
    half = DK // 2
    inv_freq = ROPE_BASE ** (-jnp.arange(half, dtype=f32) / half)
    ang = pos.astype(f32)[:, None] * inv_freq[None, :]
    cos, sin = jnp.cos(ang), jnp.sin(ang)
    return jnp.concatenate([cos, cos], axis=-1), jnp.concatenate([-sin, sin], axis=-1)


def _row_tile(n, want):
    t = min(n, want)
    assert n % t == 0, (n, t)
    return t


def kernel(x_prompt, x_sample, mem_prompt, state_ret, state_gdn, state_conv, cache_mem_k, cache_mem_v, norm_ffn1_pre, norm_ffn1_post, ffn1_w_in, ffn1_w_out, norm_mix_pre, norm_mix_post, w_in, gdn_conv_w, gdn_a_log, gdn_dt_bias, gdn_norm, norm_mem, w_mem_k, w_mem_v, w_branch_ret, w_branch_gdn, w_branch_mem, w_out, norm_ffn2_pre, norm_ffn2_post, ffn2_w_in, ffn2_w_out):
    bp, sp, _ = x_prompt.shape
    ns, ss, _ = x_sample.shape
    depth = w_in.shape[0]
    assert ss == 1 and sp % CHUNK == 0 and ns % _STATE_BLOCK == 0 and ns % _ATTN_BLOCK == 0
    tm_ffn = _row_tile(bp * sp, 512)
    tm_seq = _row_tile(sp, 512)

    cos_p, sin_p = _rope_tables(jnp.arange(sp))
    cos_s, sin_s = _rope_tables(PAST_LEN + jnp.arange(ss))
    row = lambda v: v.reshape(depth, 1, -1).astype(f32)
    pad_lanes = lambda v: jnp.pad(v.astype(f32), ((0, 0), (0, LANES - v.shape[1]))).reshape(depth, 1, LANES)
    pad_rows = lambda v: jnp.pad(v.astype(f32), ((0, 0), (0, SUBLANES - v.shape[1]))).reshape(depth, SUBLANES, 1)
    ab_end, mq_end = _MAIN_W + 2 * N_HEADS, _MAIN_W + 2 * N_HEADS + D_MODEL
    w_main = w_in[:, :, :_MAIN_W].astype(bf16)
    w_ab = w_in[:, :, _MAIN_W:ab_end]
    w_ab_col = jnp.pad(w_ab, ((0, 0), (0, 0), (0, LANES - 2 * N_HEADS))).astype(bf16)
    w_ab_row = jnp.swapaxes(w_ab, 1, 2).astype(bf16)
    w_mq, w_gates = w_in[:, :, ab_end:mq_end].astype(bf16), w_in[:, :, mq_end:].astype(bf16)
    f1_in, f1_out, f2_in, f2_out = (w.astype(bf16) for w in (ffn1_w_in, ffn1_w_out, ffn2_w_in, ffn2_w_out))
    wr, wd, wm, wo = (w.astype(bf16) for w in (w_branch_ret, w_branch_gdn, w_branch_mem, w_out))
    cw = gdn_conv_w.astype(f32)
    alog, dtb = pad_lanes(gdn_a_log), pad_lanes(gdn_dt_bias)
    alog_t, dtb_t = pad_rows(gdn_a_log), pad_rows(gdn_dt_bias)
    nw = row(gdn_norm)
    g_f1pre, g_f1post, g_f2pre, g_f2post = (row(g) for g in (norm_ffn1_pre, norm_ffn1_post, norm_ffn2_pre,
                                                              norm_ffn2_post))
    g_mpre, g_mpost = row(norm_mix_pre), row(norm_mix_post)

    mk, mv, mk_b, mv_b = _memkv(mem_prompt, row(norm_mem), w_mem_k.astype(bf16), w_mem_v.astype(bf16))
    hp = x_prompt
    hs = x_sample.reshape(ns, D_MODEL)
    outs = [[] for _ in range(4)]
    new_states = None
    ret_chunk = _row_tile(tm_seq, 256)
    for l in range(depth):
        h1 = _ffn(l, hp.reshape(bp * sp, D_MODEL), g_f1pre, f1_in, f1_out, g_f1post, tm_ffn).reshape(bp, sp, D_MODEL)
        rq, rk, rv, rg, gq, gk, gv, gz, gcol, grow, tail = _proj(
            l, h1, g_mpre, w_main, w_ab_col, w_ab_row, cw, alog, dtb, alog_t, dtb_t, cos_p, sin_p, tm_seq)
        o_ret, s_ret = _retention(rq, rk, rv, rg, tm_seq, ret_chunk)
        grow4 = grow.reshape(bp, SUBLANES, sp // CHUNK, CHUNK).transpose(0, 2, 1, 3)
        o_gdn, s_gdn = _gated_delta(l, gq, gk, gv, gz, gcol, grow4, nw, tm_seq)
        h2 = _merge(l, h1, g_mpre, w_mq, w_gates, mk_b, mv_b, o_ret, o_gdn, wr, wd, wm, wo, g_mpost, tm_seq)
        hp = _ffn(l, h2.reshape(bp * sp, D_MODEL), g_f2pre, f2_in, f2_out, g_f2post, tm_ffn).reshape(bp, sp, D_MODEL)
        outs[0].append(s_ret)
        outs[1].append(s_gdn)
        outs[2].append(tail[:, SUBLANES - (CONV_W - 1):])

        h1s = _ffn(l, hs, g_f1pre, f1_in, f1_out, g_f1post, ns)
        rq, rk, rv, rg, gq, gk, gv, gz, gcol, mq, nbuf = _sample_proj(
            l, h1s, g_mpre, w_main, w_ab_col, w_mq, cw, alog, dtb, cos_s, sin_s,
            state_conv[l].reshape(ns, (CONV_W - 1) * CONV_DIM))
        cols = jnp.stack([rq, rk, gq, gk]).reshape(4, ns // _STATE_BLOCK, _STATE_BLOCK, N_HEADS, DK)
        cols = cols.transpose(1, 0, 3, 4, 2).reshape(ns // _STATE_BLOCK, 4 * N_HEADS, DK, _STATE_BLOCK)
        n_ret, n_gdn, o_ret, o_gdn = _sample_state(l, cols, rv, gv, gcol, state_ret, state_gdn, new_states)
        new_states = (n_ret, n_gdn)
        o_mem = _sample_attn(l, mq, cache_mem_k, cache_mem_v)
        h2s = _sample_merge(l, h1s, g_mpre, w_gates, o_ret, rg, o_gdn, gz, nw, o_mem, wr, wd, wm, wo, g_mpost)
        hs = _ffn(l, h2s, g_f2pre, f2_in, f2_out, g_f2post, ns)
        outs[3].append(nbuf.reshape(ns, CONV_W - 1, CONV_DIM))

    ret_p, gdn_p, conv_p, conv_s = (jnp.stack(o) for o in outs)
    return (hp, hs.reshape(ns, ss, D_MODEL), ret_p, gdn_p, conv_p, mk, mv) + new_states + (conv_s,)
```

```python
import functools
import math

import jax
import jax.numpy as jnp
from jax import lax
from jax.experimental import pallas as pl
from jax.experimental.pallas import tpu as pltpu

f32, bf16 = jnp.float32, jnp.bfloat16

D_MODEL = 1024
N_HEADS = 4
DK = 128
DV = 256
QK_W = N_HEADS * DK
V_W = N_HEADS * DV
CONV_W = 4
CONV_DIM = 2 * QK_W + V_W
HD_X = 256
N_MEM = 256
D_FF = 2816
EPS = 1e-6
ROPE_BASE = 10000.0
PAST_LEN = 16384
CHUNK = 64
LANES = 128
SUBLANES = 8
VMEM_LIMIT = 56 * 1024 * 1024
LOG_GAMMA = tuple(math.log1p(-2.0 ** (-5.0 - h)) for h in range(N_HEADS))
HIGHEST = lax.Precision.HIGHEST
NT = (((1,), (1,)), ((), ()))
TN = (((0,), (0,)), ((), ()))


def _params(*sem):
    return pltpu.CompilerParams(dimension_semantics=sem, vmem_limit_bytes=VMEM_LIMIT)


def _resident(shape, layer=None):
    if layer is None:
        return pl.BlockSpec(shape, lambda *_: (0,) * len(shape), pipeline_mode=pl.Buffered(1))
    return pl.BlockSpec((None,) + shape, lambda *_: (layer,) + (0,) * len(shape), pipeline_mode=pl.Buffered(1))


def _rms(x, gain=None):
    y = x * lax.rsqrt(jnp.mean(x * x, axis=-1, keepdims=True) + EPS)
    return y if gain is None else y * gain


def _silu(x):
    return x * jax.nn.sigmoid(x)


def _softplus(x):
    return jnp.maximum(x, 0.0) + jnp.log1p(jnp.exp(-jnp.abs(x)))


def _dot(a, b):
    return jnp.dot(a, b, preferred_element_type=f32)


def _iota(shape, dim):
    return lax.broadcasted_iota(jnp.int32, shape, dim)


def _ffn_body(x_ref, gpre_ref, wgu_ref, wo_ref, gpost_ref, o_ref):
    x = x_ref[...]
    xn = _rms(x, gpre_ref[...]).astype(bf16)
    h = _dot(xn, wgu_ref[...])
    act = (_silu(h[:, :D_FF]) * h[:, D_FF:]).astype(bf16)
    y = _dot(act, wo_ref[...])
    o_ref[...] = x + 0.5 * _rms(y, gpost_ref[...])


def _ffn_attn_body(x_ref, gpre_ref, wgu_ref, wo_ref, gpost_ref, mq_ref, k_ref, v_ref, o_ref, oa_ref):
    _ffn_body(x_ref, gpre_ref, wgu_ref, wo_ref, gpost_ref, o_ref)
    _sattn_body(mq_ref, k_ref, v_ref, oa_ref)


def _ffn(layer, x2d, gpre, wgu, wo, gpost, tm, attn=None):
    t = x2d.shape[0]
    steps = t // tm
    row = pl.BlockSpec((tm, D_MODEL), lambda i: (i, 0))
    in_specs = [row, _resident((1, D_MODEL), layer), _resident((D_MODEL, 2 * D_FF), layer),
                _resident((D_FF, D_MODEL), layer), _resident((1, D_MODEL), layer)]
    out_row = jax.ShapeDtypeStruct((t, D_MODEL), f32)
    if attn is None:
        return pl.pallas_call(
            _ffn_body, grid=(steps,), in_specs=in_specs, out_specs=row, out_shape=out_row,
            compiler_params=_params("parallel"), name="ffn",
        )(x2d, gpre, wgu, wo, gpost)
    mq, mem_k, mem_v, part, parts = attn
    n = mq.shape[0]
    per_step = n // (parts * steps)
    assert per_step * parts * steps == n, (n, parts, steps)
    qb = pl.BlockSpec((per_step, N_HEADS, HD_X), lambda i: (part * steps + i, 0, 0))
    mem = pl.BlockSpec((None, per_step, N_MEM, N_HEADS, HD_X), lambda i: (layer, part * steps + i, 0, 0, 0))
    return pl.pallas_call(
        _ffn_attn_body, grid=(steps,), in_specs=in_specs + [qb, mem, mem],
        out_specs=[row, pl.BlockSpec((per_step, N_HEADS, HD_X), lambda i: (i, 0, 0))],
        out_shape=[out_row, jax.ShapeDtypeStruct((n // parts, N_HEADS, HD_X), f32)],
        compiler_params=_params("parallel"), name="ffn_attn",
    )(x2d, gpre, wgu, wo, gpost, mq.reshape(n, N_HEADS, HD_X), mem_k, mem_v)


def _stacked_out(layer, depth, shape, dtype, index_map):
    spec = pl.BlockSpec((None,) + shape, lambda *i: (layer,) + index_map(*i))
    return spec, lambda full: jax.ShapeDtypeStruct((depth,) + full, dtype)


def _alias_prev(n_inputs, prev):
    prev = [] if prev is None else list(prev)
    return prev, [pl.BlockSpec(memory_space=pl.ANY)] * len(prev), {n_inputs + i: i for i in range(len(prev))}


def _memkv_body(m_ref, g_ref, wk_ref, wv_ref, k_ref, v_ref, kb_ref, vb_ref):
    mn = _rms(m_ref[...], g_ref[...]).astype(bf16)
    k = _dot(mn, wk_ref[...])
    v = _dot(mn, wv_ref[...])
    for h in range(N_HEADS):
        k_ref[:, h, :] = k[:, h * HD_X:(h + 1) * HD_X]
        v_ref[:, h, :] = v[:, h * HD_X:(h + 1) * HD_X]
    kb_ref[...] = k.astype(bf16)
    vb_ref[...] = v.astype(bf16)


def _memkv(mem, g, wk, wv):
    b = mem.shape[0]
    depth = wk.shape[0]
    per_layer = lambda shape: pl.BlockSpec((None,) + shape, lambda l, i: (l,) + (0,) * len(shape))
    out5 = pl.BlockSpec((None, None, N_MEM, N_HEADS, HD_X), lambda l, i: (l, i, 0, 0, 0))
    outb = pl.BlockSpec((None, None, N_MEM, D_MODEL), lambda l, i: (l, i, 0, 0))
    sds5 = jax.ShapeDtypeStruct((depth, b, N_MEM, N_HEADS, HD_X), f32)
    sdsb = jax.ShapeDtypeStruct((depth, b, N_MEM, D_MODEL), bf16)
    return pl.pallas_call(
        _memkv_body, grid=(depth, b),
        in_specs=[pl.BlockSpec((None, N_MEM, D_MODEL), lambda l, i: (i, 0, 0)), per_layer((1, D_MODEL)),
                  per_layer((D_MODEL, D_MODEL)), per_layer((D_MODEL, D_MODEL))],
        out_specs=[out5, out5, outb, outb], out_shape=[sds5, sds5, sdsb, sdsb],
        compiler_params=_params("parallel", "parallel"), name="memkv",
    )(mem, g, wk, wv)


_RQ, _RK, _RV, _RG, _GQKV, _GZ, _MAIN_W = 0, 512, 1024, 2048, 3072, 5120, 6144
_AB_END = _MAIN_W + 2 * N_HEADS
_MQ_END = _AB_END + D_MODEL


def _split_w_in_body(w_ref, main_ref, ab_ref, mq_ref, gates_ref):
    main_ref[...] = w_ref[:, :_MAIN_W].astype(bf16)
    ab_ref[...] = w_ref[:, _MAIN_W:_AB_END]
    mq_ref[...] = w_ref[:, _AB_END:_MQ_END].astype(bf16)
    gates_ref[...] = w_ref[:, _MQ_END:].astype(bf16)


def _split_w_in(w_in, rows=256):
    depth, k, width = w_in.shape
    blk = lambda n: pl.BlockSpec((None, rows, n), lambda l, i: (l, i, 0))
    sds = lambda n, dt: jax.ShapeDtypeStruct((depth, k, n), dt)
    return pl.pallas_call(
        _split_w_in_body, grid=(depth, k // rows), in_specs=[blk(width)],
        out_specs=[blk(_MAIN_W), blk(2 * N_HEADS), blk(D_MODEL), blk(width - _MQ_END)],
        out_shape=[sds(_MAIN_W, bf16), sds(2 * N_HEADS, f32), sds(D_MODEL, bf16), sds(width - _MQ_END, bf16)],
        compiler_params=_params("parallel", "parallel"), name="split_w_in",
    )(w_in)


def _rotary(x, cos, sin_signed):
    return x * cos + pltpu.roll(x, DK // 2, axis=1) * sin_signed


def _decay_beta(ab, a_log, dt_bias, is_decay):
    g = -jnp.exp(a_log) * _softplus(ab + dt_bias)
    return jnp.where(is_decay, g, jax.nn.sigmoid(ab))


def _gdn_qkv(conv):
    a = _silu(conv)
    qs, ks = [], []
    for h in range(N_HEADS):
        q = a[:, h * DK:(h + 1) * DK]
        k = a[:, QK_W + h * DK:QK_W + (h + 1) * DK]
        qs.append(q * (lax.rsqrt(jnp.sum(q * q, axis=-1, keepdims=True) + EPS) * DK ** -0.5))
        ks.append(k * lax.rsqrt(jnp.sum(k * k, axis=-1, keepdims=True) + EPS))
    return qs, ks, a[:, 2 * QK_W:]


def _proj_body(h_ref, gpre_ref, wm_ref, wab_ref, wabt_ref, cw_ref, alog_ref, dtb_ref, alogt_ref, dtbt_ref,
               cos_ref, sin_ref,
               rq_ref, rk_ref, rv_ref, rg_ref, gq_ref, gk_ref, gv_ref, gz_ref, gcol_ref, grow_ref, tail_ref,
               carry_ref):
    j = pl.program_id(1)
    tm = h_ref.shape[0]

    @pl.when(j == 0)
    def _():
        carry_ref[...] = jnp.zeros_like(carry_ref)

    xn = _rms(h_ref[...], gpre_ref[...]).astype(bf16)
    cos, sin = cos_ref[...], sin_ref[...]

    def ret_qk(p):
        for h in range(N_HEADS):
            sl = slice(h * DK, (h + 1) * DK)
            rq_ref[:, sl] = _rotary(p[:, h * DK:(h + 1) * DK], cos, sin).astype(bf16)
            rk_ref[:, sl] = (_rotary(p[:, QK_W + h * DK:QK_W + (h + 1) * DK], cos, sin) * DK ** -0.5).astype(bf16)

    def ret_v(p):
        rv_ref[...] = p.astype(bf16)

    def ret_gate(p):
        rg_ref[...] = _silu(p).astype(bf16)

    def gdn_z(p):
        gz_ref[...] = _silu(p).astype(bf16)

    def conv_silu(x, cols):
        carry = carry_ref[:, cols]
        cw = cw_ref[:, cols]
        conv = x * cw[CONV_W - 1:CONV_W]
        conv_top = x[:SUBLANES] * cw[CONV_W - 1:CONV_W]
        first = _iota(conv_top.shape, 0)
        for s in range(1, CONV_W):
            w = cw[CONV_W - 1 - s:CONV_W - s]
            xs = pltpu.roll(x, s, axis=0)
            conv = conv + xs * w
            conv_top = conv_top + jnp.where(first < s, pltpu.roll(carry, s, axis=0), xs[:SUBLANES]) * w
        carry_ref[:, cols] = x[tm - SUBLANES:]
        tail_ref[:, cols] = x[tm - SUBLANES:]
        return (slice(None), _silu(conv)), (slice(0, SUBLANES), _silu(conv_top))

    def gdn_qk(p):
        for rows, a in conv_silu(p, slice(0, 2 * QK_W)):
            for h in range(N_HEADS):
                q = a[:, h * DK:(h + 1) * DK]
                k = a[:, QK_W + h * DK:QK_W + (h + 1) * DK]
                q = q * (lax.rsqrt(jnp.sum(q * q, axis=-1, keepdims=True) + EPS) * DK ** -0.5)
                k = k * lax.rsqrt(jnp.sum(k * k, axis=-1, keepdims=True) + EPS)
                gq_ref[rows, h * DK:(h + 1) * DK] = q.astype(bf16)
                gk_ref[rows, h * DK:(h + 1) * DK] = k.astype(bf16)

    def gdn_v(p):
        for rows, a in conv_silu(p, slice(2 * QK_W, CONV_DIM)):
            gv_ref[rows, :] = a.astype(bf16)

    groups = ((_RQ, _RV, ret_qk), (_RV, _RG, ret_v), (_RG, _GQKV, ret_gate), (_GQKV, _GQKV + 2 * QK_W, gdn_qk),
              (_GQKV + 2 * QK_W, _GZ, gdn_v), (_GZ, _MAIN_W, gdn_z))
    pending = None
    for lo, hi, epilogue in groups:
        p = _dot(xn, wm_ref[:, lo:hi])
        if pending is not None:
            pending()
        pending = functools.partial(epilogue, p)
    pending()

    ab = _dot(xn, wab_ref[...])
    gcol_ref[...] = _decay_beta(ab, alog_ref[...], dtb_ref[...], _iota(ab.shape, 1) < N_HEADS)
    abt = lax.dot_general(wabt_ref[...], xn, NT, preferred_element_type=f32)
    grow_ref[...] = _decay_beta(abt, alogt_ref[...], dtbt_ref[...], _iota(abt.shape, 0) < N_HEADS)


def _proj(layer, h3d, gpre, wm, wab, wabt, cw, alog, dtb, alogt, dtbt, cos, sin, tm):
    b, l, _ = h3d.shape
    tok = lambda w: pl.BlockSpec((None, tm, w), lambda i, j: (i, j, 0))
    tab = pl.BlockSpec((tm, DK), lambda i, j: (j, 0))
    sds = lambda w, dt=bf16: jax.ShapeDtypeStruct((b, l, w), dt)
    res = lambda *shape: _resident(shape, layer)
    return pl.pallas_call(
        _proj_body, grid=(b, l // tm),
        in_specs=[tok(D_MODEL), res(1, D_MODEL), res(D_MODEL, _MAIN_W), res(D_MODEL, LANES),
                  res(SUBLANES, D_MODEL), res(CONV_W, CONV_DIM), res(1, LANES),
                  res(1, LANES), res(SUBLANES, 1), res(SUBLANES, 1), tab, tab],
        out_specs=[tok(QK_W), tok(QK_W), tok(V_W), tok(V_W), tok(QK_W), tok(QK_W), tok(V_W), tok(V_W),
                   tok(LANES), pl.BlockSpec((None, SUBLANES, tm), lambda i, j: (i, 0, j)),
                   pl.BlockSpec((None, SUBLANES, CONV_DIM), lambda i, j: (i, 0, 0))],
        out_shape=[sds(QK_W), sds(QK_W), sds(V_W), sds(V_W), sds(QK_W), sds(QK_W), sds(V_W), sds(V_W),
                   sds(LANES, f32), jax.ShapeDtypeStruct((b, SUBLANES, l), f32),
                   jax.ShapeDtypeStruct((b, SUBLANES, CONV_DIM), f32)],
        scratch_shapes=[pltpu.VMEM((SUBLANES, CONV_DIM), f32)],
        compiler_params=_params("parallel", "arbitrary"), name="proj",
    )(h3d, gpre, wm, wab, wabt, cw, alog, dtb, alogt, dtbt, cos, sin)


def _ret_body(q_ref, k_ref, v_ref, sg_ref, o_ref, sout_ref, s_ref, *, chunk):
    j = pl.program_id(1)
    c = chunk

    @pl.when(j == 0)
    def _():
        s_ref[...] = jnp.zeros_like(s_ref)

    row, col = _iota((c, c), 0), _iota((c, c), 1)
    dist = (row - col).astype(f32)
    pos = _iota((c, 1), 0).astype(f32)
    for h in range(N_HEADS):
        lg = LOG_GAMMA[h]
        decay = jnp.exp(jnp.where(row >= col, dist * lg, -jnp.inf))
        q_scale = jnp.exp((pos + 1.0) * lg)
        k_scale = jnp.exp((c - 1.0 - pos) * lg)
        for t in range(q_ref.shape[0] // c):
            rows = slice(t * c, (t + 1) * c)
            qh = q_ref[rows, h * DK:(h + 1) * DK]
            kh = k_ref[rows, h * DK:(h + 1) * DK]
            vh = v_ref[rows, h * DV:(h + 1) * DV]
            attn = lax.dot_general(qh, kh, NT, preferred_element_type=f32) * decay
            q_dec = (qh.astype(f32) * q_scale).astype(bf16)
            k_dec = (kh.astype(f32) * k_scale).astype(bf16)
            s = s_ref[h]
            o = _dot(q_dec, s.astype(bf16)) + _dot(attn.astype(bf16), vh)
            s_ref[h] = s * math.exp(c * lg) + lax.dot_general(k_dec, vh, TN, preferred_element_type=f32)
            o_ref[rows, h * DV:(h + 1) * DV] = (sg_ref[rows, h * DV:(h + 1) * DV].astype(f32) * _rms(o)).astype(bf16)

    @pl.when(j == pl.num_programs(1) - 1)
    def _():
        sout_ref[...] = s_ref[...]


def _retention(rq, rk, rv, sg, tile, chunk):
    b, l, _ = rq.shape
    tok = lambda w: pl.BlockSpec((None, tile, w), lambda i, j: (i, j, 0))
    st = pl.BlockSpec((None, N_HEADS, DK, DV), lambda i, j: (i, 0, 0, 0))
    return pl.pallas_call(
        functools.partial(_ret_body, chunk=chunk), grid=(b, l // tile),
        in_specs=[tok(QK_W), tok(QK_W), tok(V_W), tok(V_W)],
        out_specs=[tok(V_W), st],
        out_shape=[jax.ShapeDtypeStruct((b, l, V_W), bf16), jax.ShapeDtypeStruct((b, N_HEADS, DK, DV), f32)],
        scratch_shapes=[pltpu.VMEM((N_HEADS, DK, DV), f32)],
        compiler_params=_params("parallel", "arbitrary"), name="retention",
    )(rq, rk, rv, sg)


def _gdn_body(q_ref, k_ref, v_ref, sz_ref, gcol_ref, grow_ref, nw_ref, o_ref, sout_ref, s_ref):
    j = pl.program_id(1)
    c = CHUNK
    hc = N_HEADS * c

    @pl.when(j == 0)
    def _():
        s_ref[...] = jnp.zeros_like(s_ref)

    row, lane = _iota((c, hc), 0), _iota((c, hc), 1)
    lane_head, col = lane // c, lane % c
    lower, strict = row >= col, row > col
    head_mask = [(lane_head == h).astype(bf16) for h in range(N_HEADS)]
    block_diag = lambda x: jnp.concatenate([x * m for m in head_mask], axis=0)
    spread = lambda cols: functools.reduce(
        lambda acc, h: jnp.where(lane_head == h, cols[h], acc), range(N_HEADS - 1), cols[N_HEADS - 1])
    cum_rows = (_iota((c, c), 0) >= _iota((c, c), 1)).astype(f32)
    cum_cols = (_iota((c, hc), 0) <= (_iota((c, hc), 1) % c)).astype(f32)
    own_row = _iota((SUBLANES, hc), 0) == _iota((SUBLANES, hc), 1) // c
    zeros_k = jnp.zeros((c, DK), bf16)
    nw = nw_ref[...]

    chunks = range(q_ref.shape[0] // c)
    tok = lambda t: slice(t * c, (t + 1) * c)

    def chunk_setup(t):
        rows = tok(t)
        k_heads = [k_ref[rows, h * DK:(h + 1) * DK] for h in range(N_HEADS)]
        k_bd = jnp.concatenate([jnp.concatenate([k_heads[h] if g == h else zeros_k for g in range(N_HEADS)], axis=1)
                                for h in range(N_HEADS)], axis=0)
        qk = lax.dot_general(jnp.concatenate([q_ref[rows, :], k_ref[rows, :]], axis=0), k_bd, NT,
                             preferred_element_type=f32)
        gcol = gcol_ref[rows, :]
        cum = jnp.dot(cum_rows, gcol, preferred_element_type=f32, precision=HIGHEST)
        b_cols = [cum[:, h:h + 1] for h in range(N_HEADS)]
        beta_cols = [gcol[:, N_HEADS + h:N_HEADS + h + 1] for h in range(N_HEADS)]
        b_row = jnp.sum(jnp.where(own_row, jnp.dot(grow_ref[t], cum_cols, preferred_element_type=f32,
                                                   precision=HIGHEST), 0.0), axis=0, keepdims=True)
        decay = jnp.exp(jnp.where(lower, spread(b_cols) - b_row, -jnp.inf))
        x = jnp.where(strict, qk[c:] * spread(beta_cols) * decay, 0.0) * -1.0
        return (qk[:c] * decay, x, jnp.concatenate(b_cols, axis=0), jnp.concatenate(beta_cols, axis=0))

    attns, xs, b_colv, betav = zip(*[chunk_setup(t) for t in chunks])
    ns = xs
    x_bds = [block_diag(x.astype(bf16)) for x in xs]
    for _ in range(int(math.log2(c)) - 1):
        xs = [_dot(x.astype(bf16), x_bd) for x, x_bd in zip(xs, x_bds)]
        x_bds = [block_diag(x.astype(bf16)) for x in xs]
        ns = [n + x + _dot(n.astype(bf16), x_bd) for n, x, x_bd in zip(ns, xs, x_bds)]

    def chunk_wy(t):
        rows = tok(t)
        b_col, beta = b_colv[t], betav[t]
        kst = jnp.concatenate([k_ref[rows, h * DK:(h + 1) * DK] for h in range(N_HEADS)], axis=0)
        qst = jnp.concatenate([q_ref[rows, h * DK:(h + 1) * DK] for h in range(N_HEADS)], axis=0)
        vst = jnp.concatenate([v_ref[rows, h * DV:(h + 1) * DV] for h in range(N_HEADS)], axis=0).astype(f32)
        n_bd = block_diag(ns[t].astype(bf16))
        kf = kst.astype(f32)
        e_b = jnp.exp(b_col)
        ru, rw = vst * beta, kf * (beta * e_b)
        u = ru + _dot(n_bd, ru.astype(bf16))
        w = (rw + _dot(n_bd, rw.astype(bf16))).astype(bf16)
        q_dec = (qst.astype(f32) * e_b).astype(bf16)
        return u, w, q_dec, kf, block_diag(attns[t].astype(bf16))

    wy = [chunk_wy(t) for t in chunks]
    for t in chunks:
        rows = tok(t)
        u, w, q_dec, kf, attn_bd = wy[t]
        b_col = b_colv[t]
        v_new, o_inter = [], []
        for h in range(N_HEADS):
            hs = slice(h * c, (h + 1) * c)
            r = _dot(jnp.concatenate([w[hs], q_dec[hs]], axis=0), s_ref[h].astype(bf16))
            v_new.append(u[hs] - r[:c])
            o_inter.append(r[c:])
        vb = jnp.concatenate(v_new, axis=0).astype(bf16)
        o = jnp.concatenate(o_inter, axis=0) + _dot(attn_bd, vb)
        for h in range(N_HEADS):
            hs = slice(h * c, (h + 1) * c)
            b_last = b_col[(h + 1) * c - 1:(h + 1) * c]
            k_dec = (kf[hs] * jnp.exp(b_last - b_col[hs])).astype(bf16)
            s_ref[h] = s_ref[h] * jnp.exp(b_last) + lax.dot_general(k_dec, vb[hs], TN, preferred_element_type=f32)
            o_ref[rows, h * DV:(h + 1) * DV] = (
                _rms(o[hs], nw) * sz_ref[rows, h * DV:(h + 1) * DV].astype(f32)).astype(bf16)

    @pl.when(j == pl.num_programs(1) - 1)
    def _():
        sout_ref[...] = s_ref[...]


def _gated_delta(layer, gq, gk, gv, sz, gcol, grow4, nw, tile):
    b, l, _ = gq.shape
    tok = lambda w: pl.BlockSpec((None, tile, w), lambda i, j: (i, j, 0))
    st = pl.BlockSpec((None, N_HEADS, DK, DV), lambda i, j: (i, 0, 0, 0))
    return pl.pallas_call(
        _gdn_body, grid=(b, l // tile),
        in_specs=[tok(QK_W), tok(QK_W), tok(V_W), tok(V_W), tok(LANES),
                  pl.BlockSpec((None, tile // CHUNK, SUBLANES, CHUNK), lambda i, j: (i, j, 0, 0)),
                  _resident((1, DV), layer)],
        out_specs=[tok(V_W), st],
        out_shape=[jax.ShapeDtypeStruct((b, l, V_W), bf16), jax.ShapeDtypeStruct((b, N_HEADS, DK, DV), f32)],
        scratch_shapes=[pltpu.VMEM((N_HEADS, DK, DV), f32)],
        compiler_params=_params("parallel", "arbitrary"), name="gated_delta",
    )(gq, gk, gv, sz, gcol, grow4, nw)


def _softmax(s):
    e = jnp.exp(s - jnp.max(s, axis=-1, keepdims=True))
    return e / jnp.sum(e, axis=-1, keepdims=True)


def _merge_tail(x, xn, o_ret, o_gdn, mem_proj, wg_ref, wr_ref, wd_ref, wo_ref, gpost):
    gates = jax.nn.sigmoid(_dot(xn, wg_ref[...]))
    merged = (gates[:, :D_MODEL] * _dot(o_ret, wr_ref[...])
              + gates[:, D_MODEL:2 * D_MODEL] * _dot(o_gdn, wd_ref[...])
              + gates[:, 2 * D_MODEL:] * mem_proj)
    y = _dot(merged.astype(bf16), wo_ref[...])
    return x + _rms(y, gpost)


def _merge_body(h_ref, gpre_ref, wq_ref, wg_ref, mk_ref, mv_ref, oret_ref, ogdn_ref,
                wr_ref, wd_ref, wm_ref, wo_ref, gpost_ref, o_ref):
    x = h_ref[...]
    xn = _rms(x, gpre_ref[...]).astype(bf16)
    mq = _dot(xn, wq_ref[...]).astype(bf16)
    mem_proj = None
    for h in range(N_HEADS):
        sl = slice(h * HD_X, (h + 1) * HD_X)
        s = lax.dot_general(mq[:, sl], mk_ref[:, sl], NT, preferred_element_type=f32) * HD_X ** -0.5
        o = _dot(_softmax(s).astype(bf16), mv_ref[:, sl])
        part = _dot(o.astype(bf16), wm_ref[sl, :])
        mem_proj = part if mem_proj is None else mem_proj + part
    o_ref[...] = _merge_tail(x, xn, oret_ref[...], ogdn_ref[...], mem_proj, wg_ref, wr_ref, wd_ref, wo_ref,
                             gpost_ref[...])


def _merge(layer, h3d, gpre, wq, wg, mk, mv, o_ret, o_gdn, wr, wd, wm, wo, gpost, tm):
    b, l, _ = h3d.shape
    tok = pl.BlockSpec((None, tm, D_MODEL), lambda i, j: (i, j, 0))
    mem = pl.BlockSpec((None, None, N_MEM, D_MODEL), lambda i, j: (layer, i, 0, 0))
    sq = _resident((D_MODEL, D_MODEL), layer)
    return pl.pallas_call(
        _merge_body, grid=(b, l // tm),
        in_specs=[tok, _resident((1, D_MODEL), layer), sq, _resident((D_MODEL, 3 * D_MODEL), layer), mem, mem,
                  tok, tok, sq, sq, sq, sq, _resident((1, D_MODEL), layer)],
        out_specs=tok, out_shape=jax.ShapeDtypeStruct((b, l, D_MODEL), f32),
        compiler_params=_params("parallel", "parallel"), name="merge",
    )(h3d, gpre, wq, wg, mk, mv, o_ret, o_gdn, wr, wd, wm, wo, gpost)


def _sproj_body(h_ref, gpre_ref, wm_ref, wab_ref, wq_ref, cw_ref, alog_ref, dtb_ref, cos_ref, sin_ref, buf_ref,
                rq_ref, rk_ref, rv_ref, rg_ref, gq_ref, gk_ref, gv_ref, gz_ref, gcol_ref, mq_ref, nbuf_ref):
    xn = _rms(h_ref[...], gpre_ref[...]).astype(bf16)
    p = _dot(xn, wm_ref[...])
    cos, sin = cos_ref[...], sin_ref[...]
    for h in range(N_HEADS):
        sl = slice(h * DK, (h + 1) * DK)
        rq_ref[:, sl] = _rotary(p[:, _RQ + h * DK:_RQ + (h + 1) * DK], cos, sin)
        rk_ref[:, sl] = _rotary(p[:, _RK + h * DK:_RK + (h + 1) * DK], cos, sin) * DK ** -0.5
    rv_ref[...] = p[:, _RV:_RG]
    rg_ref[...] = _silu(p[:, _RG:_GQKV])
    gz_ref[...] = _silu(p[:, _GZ:_MAIN_W])
    x = p[:, _GQKV:_GZ]
    cw = cw_ref[...]
    conv = x * cw[CONV_W - 1:CONV_W]
    for s in range(CONV_W - 1):
        prev = buf_ref[:, s * CONV_DIM:(s + 1) * CONV_DIM]
        conv = conv + prev * cw[s:s + 1]
        if s > 0:
            nbuf_ref[:, (s - 1) * CONV_DIM:s * CONV_DIM] = prev
    nbuf_ref[:, (CONV_W - 2) * CONV_DIM:] = x
    qs, ks, v = _gdn_qkv(conv)
    for h in range(N_HEADS):
        gq_ref[:, h * DK:(h + 1) * DK] = qs[h]
        gk_ref[:, h * DK:(h + 1) * DK] = ks[h]
    gv_ref[...] = v
    ab = _dot(xn, wab_ref[...])
    gcol_ref[...] = _decay_beta(ab, alog_ref[...], dtb_ref[...], _iota(ab.shape, 1) < N_HEADS)
    mq_ref[...] = _dot(xn, wq_ref[...])


def _sample_proj(layer, h2d, gpre, wm, wab, wq, cw, alog, dtb, cos, sin, buf2d):
    n = h2d.shape[0]
    sds = lambda w: jax.ShapeDtypeStruct((n, w), f32)
    res = lambda *shape: _resident(shape, layer)
    widths = (QK_W, QK_W, V_W, V_W, QK_W, QK_W, V_W, V_W, LANES, D_MODEL, (CONV_W - 1) * CONV_DIM)
    return pl.pallas_call(
        _sproj_body, grid=(1,),
        in_specs=[_resident((n, D_MODEL)), res(1, D_MODEL), res(D_MODEL, _MAIN_W), res(D_MODEL, LANES),
                  res(D_MODEL, D_MODEL), res(CONV_W, CONV_DIM), res(1, LANES), res(1, LANES),
                  _resident((1, DK)), _resident((1, DK)), _resident((n, (CONV_W - 1) * CONV_DIM))],
        out_specs=[pl.BlockSpec((n, w), lambda i: (0, 0)) for w in widths],
        out_shape=[sds(w) for w in widths],
        compiler_params=_params("arbitrary"), name="sample_proj",
    )(h2d, gpre, wm, wab, wq, cw, alog, dtb, cos, sin, buf2d)


_STATE_BLOCK = 8


def _sstate_body(cols_ref, rv_ref, gv_ref, gcol_ref, sret_ref, sgdn_ref, *refs):
    nret_ref, ngdn_ref, oret_ref, ogdn_ref = refs[-4:]
    for j in range(_STATE_BLOCK):
        for h in range(N_HEADS):
            vs = slice(h * DV, (h + 1) * DV)
            q = cols_ref[0, h, :, j:j + 1]
            k = cols_ref[0, N_HEADS + h, :, j:j + 1]
            s_new = sret_ref[j, h] * math.exp(LOG_GAMMA[h]) + k * rv_ref[j:j + 1, vs]
            nret_ref[j, h] = s_new
            oret_ref[j:j + 1, vs] = jnp.sum(q * s_new, axis=0, keepdims=True)
            q = cols_ref[0, 2 * N_HEADS + h, :, j:j + 1]
            k = cols_ref[0, 3 * N_HEADS + h, :, j:j + 1]
            eg = jnp.exp(gcol_ref[j:j + 1, h:h + 1])
            beta = gcol_ref[j:j + 1, N_HEADS + h:N_HEADS + h + 1]
            s = sgdn_ref[j, h]
            ks = jnp.sum(k * s, axis=0, keepdims=True)
            v_new = beta * (gv_ref[j:j + 1, vs] - eg * ks)
            s_new = s * eg + k * v_new
            ngdn_ref[j, h] = s_new
            ogdn_ref[j:j + 1, vs] = jnp.sum(q * s_new, axis=0, keepdims=True)


def _sample_state(layer, cols, rv, gv, gcol, s_ret, s_gdn, prev):
    depth, n = s_ret.shape[:2]
    bt = _STATE_BLOCK
    rowb = lambda w: pl.BlockSpec((bt, w), lambda i: (i, 0))
    st, sds = _stacked_out(layer, depth, (bt, N_HEADS, DK, DV), f32, lambda i: (i, 0, 0, 0))
    sds = sds((n, N_HEADS, DK, DV))
    prev, prev_specs, aliases = _alias_prev(6, prev)
    return pl.pallas_call(
        _sstate_body, grid=(n // bt,),
        in_specs=[pl.BlockSpec((1, 4 * N_HEADS, DK, bt), lambda i: (i, 0, 0, 0)), rowb(V_W), rowb(V_W), rowb(LANES),
                  st, st] + prev_specs,
        out_specs=[st, st, rowb(V_W), rowb(V_W)],
        out_shape=[sds, sds, jax.ShapeDtypeStruct((n, V_W), f32), jax.ShapeDtypeStruct((n, V_W), f32)],
        input_output_aliases=aliases, compiler_params=_params("parallel"), name="sample_state",
    )(cols, rv, gv, gcol, s_ret, s_gdn, *prev)


def _sattn_body(mq_ref, k_ref, v_ref, o_ref):
    for j in range(mq_ref.shape[0]):
        s = jnp.sum(k_ref[j] * mq_ref[j][None], axis=-1, keepdims=True) * HD_X ** -0.5
        e = jnp.exp(s - jnp.max(s, axis=0, keepdims=True))
        p = e / jnp.sum(e, axis=0, keepdims=True)
        o_ref[j] = jnp.sum(p * v_ref[j], axis=0)


def _smerge_body(h_ref, gpre_ref, wg_ref, oret_ref, rg_ref, ogdn_ref, gz_ref, nw_ref, omem_ref,
                 wr_ref, wd_ref, wm_ref, wo_ref, gpost_ref, o_ref):
    x = h_ref[...]
    xn = _rms(x, gpre_ref[...]).astype(bf16)
    nw = nw_ref[...]
    rets, gdns = [], []
    for h in range(N_HEADS):
        vs = slice(h * DV, (h + 1) * DV)
        rets.append(rg_ref[:, vs] * _rms(oret_ref[:, vs]))
        gdns.append(_rms(ogdn_ref[:, vs], nw) * gz_ref[:, vs])
    o_ret = jnp.concatenate(rets, axis=-1).astype(bf16)
    o_gdn = jnp.concatenate(gdns, axis=-1).astype(bf16)
    mem_proj = _dot(omem_ref[...].astype(bf16), wm_ref[...])
    o_ref[...] = _merge_tail(x, xn, o_ret, o_gdn, mem_proj, wg_ref, wr_ref, wd_ref, wo_ref, gpost_ref[...])


def _sample_merge(layer, h2d, gpre, wg, o_ret, rg, o_gdn, gz, nw, o_mem, wr, wd, wm, wo, gpost):
    n = h2d.shape[0]
    act = _resident((n, D_MODEL))
    sq = _resident((D_MODEL, D_MODEL), layer)
    return pl.pallas_call(
        _smerge_body, grid=(1,),
        in_specs=[act, _resident((1, D_MODEL), layer), _resident((D_MODEL, 3 * D_MODEL), layer), act, act, act, act,
                  _resident((1, DV), layer), act, sq, sq, sq, sq, _resident((1, D_MODEL), layer)],
        out_specs=pl.BlockSpec((n, D_MODEL), lambda i: (0, 0)), out_shape=jax.ShapeDtypeStruct(h2d.shape, f32),
        compiler_params=_params("arbitrary"), name="sample_merge",
    )(h2d, gpre, wg, o_ret, rg, o_gdn, gz, nw, o_mem, wr, wd, wm, wo, gpost)


def _rope_tables(pos):
    half = DK // 2
    inv_freq = ROPE_BASE ** (-jnp.arange(half, dtype=f32) / half)
    ang = pos.astype(f32)[:, None] * inv_freq[None, :]
    cos, sin = jnp.cos(ang), jnp.sin(ang)
    return jnp.concatenate([cos, cos], axis=-1), jnp.concatenate([-sin, sin], axis=-1)


def _row_tile(n, want):
    t = min(n, want)
    assert n % t == 0, (n, t)
    return t


def kernel(x_prompt, x_sample, mem_prompt, state_ret, state_gdn, state_conv, cache_mem_k, cache_mem_v, norm_ffn1_pre, norm_ffn1_post, ffn1_w_in, ffn1_w_out, norm_mix_pre, norm_mix_post, w_in, gdn_conv_w, gdn_a_log, gdn_dt_bias, gdn_norm, norm_mem, w_mem_k, w_mem_v, w_branch_ret, w_branch_gdn, w_branch_mem, w_out, norm_ffn2_pre, norm_ffn2_post, ffn2_w_in, ffn2_w_out):
    bp, sp, _ = x_prompt.shape
    ns, ss, _ = x_sample.shape
    depth = w_in.shape[0]
    assert ss == 1 and sp % CHUNK == 0 and ns % _STATE_BLOCK == 0
    tm_ffn = _row_tile(bp * sp, 512)
    tm_seq = _row_tile(sp, 512)

    cos_p, sin_p = _rope_tables(jnp.arange(sp))
    cos_s, sin_s = _rope_tables(PAST_LEN + jnp.arange(ss))
    row = lambda v: v.reshape(depth, 1, -1).astype(f32)
    pad_lanes = lambda v: jnp.pad(v.astype(f32), ((0, 0), (0, LANES - v.shape[1]))).reshape(depth, 1, LANES)
    pad_rows = lambda v: jnp.pad(v.astype(f32), ((0, 0), (0, SUBLANES - v.shape[1]))).reshape(depth, SUBLANES, 1)
    w_main, w_ab, w_mq, w_gates = _split_w_in(w_in)
    w_ab_col = jnp.pad(w_ab, ((0, 0), (0, 0), (0, LANES - 2 * N_HEADS))).astype(bf16)
    w_ab_row = jnp.swapaxes(w_ab, 1, 2).astype(bf16)
    f1_in, f1_out, f2_in, f2_out = (w.astype(bf16) for w in (ffn1_w_in, ffn1_w_out, ffn2_w_in, ffn2_w_out))
    wr, wd, wm, wo = (w.astype(bf16) for w in (w_branch_ret, w_branch_gdn, w_branch_mem, w_out))
    cw = gdn_conv_w.astype(f32)
    alog, dtb = pad_lanes(gdn_a_log), pad_lanes(gdn_dt_bias)
    alog_t, dtb_t = pad_rows(gdn_a_log), pad_rows(gdn_dt_bias)
    nw = row(gdn_norm)
    g_f1pre, g_f1post, g_f2pre, g_f2post = (row(g) for g in (norm_ffn1_pre, norm_ffn1_post, norm_ffn2_pre,
                                                              norm_ffn2_post))
    g_mpre, g_mpost = row(norm_mix_pre), row(norm_mix_post)

    mk, mv, mk_b, mv_b = _memkv(mem_prompt, row(norm_mem), w_mem_k.astype(bf16), w_mem_v.astype(bf16))
    hp = x_prompt
    hs = x_sample.reshape(ns, D_MODEL)
    outs = [[] for _ in range(4)]
    new_states = None
    ret_chunk = _row_tile(tm_seq, 256)
    for l in range(depth):
        h1s = _ffn(l, hs, g_f1pre, f1_in, f1_out, g_f1post, ns)
        s_rq, s_rk, s_rv, s_rg, s_gq, s_gk, s_gv, s_gz, s_gcol, mq, nbuf = _sample_proj(
            l, h1s, g_mpre, w_main, w_ab_col, w_mq, cw, alog, dtb, cos_s, sin_s,
            state_conv[l].reshape(ns, (CONV_W - 1) * CONV_DIM))
        cols = jnp.stack([s_rq, s_rk, s_gq, s_gk]).reshape(4, ns // _STATE_BLOCK, _STATE_BLOCK, N_HEADS, DK)
        cols = cols.transpose(1, 0, 3, 4, 2).reshape(ns // _STATE_BLOCK, 4 * N_HEADS, DK, _STATE_BLOCK)
        n_ret, n_gdn, so_ret, so_gdn = _sample_state(l, cols, s_rv, s_gv, s_gcol, state_ret, state_gdn, new_states)
        new_states = (n_ret, n_gdn)
        outs[3].append(nbuf.reshape(ns, CONV_W - 1, CONV_DIM))

        h1, o_mem_a = _ffn(l, hp.reshape(bp * sp, D_MODEL), g_f1pre, f1_in, f1_out, g_f1post, tm_ffn,
                           attn=(mq, cache_mem_k, cache_mem_v, 0, 2))
        h1 = h1.reshape(bp, sp, D_MODEL)
        rq, rk, rv, rg, gq, gk, gv, gz, gcol, grow, tail = _proj(
            l, h1, g_mpre, w_main, w_ab_col, w_ab_row, cw, alog, dtb, alog_t, dtb_t, cos_p, sin_p, tm_seq)
        o_ret, s_ret = _retention(rq, rk, rv, rg, tm_seq, ret_chunk)
        grow4 = grow.reshape(bp, SUBLANES, sp // CHUNK, CHUNK).transpose(0, 2, 1, 3)
        o_gdn, s_gdn = _gated_delta(l, gq, gk, gv, gz, gcol, grow4, nw, tm_seq)
        h2 = _merge(l, h1, g_mpre, w_mq, w_gates, mk_b, mv_b, o_ret, o_gdn, wr, wd, wm, wo, g_mpost, tm_seq)
        hp, o_mem_b = _ffn(l, h2.reshape(bp * sp, D_MODEL), g_f2pre, f2_in, f2_out, g_f2post, tm_ffn,
                           attn=(mq, cache_mem_k, cache_mem_v, 1, 2))
        hp = hp.reshape(bp, sp, D_MODEL)
        outs[0].append(s_ret)
        outs[1].append(s_gdn)
        outs[2].append(tail[:, SUBLANES - (CONV_W - 1):])

        o_mem = jnp.concatenate([o_mem_a, o_mem_b], axis=0).reshape(ns, D_MODEL)
        h2s = _sample_merge(l, h1s, g_mpre, w_gates, so_ret, s_rg, so_gdn, s_gz, nw, o_mem, wr, wd, wm, wo, g_mpost)
        hs = _ffn(l, h2s, g_f2pre, f2_in, f2_out, g_f2post, ns)

    ret_p, gdn_p, conv_p, conv_s = (jnp.stack(o) for o in outs)
    return (hp, hs.reshape(ns, ss, D_MODEL), ret_p, gdn_p, conv_p, mk, mv) + new_states + (conv_s,)
```

```python
import functools
import math

import jax
import jax.numpy as jnp
from jax import lax
from jax.experimental import pallas as pl
from jax.experimental.pallas import tpu as pltpu

f32, bf16 = jnp.float32, jnp.bfloat16

D_MODEL = 1024
N_HEADS = 4
DK = 128
DV = 256
QK_W = N_HEADS * DK
V_W = N_HEADS * DV
CONV_W = 4
CONV_DIM = 2 * QK_W + V_W
HD_X = 256
N_MEM = 256
D_FF = 2816
EPS = 1e-6
ROPE_BASE = 10000.0
PAST_LEN = 16384
CHUNK = 64
LANES = 128
SUBLANES = 8
VMEM_LIMIT = 56 * 1024 * 1024
LOG_GAMMA = tuple(math.log1p(-2.0 ** (-5.0 - h)) for h in range(N_HEADS))
HIGHEST = lax.Precision.HIGHEST
NT = (((1,), (1,)), ((), ()))
TN = (((0,), (0,)), ((), ()))


def _params(*sem):
    return pltpu.CompilerParams(dimension_semantics=sem, vmem_limit_bytes=VMEM_LIMIT)


def _resident(shape, layer=None, col_block=0):
    if layer is None:
        return pl.BlockSpec(shape, lambda *_: (0,) * len(shape), pipeline_mode=pl.Buffered(1))
    index = (layer,) + (0,) * (len(shape) - 1) + (col_block,)
    return pl.BlockSpec((None,) + shape, lambda *_: index, pipeline_mode=pl.Buffered(1))


def _rms(x, gain=None):
    y = x * lax.rsqrt(jnp.mean(x * x, axis=-1, keepdims=True) + EPS)
    return y if gain is None else y * gain


def _silu(x):
    return x * jax.nn.sigmoid(x)


def _softplus(x):
    return jnp.maximum(x, 0.0) + jnp.log1p(jnp.exp(-jnp.abs(x)))


def _dot(a, b):
    return jnp.dot(a, b, preferred_element_type=f32)


def _iota(shape, dim):
    return lax.broadcasted_iota(jnp.int32, shape, dim)


def _ffn_body(x_ref, gpre_ref, wgu_ref, wo_ref, gpost_ref, o_ref):
    x = x_ref[...]
    xn = _rms(x, gpre_ref[...]).astype(bf16)
    h = _dot(xn, wgu_ref[...])
    act = (_silu(h[:, :D_FF]) * h[:, D_FF:]).astype(bf16)
    y = _dot(act, wo_ref[...])
    o_ref[...] = x + 0.5 * _rms(y, gpost_ref[...])


def _ffn_hosting_body(x_ref, gpre_ref, wgu_ref, wo_ref, gpost_ref, mq_ref, k_ref, v_ref,
                      cols_ref, rv_ref, gv_ref, gcol_ref, sret_ref, sgdn_ref, *refs):
    o_ref, omem_ref, nret_ref, ngdn_ref, oret_ref, ogdn_ref = refs[-6:]
    _ffn_body(x_ref, gpre_ref, wgu_ref, wo_ref, gpost_ref, o_ref)
    _sattn_body(mq_ref, k_ref, v_ref, omem_ref)
    _sstate_body(cols_ref, rv_ref, gv_ref, gcol_ref, sret_ref, sgdn_ref, nret_ref, ngdn_ref, oret_ref, ogdn_ref)


def _ffn(layer, x2d, gpre, wgu, wo, gpost, tm, hosted=None):
    t = x2d.shape[0]
    steps = t // tm
    row = pl.BlockSpec((tm, D_MODEL), lambda i: (i, 0))
    in_specs = [row, _resident((1, D_MODEL), layer), _resident((D_MODEL, 2 * D_FF), layer),
                _resident((D_FF, D_MODEL), layer), _resident((1, D_MODEL), layer)]
    out_row = jax.ShapeDtypeStruct((t, D_MODEL), f32)
    if hosted is None:
        return pl.pallas_call(
            _ffn_body, grid=(steps,), in_specs=in_specs, out_specs=row, out_shape=out_row,
            compiler_params=_params("parallel"), name="ffn",
        )(x2d, gpre, wgu, wo, gpost)
    part, parts, ps, mq, mem_k, mem_v, cols, rv, gv, gcol, s_ret, s_gdn, prev = hosted
    depth, n = s_ret.shape[:2]
    assert ps * parts * steps == n, (n, parts, steps, ps)
    first = part * steps
    grp = lambda *shape: pl.BlockSpec((None,) + shape, lambda i: (first + i,) + (0,) * len(shape))
    out_grp = lambda *shape: pl.BlockSpec((None,) + shape, lambda i: (i,) + (0,) * len(shape))
    mem = pl.BlockSpec((None, ps, N_MEM, N_HEADS, HD_X), lambda i: (layer, first + i, 0, 0, 0))
    st = pl.BlockSpec((None, ps, N_HEADS, DK, DV), lambda i: (layer, first + i, 0, 0, 0))
    st_sds = jax.ShapeDtypeStruct((depth, n, N_HEADS, DK, DV), f32)
    grp_sds = lambda *shape: jax.ShapeDtypeStruct((steps,) + shape, f32)
    prev, prev_specs, aliases = _alias_prev(14, prev)
    aliases = {k: v + 2 for k, v in aliases.items()}
    return pl.pallas_call(
        _ffn_hosting_body, grid=(steps,),
        in_specs=in_specs + [grp(ps, N_HEADS, HD_X), mem, mem,
                             pl.BlockSpec((1, 4 * N_HEADS, DK, ps), lambda i: (first + i, 0, 0, 0)),
                             grp(ps, V_W), grp(ps, V_W), grp(ps, LANES), st, st] + prev_specs,
        out_specs=[row, out_grp(ps, N_HEADS, HD_X), st, st, out_grp(ps, V_W), out_grp(ps, V_W)],
        out_shape=[out_row, grp_sds(ps, N_HEADS, HD_X), st_sds, st_sds, grp_sds(ps, V_W), grp_sds(ps, V_W)],
        input_output_aliases=aliases, compiler_params=_params("parallel"), name="ffn_hosting",
    )(x2d, gpre, wgu, wo, gpost, mq, mem_k, mem_v, cols, rv, gv, gcol, s_ret, s_gdn, *prev)


def _stacked_out(layer, depth, shape, dtype, index_map):
    spec = pl.BlockSpec((None,) + shape, lambda *i: (layer,) + index_map(*i))
    return spec, lambda full: jax.ShapeDtypeStruct((depth,) + full, dtype)


def _alias_prev(n_inputs, prev):
    prev = [] if prev is None else list(prev)
    return prev, [pl.BlockSpec(memory_space=pl.ANY)] * len(prev), {n_inputs + i: i for i in range(len(prev))}


def _memkv_body(m_ref, g_ref, wk_ref, wv_ref, k_ref, v_ref, kb_ref, vb_ref):
    mn = _rms(m_ref[...], g_ref[...]).astype(bf16)
    k = _dot(mn, wk_ref[...])
    v = _dot(mn, wv_ref[...])
    for h in range(N_HEADS):
        k_ref[:, h, :] = k[:, h * HD_X:(h + 1) * HD_X]
        v_ref[:, h, :] = v[:, h * HD_X:(h + 1) * HD_X]
    kb_ref[...] = k.astype(bf16)
    vb_ref[...] = v.astype(bf16)


def _memkv(mem, g, wk, wv):
    b = mem.shape[0]
    depth = wk.shape[0]
    per_layer = lambda shape: pl.BlockSpec((None,) + shape, lambda l, i: (l,) + (0,) * len(shape))
    out5 = pl.BlockSpec((None, None, N_MEM, N_HEADS, HD_X), lambda l, i: (l, i, 0, 0, 0))
    outb = pl.BlockSpec((None, None, N_MEM, D_MODEL), lambda l, i: (l, i, 0, 0))
    sds5 = jax.ShapeDtypeStruct((depth, b, N_MEM, N_HEADS, HD_X), f32)
    sdsb = jax.ShapeDtypeStruct((depth, b, N_MEM, D_MODEL), bf16)
    return pl.pallas_call(
        _memkv_body, grid=(depth, b),
        in_specs=[pl.BlockSpec((None, N_MEM, D_MODEL), lambda l, i: (i, 0, 0)), per_layer((1, D_MODEL)),
                  per_layer((D_MODEL, D_MODEL)), per_layer((D_MODEL, D_MODEL))],
        out_specs=[out5, out5, outb, outb], out_shape=[sds5, sds5, sdsb, sdsb],
        compiler_params=_params("parallel", "parallel"), name="memkv",
    )(mem, g, wk, wv)


_RQ, _RK, _RV, _RG, _GQKV, _GZ, _MAIN_W = 0, 512, 1024, 2048, 3072, 5120, 6144
_AB_END = _MAIN_W + 2 * N_HEADS
_MQ_END = _AB_END + D_MODEL


def _packed_mq(layer):
    return _resident((D_MODEL, D_MODEL), layer)


def _packed_gates(layer):
    return _resident((D_MODEL, 3 * D_MODEL), layer)


def _w_in_cols_body(wt_ref, o_ref):
    o_ref[...] = wt_ref[0].T.astype(bf16)


def _w_in_cols(w_in_t, first, width, blk=256):
    depth = w_in_t.shape[0]
    assert first % SUBLANES == 0 and width % blk == 0
    return pl.pallas_call(
        _w_in_cols_body, grid=(depth, width // blk),
        in_specs=[pl.BlockSpec((pl.Element(1), pl.Element(blk), pl.Element(D_MODEL)),
                               lambda l, c: (l, pl.multiple_of(first + c * blk, SUBLANES), 0))],
        out_specs=pl.BlockSpec((None, D_MODEL, blk), lambda l, c: (l, 0, c)),
        out_shape=jax.ShapeDtypeStruct((depth, D_MODEL, width), bf16),
        compiler_params=_params("parallel", "parallel"), name="w_in_cols",
    )(w_in_t)


def _rotary(x, cos, sin_signed):
    return x * cos + pltpu.roll(x, DK // 2, axis=1) * sin_signed


def _decay_beta(ab, a_log, dt_bias, is_decay):
    g = -jnp.exp(a_log) * _softplus(ab + dt_bias)
    return jnp.where(is_decay, g, jax.nn.sigmoid(ab))


def _gdn_qkv(conv):
    a = _silu(conv)
    qs, ks = [], []
    for h in range(N_HEADS):
        q = a[:, h * DK:(h + 1) * DK]
        k = a[:, QK_W + h * DK:QK_W + (h + 1) * DK]
        qs.append(q * (lax.rsqrt(jnp.sum(q * q, axis=-1, keepdims=True) + EPS) * DK ** -0.5))
        ks.append(k * lax.rsqrt(jnp.sum(k * k, axis=-1, keepdims=True) + EPS))
    return qs, ks, a[:, 2 * QK_W:]


def _proj_body(h_ref, gpre_ref, wm_ref, wab_ref, wabt_ref, cw_ref, alog_ref, dtb_ref, alogt_ref, dtbt_ref,
               cos_ref, sin_ref,
               rq_ref, rk_ref, rv_ref, rg_ref, gq_ref, gk_ref, gv_ref, gz_ref, gcol_ref, grow_ref, tail_ref,
               carry_ref):
    j = pl.program_id(1)
    tm = h_ref.shape[0]

    @pl.when(j == 0)
    def _():
        carry_ref[...] = jnp.zeros_like(carry_ref)

    xn = _rms(h_ref[...], gpre_ref[...]).astype(bf16)
    cos, sin = cos_ref[...], sin_ref[...]

    def ret_qk(p):
        for h in range(N_HEADS):
            sl = slice(h * DK, (h + 1) * DK)
            rq_ref[:, sl] = _rotary(p[:, h * DK:(h + 1) * DK], cos, sin).astype(bf16)
            rk_ref[:, sl] = (_rotary(p[:, QK_W + h * DK:QK_W + (h + 1) * DK], cos, sin) * DK ** -0.5).astype(bf16)

    def ret_v(p):
        rv_ref[...] = p.astype(bf16)

    def ret_gate(p):
        rg_ref[...] = _silu(p).astype(bf16)

    def gdn_z(p):
        gz_ref[...] = _silu(p).astype(bf16)

    def conv_silu(x, cols):
        carry = carry_ref[:, cols]
        cw = cw_ref[:, cols]
        conv = x * cw[CONV_W - 1:CONV_W]
        conv_top = x[:SUBLANES] * cw[CONV_W - 1:CONV_W]
        first = _iota(conv_top.shape, 0)
        for s in range(1, CONV_W):
            w = cw[CONV_W - 1 - s:CONV_W - s]
            xs = pltpu.roll(x, s, axis=0)
            conv = conv + xs * w
            conv_top = conv_top + jnp.where(first < s, pltpu.roll(carry, s, axis=0), xs[:SUBLANES]) * w
        carry_ref[:, cols] = x[tm - SUBLANES:]
        tail_ref[:, cols] = x[tm - SUBLANES:]
        return (slice(None), _silu(conv)), (slice(0, SUBLANES), _silu(conv_top))

    def gdn_qk(p):
        for rows, a in conv_silu(p, slice(0, 2 * QK_W)):
            for h in range(N_HEADS):
                q = a[:, h * DK:(h + 1) * DK]
                k = a[:, QK_W + h * DK:QK_W + (h + 1) * DK]
                q = q * (lax.rsqrt(jnp.sum(q * q, axis=-1, keepdims=True) + EPS) * DK ** -0.5)
                k = k * lax.rsqrt(jnp.sum(k * k, axis=-1, keepdims=True) + EPS)
                gq_ref[rows, h * DK:(h + 1) * DK] = q.astype(bf16)
                gk_ref[rows, h * DK:(h + 1) * DK] = k.astype(bf16)

    def gdn_v(p):
        for rows, a in conv_silu(p, slice(2 * QK_W, CONV_DIM)):
            gv_ref[rows, :] = a.astype(bf16)

    groups = ((_RQ, _RV, ret_qk), (_RV, _RG, ret_v), (_RG, _GQKV, ret_gate), (_GQKV, _GQKV + 2 * QK_W, gdn_qk),
              (_GQKV + 2 * QK_W, _GZ, gdn_v), (_GZ, _MAIN_W, gdn_z))
    pending = None
    for lo, hi, epilogue in groups:
        p = _dot(xn, wm_ref[:, lo:hi])
        if pending is not None:
            pending()
        pending = functools.partial(epilogue, p)
    pending()

    ab = _dot(xn, wab_ref[...])
    gcol_ref[...] = _decay_beta(ab, alog_ref[...], dtb_ref[...], _iota(ab.shape, 1) < N_HEADS)
    abt = lax.dot_general(wabt_ref[...], xn, NT, preferred_element_type=f32)
    grow_ref[...] = _decay_beta(abt, alogt_ref[...], dtbt_ref[...], _iota(abt.shape, 0) < N_HEADS)


def _proj(layer, h3d, gpre, wm, wab, wabt, cw, alog, dtb, alogt, dtbt, cos, sin, tm):
    b, l, _ = h3d.shape
    tok = lambda w: pl.BlockSpec((None, tm, w), lambda i, j: (i, j, 0))
    tab = pl.BlockSpec((tm, DK), lambda i, j: (j, 0))
    sds = lambda w, dt=bf16: jax.ShapeDtypeStruct((b, l, w), dt)
    res = lambda *shape: _resident(shape, layer)
    return pl.pallas_call(
        _proj_body, grid=(b, l // tm),
        in_specs=[tok(D_MODEL), res(1, D_MODEL), res(D_MODEL, _MAIN_W), res(D_MODEL, LANES),
                  res(SUBLANES, D_MODEL), res(CONV_W, CONV_DIM), res(1, LANES),
                  res(1, LANES), res(SUBLANES, 1), res(SUBLANES, 1), tab, tab],
        out_specs=[tok(QK_W), tok(QK_W), tok(V_W), tok(V_W), tok(QK_W), tok(QK_W), tok(V_W), tok(V_W),
                   tok(LANES), pl.BlockSpec((None, SUBLANES, tm), lambda i, j: (i, 0, j)),
                   pl.BlockSpec((None, SUBLANES, CONV_DIM), lambda i, j: (i, 0, 0))],
        out_shape=[sds(QK_W), sds(QK_W), sds(V_W), sds(V_W), sds(QK_W), sds(QK_W), sds(V_W), sds(V_W),
                   sds(LANES, f32), jax.ShapeDtypeStruct((b, SUBLANES, l), f32),
                   jax.ShapeDtypeStruct((b, SUBLANES, CONV_DIM), f32)],
        scratch_shapes=[pltpu.VMEM((SUBLANES, CONV_DIM), f32)],
        compiler_params=_params("parallel", "arbitrary"), name="proj",
    )(h3d, gpre, wm, wab, wabt, cw, alog, dtb, alogt, dtbt, cos, sin)


def _ret_body(q_ref, k_ref, v_ref, sg_ref, o_ref, sout_ref, s_ref, *, chunk):
    j = pl.program_id(1)
    c = chunk

    @pl.when(j == 0)
    def _():
        s_ref[...] = jnp.zeros_like(s_ref)

    row, col = _iota((c, c), 0), _iota((c, c), 1)
    dist = (row - col).astype(f32)
    pos = _iota((c, 1), 0).astype(f32)
    for h in range(N_HEADS):
        lg = LOG_GAMMA[h]
        decay = jnp.exp(jnp.where(row >= col, dist * lg, -jnp.inf))
        q_scale = jnp.exp((pos + 1.0) * lg)
        k_scale = jnp.exp((c - 1.0 - pos) * lg)
        for t in range(q_ref.shape[0] // c):
            rows = slice(t * c, (t + 1) * c)
            qh = q_ref[rows, h * DK:(h + 1) * DK]
            kh = k_ref[rows, h * DK:(h + 1) * DK]
            vh = v_ref[rows, h * DV:(h + 1) * DV]
            attn = lax.dot_general(qh, kh, NT, preferred_element_type=f32) * decay
            q_dec = (qh.astype(f32) * q_scale).astype(bf16)
            k_dec = (kh.astype(f32) * k_scale).astype(bf16)
            s = s_ref[h]
            o = _dot(q_dec, s.astype(bf16)) + _dot(attn.astype(bf16), vh)
            s_ref[h] = s * math.exp(c * lg) + lax.dot_general(k_dec, vh, TN, preferred_element_type=f32)
            o_ref[rows, h * DV:(h + 1) * DV] = (sg_ref[rows, h * DV:(h + 1) * DV].astype(f32) * _rms(o)).astype(bf16)

    @pl.when(j == pl.num_programs(1) - 1)
    def _():
        sout_ref[...] = s_ref[...]


def _retention(rq, rk, rv, sg, tile, chunk):
    b, l, _ = rq.shape
    tok = lambda w: pl.BlockSpec((None, tile, w), lambda i, j: (i, j, 0))
    st = pl.BlockSpec((None, N_HEADS, DK, DV), lambda i, j: (i, 0, 0, 0))
    return pl.pallas_call(
        functools.partial(_ret_body, chunk=chunk), grid=(b, l // tile),
        in_specs=[tok(QK_W), tok(QK_W), tok(V_W), tok(V_W)],
        out_specs=[tok(V_W), st],
        out_shape=[jax.ShapeDtypeStruct((b, l, V_W), bf16), jax.ShapeDtypeStruct((b, N_HEADS, DK, DV), f32)],
        scratch_shapes=[pltpu.VMEM((N_HEADS, DK, DV), f32)],
        compiler_params=_params("parallel", "arbitrary"), name="retention",
    )(rq, rk, rv, sg)


def _gdn_body(q_ref, k_ref, v_ref, sz_ref, gcol_ref, grow_ref, nw_ref, o_ref, sout_ref, s_ref):
    j = pl.program_id(1)
    c = CHUNK
    hc = N_HEADS * c

    @pl.when(j == 0)
    def _():
        s_ref[...] = jnp.zeros_like(s_ref)

    row, lane = _iota((c, hc), 0), _iota((c, hc), 1)
    lane_head, col = lane // c, lane % c
    lower, strict = row >= col, row > col
    head_mask = [(lane_head == h).astype(bf16) for h in range(N_HEADS)]
    block_diag = lambda x: jnp.concatenate([x * m for m in head_mask], axis=0)
    spread = lambda cols: functools.reduce(
        lambda acc, h: jnp.where(lane_head == h, cols[h], acc), range(N_HEADS - 1), cols[N_HEADS - 1])
    cum_rows = (_iota((c, c), 0) >= _iota((c, c), 1)).astype(f32)
    cum_cols = (_iota((c, hc), 0) <= (_iota((c, hc), 1) % c)).astype(f32)
    own_row = _iota((SUBLANES, hc), 0) == _iota((SUBLANES, hc), 1) // c
    zeros_k = jnp.zeros((c, DK), bf16)
    nw = nw_ref[...]

    chunks = range(q_ref.shape[0] // c)
    tok = lambda t: slice(t * c, (t + 1) * c)

    def chunk_setup(t):
        rows = tok(t)
        k_heads = [k_ref[rows, h * DK:(h + 1) * DK] for h in range(N_HEADS)]
        k_bd = jnp.concatenate([jnp.concatenate([k_heads[h] if g == h else zeros_k for g in range(N_HEADS)], axis=1)
                                for h in range(N_HEADS)], axis=0)
        qk = lax.dot_general(jnp.concatenate([q_ref[rows, :], k_ref[rows, :]], axis=0), k_bd, NT,
                             preferred_element_type=f32)
        gcol = gcol_ref[rows, :]
        cum = jnp.dot(cum_rows, gcol, preferred_element_type=f32, precision=HIGHEST)
        b_cols = [cum[:, h:h + 1] for h in range(N_HEADS)]
        beta_cols = [gcol[:, N_HEADS + h:N_HEADS + h + 1] for h in range(N_HEADS)]
        b_row = jnp.sum(jnp.where(own_row, jnp.dot(grow_ref[t], cum_cols, preferred_element_type=f32,
                                                   precision=HIGHEST), 0.0), axis=0, keepdims=True)
        decay = jnp.exp(jnp.where(lower, spread(b_cols) - b_row, -jnp.inf))
        x = jnp.where(strict, qk[c:] * spread(beta_cols) * decay, 0.0) * -1.0
        return (qk[:c] * decay, x, jnp.concatenate(b_cols, axis=0), jnp.concatenate(beta_cols, axis=0))

    attns, xs, b_colv, betav = zip(*[chunk_setup(t) for t in chunks])
    ns = xs
    x_bds = [block_diag(x.astype(bf16)) for x in xs]
    for _ in range(int(math.log2(c)) - 1):
        xs = [_dot(x.astype(bf16), x_bd) for x, x_bd in zip(xs, x_bds)]
        x_bds = [block_diag(x.astype(bf16)) for x in xs]
        ns = [n + x + _dot(n.astype(bf16), x_bd) for n, x, x_bd in zip(ns, xs, x_bds)]

    def chunk_wy(t):
        rows = tok(t)
        b_col, beta = b_colv[t], betav[t]
        kst = jnp.concatenate([k_ref[rows, h * DK:(h + 1) * DK] for h in range(N_HEADS)], axis=0)
        qst = jnp.concatenate([q_ref[rows, h * DK:(h + 1) * DK] for h in range(N_HEADS)], axis=0)
        vst = jnp.concatenate([v_ref[rows, h * DV:(h + 1) * DV] for h in range(N_HEADS)], axis=0).astype(f32)
        n_bd = block_diag(ns[t].astype(bf16))
        kf = kst.astype(f32)
        e_b = jnp.exp(b_col)
        ru, rw = vst * beta, kf * (beta * e_b)
        u = ru + _dot(n_bd, ru.astype(bf16))
        w = (rw + _dot(n_bd, rw.astype(bf16))).astype(bf16)
        q_dec = (qst.astype(f32) * e_b).astype(bf16)
        return u, w, q_dec, kf, block_diag(attns[t].astype(bf16))

    wy = [chunk_wy(t) for t in chunks]
    for t in chunks:
        rows = tok(t)
        u, w, q_dec, kf, attn_bd = wy[t]
        b_col = b_colv[t]
        v_new, o_inter = [], []
        for h in range(N_HEADS):
            hs = slice(h * c, (h + 1) * c)
            r = _dot(jnp.concatenate([w[hs], q_dec[hs]], axis=0), s_ref[h].astype(bf16))
            v_new.append(u[hs] - r[:c])
            o_inter.append(r[c:])
        vb = jnp.concatenate(v_new, axis=0).astype(bf16)
        o = jnp.concatenate(o_inter, axis=0) + _dot(attn_bd, vb)
        for h in range(N_HEADS):
            hs = slice(h * c, (h + 1) * c)
            b_last = b_col[(h + 1) * c - 1:(h + 1) * c]
            k_dec = (kf[hs] * jnp.exp(b_last - b_col[hs])).astype(bf16)
            s_ref[h] = s_ref[h] * jnp.exp(b_last) + lax.dot_general(k_dec, vb[hs], TN, preferred_element_type=f32)
            o_ref[rows, h * DV:(h + 1) * DV] = (
                _rms(o[hs], nw) * sz_ref[rows, h * DV:(h + 1) * DV].astype(f32)).astype(bf16)

    @pl.when(j == pl.num_programs(1) - 1)
    def _():
        sout_ref[...] = s_ref[...]


def _gated_delta(layer, gq, gk, gv, sz, gcol, grow4, nw, tile):
    b, l, _ = gq.shape
    tok = lambda w: pl.BlockSpec((None, tile, w), lambda i, j: (i, j, 0))
    st = pl.BlockSpec((None, N_HEADS, DK, DV), lambda i, j: (i, 0, 0, 0))
    return pl.pallas_call(
        _gdn_body, grid=(b, l // tile),
        in_specs=[tok(QK_W), tok(QK_W), tok(V_W), tok(V_W), tok(LANES),
                  pl.BlockSpec((None, tile // CHUNK, SUBLANES, CHUNK), lambda i, j: (i, j, 0, 0)),
                  _resident((1, DV), layer)],
        out_specs=[tok(V_W), st],
        out_shape=[jax.ShapeDtypeStruct((b, l, V_W), bf16), jax.ShapeDtypeStruct((b, N_HEADS, DK, DV), f32)],
        scratch_shapes=[pltpu.VMEM((N_HEADS, DK, DV), f32)],
        compiler_params=_params("parallel", "arbitrary"), name="gated_delta",
    )(gq, gk, gv, sz, gcol, grow4, nw)


def _softmax(s):
    e = jnp.exp(s - jnp.max(s, axis=-1, keepdims=True))
    return e / jnp.sum(e, axis=-1, keepdims=True)


def _merge_tail(x, xn, o_ret, o_gdn, mem_proj, wg_ref, wr_ref, wd_ref, wo_ref, gpost):
    gates = jax.nn.sigmoid(_dot(xn, wg_ref[...]))
    merged = (gates[:, :D_MODEL] * _dot(o_ret, wr_ref[...])
              + gates[:, D_MODEL:2 * D_MODEL] * _dot(o_gdn, wd_ref[...])
              + gates[:, 2 * D_MODEL:] * mem_proj)
    y = _dot(merged.astype(bf16), wo_ref[...])
    return x + _rms(y, gpost)


def _merge_body(h_ref, gpre_ref, wq_ref, wg_ref, mk_ref, mv_ref, oret_ref, ogdn_ref,
                wr_ref, wd_ref, wm_ref, wo_ref, gpost_ref, o_ref):
    x = h_ref[...]
    xn = _rms(x, gpre_ref[...]).astype(bf16)
    mq = _dot(xn, wq_ref[...]).astype(bf16)
    mem_proj = None
    for h in range(N_HEADS):
        sl = slice(h * HD_X, (h + 1) * HD_X)
        s = lax.dot_general(mq[:, sl], mk_ref[:, sl], NT, preferred_element_type=f32) * HD_X ** -0.5
        o = _dot(_softmax(s).astype(bf16), mv_ref[:, sl])
        part = _dot(o.astype(bf16), wm_ref[sl, :])
        mem_proj = part if mem_proj is None else mem_proj + part
    o_ref[...] = _merge_tail(x, xn, oret_ref[...], ogdn_ref[...], mem_proj, wg_ref, wr_ref, wd_ref, wo_ref,
                             gpost_ref[...])


def _merge(layer, h3d, gpre, wq, wg, mk, mv, o_ret, o_gdn, wr, wd, wm, wo, gpost, tm):
    b, l, _ = h3d.shape
    tok = pl.BlockSpec((None, tm, D_MODEL), lambda i, j: (i, j, 0))
    mem = pl.BlockSpec((None, None, N_MEM, D_MODEL), lambda i, j: (layer, i, 0, 0))
    sq = _resident((D_MODEL, D_MODEL), layer)
    return pl.pallas_call(
        _merge_body, grid=(b, l // tm),
        in_specs=[tok, _resident((1, D_MODEL), layer), _packed_mq(layer), _packed_gates(layer), mem, mem,
                  tok, tok, sq, sq, sq, sq, _resident((1, D_MODEL), layer)],
        out_specs=tok, out_shape=jax.ShapeDtypeStruct((b, l, D_MODEL), f32),
        compiler_params=_params("parallel", "parallel"), name="merge",
    )(h3d, gpre, wq, wg, mk, mv, o_ret, o_gdn, wr, wd, wm, wo, gpost)


def _sproj_body(h_ref, gpre_ref, wm_ref, wab_ref, wq_ref, cw_ref, alog_ref, dtb_ref, cos_ref, sin_ref, buf_ref,
                rq_ref, rk_ref, rv_ref, rg_ref, gq_ref, gk_ref, gv_ref, gz_ref, gcol_ref, mq_ref, nbuf_ref):
    xn = _rms(h_ref[...], gpre_ref[...]).astype(bf16)
    p = _dot(xn, wm_ref[...])
    cos, sin = cos_ref[...], sin_ref[...]
    for h in range(N_HEADS):
        sl = slice(h * DK, (h + 1) * DK)
        rq_ref[:, sl] = _rotary(p[:, _RQ + h * DK:_RQ + (h + 1) * DK], cos, sin)
        rk_ref[:, sl] = _rotary(p[:, _RK + h * DK:_RK + (h + 1) * DK], cos, sin) * DK ** -0.5
    rv_ref[...] = p[:, _RV:_RG]
    rg_ref[...] = _silu(p[:, _RG:_GQKV])
    gz_ref[...] = _silu(p[:, _GZ:_MAIN_W])
    x = p[:, _GQKV:_GZ]
    cw = cw_ref[...]
    conv = x * cw[CONV_W - 1:CONV_W]
    for s in range(CONV_W - 1):
        prev = buf_ref[:, s * CONV_DIM:(s + 1) * CONV_DIM]
        conv = conv + prev * cw[s:s + 1]
        if s > 0:
            nbuf_ref[:, (s - 1) * CONV_DIM:s * CONV_DIM] = prev
    nbuf_ref[:, (CONV_W - 2) * CONV_DIM:] = x
    qs, ks, v = _gdn_qkv(conv)
    for h in range(N_HEADS):
        gq_ref[:, h * DK:(h + 1) * DK] = qs[h]
        gk_ref[:, h * DK:(h + 1) * DK] = ks[h]
    gv_ref[...] = v
    ab = _dot(xn, wab_ref[...])
    gcol_ref[...] = _decay_beta(ab, alog_ref[...], dtb_ref[...], _iota(ab.shape, 1) < N_HEADS)
    mq_ref[...] = _dot(xn, wq_ref[...])


def _sample_proj(layer, h2d, gpre, wm, wab, wq, cw, alog, dtb, cos, sin, buf2d):
    n = h2d.shape[0]
    sds = lambda w: jax.ShapeDtypeStruct((n, w), f32)
    res = lambda *shape: _resident(shape, layer)
    widths = (QK_W, QK_W, V_W, V_W, QK_W, QK_W, V_W, V_W, LANES, D_MODEL, (CONV_W - 1) * CONV_DIM)
    return pl.pallas_call(
        _sproj_body, grid=(1,),
        in_specs=[_resident((n, D_MODEL)), res(1, D_MODEL), res(D_MODEL, _MAIN_W), res(D_MODEL, LANES),
                  _packed_mq(layer), res(CONV_W, CONV_DIM), res(1, LANES), res(1, LANES),
                  _resident((1, DK)), _resident((1, DK)), _resident((n, (CONV_W - 1) * CONV_DIM))],
        out_specs=[pl.BlockSpec((n, w), lambda i: (0, 0)) for w in widths],
        out_shape=[sds(w) for w in widths],
        compiler_params=_params("arbitrary"), name="sample_proj",
    )(h2d, gpre, wm, wab, wq, cw, alog, dtb, cos, sin, buf2d)


def _sstate_body(cols_ref, rv_ref, gv_ref, gcol_ref, sret_ref, sgdn_ref, nret_ref, ngdn_ref, oret_ref, ogdn_ref):
    for j in range(rv_ref.shape[0]):
        for h in range(N_HEADS):
            vs = slice(h * DV, (h + 1) * DV)
            q = cols_ref[0, h, :, j:j + 1]
            k = cols_ref[0, N_HEADS + h, :, j:j + 1]
            s_new = sret_ref[j, h] * math.exp(LOG_GAMMA[h]) + k * rv_ref[j:j + 1, vs]
            nret_ref[j, h] = s_new
            oret_ref[j:j + 1, vs] = jnp.sum(q * s_new, axis=0, keepdims=True)
            q = cols_ref[0, 2 * N_HEADS + h, :, j:j + 1]
            k = cols_ref[0, 3 * N_HEADS + h, :, j:j + 1]
            eg = jnp.exp(gcol_ref[j:j + 1, h:h + 1])
            beta = gcol_ref[j:j + 1, N_HEADS + h:N_HEADS + h + 1]
            s = sgdn_ref[j, h]
            ks = jnp.sum(k * s, axis=0, keepdims=True)
            v_new = beta * (gv_ref[j:j + 1, vs] - eg * ks)
            s_new = s * eg + k * v_new
            ngdn_ref[j, h] = s_new
            ogdn_ref[j:j + 1, vs] = jnp.sum(q * s_new, axis=0, keepdims=True)


def _sattn_body(mq_ref, k_ref, v_ref, o_ref):
    for j in range(mq_ref.shape[0]):
        s = jnp.sum(k_ref[j] * mq_ref[j][None], axis=-1, keepdims=True) * HD_X ** -0.5
        e = jnp.exp(s - jnp.max(s, axis=0, keepdims=True))
        p = e / jnp.sum(e, axis=0, keepdims=True)
        o_ref[j] = jnp.sum(p * v_ref[j], axis=0)


def _smerge_body(h_ref, gpre_ref, wg_ref, oret_ref, rg_ref, ogdn_ref, gz_ref, nw_ref, omem_ref,
                 wr_ref, wd_ref, wm_ref, wo_ref, gpost_ref, o_ref):
    x = h_ref[...]
    xn = _rms(x, gpre_ref[...]).astype(bf16)
    nw = nw_ref[...]
    rets, gdns = [], []
    for h in range(N_HEADS):
        vs = slice(h * DV, (h + 1) * DV)
        rets.append(rg_ref[:, vs] * _rms(oret_ref[:, vs]))
        gdns.append(_rms(ogdn_ref[:, vs], nw) * gz_ref[:, vs])
    o_ret = jnp.concatenate(rets, axis=-1).astype(bf16)
    o_gdn = jnp.concatenate(gdns, axis=-1).astype(bf16)
    mem_proj = _dot(omem_ref[...].astype(bf16), wm_ref[...])
    o_ref[...] = _merge_tail(x, xn, o_ret, o_gdn, mem_proj, wg_ref, wr_ref, wd_ref, wo_ref, gpost_ref[...])


def _sample_merge(layer, h2d, gpre, wg, o_ret, rg, o_gdn, gz, nw, o_mem, wr, wd, wm, wo, gpost):
    n = h2d.shape[0]
    act = _resident((n, D_MODEL))
    sq = _resident((D_MODEL, D_MODEL), layer)
    return pl.pallas_call(
        _smerge_body, grid=(1,),
        in_specs=[act, _resident((1, D_MODEL), layer), _packed_gates(layer), act, act, act, act,
                  _resident((1, DV), layer), act, sq, sq, sq, sq, _resident((1, D_MODEL), layer)],
        out_specs=pl.BlockSpec((n, D_MODEL), lambda i: (0, 0)), out_shape=jax.ShapeDtypeStruct(h2d.shape, f32),
        compiler_params=_params("arbitrary"), name="sample_merge",
    )(h2d, gpre, wg, o_ret, rg, o_gdn, gz, nw, o_mem, wr, wd, wm, wo, gpost)


def _rope_tables(pos):
    half = DK // 2
    inv_freq = ROPE_BASE ** (-jnp.arange(half, dtype=f32) / half)
    ang = pos.astype(f32)[:, None] * inv_freq[None, :]
    cos, sin = jnp.cos(ang), jnp.sin(ang)
    return jnp.concatenate([cos, cos], axis=-1), jnp.concatenate([-sin, sin], axis=-1)


def _row_tile(n, want):
    t = min(n, want)
    assert n % t == 0, (n, t)
    return t


def kernel(x_prompt, x_sample, mem_prompt, state_ret, state_gdn, state_conv, cache_mem_k, cache_mem_v, norm_ffn1_pre, norm_ffn1_post, ffn1_w_in, ffn1_w_out, norm_mix_pre, norm_mix_post, w_in, gdn_conv_w, gdn_a_log, gdn_dt_bias, gdn_norm, norm_mem, w_mem_k, w_mem_v, w_branch_ret, w_branch_gdn, w_branch_mem, w_out, norm_ffn2_pre, norm_ffn2_post, ffn2_w_in, ffn2_w_out):
    bp, sp, _ = x_prompt.shape
    ns, ss, _ = x_sample.shape
    depth = w_in.shape[0]
    assert ss == 1 and sp % CHUNK == 0
    tm_ffn = _row_tile(bp * sp, 512)
    tm_seq = _row_tile(sp, 512)

    cos_p, sin_p = _rope_tables(jnp.arange(sp))
    cos_s, sin_s = _rope_tables(PAST_LEN + jnp.arange(ss))
    row = lambda v: v.reshape(depth, 1, -1).astype(f32)
    pad_lanes = lambda v: jnp.pad(v.astype(f32), ((0, 0), (0, LANES - v.shape[1]))).reshape(depth, 1, LANES)
    pad_rows = lambda v: jnp.pad(v.astype(f32), ((0, 0), (0, SUBLANES - v.shape[1]))).reshape(depth, SUBLANES, 1)
    w_in_t = jnp.swapaxes(w_in, 1, 2)
    w_main = _w_in_cols(w_in_t, 0, _MAIN_W)
    w_mq = _w_in_cols(w_in_t, _AB_END, D_MODEL)
    w_gates = _w_in_cols(w_in_t, _MQ_END, 3 * D_MODEL)
    w_ab_row = w_in_t[:, _MAIN_W:_AB_END].astype(bf16)
    w_ab_col = jnp.pad(jnp.swapaxes(w_ab_row, 1, 2), ((0, 0), (0, 0), (0, LANES - 2 * N_HEADS)))
    f1_in, f1_out, f2_in, f2_out = (w.astype(bf16) for w in (ffn1_w_in, ffn1_w_out, ffn2_w_in, ffn2_w_out))
    wr, wd, wm, wo = (w.astype(bf16) for w in (w_branch_ret, w_branch_gdn, w_branch_mem, w_out))
    cw = gdn_conv_w.astype(f32)
    alog, dtb = pad_lanes(gdn_a_log), pad_lanes(gdn_dt_bias)
    alog_t, dtb_t = pad_rows(gdn_a_log), pad_rows(gdn_dt_bias)
    nw = row(gdn_norm)
    g_f1pre, g_f1post, g_f2pre, g_f2post = (row(g) for g in (norm_ffn1_pre, norm_ffn1_post, norm_ffn2_pre,
                                                              norm_ffn2_post))
    g_mpre, g_mpost = row(norm_mix_pre), row(norm_mix_post)

    mk, mv, mk_b, mv_b = _memkv(mem_prompt, row(norm_mem), w_mem_k.astype(bf16), w_mem_v.astype(bf16))
    hp = x_prompt
    hs = x_sample.reshape(ns, D_MODEL)
    outs = [[] for _ in range(4)]
    new_states = None
    ret_chunk = _row_tile(tm_seq, 256)
    hosted_per_step = ns // (2 * (bp * sp // tm_ffn))
    assert hosted_per_step * 2 * (bp * sp // tm_ffn) == ns
    for l in range(depth):
        h1s = _ffn(l, hs, g_f1pre, f1_in, f1_out, g_f1post, ns)
        s_rq, s_rk, s_rv, s_rg, s_gq, s_gk, s_gv, s_gz, s_gcol, mq, nbuf = _sample_proj(
            l, h1s, g_mpre, w_main, w_ab_col, w_mq, cw, alog, dtb, cos_s, sin_s,
            state_conv[l].reshape(ns, (CONV_W - 1) * CONV_DIM))
        outs[3].append(nbuf.reshape(ns, CONV_W - 1, CONV_DIM))
        groups = ns // hosted_per_step
        by_step = lambda a: a.reshape((groups, hosted_per_step) + a.shape[1:])
        cols = jnp.stack([s_rq, s_rk, s_gq, s_gk]).reshape(4, groups, hosted_per_step, N_HEADS, DK)
        cols = cols.transpose(1, 0, 3, 4, 2).reshape(groups, 4 * N_HEADS, DK, hosted_per_step)
        host = lambda part: (part, 2, hosted_per_step, by_step(mq.reshape(ns, N_HEADS, HD_X)), cache_mem_k,
                             cache_mem_v, cols, by_step(s_rv), by_step(s_gv), by_step(s_gcol), state_ret,
                             state_gdn, new_states)

        h1, o_mem_a, n_ret, n_gdn, so_ret_a, so_gdn_a = _ffn(
            l, hp.reshape(bp * sp, D_MODEL), g_f1pre, f1_in, f1_out, g_f1post, tm_ffn, hosted=host(0))
        new_states = (n_ret, n_gdn)
        h1 = h1.reshape(bp, sp, D_MODEL)
        rq, rk, rv, rg, gq, gk, gv, gz, gcol, grow, tail = _proj(
            l, h1, g_mpre, w_main, w_ab_col, w_ab_row, cw, alog, dtb, alog_t, dtb_t, cos_p, sin_p, tm_seq)
        o_ret, s_ret = _retention(rq, rk, rv, rg, tm_seq, ret_chunk)
        grow4 = grow.reshape(bp, SUBLANES, sp // CHUNK, CHUNK).transpose(0, 2, 1, 3)
        o_gdn, s_gdn = _gated_delta(l, gq, gk, gv, gz, gcol, grow4, nw, tm_seq)
        h2 = _merge(l, h1, g_mpre, w_mq, w_gates, mk_b, mv_b, o_ret, o_gdn, wr, wd, wm, wo, g_mpost, tm_seq)
        hp, o_mem_b, n_ret, n_gdn, so_ret_b, so_gdn_b = _ffn(
            l, h2.reshape(bp * sp, D_MODEL), g_f2pre, f2_in, f2_out, g_f2post, tm_ffn, hosted=host(1))
        new_states = (n_ret, n_gdn)
        hp = hp.reshape(bp, sp, D_MODEL)
        outs[0].append(s_ret)
        outs[1].append(s_gdn)
        outs[2].append(tail[:, SUBLANES - (CONV_W - 1):])

        join = lambda a, b: jnp.concatenate([a, b], axis=0).reshape(ns, -1)
        o_mem, so_ret, so_gdn = join(o_mem_a, o_mem_b), join(so_ret_a, so_ret_b), join(so_gdn_a, so_gdn_b)
        h2s = _sample_merge(l, h1s, g_mpre, w_gates, so_ret, s_rg, so_gdn, s_gz, nw, o_mem, wr, wd, wm, wo, g_mpost)
        hs = _ffn(l, h2s, g_f2pre, f2_in, f2_out, g_f2post, ns)

    ret_p, gdn_p, conv_p, conv_s = (jnp.stack(o) for o in outs)
    return (hp, hs.reshape(ns, ss, D_MODEL), ret_p, gdn_p, conv_p, mk, mv) + new_states + (conv_s,)
```

```python
import functools
import math

import jax
import jax.numpy as jnp
from jax import lax
from jax.experimental import pallas as pl
from jax.experimental.pallas import tpu as pltpu

f32, bf16 = jnp.float32, jnp.bfloat16

D_MODEL = 1024
N_HEADS = 4
DK = 128
DV = 256
QK_W = N_HEADS * DK
V_W = N_HEADS * DV
CONV_W = 4
CONV_DIM = 2 * QK_W + V_W
HD_X = 256
N_MEM = 256
D_FF = 2816
EPS = 1e-6
ROPE_BASE = 10000.0
PAST_LEN = 16384
CHUNK = 64
LANES = 128
SUBLANES = 8
VMEM_LIMIT = 56 * 1024 * 1024
LOG_GAMMA = tuple(math.log1p(-2.0 ** (-5.0 - h)) for h in range(N_HEADS))
HIGHEST = lax.Precision.HIGHEST
NT = (((1,), (1,)), ((), ()))
TN = (((0,), (0,)), ((), ()))


def _params(*sem):
    return pltpu.CompilerParams(dimension_semantics=sem, vmem_limit_bytes=VMEM_LIMIT)


def _resident(shape, layer=None, col_block=0):
    if layer is None:
        return pl.BlockSpec(shape, lambda *_: (0,) * len(shape), pipeline_mode=pl.Buffered(1))
    index = (layer,) + (0,) * (len(shape) - 1) + (col_block,)
    return pl.BlockSpec((None,) + shape, lambda *_: index, pipeline_mode=pl.Buffered(1))


def _rms(x, gain=None):
    y = x * lax.rsqrt(jnp.mean(x * x, axis=-1, keepdims=True) + EPS)
    return y if gain is None else y * gain


def _silu(x):
    return x * jax.nn.sigmoid(x)


def _softplus(x):
    return jnp.maximum(x, 0.0) + jnp.log1p(jnp.exp(-jnp.abs(x)))


def _dot(a, b):
    return jnp.dot(a, b, preferred_element_type=f32)


def _iota(shape, dim):
    return lax.broadcasted_iota(jnp.int32, shape, dim)


def _ffn_body(x_ref, gpre_ref, wgu_ref, wo_ref, gpost_ref, o_ref):
    x = x_ref[...]
    xn = _rms(x, gpre_ref[...]).astype(bf16)
    h = _dot(xn, wgu_ref[...])
    act = (_silu(h[:, :D_FF]) * h[:, D_FF:]).astype(bf16)
    y = _dot(act, wo_ref[...])
    o_ref[...] = x + 0.5 * _rms(y, gpost_ref[...])


def _ffn_hosting_body(x_ref, gpre_ref, wgu_ref, wo_ref, gpost_ref, mq_ref, k_ref, v_ref,
                      cols_ref, rv_ref, gv_ref, gcol_ref, sret_ref, sgdn_ref, *refs):
    o_ref, omem_ref, nret_ref, ngdn_ref, oret_ref, ogdn_ref = refs[-6:]
    _ffn_body(x_ref, gpre_ref, wgu_ref, wo_ref, gpost_ref, o_ref)
    _sattn_body(mq_ref, k_ref, v_ref, omem_ref)
    _sstate_body(cols_ref, rv_ref, gv_ref, gcol_ref, sret_ref, sgdn_ref, nret_ref, ngdn_ref, oret_ref, ogdn_ref)


def _ffn(layer, x2d, gpre, wgu, wo, gpost, tm, hosted=None):
    t = x2d.shape[0]
    steps = t // tm
    row = pl.BlockSpec((tm, D_MODEL), lambda i: (i, 0))
    in_specs = [row, _resident((1, D_MODEL), layer), _resident((D_MODEL, 2 * D_FF), layer),
                _resident((D_FF, D_MODEL), layer), _resident((1, D_MODEL), layer)]
    out_row = jax.ShapeDtypeStruct((t, D_MODEL), f32)
    if hosted is None:
        return pl.pallas_call(
            _ffn_body, grid=(steps,), in_specs=in_specs, out_specs=row, out_shape=out_row,
            compiler_params=_params("parallel"), name="ffn",
        )(x2d, gpre, wgu, wo, gpost)
    part, parts, ps, mq, mem_k, mem_v, cols, rv, gv, gcol, s_ret, s_gdn, prev = hosted
    depth, n = s_ret.shape[:2]
    assert ps * parts * steps == n, (n, parts, steps, ps)
    first = part * steps
    grp = lambda *shape: pl.BlockSpec((None,) + shape, lambda i: (first + i,) + (0,) * len(shape))
    out_grp = lambda *shape: pl.BlockSpec((None,) + shape, lambda i: (i,) + (0,) * len(shape))
    mem = pl.BlockSpec((None, ps, N_MEM, N_HEADS, HD_X), lambda i: (layer, first + i, 0, 0, 0))
    st = pl.BlockSpec((None, ps, N_HEADS, DK, DV), lambda i: (layer, first + i, 0, 0, 0))
    st_sds = jax.ShapeDtypeStruct((depth, n, N_HEADS, DK, DV), f32)
    grp_sds = lambda *shape: jax.ShapeDtypeStruct((steps,) + shape, f32)
    prev, prev_specs, aliases = _alias_prev(14, prev)
    aliases = {k: v + 2 for k, v in aliases.items()}
    return pl.pallas_call(
        _ffn_hosting_body, grid=(steps,),
        in_specs=in_specs + [grp(ps, N_HEADS, HD_X), mem, mem,
                             grp(ps * 4 * N_HEADS, DK),
                             grp(ps, V_W), grp(ps, V_W), grp(ps, LANES), st, st] + prev_specs,
        out_specs=[row, out_grp(ps, N_HEADS, HD_X), st, st, out_grp(ps, V_W), out_grp(ps, V_W)],
        out_shape=[out_row, grp_sds(ps, N_HEADS, HD_X), st_sds, st_sds, grp_sds(ps, V_W), grp_sds(ps, V_W)],
        input_output_aliases=aliases, compiler_params=_params("parallel"), name="ffn_hosting",
    )(x2d, gpre, wgu, wo, gpost, mq, mem_k, mem_v, cols, rv, gv, gcol, s_ret, s_gdn, *prev)


def _stacked_out(layer, depth, shape, dtype, index_map):
    spec = pl.BlockSpec((None,) + shape, lambda *i: (layer,) + index_map(*i))
    return spec, lambda full: jax.ShapeDtypeStruct((depth,) + full, dtype)


def _alias_prev(n_inputs, prev):
    prev = [] if prev is None else list(prev)
    return prev, [pl.BlockSpec(memory_space=pl.ANY)] * len(prev), {n_inputs + i: i for i in range(len(prev))}


def _memkv_body(m_ref, g_ref, wk_ref, wv_ref, k_ref, v_ref, kb_ref, vb_ref):
    mn = _rms(m_ref[...], g_ref[...]).astype(bf16)
    k = _dot(mn, wk_ref[...])
    v = _dot(mn, wv_ref[...])
    for h in range(N_HEADS):
        k_ref[:, h, :] = k[:, h * HD_X:(h + 1) * HD_X]
        v_ref[:, h, :] = v[:, h * HD_X:(h + 1) * HD_X]
    kb_ref[...] = k.astype(bf16)
    vb_ref[...] = v.astype(bf16)


def _memkv(mem, g, wk, wv):
    b = mem.shape[0]
    depth = wk.shape[0]
    per_layer = lambda shape: pl.BlockSpec((None,) + shape, lambda l, i: (l,) + (0,) * len(shape))
    out5 = pl.BlockSpec((None, None, N_MEM, N_HEADS, HD_X), lambda l, i: (l, i, 0, 0, 0))
    outb = pl.BlockSpec((None, None, N_MEM, D_MODEL), lambda l, i: (l, i, 0, 0))
    sds5 = jax.ShapeDtypeStruct((depth, b, N_MEM, N_HEADS, HD_X), f32)
    sdsb = jax.ShapeDtypeStruct((depth, b, N_MEM, D_MODEL), bf16)
    return pl.pallas_call(
        _memkv_body, grid=(depth, b),
        in_specs=[pl.BlockSpec((None, N_MEM, D_MODEL), lambda l, i: (i, 0, 0)), per_layer((1, D_MODEL)),
                  per_layer((D_MODEL, D_MODEL)), per_layer((D_MODEL, D_MODEL))],
        out_specs=[out5, out5, outb, outb], out_shape=[sds5, sds5, sdsb, sdsb],
        compiler_params=_params("parallel", "parallel"), name="memkv",
    )(mem, g, wk, wv)


_RQ, _RK, _RV, _RG, _GQKV, _GZ, _MAIN_W = 0, 512, 1024, 2048, 3072, 5120, 6144
_AB_END = _MAIN_W + 2 * N_HEADS
_MQ_END = _AB_END + D_MODEL


def _packed_mq(layer):
    return _resident((D_MODEL, D_MODEL), layer)


def _packed_gates(layer):
    return _resident((D_MODEL, 3 * D_MODEL), layer)


def _w_in_cols_body(wt_ref, o_ref):
    o_ref[...] = wt_ref[0].T.astype(bf16)


def _w_in_cols(w_in_t, first, width, blk=1024):
    depth = w_in_t.shape[0]
    assert first % SUBLANES == 0 and width % blk == 0
    return pl.pallas_call(
        _w_in_cols_body, grid=(depth, width // blk),
        in_specs=[pl.BlockSpec((pl.Element(1), pl.Element(blk), pl.Element(D_MODEL)),
                               lambda l, c: (l, pl.multiple_of(first + c * blk, SUBLANES), 0))],
        out_specs=pl.BlockSpec((None, D_MODEL, blk), lambda l, c: (l, 0, c)),
        out_shape=jax.ShapeDtypeStruct((depth, D_MODEL, width), bf16),
        compiler_params=_params("parallel", "parallel"), name="w_in_cols",
    )(w_in_t)


def _rotary(x, cos, sin_signed):
    return x * cos + pltpu.roll(x, DK // 2, axis=1) * sin_signed


def _decay_beta(ab, a_log, dt_bias, is_decay):
    g = -jnp.exp(a_log) * _softplus(ab + dt_bias)
    return jnp.where(is_decay, g, jax.nn.sigmoid(ab))


def _gdn_qkv(conv):
    a = _silu(conv)
    qs, ks = [], []
    for h in range(N_HEADS):
        q = a[:, h * DK:(h + 1) * DK]
        k = a[:, QK_W + h * DK:QK_W + (h + 1) * DK]
        qs.append(q * (lax.rsqrt(jnp.sum(q * q, axis=-1, keepdims=True) + EPS) * DK ** -0.5))
        ks.append(k * lax.rsqrt(jnp.sum(k * k, axis=-1, keepdims=True) + EPS))
    return qs, ks, a[:, 2 * QK_W:]


def _proj_body(h_ref, gpre_ref, wm_ref, wab_ref, wabt_ref, cw_ref, alog_ref, dtb_ref, alogt_ref, dtbt_ref,
               cos_ref, sin_ref,
               rq_ref, rk_ref, rv_ref, rg_ref, gq_ref, gk_ref, gv_ref, gz_ref, gcol_ref, grow_ref, tail_ref,
               carry_ref):
    j = pl.program_id(1)
    tm = h_ref.shape[0]

    @pl.when(j == 0)
    def _():
        carry_ref[...] = jnp.zeros_like(carry_ref)

    xn = _rms(h_ref[...], gpre_ref[...]).astype(bf16)
    cos, sin = cos_ref[...], sin_ref[...]

    def ret_qk(p):
        for h in range(N_HEADS):
            sl = slice(h * DK, (h + 1) * DK)
            rq_ref[:, sl] = _rotary(p[:, h * DK:(h + 1) * DK], cos, sin).astype(bf16)
            rk_ref[:, sl] = (_rotary(p[:, QK_W + h * DK:QK_W + (h + 1) * DK], cos, sin) * DK ** -0.5).astype(bf16)

    def ret_v(p):
        rv_ref[...] = p.astype(bf16)

    def ret_gate(p):
        rg_ref[...] = _silu(p).astype(bf16)

    def gdn_z(p):
        gz_ref[...] = _silu(p).astype(bf16)

    def conv_silu(x, cols):
        carry = carry_ref[:, cols]
        cw = cw_ref[:, cols]
        conv = x * cw[CONV_W - 1:CONV_W]
        conv_top = x[:SUBLANES] * cw[CONV_W - 1:CONV_W]
        first = _iota(conv_top.shape, 0)
        for s in range(1, CONV_W):
            w = cw[CONV_W - 1 - s:CONV_W - s]
            xs = pltpu.roll(x, s, axis=0)
            conv = conv + xs * w
            conv_top = conv_top + jnp.where(first < s, pltpu.roll(carry, s, axis=0), xs[:SUBLANES]) * w
        carry_ref[:, cols] = x[tm - SUBLANES:]
        tail_ref[:, cols] = x[tm - SUBLANES:]
        return (slice(None), _silu(conv)), (slice(0, SUBLANES), _silu(conv_top))

    def gdn_qk(p):
        for rows, a in conv_silu(p, slice(0, 2 * QK_W)):
            for h in range(N_HEADS):
                q = a[:, h * DK:(h + 1) * DK]
                k = a[:, QK_W + h * DK:QK_W + (h + 1) * DK]
                q = q * (lax.rsqrt(jnp.sum(q * q, axis=-1, keepdims=True) + EPS) * DK ** -0.5)
                k = k * lax.rsqrt(jnp.sum(k * k, axis=-1, keepdims=True) + EPS)
                gq_ref[rows, h * DK:(h + 1) * DK] = q.astype(bf16)
                gk_ref[rows, h * DK:(h + 1) * DK] = k.astype(bf16)

    def gdn_v(p):
        for rows, a in conv_silu(p, slice(2 * QK_W, CONV_DIM)):
            gv_ref[rows, :] = a.astype(bf16)

    groups = ((_RQ, _RV, ret_qk), (_RV, _RG, ret_v), (_RG, _GQKV, ret_gate), (_GQKV, _GQKV + 2 * QK_W, gdn_qk),
              (_GQKV + 2 * QK_W, _GZ, gdn_v), (_GZ, _MAIN_W, gdn_z))
    pending = None
    for lo, hi, epilogue in groups:
        p = _dot(xn, wm_ref[:, lo:hi])
        if pending is not None:
            pending()
        pending = functools.partial(epilogue, p)
    pending()

    ab = _dot(xn, wab_ref[...])
    gcol_ref[...] = _decay_beta(ab, alog_ref[...], dtb_ref[...], _iota(ab.shape, 1) < N_HEADS)
    abt = lax.dot_general(wabt_ref[...], xn, NT, preferred_element_type=f32)
    grow_ref[...] = _decay_beta(abt, alogt_ref[...], dtbt_ref[...], _iota(abt.shape, 0) < N_HEADS)


def _proj(layer, h3d, gpre, wm, wab, wabt, cw, alog, dtb, alogt, dtbt, cos, sin, tm):
    b, l, _ = h3d.shape
    tok = lambda w: pl.BlockSpec((None, tm, w), lambda i, j: (i, j, 0))
    tab = pl.BlockSpec((tm, DK), lambda i, j: (j, 0))
    sds = lambda w, dt=bf16: jax.ShapeDtypeStruct((b, l, w), dt)
    res = lambda *shape: _resident(shape, layer)
    return pl.pallas_call(
        _proj_body, grid=(b, l // tm),
        in_specs=[tok(D_MODEL), res(1, D_MODEL), res(D_MODEL, _MAIN_W), res(D_MODEL, LANES),
                  res(SUBLANES, D_MODEL), res(CONV_W, CONV_DIM), res(1, LANES),
                  res(1, LANES), res(SUBLANES, 1), res(SUBLANES, 1), tab, tab],
        out_specs=[tok(QK_W), tok(QK_W), tok(V_W), tok(V_W), tok(QK_W), tok(QK_W), tok(V_W), tok(V_W),
                   tok(LANES), pl.BlockSpec((None, SUBLANES, tm), lambda i, j: (i, 0, j)),
                   pl.BlockSpec((None, SUBLANES, CONV_DIM), lambda i, j: (i, 0, 0))],
        out_shape=[sds(QK_W), sds(QK_W), sds(V_W), sds(V_W), sds(QK_W), sds(QK_W), sds(V_W), sds(V_W),
                   sds(LANES, f32), jax.ShapeDtypeStruct((b, SUBLANES, l), f32),
                   jax.ShapeDtypeStruct((b, SUBLANES, CONV_DIM), f32)],
        scratch_shapes=[pltpu.VMEM((SUBLANES, CONV_DIM), f32)],
        compiler_params=_params("parallel", "arbitrary"), name="proj",
    )(h3d, gpre, wm, wab, wabt, cw, alog, dtb, alogt, dtbt, cos, sin)


def _ret_steps(q_ref, k_ref, v_ref, sg_ref, o_ref, s_ref, chunk):
    c = chunk
    row, col = _iota((c, c), 0), _iota((c, c), 1)
    dist = (row - col).astype(f32)
    pos = _iota((c, 1), 0).astype(f32)
    per_head = {}

    def head_consts(h):
        if h not in per_head:
            lg = LOG_GAMMA[h]
            per_head[h] = (jnp.exp(jnp.where(row >= col, dist * lg, -jnp.inf)), jnp.exp((pos + 1.0) * lg),
                           jnp.exp((c - 1.0 - pos) * lg))
        return per_head[h]

    def step(h, t, b):
        decay, q_scale, k_scale = head_consts(h)
        rows = slice(t * c, (t + 1) * c)
        qh = q_ref[b, rows, h * DK:(h + 1) * DK]
        kh = k_ref[b, rows, h * DK:(h + 1) * DK]
        vh = v_ref[b, rows, h * DV:(h + 1) * DV]
        attn = lax.dot_general(qh, kh, NT, preferred_element_type=f32) * decay
        q_dec = (qh.astype(f32) * q_scale).astype(bf16)
        k_dec = (kh.astype(f32) * k_scale).astype(bf16)
        s = s_ref[b, h]
        o = _dot(q_dec, s.astype(bf16)) + _dot(attn.astype(bf16), vh)
        s_ref[b, h] = s * math.exp(c * LOG_GAMMA[h]) + lax.dot_general(k_dec, vh, TN, preferred_element_type=f32)
        o_ref[b, rows, h * DV:(h + 1) * DV] = (
            sg_ref[b, rows, h * DV:(h + 1) * DV].astype(f32) * _rms(o)).astype(bf16)

    return [functools.partial(step, h, t, b) for h in range(N_HEADS) for t in range(q_ref.shape[1] // c)
            for b in range(q_ref.shape[0])]


def _gdn_steps(q_ref, k_ref, v_ref, sz_ref, gcol_ref, grow_ref, nw_ref, o_ref, s_ref):
    c = CHUNK
    hc = N_HEADS * c
    row, lane = _iota((c, hc), 0), _iota((c, hc), 1)
    lane_head, col = lane // c, lane % c
    lower, strict = row >= col, row > col
    head_mask = [(lane_head == h).astype(bf16) for h in range(N_HEADS)]
    block_diag = lambda x: jnp.concatenate([x * m for m in head_mask], axis=0)
    spread = lambda cols: functools.reduce(
        lambda acc, h: jnp.where(lane_head == h, cols[h], acc), range(N_HEADS - 1), cols[N_HEADS - 1])
    cum_rows = (_iota((c, c), 0) >= _iota((c, c), 1)).astype(f32)
    cum_cols = (_iota((c, hc), 0) <= (_iota((c, hc), 1) % c)).astype(f32)
    own_row = _iota((SUBLANES, hc), 0) == _iota((SUBLANES, hc), 1) // c
    zeros_k = jnp.zeros((c, DK), bf16)
    nw = nw_ref[...]

    items = [(b, t) for t in range(q_ref.shape[1] // c) for b in range(q_ref.shape[0])]
    chunks = range(len(items))
    tok = lambda t: slice(t * c, (t + 1) * c)

    def chunk_setup(b, t):
        rows = tok(t)
        k_heads = [k_ref[b, rows, h * DK:(h + 1) * DK] for h in range(N_HEADS)]
        k_bd = jnp.concatenate([jnp.concatenate([k_heads[h] if g == h else zeros_k for g in range(N_HEADS)], axis=1)
                                for h in range(N_HEADS)], axis=0)
        qk = lax.dot_general(jnp.concatenate([q_ref[b, rows, :], k_ref[b, rows, :]], axis=0), k_bd, NT,
                             preferred_element_type=f32)
        gcol = gcol_ref[b, rows, :]
        cum = jnp.dot(cum_rows, gcol, preferred_element_type=f32, precision=HIGHEST)
        b_cols = [cum[:, h:h + 1] for h in range(N_HEADS)]
        beta_cols = [gcol[:, N_HEADS + h:N_HEADS + h + 1] for h in range(N_HEADS)]
        b_row = jnp.sum(jnp.where(own_row, jnp.dot(grow_ref[b, t], cum_cols, preferred_element_type=f32,
                                                   precision=HIGHEST), 0.0), axis=0, keepdims=True)
        decay = jnp.exp(jnp.where(lower, spread(b_cols) - b_row, -jnp.inf))
        x = jnp.where(strict, qk[c:] * spread(beta_cols) * decay, 0.0) * -1.0
        return (qk[:c] * decay, x, jnp.concatenate(b_cols, axis=0), jnp.concatenate(beta_cols, axis=0))

    attns, xs, b_colv, betav = zip(*[chunk_setup(b, t) for b, t in items])
    ns = xs
    x_bds = [block_diag(x.astype(bf16)) for x in xs]
    for _ in range(int(math.log2(c)) - 1):
        xs = [_dot(x.astype(bf16), x_bd) for x, x_bd in zip(xs, x_bds)]
        x_bds = [block_diag(x.astype(bf16)) for x in xs]
        ns = [n + x + _dot(n.astype(bf16), x_bd) for n, x, x_bd in zip(ns, xs, x_bds)]

    def chunk_wy(i):
        b, rows = items[i][0], tok(items[i][1])
        b_col, beta = b_colv[i], betav[i]
        kst = jnp.concatenate([k_ref[b, rows, h * DK:(h + 1) * DK] for h in range(N_HEADS)], axis=0)
        qst = jnp.concatenate([q_ref[b, rows, h * DK:(h + 1) * DK] for h in range(N_HEADS)], axis=0)
        vst = jnp.concatenate([v_ref[b, rows, h * DV:(h + 1) * DV] for h in range(N_HEADS)], axis=0).astype(f32)
        n_bd = block_diag(ns[i].astype(bf16))
        kf = kst.astype(f32)
        e_b = jnp.exp(b_col)
        ru, rw = vst * beta, kf * (beta * e_b)
        u = ru + _dot(n_bd, ru.astype(bf16))
        w = (rw + _dot(n_bd, rw.astype(bf16))).astype(bf16)
        q_dec = (qst.astype(f32) * e_b).astype(bf16)
        return u, w, q_dec, kf, block_diag(attns[i].astype(bf16))

    wy = [chunk_wy(i) for i in chunks]
    for i in chunks:
        b, rows = items[i][0], tok(items[i][1])
        u, w, q_dec, kf, attn_bd = wy[i]
        b_col = b_colv[i]
        v_new, o_inter = [], []
        for h in range(N_HEADS):
            hs = slice(h * c, (h + 1) * c)
            r = _dot(jnp.concatenate([w[hs], q_dec[hs]], axis=0), s_ref[b, h].astype(bf16))
            v_new.append(u[hs] - r[:c])
            o_inter.append(r[c:])
        vb = jnp.concatenate(v_new, axis=0).astype(bf16)
        o = jnp.concatenate(o_inter, axis=0) + _dot(attn_bd, vb)
        for h in range(N_HEADS):
            hs = slice(h * c, (h + 1) * c)
            b_last = b_col[(h + 1) * c - 1:(h + 1) * c]
            k_dec = (kf[hs] * jnp.exp(b_last - b_col[hs])).astype(bf16)
            s_ref[b, h] = (s_ref[b, h] * jnp.exp(b_last)
                           + lax.dot_general(k_dec, vb[hs], TN, preferred_element_type=f32))
            o_ref[b, rows, h * DV:(h + 1) * DV] = (
                _rms(o[hs], nw) * sz_ref[b, rows, h * DV:(h + 1) * DV].astype(f32)).astype(bf16)


def _scans_body(rq_ref, rk_ref, rv_ref, rg_ref, gq_ref, gk_ref, gv_ref, gz_ref, gcol_ref, grow_ref, nw_ref,
                oret_ref, sret_ref, ogdn_ref, sgdn_ref, ret_state, gdn_state, *, ret_chunk):
    j = pl.program_id(1)

    @pl.when(j == 0)
    def _():
        ret_state[...] = jnp.zeros_like(ret_state)
        gdn_state[...] = jnp.zeros_like(gdn_state)

    for step in _ret_steps(rq_ref, rk_ref, rv_ref, rg_ref, oret_ref, ret_state, ret_chunk):
        step()
    _gdn_steps(gq_ref, gk_ref, gv_ref, gz_ref, gcol_ref, grow_ref, nw_ref, ogdn_ref, gdn_state)

    @pl.when(j == pl.num_programs(1) - 1)
    def _():
        sret_ref[...] = ret_state[...]
        sgdn_ref[...] = gdn_state[...]


def _scans(layer, rq, rk, rv, rg, gq, gk, gv, gz, gcol, grow4, nw, tile, nb, ret_chunk):
    b, l, _ = gq.shape
    assert b % nb == 0 and tile % ret_chunk == 0
    tok = lambda w: pl.BlockSpec((nb, tile, w), lambda i, j: (i, j, 0))
    st = pl.BlockSpec((nb, N_HEADS, DK, DV), lambda i, j: (i, 0, 0, 0))
    o_sds = jax.ShapeDtypeStruct((b, l, V_W), bf16)
    s_sds = jax.ShapeDtypeStruct((b, N_HEADS, DK, DV), f32)
    state = pltpu.VMEM((nb, N_HEADS, DK, DV), f32)
    return pl.pallas_call(
        functools.partial(_scans_body, ret_chunk=ret_chunk), grid=(b // nb, l // tile),
        in_specs=[tok(QK_W), tok(QK_W), tok(V_W), tok(V_W), tok(QK_W), tok(QK_W), tok(V_W), tok(V_W), tok(LANES),
                  pl.BlockSpec((nb, tile // CHUNK, SUBLANES, CHUNK), lambda i, j: (i, j, 0, 0)),
                  _resident((1, DV), layer)],
        out_specs=[tok(V_W), st, tok(V_W), st], out_shape=[o_sds, s_sds, o_sds, s_sds],
        scratch_shapes=[state, state],
        compiler_params=_params("parallel", "arbitrary"), name="scans",
    )(rq, rk, rv, rg, gq, gk, gv, gz, gcol, grow4, nw)


def _softmax(s):
    e = jnp.exp(s - jnp.max(s, axis=-1, keepdims=True))
    return e / jnp.sum(e, axis=-1, keepdims=True)


def _merge_tail(x, xn, o_ret, o_gdn, mem_proj, wg_ref, wr_ref, wd_ref, wo_ref, gpost):
    gates = jax.nn.sigmoid(_dot(xn, wg_ref[...]))
    merged = (gates[:, :D_MODEL] * _dot(o_ret, wr_ref[...])
              + gates[:, D_MODEL:2 * D_MODEL] * _dot(o_gdn, wd_ref[...])
              + gates[:, 2 * D_MODEL:] * mem_proj)
    y = _dot(merged.astype(bf16), wo_ref[...])
    return x + _rms(y, gpost)


def _merge_body(h_ref, gpre_ref, wq_ref, wg_ref, mk_ref, mv_ref, oret_ref, ogdn_ref,
                wr_ref, wd_ref, wm_ref, wo_ref, gpost_ref, o_ref):
    x = h_ref[...]
    xn = _rms(x, gpre_ref[...]).astype(bf16)
    mq = _dot(xn, wq_ref[...]).astype(bf16)
    mem_proj = None
    for h in range(N_HEADS):
        sl = slice(h * HD_X, (h + 1) * HD_X)
        s = lax.dot_general(mq[:, sl], mk_ref[:, sl], NT, preferred_element_type=f32) * HD_X ** -0.5
        o = _dot(_softmax(s).astype(bf16), mv_ref[:, sl])
        part = _dot(o.astype(bf16), wm_ref[sl, :])
        mem_proj = part if mem_proj is None else mem_proj + part
    o_ref[...] = _merge_tail(x, xn, oret_ref[...], ogdn_ref[...], mem_proj, wg_ref, wr_ref, wd_ref, wo_ref,
                             gpost_ref[...])


def _merge(layer, h3d, gpre, wq, wg, mk, mv, o_ret, o_gdn, wr, wd, wm, wo, gpost, tm):
    b, l, _ = h3d.shape
    tok = pl.BlockSpec((None, tm, D_MODEL), lambda i, j: (i, j, 0))
    mem = pl.BlockSpec((None, None, N_MEM, D_MODEL), lambda i, j: (layer, i, 0, 0))
    sq = _resident((D_MODEL, D_MODEL), layer)
    return pl.pallas_call(
        _merge_body, grid=(b, l // tm),
        in_specs=[tok, _resident((1, D_MODEL), layer), _packed_mq(layer), _packed_gates(layer), mem, mem,
                  tok, tok, sq, sq, sq, sq, _resident((1, D_MODEL), layer)],
        out_specs=tok, out_shape=jax.ShapeDtypeStruct((b, l, D_MODEL), f32),
        compiler_params=_params("parallel", "parallel"), name="merge",
    )(h3d, gpre, wq, wg, mk, mv, o_ret, o_gdn, wr, wd, wm, wo, gpost)


def _sproj_body(h_ref, gpre_ref, wm_ref, wab_ref, wq_ref, cw_ref, alog_ref, dtb_ref, cos_ref, sin_ref, buf_ref,
                rq_ref, rk_ref, rv_ref, rg_ref, gq_ref, gk_ref, gv_ref, gz_ref, gcol_ref, mq_ref, nbuf_ref):
    xn = _rms(h_ref[...], gpre_ref[...]).astype(bf16)
    p = _dot(xn, wm_ref[...])
    cos, sin = cos_ref[...], sin_ref[...]
    for h in range(N_HEADS):
        sl = slice(h * DK, (h + 1) * DK)
        rq_ref[:, sl] = _rotary(p[:, _RQ + h * DK:_RQ + (h + 1) * DK], cos, sin)
        rk_ref[:, sl] = _rotary(p[:, _RK + h * DK:_RK + (h + 1) * DK], cos, sin) * DK ** -0.5
    rv_ref[...] = p[:, _RV:_RG]
    rg_ref[...] = _silu(p[:, _RG:_GQKV])
    gz_ref[...] = _silu(p[:, _GZ:_MAIN_W])
    x = p[:, _GQKV:_GZ]
    cw = cw_ref[...]
    conv = x * cw[CONV_W - 1:CONV_W]
    for s in range(CONV_W - 1):
        prev = buf_ref[:, s * CONV_DIM:(s + 1) * CONV_DIM]
        conv = conv + prev * cw[s:s + 1]
        if s > 0:
            nbuf_ref[:, (s - 1) * CONV_DIM:s * CONV_DIM] = prev
    nbuf_ref[:, (CONV_W - 2) * CONV_DIM:] = x
    qs, ks, v = _gdn_qkv(conv)
    for h in range(N_HEADS):
        gq_ref[:, h * DK:(h + 1) * DK] = qs[h]
        gk_ref[:, h * DK:(h + 1) * DK] = ks[h]
    gv_ref[...] = v
    ab = _dot(xn, wab_ref[...])
    gcol_ref[...] = _decay_beta(ab, alog_ref[...], dtb_ref[...], _iota(ab.shape, 1) < N_HEADS)
    mq_ref[...] = _dot(xn, wq_ref[...])


def _sample_proj(layer, h2d, gpre, wm, wab, wq, cw, alog, dtb, cos, sin, buf2d):
    n = h2d.shape[0]
    sds = lambda w: jax.ShapeDtypeStruct((n, w), f32)
    res = lambda *shape: _resident(shape, layer)
    widths = (QK_W, QK_W, V_W, V_W, QK_W, QK_W, V_W, V_W, LANES, D_MODEL, (CONV_W - 1) * CONV_DIM)
    return pl.pallas_call(
        _sproj_body, grid=(1,),
        in_specs=[_resident((n, D_MODEL)), res(1, D_MODEL), res(D_MODEL, _MAIN_W), res(D_MODEL, LANES),
                  _packed_mq(layer), res(CONV_W, CONV_DIM), res(1, LANES), res(1, LANES),
                  _resident((1, DK)), _resident((1, DK)), _resident((n, (CONV_W - 1) * CONV_DIM))],
        out_specs=[pl.BlockSpec((n, w), lambda i: (0, 0)) for w in widths],
        out_shape=[sds(w) for w in widths],
        compiler_params=_params("arbitrary"), name="sample_proj",
    )(h2d, gpre, wm, wab, wq, cw, alog, dtb, cos, sin, buf2d)


def _sstate_body(cols_ref, rv_ref, gv_ref, gcol_ref, sret_ref, sgdn_ref, nret_ref, ngdn_ref, oret_ref, ogdn_ref):
    vecs = cols_ref[...].T
    col = lambda j, v, h: vecs[:, (j * 4 + v) * N_HEADS + h:(j * 4 + v) * N_HEADS + h + 1]
    for j in range(rv_ref.shape[0]):
        for h in range(N_HEADS):
            vs = slice(h * DV, (h + 1) * DV)
            q, k = col(j, 0, h), col(j, 1, h)
            s_new = sret_ref[j, h] * math.exp(LOG_GAMMA[h]) + k * rv_ref[j:j + 1, vs]
            nret_ref[j, h] = s_new
            oret_ref[j:j + 1, vs] = jnp.sum(q * s_new, axis=0, keepdims=True)
            q, k = col(j, 2, h), col(j, 3, h)
            eg = jnp.exp(gcol_ref[j:j + 1, h:h + 1])
            beta = gcol_ref[j:j + 1, N_HEADS + h:N_HEADS + h + 1]
            s = sgdn_ref[j, h]
            ks = jnp.sum(k * s, axis=0, keepdims=True)
            v_new = beta * (gv_ref[j:j + 1, vs] - eg * ks)
            s_new = s * eg + k * v_new
            ngdn_ref[j, h] = s_new
            ogdn_ref[j:j + 1, vs] = jnp.sum(q * s_new, axis=0, keepdims=True)


def _sattn_body(mq_ref, k_ref, v_ref, o_ref):
    for j in range(mq_ref.shape[0]):
        s = jnp.sum(k_ref[j] * mq_ref[j][None], axis=-1, keepdims=True) * HD_X ** -0.5
        e = jnp.exp(s - jnp.max(s, axis=0, keepdims=True))
        p = e / jnp.sum(e, axis=0, keepdims=True)
        o_ref[j] = jnp.sum(p * v_ref[j], axis=0)


def _smerge_body(h_ref, gpre_ref, wg_ref, oret_ref, rg_ref, ogdn_ref, gz_ref, nw_ref, omem_ref,
                 wr_ref, wd_ref, wm_ref, wo_ref, gpost_ref, o_ref):
    x = h_ref[...]
    xn = _rms(x, gpre_ref[...]).astype(bf16)
    nw = nw_ref[...]
    rets, gdns = [], []
    for h in range(N_HEADS):
        vs = slice(h * DV, (h + 1) * DV)
        rets.append(rg_ref[:, vs] * _rms(oret_ref[:, vs]))
        gdns.append(_rms(ogdn_ref[:, vs], nw) * gz_ref[:, vs])
    o_ret = jnp.concatenate(rets, axis=-1).astype(bf16)
    o_gdn = jnp.concatenate(gdns, axis=-1).astype(bf16)
    mem_proj = _dot(omem_ref[...].astype(bf16), wm_ref[...])
    o_ref[...] = _merge_tail(x, xn, o_ret, o_gdn, mem_proj, wg_ref, wr_ref, wd_ref, wo_ref, gpost_ref[...])


def _sample_merge(layer, h2d, gpre, wg, o_ret, rg, o_gdn, gz, nw, o_mem, wr, wd, wm, wo, gpost):
    n = h2d.shape[0]
    act = _resident((n, D_MODEL))
    sq = _resident((D_MODEL, D_MODEL), layer)
    return pl.pallas_call(
        _smerge_body, grid=(1,),
        in_specs=[act, _resident((1, D_MODEL), layer), _packed_gates(layer), act, act, act, act,
                  _resident((1, DV), layer), act, sq, sq, sq, sq, _resident((1, D_MODEL), layer)],
        out_specs=pl.BlockSpec((n, D_MODEL), lambda i: (0, 0)), out_shape=jax.ShapeDtypeStruct(h2d.shape, f32),
        compiler_params=_params("arbitrary"), name="sample_merge",
    )(h2d, gpre, wg, o_ret, rg, o_gdn, gz, nw, o_mem, wr, wd, wm, wo, gpost)


def _rope_tables(pos):
    half = DK // 2
    inv_freq = ROPE_BASE ** (-jnp.arange(half, dtype=f32) / half)
    ang = pos.astype(f32)[:, None] * inv_freq[None, :]
    cos, sin = jnp.cos(ang), jnp.sin(ang)
    return jnp.concatenate([cos, cos], axis=-1), jnp.concatenate([-sin, sin], axis=-1)


def _row_tile(n, want):
    t = min(n, want)
    assert n % t == 0, (n, t)
    return t


def kernel(x_prompt, x_sample, mem_prompt, state_ret, state_gdn, state_conv, cache_mem_k, cache_mem_v, norm_ffn1_pre, norm_ffn1_post, ffn1_w_in, ffn1_w_out, norm_mix_pre, norm_mix_post, w_in, gdn_conv_w, gdn_a_log, gdn_dt_bias, gdn_norm, norm_mem, w_mem_k, w_mem_v, w_branch_ret, w_branch_gdn, w_branch_mem, w_out, norm_ffn2_pre, norm_ffn2_post, ffn2_w_in, ffn2_w_out):
    bp, sp, _ = x_prompt.shape
    ns, ss, _ = x_sample.shape
    depth = w_in.shape[0]
    assert ss == 1 and sp % CHUNK == 0
    tm_ffn = _row_tile(bp * sp, 512)
    tm_seq = _row_tile(sp, 512)

    cos_p, sin_p = _rope_tables(jnp.arange(sp))
    cos_s, sin_s = _rope_tables(PAST_LEN + jnp.arange(ss))
    row = lambda v: v.reshape(depth, 1, -1).astype(f32)
    pad_lanes = lambda v: jnp.pad(v.astype(f32), ((0, 0), (0, LANES - v.shape[1]))).reshape(depth, 1, LANES)
    pad_rows = lambda v: jnp.pad(v.astype(f32), ((0, 0), (0, SUBLANES - v.shape[1]))).reshape(depth, SUBLANES, 1)
    w_in_t = jnp.swapaxes(w_in, 1, 2)
    w_main = _w_in_cols(w_in_t, 0, _MAIN_W)
    w_mq = _w_in_cols(w_in_t, _AB_END, D_MODEL)
    w_gates = _w_in_cols(w_in_t, _MQ_END, 3 * D_MODEL)
    w_ab_row = w_in_t[:, _MAIN_W:_AB_END].astype(bf16)
    w_ab_col = jnp.pad(jnp.swapaxes(w_ab_row, 1, 2), ((0, 0), (0, 0), (0, LANES - 2 * N_HEADS)))
    f1_in, f1_out, f2_in, f2_out = (w.astype(bf16) for w in (ffn1_w_in, ffn1_w_out, ffn2_w_in, ffn2_w_out))
    wr, wd, wm, wo = (w.astype(bf16) for w in (w_branch_ret, w_branch_gdn, w_branch_mem, w_out))
    cw = gdn_conv_w.astype(f32)
    alog, dtb = pad_lanes(gdn_a_log), pad_lanes(gdn_dt_bias)
    alog_t, dtb_t = pad_rows(gdn_a_log), pad_rows(gdn_dt_bias)
    nw = row(gdn_norm)
    g_f1pre, g_f1post, g_f2pre, g_f2post = (row(g) for g in (norm_ffn1_pre, norm_ffn1_post, norm_ffn2_pre,
                                                              norm_ffn2_post))
    g_mpre, g_mpost = row(norm_mix_pre), row(norm_mix_post)

    mk, mv, mk_b, mv_b = _memkv(mem_prompt, row(norm_mem), w_mem_k.astype(bf16), w_mem_v.astype(bf16))
    hp = x_prompt
    hs = x_sample.reshape(ns, D_MODEL)
    outs = [[] for _ in range(4)]
    new_states = None
    scan_seqs = 2 if bp % 2 == 0 else 1
    scan_tile = _row_tile(sp, 256)
    ret_chunk = scan_tile
    hosted_per_step = ns // (2 * (bp * sp // tm_ffn))
    assert hosted_per_step * 2 * (bp * sp // tm_ffn) == ns
    for l in range(depth):
        h1s = _ffn(l, hs, g_f1pre, f1_in, f1_out, g_f1post, ns)
        s_rq, s_rk, s_rv, s_rg, s_gq, s_gk, s_gv, s_gz, s_gcol, mq, nbuf = _sample_proj(
            l, h1s, g_mpre, w_main, w_ab_col, w_mq, cw, alog, dtb, cos_s, sin_s,
            state_conv[l].reshape(ns, (CONV_W - 1) * CONV_DIM))
        outs[3].append(nbuf.reshape(ns, CONV_W - 1, CONV_DIM))
        groups = ns // hosted_per_step
        by_step = lambda a: a.reshape((groups, hosted_per_step) + a.shape[1:])
        cols = jnp.stack([s_rq, s_rk, s_gq, s_gk], axis=1).reshape(groups, hosted_per_step * 4 * N_HEADS, DK)
        host = lambda part: (part, 2, hosted_per_step, by_step(mq.reshape(ns, N_HEADS, HD_X)), cache_mem_k,
                             cache_mem_v, cols, by_step(s_rv), by_step(s_gv), by_step(s_gcol), state_ret,
                             state_gdn, new_states)

        h1, o_mem_a, n_ret, n_gdn, so_ret_a, so_gdn_a = _ffn(
            l, hp.reshape(bp * sp, D_MODEL), g_f1pre, f1_in, f1_out, g_f1post, tm_ffn, hosted=host(0))
        new_states = (n_ret, n_gdn)
        h1 = h1.reshape(bp, sp, D_MODEL)
        rq, rk, rv, rg, gq, gk, gv, gz, gcol, grow, tail = _proj(
            l, h1, g_mpre, w_main, w_ab_col, w_ab_row, cw, alog, dtb, alog_t, dtb_t, cos_p, sin_p, tm_seq)
        grow4 = grow.reshape(bp, SUBLANES, sp // CHUNK, CHUNK).transpose(0, 2, 1, 3)
        o_ret, s_ret, o_gdn, s_gdn = _scans(l, rq, rk, rv, rg, gq, gk, gv, gz, gcol, grow4, nw, scan_tile,
                                            scan_seqs, ret_chunk)
        h2 = _merge(l, h1, g_mpre, w_mq, w_gates, mk_b, mv_b, o_ret, o_gdn, wr, wd, wm, wo, g_mpost, tm_seq)
        hp, o_mem_b, n_ret, n_gdn, so_ret_b, so_gdn_b = _ffn(
            l, h2.reshape(bp * sp, D_MODEL), g_f2pre, f2_in, f2_out, g_f2post, tm_ffn, hosted=host(1))
        new_states = (n_ret, n_gdn)
        hp = hp.reshape(bp, sp, D_MODEL)
        outs[0].append(s_ret)
        outs[1].append(s_gdn)
        outs[2].append(tail[:, SUBLANES - (CONV_W - 1):])

        join = lambda a, b: jnp.concatenate([a, b], axis=0).reshape(ns, -1)
        o_mem, so_ret, so_gdn = join(o_mem_a, o_mem_b), join(so_ret_a, so_ret_b), join(so_gdn_a, so_gdn_b)
        h2s = _sample_merge(l, h1s, g_mpre, w_gates, so_ret, s_rg, so_gdn, s_gz, nw, o_mem, wr, wd, wm, wo, g_mpost)
        hs = _ffn(l, h2s, g_f2pre, f2_in, f2_out, g_f2post, ns)

    ret_p, gdn_p, conv_p, conv_s = (jnp.stack(o) for o in outs)
    return (hp, hs.reshape(ns, ss, D_MODEL), ret_p, gdn_p, conv_p, mk, mv) + new_states + (conv_s,)
```

```python
import functools
import math

import jax
import jax.numpy as jnp
from jax import lax
from jax.experimental import pallas as pl
from jax.experimental.pallas import tpu as pltpu

f32, bf16 = jnp.float32, jnp.bfloat16

D_MODEL = 1024
N_HEADS = 4
DK = 128
DV = 256
QK_W = N_HEADS * DK
V_W = N_HEADS * DV
CONV_W = 4
CONV_DIM = 2 * QK_W + V_W
HD_X = 256
N_MEM = 256
D_FF = 2816
EPS = 1e-6
ROPE_BASE = 10000.0
PAST_LEN = 16384
CHUNK = 64
LANES = 128
SUBLANES = 8
VMEM_LIMIT = 56 * 1024 * 1024
LOG_GAMMA = tuple(math.log1p(-2.0 ** (-5.0 - h)) for h in range(N_HEADS))
HIGHEST = lax.Precision.HIGHEST
NT = (((1,), (1,)), ((), ()))
TN = (((0,), (0,)), ((), ()))


def _params(*sem):
    return pltpu.CompilerParams(dimension_semantics=sem, vmem_limit_bytes=VMEM_LIMIT)


def _resident(shape, layer=None, col_block=0):
    if layer is None:
        return pl.BlockSpec(shape, lambda *_: (0,) * len(shape), pipeline_mode=pl.Buffered(1))
    index = (layer,) + (0,) * (len(shape) - 1) + (col_block,)
    return pl.BlockSpec((None,) + shape, lambda *_: index, pipeline_mode=pl.Buffered(1))


def _rms(x, gain=None):
    y = x * lax.rsqrt(jnp.mean(x * x, axis=-1, keepdims=True) + EPS)
    return y if gain is None else y * gain


def _silu(x):
    return x * jax.nn.sigmoid(x)


def _softplus(x):
    return jnp.maximum(x, 0.0) + jnp.log1p(jnp.exp(-jnp.abs(x)))


def _dot(a, b):
    return jnp.dot(a, b, preferred_element_type=f32)


def _iota(shape, dim):
    return lax.broadcasted_iota(jnp.int32, shape, dim)


def _ffn_body(x_ref, gpre_ref, wgu_ref, wo_ref, gpost_ref, o_ref):
    x = x_ref[...]
    xn = _rms(x, gpre_ref[...]).astype(bf16)
    h = _dot(xn, wgu_ref[...])
    act = (_silu(h[:, :D_FF]) * h[:, D_FF:]).astype(bf16)
    y = _dot(act, wo_ref[...])
    o_ref[...] = x + 0.5 * _rms(y, gpost_ref[...])


def _ffn_hosting_body(x_ref, gpre_ref, wgu_ref, wo_ref, gpost_ref, mq_ref, k_ref, v_ref,
                      cols_ref, rv_ref, gv_ref, gcol_ref, sret_ref, sgdn_ref, *refs):
    o_ref, omem_ref, nret_ref, ngdn_ref, oret_ref, ogdn_ref = refs[-6:]
    _ffn_body(x_ref, gpre_ref, wgu_ref, wo_ref, gpost_ref, o_ref)
    _sattn_body(mq_ref, k_ref, v_ref, omem_ref)
    _sstate_body(cols_ref, rv_ref, gv_ref, gcol_ref, sret_ref, sgdn_ref, nret_ref, ngdn_ref, oret_ref, ogdn_ref)


def _ffn(layer, x2d, gpre, wgu, wo, gpost, tm, hosted=None):
    t = x2d.shape[0]
    steps = t // tm
    row = pl.BlockSpec((tm, D_MODEL), lambda i: (i, 0))
    in_specs = [row, _resident((1, D_MODEL), layer), _resident((D_MODEL, 2 * D_FF), layer),
                _resident((D_FF, D_MODEL), layer), _resident((1, D_MODEL), layer)]
    out_row = jax.ShapeDtypeStruct((t, D_MODEL), f32)
    if hosted is None:
        return pl.pallas_call(
            _ffn_body, grid=(steps,), in_specs=in_specs, out_specs=row, out_shape=out_row,
            compiler_params=_params("parallel"), name="ffn",
        )(x2d, gpre, wgu, wo, gpost)
    part, parts, ps, mq, mem_k, mem_v, cols, rv, gv, gcol, s_ret, s_gdn, prev = hosted
    depth, n = s_ret.shape[:2]
    assert ps * parts * steps == n, (n, parts, steps, ps)
    first = part * steps
    grp = lambda *shape: pl.BlockSpec((None,) + shape, lambda i: (first + i,) + (0,) * len(shape))
    out_grp = lambda *shape: pl.BlockSpec((None,) + shape, lambda i: (i,) + (0,) * len(shape))
    mem = pl.BlockSpec((None, ps, N_MEM, N_HEADS, HD_X), lambda i: (layer, first + i, 0, 0, 0))
    st = pl.BlockSpec((None, ps, N_HEADS, DK, DV), lambda i: (layer, first + i, 0, 0, 0))
    st_sds = jax.ShapeDtypeStruct((depth, n, N_HEADS, DK, DV), f32)
    grp_sds = lambda *shape: jax.ShapeDtypeStruct((steps,) + shape, f32)
    prev, prev_specs, aliases = _alias_prev(14, prev)
    aliases = {k: v + 2 for k, v in aliases.items()}
    return pl.pallas_call(
        _ffn_hosting_body, grid=(steps,),
        in_specs=in_specs + [grp(ps, N_HEADS, HD_X), mem, mem,
                             grp(ps * 4 * N_HEADS, DK),
                             grp(ps, V_W), grp(ps, V_W), grp(ps, LANES), st, st] + prev_specs,
        out_specs=[row, out_grp(ps, N_HEADS, HD_X), st, st, out_grp(ps, V_W), out_grp(ps, V_W)],
        out_shape=[out_row, grp_sds(ps, N_HEADS, HD_X), st_sds, st_sds, grp_sds(ps, V_W), grp_sds(ps, V_W)],
        input_output_aliases=aliases, compiler_params=_params("parallel"), name="ffn_hosting",
    )(x2d, gpre, wgu, wo, gpost, mq, mem_k, mem_v, cols, rv, gv, gcol, s_ret, s_gdn, *prev)


def _stacked_out(layer, depth, shape, dtype, index_map):
    spec = pl.BlockSpec((None,) + shape, lambda *i: (layer,) + index_map(*i))
    return spec, lambda full: jax.ShapeDtypeStruct((depth,) + full, dtype)


def _alias_prev(n_inputs, prev):
    prev = [] if prev is None else list(prev)
    return prev, [pl.BlockSpec(memory_space=pl.ANY)] * len(prev), {n_inputs + i: i for i in range(len(prev))}


def _memkv_body(m_ref, g_ref, wk_ref, wv_ref, k_ref, v_ref, kb_ref, vb_ref):
    nb = m_ref.shape[0]
    mn = _rms(m_ref[...].reshape(nb * N_MEM, D_MODEL), g_ref[...]).astype(bf16)
    k = _dot(mn, wk_ref[...])
    v = _dot(mn, wv_ref[...])
    for b in range(nb):
        rows = slice(b * N_MEM, (b + 1) * N_MEM)
        for h in range(N_HEADS):
            k_ref[b, :, h, :] = k[rows, h * HD_X:(h + 1) * HD_X]
            v_ref[b, :, h, :] = v[rows, h * HD_X:(h + 1) * HD_X]
        kb_ref[b] = k[rows].astype(bf16)
        vb_ref[b] = v[rows].astype(bf16)


def _memkv(mem, g, wk, wv):
    b = mem.shape[0]
    depth = wk.shape[0]
    nb = 4 if b % 4 == 0 else 1
    per_layer = lambda shape: pl.BlockSpec((None,) + shape, lambda l, i: (l,) + (0,) * len(shape))
    out5 = pl.BlockSpec((None, nb, N_MEM, N_HEADS, HD_X), lambda l, i: (l, i, 0, 0, 0))
    outb = pl.BlockSpec((None, nb, N_MEM, D_MODEL), lambda l, i: (l, i, 0, 0))
    sds5 = jax.ShapeDtypeStruct((depth, b, N_MEM, N_HEADS, HD_X), f32)
    sdsb = jax.ShapeDtypeStruct((depth, b, N_MEM, D_MODEL), bf16)
    return pl.pallas_call(
        _memkv_body, grid=(depth, b // nb),
        in_specs=[pl.BlockSpec((nb, N_MEM, D_MODEL), lambda l, i: (i, 0, 0)), per_layer((1, D_MODEL)),
                  per_layer((D_MODEL, D_MODEL)), per_layer((D_MODEL, D_MODEL))],
        out_specs=[out5, out5, outb, outb], out_shape=[sds5, sds5, sdsb, sdsb],
        compiler_params=_params("parallel", "parallel"), name="memkv",
    )(mem, g, wk, wv)


_RQ, _RK, _RV, _RG, _GQKV, _GZ, _MAIN_W = 0, 512, 1024, 2048, 3072, 5120, 6144
_AB_END = _MAIN_W + 2 * N_HEADS
_MQ_END = _AB_END + D_MODEL


def _packed_mq(layer):
    return _resident((D_MODEL, D_MODEL), layer)


def _packed_gates(layer):
    return _resident((D_MODEL, 3 * D_MODEL), layer)


def _w_in_cols_body(wt_ref, o_ref):
    o_ref[...] = wt_ref[0].T.astype(bf16)


def _w_in_cols(w_in_t, first, width, blk=1024):
    depth = w_in_t.shape[0]
    assert first % SUBLANES == 0 and width % blk == 0
    return pl.pallas_call(
        _w_in_cols_body, grid=(depth, width // blk),
        in_specs=[pl.BlockSpec((pl.Element(1), pl.Element(blk), pl.Element(D_MODEL)),
                               lambda l, c: (l, pl.multiple_of(first + c * blk, SUBLANES), 0))],
        out_specs=pl.BlockSpec((None, D_MODEL, blk), lambda l, c: (l, 0, c)),
        out_shape=jax.ShapeDtypeStruct((depth, D_MODEL, width), bf16),
        compiler_params=_params("parallel", "parallel"), name="w_in_cols",
    )(w_in_t)


def _rotary(x, cos, sin_signed):
    return x * cos + pltpu.roll(x, DK // 2, axis=1) * sin_signed


def _decay_beta(ab, a_log, dt_bias, is_decay):
    g = -jnp.exp(a_log) * _softplus(ab + dt_bias)
    return jnp.where(is_decay, g, jax.nn.sigmoid(ab))


def _gdn_qkv(conv):
    a = _silu(conv)
    qs, ks = [], []
    for h in range(N_HEADS):
        q = a[:, h * DK:(h + 1) * DK]
        k = a[:, QK_W + h * DK:QK_W + (h + 1) * DK]
        qs.append(q * (lax.rsqrt(jnp.sum(q * q, axis=-1, keepdims=True) + EPS) * DK ** -0.5))
        ks.append(k * lax.rsqrt(jnp.sum(k * k, axis=-1, keepdims=True) + EPS))
    return qs, ks, a[:, 2 * QK_W:]


def _proj_body(h_ref, gpre_ref, wm_ref, wab_ref, wabt_ref, cw_ref, alog_ref, dtb_ref, alogt_ref, dtbt_ref,
               cos_ref, sin_ref,
               rq_ref, rk_ref, rv_ref, rg_ref, gq_ref, gk_ref, gv_ref, gz_ref, gcol_ref, grow_ref, tail_ref,
               carry_ref):
    j = pl.program_id(1)
    tm = h_ref.shape[0]

    @pl.when(j == 0)
    def _():
        carry_ref[...] = jnp.zeros_like(carry_ref)

    xn = _rms(h_ref[...], gpre_ref[...]).astype(bf16)
    cos, sin = cos_ref[...], sin_ref[...]

    def ret_qk(p):
        for h in range(N_HEADS):
            sl = slice(h * DK, (h + 1) * DK)
            rq_ref[:, sl] = _rotary(p[:, h * DK:(h + 1) * DK], cos, sin).astype(bf16)
            rk_ref[:, sl] = (_rotary(p[:, QK_W + h * DK:QK_W + (h + 1) * DK], cos, sin) * DK ** -0.5).astype(bf16)

    def ret_v(p):
        rv_ref[...] = p.astype(bf16)

    def ret_gate(p):
        rg_ref[...] = _silu(p).astype(bf16)

    def gdn_z(p):
        gz_ref[...] = _silu(p).astype(bf16)

    def conv_silu(x, cols):
        carry = carry_ref[:, cols]
        cw = cw_ref[:, cols]
        conv = x * cw[CONV_W - 1:CONV_W]
        conv_top = x[:SUBLANES] * cw[CONV_W - 1:CONV_W]
        first = _iota(conv_top.shape, 0)
        for s in range(1, CONV_W):
            w = cw[CONV_W - 1 - s:CONV_W - s]
            xs = pltpu.roll(x, s, axis=0)
            conv = conv + xs * w
            conv_top = conv_top + jnp.where(first < s, pltpu.roll(carry, s, axis=0), xs[:SUBLANES]) * w
        carry_ref[:, cols] = x[tm - SUBLANES:]
        tail_ref[:, cols] = x[tm - SUBLANES:]
        return (slice(None), _silu(conv)), (slice(0, SUBLANES), _silu(conv_top))

    def gdn_qk(p):
        for rows, a in conv_silu(p, slice(0, 2 * QK_W)):
            for h in range(N_HEADS):
                q = a[:, h * DK:(h + 1) * DK]
                k = a[:, QK_W + h * DK:QK_W + (h + 1) * DK]
                q = q * (lax.rsqrt(jnp.sum(q * q, axis=-1, keepdims=True) + EPS) * DK ** -0.5)
                k = k * lax.rsqrt(jnp.sum(k * k, axis=-1, keepdims=True) + EPS)
                gq_ref[rows, h * DK:(h + 1) * DK] = q.astype(bf16)
                gk_ref[rows, h * DK:(h + 1) * DK] = k.astype(bf16)

    def gdn_v(p):
        for rows, a in conv_silu(p, slice(2 * QK_W, CONV_DIM)):
            gv_ref[rows, :] = a.astype(bf16)

    groups = ((_RQ, _RV, ret_qk), (_RV, _RG, ret_v), (_RG, _GQKV, ret_gate), (_GQKV, _GQKV + 2 * QK_W, gdn_qk),
              (_GQKV + 2 * QK_W, _GZ, gdn_v), (_GZ, _MAIN_W, gdn_z))
    pending = None
    for lo, hi, epilogue in groups:
        p = _dot(xn, wm_ref[:, lo:hi])
        if pending is not None:
            pending()
        pending = functools.partial(epilogue, p)
    pending()

    ab = _dot(xn, wab_ref[...])
    gcol_ref[...] = _decay_beta(ab, alog_ref[...], dtb_ref[...], _iota(ab.shape, 1) < N_HEADS)
    abt = lax.dot_general(wabt_ref[...], xn, NT, preferred_element_type=f32)
    grow_ref[...] = _decay_beta(abt, alogt_ref[...], dtbt_ref[...], _iota(abt.shape, 0) < N_HEADS)


def _proj(layer, h3d, gpre, wm, wab, wabt, cw, alog, dtb, alogt, dtbt, cos, sin, tm):
    b, l, _ = h3d.shape
    tok = lambda w: pl.BlockSpec((None, tm, w), lambda i, j: (i, j, 0))
    tab = pl.BlockSpec((tm, DK), lambda i, j: (j, 0))
    sds = lambda w, dt=bf16: jax.ShapeDtypeStruct((b, l, w), dt)
    res = lambda *shape: _resident(shape, layer)
    return pl.pallas_call(
        _proj_body, grid=(b, l // tm),
        in_specs=[tok(D_MODEL), res(1, D_MODEL), res(D_MODEL, _MAIN_W), res(D_MODEL, LANES),
                  res(SUBLANES, D_MODEL), res(CONV_W, CONV_DIM), res(1, LANES),
                  res(1, LANES), res(SUBLANES, 1), res(SUBLANES, 1), tab, tab],
        out_specs=[tok(QK_W), tok(QK_W), tok(V_W), tok(V_W), tok(QK_W), tok(QK_W), tok(V_W), tok(V_W),
                   tok(LANES), pl.BlockSpec((None, SUBLANES, tm), lambda i, j: (i, 0, j)),
                   pl.BlockSpec((None, SUBLANES, CONV_DIM), lambda i, j: (i, 0, 0))],
        out_shape=[sds(QK_W), sds(QK_W), sds(V_W), sds(V_W), sds(QK_W), sds(QK_W), sds(V_W), sds(V_W),
                   sds(LANES, f32), jax.ShapeDtypeStruct((b, SUBLANES, l), f32),
                   jax.ShapeDtypeStruct((b, SUBLANES, CONV_DIM), f32)],
        scratch_shapes=[pltpu.VMEM((SUBLANES, CONV_DIM), f32)],
        compiler_params=_params("parallel", "arbitrary"), name="proj",
    )(h3d, gpre, wm, wab, wabt, cw, alog, dtb, alogt, dtbt, cos, sin)


def _ret_steps(q_ref, k_ref, v_ref, sg_ref, o_ref, s_ref, chunk):
    c = chunk
    row, col = _iota((c, c), 0), _iota((c, c), 1)
    dist = (row - col).astype(f32)
    pos = _iota((c, 1), 0).astype(f32)
    per_head = {}

    def head_consts(h):
        if h not in per_head:
            lg = LOG_GAMMA[h]
            per_head[h] = (jnp.exp(jnp.where(row >= col, dist * lg, -jnp.inf)), jnp.exp((pos + 1.0) * lg),
                           jnp.exp((c - 1.0 - pos) * lg))
        return per_head[h]

    def step(h, t, b):
        decay, q_scale, k_scale = head_consts(h)
        rows = slice(t * c, (t + 1) * c)
        qh = q_ref[b, rows, h * DK:(h + 1) * DK]
        kh = k_ref[b, rows, h * DK:(h + 1) * DK]
        vh = v_ref[b, rows, h * DV:(h + 1) * DV]
        attn = lax.dot_general(qh, kh, NT, preferred_element_type=f32) * decay
        q_dec = (qh.astype(f32) * q_scale).astype(bf16)
        k_dec = (kh.astype(f32) * k_scale).astype(bf16)
        s = s_ref[b, h]
        o = _dot(q_dec, s.astype(bf16)) + _dot(attn.astype(bf16), vh)
        s_ref[b, h] = s * math.exp(c * LOG_GAMMA[h]) + lax.dot_general(k_dec, vh, TN, preferred_element_type=f32)
        o_ref[b, rows, h * DV:(h + 1) * DV] = (
            sg_ref[b, rows, h * DV:(h + 1) * DV].astype(f32) * _rms(o)).astype(bf16)

    return [functools.partial(step, h, t, b) for h in range(N_HEADS) for t in range(q_ref.shape[1] // c)
            for b in range(q_ref.shape[0])]


def _gdn_steps(q_ref, k_ref, v_ref, sz_ref, gcol_ref, grow_ref, nw_ref, o_ref, s_ref):
    c = CHUNK
    hc = N_HEADS * c
    row, lane = _iota((c, hc), 0), _iota((c, hc), 1)
    lane_head, col = lane // c, lane % c
    lower, strict = row >= col, row > col
    head_mask = [(lane_head == h).astype(bf16) for h in range(N_HEADS)]
    block_diag = lambda x: jnp.concatenate([x * m for m in head_mask], axis=0)
    spread = lambda cols: functools.reduce(
        lambda acc, h: jnp.where(lane_head == h, cols[h], acc), range(N_HEADS - 1), cols[N_HEADS - 1])
    cum_rows = (_iota((c, c), 0) >= _iota((c, c), 1)).astype(f32)
    cum_cols = (_iota((c, hc), 0) <= (_iota((c, hc), 1) % c)).astype(f32)
    own_row = _iota((SUBLANES, hc), 0) == _iota((SUBLANES, hc), 1) // c
    zeros_k = jnp.zeros((c, DK), bf16)
    nw = nw_ref[...]

    items = [(b, t) for t in range(q_ref.shape[1] // c) for b in range(q_ref.shape[0])]
    chunks = range(len(items))
    tok = lambda t: slice(t * c, (t + 1) * c)

    def chunk_setup(b, t):
        rows = tok(t)
        k_heads = [k_ref[b, rows, h * DK:(h + 1) * DK] for h in range(N_HEADS)]
        k_bd = jnp.concatenate([jnp.concatenate([k_heads[h] if g == h else zeros_k for g in range(N_HEADS)], axis=1)
                                for h in range(N_HEADS)], axis=0)
        qk = lax.dot_general(jnp.concatenate([q_ref[b, rows, :], k_ref[b, rows, :]], axis=0), k_bd, NT,
                             preferred_element_type=f32)
        gcol = gcol_ref[b, rows, :]
        cum = jnp.dot(cum_rows, gcol, preferred_element_type=f32, precision=HIGHEST)
        b_cols = [cum[:, h:h + 1] for h in range(N_HEADS)]
        beta_cols = [gcol[:, N_HEADS + h:N_HEADS + h + 1] for h in range(N_HEADS)]
        b_row = jnp.sum(jnp.where(own_row, jnp.dot(grow_ref[b, t], cum_cols, preferred_element_type=f32,
                                                   precision=HIGHEST), 0.0), axis=0, keepdims=True)
        decay = jnp.exp(jnp.where(lower, spread(b_cols) - b_row, -jnp.inf))
        x = jnp.where(strict, qk[c:] * spread(beta_cols) * decay, 0.0) * -1.0
        return (qk[:c] * decay, x, jnp.concatenate(b_cols, axis=0), jnp.concatenate(beta_cols, axis=0))

    attns, xs, b_colv, betav = zip(*[chunk_setup(b, t) for b, t in items])
    ns = xs
    x_bds = [block_diag(x.astype(bf16)) for x in xs]
    for _ in range(int(math.log2(c)) - 1):
        xs = [_dot(x.astype(bf16), x_bd) for x, x_bd in zip(xs, x_bds)]
        x_bds = [block_diag(x.astype(bf16)) for x in xs]
        ns = [n + x + _dot(n.astype(bf16), x_bd) for n, x, x_bd in zip(ns, xs, x_bds)]

    def chunk_wy(i):
        b, rows = items[i][0], tok(items[i][1])
        b_col, beta = b_colv[i], betav[i]
        kst = jnp.concatenate([k_ref[b, rows, h * DK:(h + 1) * DK] for h in range(N_HEADS)], axis=0)
        qst = jnp.concatenate([q_ref[b, rows, h * DK:(h + 1) * DK] for h in range(N_HEADS)], axis=0)
        vst = jnp.concatenate([v_ref[b, rows, h * DV:(h + 1) * DV] for h in range(N_HEADS)], axis=0).astype(f32)
        n_bd = block_diag(ns[i].astype(bf16))
        kf = kst.astype(f32)
        e_b = jnp.exp(b_col)
        ru, rw = vst * beta, kf * (beta * e_b)
        u = ru + _dot(n_bd, ru.astype(bf16))
        w = (rw + _dot(n_bd, rw.astype(bf16))).astype(bf16)
        q_dec = (qst.astype(f32) * e_b).astype(bf16)
        return u, w, q_dec, kf, block_diag(attns[i].astype(bf16))

    wy = [chunk_wy(i) for i in chunks]
    for i in chunks:
        b, rows = items[i][0], tok(items[i][1])
        u, w, q_dec, kf, attn_bd = wy[i]
        b_col = b_colv[i]
        v_new, o_inter = [], []
        for h in range(N_HEADS):
            hs = slice(h * c, (h + 1) * c)
            r = _dot(jnp.concatenate([w[hs], q_dec[hs]], axis=0), s_ref[b, h].astype(bf16))
            v_new.append(u[hs] - r[:c])
            o_inter.append(r[c:])
        vb = jnp.concatenate(v_new, axis=0).astype(bf16)
        o = jnp.concatenate(o_inter, axis=0) + _dot(attn_bd, vb)
        for h in range(N_HEADS):
            hs = slice(h * c, (h + 1) * c)
            b_last = b_col[(h + 1) * c - 1:(h + 1) * c]
            k_dec = (kf[hs] * jnp.exp(b_last - b_col[hs])).astype(bf16)
            s_ref[b, h] = (s_ref[b, h] * jnp.exp(b_last)
                           + lax.dot_general(k_dec, vb[hs], TN, preferred_element_type=f32))
            o_ref[b, rows, h * DV:(h + 1) * DV] = (
                _rms(o[hs], nw) * sz_ref[b, rows, h * DV:(h + 1) * DV].astype(f32)).astype(bf16)


def _scans_body(rq_ref, rk_ref, rv_ref, rg_ref, gq_ref, gk_ref, gv_ref, gz_ref, gcol_ref, grow_ref, nw_ref,
                oret_ref, sret_ref, ogdn_ref, sgdn_ref, ret_state, gdn_state, *, ret_chunk):
    j = pl.program_id(1)

    @pl.when(j == 0)
    def _():
        ret_state[...] = jnp.zeros_like(ret_state)
        gdn_state[...] = jnp.zeros_like(gdn_state)

    for step in _ret_steps(rq_ref, rk_ref, rv_ref, rg_ref, oret_ref, ret_state, ret_chunk):
        step()
    _gdn_steps(gq_ref, gk_ref, gv_ref, gz_ref, gcol_ref, grow_ref, nw_ref, ogdn_ref, gdn_state)

    @pl.when(j == pl.num_programs(1) - 1)
    def _():
        sret_ref[...] = ret_state[...]
        sgdn_ref[...] = gdn_state[...]


def _scans(layer, rq, rk, rv, rg, gq, gk, gv, gz, gcol, grow4, nw, tile, nb, ret_chunk):
    b, l, _ = gq.shape
    assert b % nb == 0 and tile % ret_chunk == 0
    tok = lambda w: pl.BlockSpec((nb, tile, w), lambda i, j: (i, j, 0))
    st = pl.BlockSpec((nb, N_HEADS, DK, DV), lambda i, j: (i, 0, 0, 0))
    o_sds = jax.ShapeDtypeStruct((b, l, V_W), bf16)
    s_sds = jax.ShapeDtypeStruct((b, N_HEADS, DK, DV), f32)
    state = pltpu.VMEM((nb, N_HEADS, DK, DV), f32)
    return pl.pallas_call(
        functools.partial(_scans_body, ret_chunk=ret_chunk), grid=(b // nb, l // tile),
        in_specs=[tok(QK_W), tok(QK_W), tok(V_W), tok(V_W), tok(QK_W), tok(QK_W), tok(V_W), tok(V_W), tok(LANES),
                  pl.BlockSpec((nb, tile // CHUNK, SUBLANES, CHUNK), lambda i, j: (i, j, 0, 0)),
                  _resident((1, DV), layer)],
        out_specs=[tok(V_W), st, tok(V_W), st], out_shape=[o_sds, s_sds, o_sds, s_sds],
        scratch_shapes=[state, state],
        compiler_params=_params("parallel", "arbitrary"), name="scans",
    )(rq, rk, rv, rg, gq, gk, gv, gz, gcol, grow4, nw)


def _softmax(s):
    e = jnp.exp(s - jnp.max(s, axis=-1, keepdims=True))
    return e / jnp.sum(e, axis=-1, keepdims=True)


def _merge_tail(x, xn, o_ret, o_gdn, mem_proj, wg_ref, wr_ref, wd_ref, wo_ref, gpost):
    gates = jax.nn.sigmoid(_dot(xn, wg_ref[...]))
    merged = (gates[:, :D_MODEL] * _dot(o_ret, wr_ref[...])
              + gates[:, D_MODEL:2 * D_MODEL] * _dot(o_gdn, wd_ref[...])
              + gates[:, 2 * D_MODEL:] * mem_proj)
    y = _dot(merged.astype(bf16), wo_ref[...])
    return x + _rms(y, gpost)


def _merge_body(h_ref, gpre_ref, wq_ref, wg_ref, mk_ref, mv_ref, oret_ref, ogdn_ref,
                wr_ref, wd_ref, wm_ref, wo_ref, gpost_ref, o_ref):
    x = h_ref[...]
    xn = _rms(x, gpre_ref[...]).astype(bf16)
    mq = _dot(xn, wq_ref[...]).astype(bf16)
    mem_proj = None
    for h in range(N_HEADS):
        sl = slice(h * HD_X, (h + 1) * HD_X)
        s = lax.dot_general(mq[:, sl], mk_ref[:, sl], NT, preferred_element_type=f32) * HD_X ** -0.5
        o = _dot(_softmax(s).astype(bf16), mv_ref[:, sl])
        part = _dot(o.astype(bf16), wm_ref[sl, :])
        mem_proj = part if mem_proj is None else mem_proj + part
    o_ref[...] = _merge_tail(x, xn, oret_ref[...], ogdn_ref[...], mem_proj, wg_ref, wr_ref, wd_ref, wo_ref,
                             gpost_ref[...])


def _merge(layer, h3d, gpre, wq, wg, mk, mv, o_ret, o_gdn, wr, wd, wm, wo, gpost, tm):
    b, l, _ = h3d.shape
    tok = pl.BlockSpec((None, tm, D_MODEL), lambda i, j: (i, j, 0))
    mem = pl.BlockSpec((None, None, N_MEM, D_MODEL), lambda i, j: (layer, i, 0, 0))
    sq = _resident((D_MODEL, D_MODEL), layer)
    return pl.pallas_call(
        _merge_body, grid=(b, l // tm),
        in_specs=[tok, _resident((1, D_MODEL), layer), _packed_mq(layer), _packed_gates(layer), mem, mem,
                  tok, tok, sq, sq, sq, sq, _resident((1, D_MODEL), layer)],
        out_specs=tok, out_shape=jax.ShapeDtypeStruct((b, l, D_MODEL), f32),
        compiler_params=_params("parallel", "parallel"), name="merge",
    )(h3d, gpre, wq, wg, mk, mv, o_ret, o_gdn, wr, wd, wm, wo, gpost)


def _sproj_body(h_ref, gpre_ref, wm_ref, wab_ref, wq_ref, cw_ref, alog_ref, dtb_ref, cos_ref, sin_ref, buf_ref,
                rq_ref, rk_ref, rv_ref, rg_ref, gq_ref, gk_ref, gv_ref, gz_ref, gcol_ref, mq_ref, nbuf_ref):
    xn = _rms(h_ref[...], gpre_ref[...]).astype(bf16)
    p = _dot(xn, wm_ref[...])
    cos, sin = cos_ref[...], sin_ref[...]
    for h in range(N_HEADS):
        sl = slice(h * DK, (h + 1) * DK)
        rq_ref[:, sl] = _rotary(p[:, _RQ + h * DK:_RQ + (h + 1) * DK], cos, sin)
        rk_ref[:, sl] = _rotary(p[:, _RK + h * DK:_RK + (h + 1) * DK], cos, sin) * DK ** -0.5
    rv_ref[...] = p[:, _RV:_RG]
    rg_ref[...] = _silu(p[:, _RG:_GQKV])
    gz_ref[...] = _silu(p[:, _GZ:_MAIN_W])
    x = p[:, _GQKV:_GZ]
    cw = cw_ref[...]
    conv = x * cw[CONV_W - 1:CONV_W]
    for s in range(CONV_W - 1):
        prev = buf_ref[:, s * CONV_DIM:(s + 1) * CONV_DIM]
        conv = conv + prev * cw[s:s + 1]
        if s > 0:
            nbuf_ref[:, (s - 1) * CONV_DIM:s * CONV_DIM] = prev
    nbuf_ref[:, (CONV_W - 2) * CONV_DIM:] = x
    qs, ks, v = _gdn_qkv(conv)
    for h in range(N_HEADS):
        gq_ref[:, h * DK:(h + 1) * DK] = qs[h]
        gk_ref[:, h * DK:(h + 1) * DK] = ks[h]
    gv_ref[...] = v
    ab = _dot(xn, wab_ref[...])
    gcol_ref[...] = _decay_beta(ab, alog_ref[...], dtb_ref[...], _iota(ab.shape, 1) < N_HEADS)
    mq_ref[...] = _dot(xn, wq_ref[...])


def _sample_proj(layer, h2d, gpre, wm, wab, wq, cw, alog, dtb, cos, sin, buf2d):
    n = h2d.shape[0]
    sds = lambda w: jax.ShapeDtypeStruct((n, w), f32)
    res = lambda *shape: _resident(shape, layer)
    widths = (QK_W, QK_W, V_W, V_W, QK_W, QK_W, V_W, V_W, LANES, D_MODEL, (CONV_W - 1) * CONV_DIM)
    return pl.pallas_call(
        _sproj_body, grid=(1,),
        in_specs=[_resident((n, D_MODEL)), res(1, D_MODEL), res(D_MODEL, _MAIN_W), res(D_MODEL, LANES),
                  _packed_mq(layer), res(CONV_W, CONV_DIM), res(1, LANES), res(1, LANES),
                  _resident((1, DK)), _resident((1, DK)), _resident((n, (CONV_W - 1) * CONV_DIM))],
        out_specs=[pl.BlockSpec((n, w), lambda i: (0, 0)) for w in widths],
        out_shape=[sds(w) for w in widths],
        compiler_params=_params("arbitrary"), name="sample_proj",
    )(h2d, gpre, wm, wab, wq, cw, alog, dtb, cos, sin, buf2d)


def _sstate_body(cols_ref, rv_ref, gv_ref, gcol_ref, sret_ref, sgdn_ref, nret_ref, ngdn_ref, oret_ref, ogdn_ref):
    vecs = cols_ref[...].T
    col = lambda j, v, h: vecs[:, (j * 4 + v) * N_HEADS + h:(j * 4 + v) * N_HEADS + h + 1]
    for j in range(rv_ref.shape[0]):
        for h in range(N_HEADS):
            vs = slice(h * DV, (h + 1) * DV)
            q, k = col(j, 0, h), col(j, 1, h)
            s_new = sret_ref[j, h] * math.exp(LOG_GAMMA[h]) + k * rv_ref[j:j + 1, vs]
            nret_ref[j, h] = s_new
            oret_ref[j:j + 1, vs] = jnp.sum(q * s_new, axis=0, keepdims=True)
            q, k = col(j, 2, h), col(j, 3, h)
            eg = jnp.exp(gcol_ref[j:j + 1, h:h + 1])
            beta = gcol_ref[j:j + 1, N_HEADS + h:N_HEADS + h + 1]
            s = sgdn_ref[j, h]
            ks = jnp.sum(k * s, axis=0, keepdims=True)
            v_new = beta * (gv_ref[j:j + 1, vs] - eg * ks)
            s_new = s * eg + k * v_new
            ngdn_ref[j, h] = s_new
            ogdn_ref[j:j + 1, vs] = jnp.sum(q * s_new, axis=0, keepdims=True)


def _sattn_body(mq_ref, k_ref, v_ref, o_ref):
    for j in range(mq_ref.shape[0]):
        s = jnp.sum(k_ref[j] * mq_ref[j][None], axis=-1, keepdims=True) * HD_X ** -0.5
        e = jnp.exp(s - jnp.max(s, axis=0, keepdims=True))
        p = e / jnp.sum(e, axis=0, keepdims=True)
        o_ref[j] = jnp.sum(p * v_ref[j], axis=0)


def _smerge_body(h_ref, gpre_ref, wg_ref, oret_ref, rg_ref, ogdn_ref, gz_ref, nw_ref, omem_ref,
                 wr_ref, wd_ref, wm_ref, wo_ref, gpost_ref, o_ref):
    x = h_ref[...]
    xn = _rms(x, gpre_ref[...]).astype(bf16)
    nw = nw_ref[...]
    rets, gdns = [], []
    for h in range(N_HEADS):
        vs = slice(h * DV, (h + 1) * DV)
        rets.append(rg_ref[:, vs] * _rms(oret_ref[:, vs]))
        gdns.append(_rms(ogdn_ref[:, vs], nw) * gz_ref[:, vs])
    o_ret = jnp.concatenate(rets, axis=-1).astype(bf16)
    o_gdn = jnp.concatenate(gdns, axis=-1).astype(bf16)
    mem_proj = _dot(omem_ref[...].astype(bf16), wm_ref[...])
    o_ref[...] = _merge_tail(x, xn, o_ret, o_gdn, mem_proj, wg_ref, wr_ref, wd_ref, wo_ref, gpost_ref[...])


def _sample_merge(layer, h2d, gpre, wg, o_ret, rg, o_gdn, gz, nw, o_mem, wr, wd, wm, wo, gpost):
    n = h2d.shape[0]
    act = _resident((n, D_MODEL))
    sq = _resident((D_MODEL, D_MODEL), layer)
    return pl.pallas_call(
        _smerge_body, grid=(1,),
        in_specs=[act, _resident((1, D_MODEL), layer), _packed_gates(layer), act, act, act, act,
                  _resident((1, DV), layer), act, sq, sq, sq, sq, _resident((1, D_MODEL), layer)],
        out_specs=pl.BlockSpec((n, D_MODEL), lambda i: (0, 0)), out_shape=jax.ShapeDtypeStruct(h2d.shape, f32),
        compiler_params=_params("arbitrary"), name="sample_merge",
    )(h2d, gpre, wg, o_ret, rg, o_gdn, gz, nw, o_mem, wr, wd, wm, wo, gpost)


def _rope_tables(pos):
    half = DK // 2
    inv_freq = ROPE_BASE ** (-jnp.arange(half, dtype=f32) / half)
    ang = pos.astype(f32)[:, None] * inv_freq[None, :]
    cos, sin = jnp.cos(ang), jnp.sin(ang)
    return jnp.concatenate([cos, cos], axis=-1), jnp.concatenate([-sin, sin], axis=-1)


def _row_tile(n, want):
    t = min(n, want)
    assert n % t == 0, (n, t)
    return t


def kernel(x_prompt, x_sample, mem_prompt, state_ret, state_gdn, state_conv, cache_mem_k, cache_mem_v, norm_ffn1_pre, norm_ffn1_post, ffn1_w_in, ffn1_w_out, norm_mix_pre, norm_mix_post, w_in, gdn_conv_w, gdn_a_log, gdn_dt_bias, gdn_norm, norm_mem, w_mem_k, w_mem_v, w_branch_ret, w_branch_gdn, w_branch_mem, w_out, norm_ffn2_pre, norm_ffn2_post, ffn2_w_in, ffn2_w_out):
    bp, sp, _ = x_prompt.shape
    ns, ss, _ = x_sample.shape
    depth = w_in.shape[0]
    assert ss == 1 and sp % CHUNK == 0
    tm_ffn = _row_tile(bp * sp, 512)
    tm_seq = _row_tile(sp, 512)
    tm_proj = _row_tile(sp, 256)

    cos_p, sin_p = _rope_tables(jnp.arange(sp))
    cos_s, sin_s = _rope_tables(PAST_LEN + jnp.arange(ss))
    row = lambda v: v.reshape(depth, 1, -1).astype(f32)
    pad_lanes = lambda v: jnp.pad(v.astype(f32), ((0, 0), (0, LANES - v.shape[1]))).reshape(depth, 1, LANES)
    pad_rows = lambda v: jnp.pad(v.astype(f32), ((0, 0), (0, SUBLANES - v.shape[1]))).reshape(depth, SUBLANES, 1)
    w_in_t = jnp.swapaxes(w_in, 1, 2)
    w_main = _w_in_cols(w_in_t, 0, _MAIN_W)
    w_mq = _w_in_cols(w_in_t, _AB_END, D_MODEL)
    w_gates = _w_in_cols(w_in_t, _MQ_END, 3 * D_MODEL)
    w_ab_row = w_in_t[:, _MAIN_W:_AB_END].astype(bf16)
    w_ab_col = jnp.pad(jnp.swapaxes(w_ab_row, 1, 2), ((0, 0), (0, 0), (0, LANES - 2 * N_HEADS)))
    f1_in, f1_out, f2_in, f2_out = (w.astype(bf16) for w in (ffn1_w_in, ffn1_w_out, ffn2_w_in, ffn2_w_out))
    wr, wd, wm, wo = (w.astype(bf16) for w in (w_branch_ret, w_branch_gdn, w_branch_mem, w_out))
    cw = gdn_conv_w.astype(f32)
    alog, dtb = pad_lanes(gdn_a_log), pad_lanes(gdn_dt_bias)
    alog_t, dtb_t = pad_rows(gdn_a_log), pad_rows(gdn_dt_bias)
    nw = row(gdn_norm)
    g_f1pre, g_f1post, g_f2pre, g_f2post = (row(g) for g in (norm_ffn1_pre, norm_ffn1_post, norm_ffn2_pre,
                                                              norm_ffn2_post))
    g_mpre, g_mpost = row(norm_mix_pre), row(norm_mix_post)

    mk, mv, mk_b, mv_b = _memkv(mem_prompt, row(norm_mem), w_mem_k.astype(bf16), w_mem_v.astype(bf16))
    hp = x_prompt
    hs = x_sample.reshape(ns, D_MODEL)
    outs = [[] for _ in range(4)]
    new_states = None
    scan_seqs = 2 if bp % 2 == 0 else 1
    scan_tile = _row_tile(sp, 256)
    ret_chunk = scan_tile
    hosted_per_step = ns // (2 * (bp * sp // tm_ffn))
    assert hosted_per_step * 2 * (bp * sp // tm_ffn) == ns
    for l in range(depth):
        h1s = _ffn(l, hs, g_f1pre, f1_in, f1_out, g_f1post, ns)
        s_rq, s_rk, s_rv, s_rg, s_gq, s_gk, s_gv, s_gz, s_gcol, mq, nbuf = _sample_proj(
            l, h1s, g_mpre, w_main, w_ab_col, w_mq, cw, alog, dtb, cos_s, sin_s,
            state_conv[l].reshape(ns, (CONV_W - 1) * CONV_DIM))
        outs[3].append(nbuf.reshape(ns, CONV_W - 1, CONV_DIM))
        groups = ns // hosted_per_step
        by_step = lambda a: a.reshape((groups, hosted_per_step) + a.shape[1:])
        cols = jnp.stack([s_rq, s_rk, s_gq, s_gk], axis=1).reshape(groups, hosted_per_step * 4 * N_HEADS, DK)
        host = lambda part: (part, 2, hosted_per_step, by_step(mq.reshape(ns, N_HEADS, HD_X)), cache_mem_k,
                             cache_mem_v, cols, by_step(s_rv), by_step(s_gv), by_step(s_gcol), state_ret,
                             state_gdn, new_states)

        h1, o_mem_a, n_ret, n_gdn, so_ret_a, so_gdn_a = _ffn(
            l, hp.reshape(bp * sp, D_MODEL), g_f1pre, f1_in, f1_out, g_f1post, tm_ffn, hosted=host(0))
        new_states = (n_ret, n_gdn)
        h1 = h1.reshape(bp, sp, D_MODEL)
        rq, rk, rv, rg, gq, gk, gv, gz, gcol, grow, tail = _proj(
            l, h1, g_mpre, w_main, w_ab_col, w_ab_row, cw, alog, dtb, alog_t, dtb_t, cos_p, sin_p, tm_proj)
        grow4 = grow.reshape(bp, SUBLANES, sp // CHUNK, CHUNK).transpose(0, 2, 1, 3)
        o_ret, s_ret, o_gdn, s_gdn = _scans(l, rq, rk, rv, rg, gq, gk, gv, gz, gcol, grow4, nw, scan_tile,
                                            scan_seqs, ret_chunk)
        h2 = _merge(l, h1, g_mpre, w_mq, w_gates, mk_b, mv_b, o_ret, o_gdn, wr, wd, wm, wo, g_mpost, tm_seq)
        hp, o_mem_b, n_ret, n_gdn, so_ret_b, so_gdn_b = _ffn(
            l, h2.reshape(bp * sp, D_MODEL), g_f2pre, f2_in, f2_out, g_f2post, tm_ffn, hosted=host(1))
        new_states = (n_ret, n_gdn)
        hp = hp.reshape(bp, sp, D_MODEL)
        outs[0].append(s_ret)
        outs[1].append(s_gdn)
        outs[2].append(tail[:, SUBLANES - (CONV_W - 1):])

        join = lambda a, b: jnp.concatenate([a, b], axis=0).reshape(ns, -1)
        o_mem, so_ret, so_gdn = join(o_mem_a, o_mem_b), join(so_ret_a, so_ret_b), join(so_gdn_a, so_gdn_b)
        h2s = _sample_merge(l, h1s, g_mpre, w_gates, so_ret, s_rg, so_gdn, s_gz, nw, o_mem, wr, wd, wm, wo, g_mpost)
        hs = _ffn(l, h2s, g_f2pre, f2_in, f2_out, g_f2post, ns)

    ret_p, gdn_p, conv_p, conv_s = (jnp.stack(o) for o in outs)
    return (hp, hs.reshape(ns, ss, D_MODEL), ret_p, gdn_p, conv_p, mk, mv) + new_states + (conv_s,)
```

```python
import functools
import math

import jax
import jax.numpy as jnp
from jax import lax
from jax.experimental import pallas as pl
from jax.experimental.pallas import tpu as pltpu

f32, bf16 = jnp.float32, jnp.bfloat16

D_MODEL = 1024
N_HEADS = 4
DK = 128
DV = 256
QK_W = N_HEADS * DK
V_W = N_HEADS * DV
CONV_W = 4
CONV_DIM = 2 * QK_W + V_W
HD_X = 256
N_MEM = 256
D_FF = 2816
EPS = 1e-6
ROPE_BASE = 10000.0
PAST_LEN = 16384
CHUNK = 64
LANES = 128
SUBLANES = 8
VMEM_LIMIT = 56 * 1024 * 1024
LOG_GAMMA = tuple(math.log1p(-2.0 ** (-5.0 - h)) for h in range(N_HEADS))
HIGHEST = lax.Precision.HIGHEST
NT = (((1,), (1,)), ((), ()))
TN = (((0,), (0,)), ((), ()))


def _params(*sem):
    return pltpu.CompilerParams(dimension_semantics=sem, vmem_limit_bytes=VMEM_LIMIT)


def _resident(shape, layer=None, col_block=0):
    if layer is None:
        return pl.BlockSpec(shape, lambda *_: (0,) * len(shape), pipeline_mode=pl.Buffered(1))
    index = (layer,) + (0,) * (len(shape) - 1) + (col_block,)
    return pl.BlockSpec((None,) + shape, lambda *_: index, pipeline_mode=pl.Buffered(1))


def _rms(x, gain=None):
    y = x * lax.rsqrt(jnp.mean(x * x, axis=-1, keepdims=True) + EPS)
    return y if gain is None else y * gain


def _silu(x):
    return x * jax.nn.sigmoid(x)


def _softplus(x):
    return jnp.maximum(x, 0.0) + jnp.log1p(jnp.exp(-jnp.abs(x)))


def _dot(a, b):
    return jnp.dot(a, b, preferred_element_type=f32)


def _iota(shape, dim):
    return lax.broadcasted_iota(jnp.int32, shape, dim)


def _ffn_body(x_ref, gpre_ref, wgu_ref, wo_ref, gpost_ref, o_ref):
    x = x_ref[...]
    xn = _rms(x, gpre_ref[...]).astype(bf16)
    h = _dot(xn, wgu_ref[...])
    act = (_silu(h[:, :D_FF]) * h[:, D_FF:]).astype(bf16)
    y = _dot(act, wo_ref[...])
    o_ref[...] = x + 0.5 * _rms(y, gpost_ref[...])


def _ffn_hosting_body(x_ref, gpre_ref, wgu_ref, wo_ref, gpost_ref, mq_ref, k_ref, v_ref,
                      cols_ref, rv_ref, gv_ref, gcol_ref, sret_ref, sgdn_ref, *refs):
    o_ref, omem_ref, nret_ref, ngdn_ref, oret_ref, ogdn_ref = refs[-6:]
    _sstate_body(cols_ref, rv_ref, gv_ref, gcol_ref, sret_ref, sgdn_ref, nret_ref, ngdn_ref, oret_ref, ogdn_ref)
    _sattn_body(mq_ref, k_ref, v_ref, omem_ref)
    _ffn_body(x_ref, gpre_ref, wgu_ref, wo_ref, gpost_ref, o_ref)


def _ffn(layer, x2d, gpre, wgu, wo, gpost, tm, hosted=None):
    t = x2d.shape[0]
    steps = t // tm
    row = pl.BlockSpec((tm, D_MODEL), lambda i: (i, 0))
    in_specs = [row, _resident((1, D_MODEL), layer), _resident((D_MODEL, 2 * D_FF), layer),
                _resident((D_FF, D_MODEL), layer), _resident((1, D_MODEL), layer)]
    out_row = jax.ShapeDtypeStruct((t, D_MODEL), f32)
    if hosted is None:
        return pl.pallas_call(
            _ffn_body, grid=(steps,), in_specs=in_specs, out_specs=row, out_shape=out_row,
            compiler_params=_params("parallel"), name="ffn",
        )(x2d, gpre, wgu, wo, gpost)
    part, parts, ps, mq, mem_k, mem_v, cols, rv, gv, gcol, s_ret, s_gdn, prev = hosted
    depth, n = s_ret.shape[:2]
    assert ps * parts * steps == n, (n, parts, steps, ps)
    first = part * steps
    grp = lambda *shape: pl.BlockSpec((None,) + shape, lambda i: (first + i,) + (0,) * len(shape))
    out_grp = lambda *shape: pl.BlockSpec((None,) + shape, lambda i: (i,) + (0,) * len(shape))
    mem = pl.BlockSpec((None, ps, N_MEM, N_HEADS, HD_X), lambda i: (layer, first + i, 0, 0, 0))
    st = pl.BlockSpec((None, ps, N_HEADS, DK, DV), lambda i: (layer, first + i, 0, 0, 0))
    st_sds = jax.ShapeDtypeStruct((depth, n, N_HEADS, DK, DV), f32)
    grp_sds = lambda *shape: jax.ShapeDtypeStruct((steps,) + shape, f32)
    prev = [] if prev is None else list(prev)
    prev_specs = [pl.BlockSpec(memory_space=pl.ANY)] * len(prev)
    aliases = {14 + i: 2 + i for i in range(len(prev))}
    return pl.pallas_call(
        _ffn_hosting_body, grid=(steps,),
        in_specs=in_specs + [grp(ps, N_HEADS, HD_X), mem, mem,
                             grp(ps * 4 * N_HEADS, DK),
                             grp(ps, V_W), grp(ps, V_W), grp(ps, LANES), st, st] + prev_specs,
        out_specs=[row, out_grp(ps, N_HEADS, HD_X), st, st, out_grp(ps, V_W), out_grp(ps, V_W)],
        out_shape=[out_row, grp_sds(ps, N_HEADS, HD_X), st_sds, st_sds, grp_sds(ps, V_W), grp_sds(ps, V_W)],
        input_output_aliases=aliases, compiler_params=_params("parallel"), name="ffn_hosting",
    )(x2d, gpre, wgu, wo, gpost, mq, mem_k, mem_v, cols, rv, gv, gcol, s_ret, s_gdn, *prev)


def _memkv_body(m_ref, g_ref, wk_ref, wv_ref, k_ref, v_ref, kb_ref, vb_ref):
    nb = m_ref.shape[0]
    mn = _rms(m_ref[...].reshape(nb * N_MEM, D_MODEL), g_ref[...]).astype(bf16)
    k = _dot(mn, wk_ref[...])
    v = _dot(mn, wv_ref[...])
    for b in range(nb):
        rows = slice(b * N_MEM, (b + 1) * N_MEM)
        for h in range(N_HEADS):
            k_ref[b, :, h, :] = k[rows, h * HD_X:(h + 1) * HD_X]
            v_ref[b, :, h, :] = v[rows, h * HD_X:(h + 1) * HD_X]
        kb_ref[b] = k[rows].astype(bf16)
        vb_ref[b] = v[rows].astype(bf16)


def _memkv(mem, g, wk, wv):
    b = mem.shape[0]
    depth = wk.shape[0]
    nb = 4 if b % 4 == 0 else 1
    per_layer = lambda shape: pl.BlockSpec((None,) + shape, lambda l, i: (l,) + (0,) * len(shape))
    out5 = pl.BlockSpec((None, nb, N_MEM, N_HEADS, HD_X), lambda l, i: (l, i, 0, 0, 0))
    outb = pl.BlockSpec((None, nb, N_MEM, D_MODEL), lambda l, i: (l, i, 0, 0))
    sds5 = jax.ShapeDtypeStruct((depth, b, N_MEM, N_HEADS, HD_X), f32)
    sdsb = jax.ShapeDtypeStruct((depth, b, N_MEM, D_MODEL), bf16)
    return pl.pallas_call(
        _memkv_body, grid=(depth, b // nb),
        in_specs=[pl.BlockSpec((nb, N_MEM, D_MODEL), lambda l, i: (i, 0, 0)), per_layer((1, D_MODEL)),
                  per_layer((D_MODEL, D_MODEL)), per_layer((D_MODEL, D_MODEL))],
        out_specs=[out5, out5, outb, outb], out_shape=[sds5, sds5, sdsb, sdsb],
        compiler_params=_params("parallel", "parallel"), name="memkv",
    )(mem, g, wk, wv)


_RQ, _RK, _RV, _RG, _GQKV, _GZ, _MAIN_W = 0, 512, 1024, 2048, 3072, 5120, 6144
_AB_END = _MAIN_W + 2 * N_HEADS
_MQ_END = _AB_END + D_MODEL


def _packed_mq(layer):
    return _resident((D_MODEL, D_MODEL), layer)


def _packed_gates(layer):
    return _resident((D_MODEL, 3 * D_MODEL), layer)


def _w_in_cols_body(wt_ref, o_ref):
    o_ref[...] = wt_ref[0].T.astype(bf16)


def _w_in_cols(w_in_t, first, width, blk=1024):
    depth = w_in_t.shape[0]
    assert first % SUBLANES == 0 and width % blk == 0
    return pl.pallas_call(
        _w_in_cols_body, grid=(depth, width // blk),
        in_specs=[pl.BlockSpec((pl.Element(1), pl.Element(blk), pl.Element(D_MODEL)),
                               lambda l, c: (l, pl.multiple_of(first + c * blk, SUBLANES), 0))],
        out_specs=pl.BlockSpec((None, D_MODEL, blk), lambda l, c: (l, 0, c)),
        out_shape=jax.ShapeDtypeStruct((depth, D_MODEL, width), bf16),
        compiler_params=_params("parallel", "parallel"), name="w_in_cols",
    )(w_in_t)


def _rotary(x, cos, sin_signed):
    return x * cos + pltpu.roll(x, DK // 2, axis=1) * sin_signed


def _decay_beta(ab, a_log, dt_bias, is_decay):
    g = -jnp.exp(a_log) * _softplus(ab + dt_bias)
    return jnp.where(is_decay, g, jax.nn.sigmoid(ab))


def _gdn_qkv(conv):
    a = _silu(conv)
    qs, ks = [], []
    for h in range(N_HEADS):
        q = a[:, h * DK:(h + 1) * DK]
        k = a[:, QK_W + h * DK:QK_W + (h + 1) * DK]
        qs.append(q * (lax.rsqrt(jnp.sum(q * q, axis=-1, keepdims=True) + EPS) * DK ** -0.5))
        ks.append(k * lax.rsqrt(jnp.sum(k * k, axis=-1, keepdims=True) + EPS))
    return qs, ks, a[:, 2 * QK_W:]


def _proj_body(h_ref, gpre_ref, wm_ref, wab_ref, wabt_ref, cw_ref, alog_ref, dtb_ref, alogt_ref, dtbt_ref,
               cos_ref, sin_ref,
               rq_ref, rk_ref, rv_ref, rg_ref, gq_ref, gk_ref, gv_ref, gz_ref, gcol_ref, grow_ref, tail_ref,
               carry_ref):
    j = pl.program_id(1)
    tm = h_ref.shape[0]

    @pl.when(j == 0)
    def _():
        carry_ref[...] = jnp.zeros_like(carry_ref)

    xn = _rms(h_ref[...], gpre_ref[...]).astype(bf16)
    cos, sin = cos_ref[...], sin_ref[...]

    def ret_qk(p):
        for h in range(N_HEADS):
            sl = slice(h * DK, (h + 1) * DK)
            rq_ref[:, sl] = _rotary(p[:, h * DK:(h + 1) * DK], cos, sin).astype(bf16)
            rk_ref[:, sl] = (_rotary(p[:, QK_W + h * DK:QK_W + (h + 1) * DK], cos, sin) * DK ** -0.5).astype(bf16)

    def ret_v(p):
        rv_ref[...] = p.astype(bf16)

    def ret_gate(p):
        rg_ref[...] = _silu(p).astype(bf16)

    def gdn_z(p):
        gz_ref[...] = _silu(p).astype(bf16)

    def conv_silu(x, cols):
        carry = carry_ref[:, cols]
        cw = cw_ref[:, cols]
        conv = x * cw[CONV_W - 1:CONV_W]
        conv_top = x[:SUBLANES] * cw[CONV_W - 1:CONV_W]
        first = _iota(conv_top.shape, 0)
        for s in range(1, CONV_W):
            w = cw[CONV_W - 1 - s:CONV_W - s]
            xs = pltpu.roll(x, s, axis=0)
            conv = conv + xs * w
            conv_top = conv_top + jnp.where(first < s, pltpu.roll(carry, s, axis=0), xs[:SUBLANES]) * w
        carry_ref[:, cols] = x[tm - SUBLANES:]
        tail_ref[:, cols] = x[tm - SUBLANES:]
        return (slice(None), _silu(conv)), (slice(0, SUBLANES), _silu(conv_top))

    def gdn_qk(p):
        for rows, a in conv_silu(p, slice(0, 2 * QK_W)):
            for h in range(N_HEADS):
                q = a[:, h * DK:(h + 1) * DK]
                k = a[:, QK_W + h * DK:QK_W + (h + 1) * DK]
                q = q * (lax.rsqrt(jnp.sum(q * q, axis=-1, keepdims=True) + EPS) * DK ** -0.5)
                k = k * lax.rsqrt(jnp.sum(k * k, axis=-1, keepdims=True) + EPS)
                gq_ref[rows, h * DK:(h + 1) * DK] = q.astype(bf16)
                gk_ref[rows, h * DK:(h + 1) * DK] = k.astype(bf16)

    def gdn_v(p):
        for rows, a in conv_silu(p, slice(2 * QK_W, CONV_DIM)):
            gv_ref[rows, :] = a.astype(bf16)

    groups = ((_RQ, _RV, ret_qk), (_RV, _RG, ret_v), (_RG, _GQKV, ret_gate), (_GQKV, _GQKV + 2 * QK_W, gdn_qk),
              (_GQKV + 2 * QK_W, _GZ, gdn_v), (_GZ, _MAIN_W, gdn_z))
    pending = None
    for lo, hi, epilogue in groups:
        p = _dot(xn, wm_ref[:, lo:hi])
        if pending is not None:
            pending()
        pending = functools.partial(epilogue, p)
    pending()

    ab = _dot(xn, wab_ref[...])
    gcol_ref[...] = _decay_beta(ab, alog_ref[...], dtb_ref[...], _iota(ab.shape, 1) < N_HEADS)
    abt = lax.dot_general(wabt_ref[...], xn, NT, preferred_element_type=f32)
    grow_ref[...] = _decay_beta(abt, alogt_ref[...], dtbt_ref[...], _iota(abt.shape, 0) < N_HEADS)


def _proj(layer, h3d, gpre, wm, wab, wabt, cw, alog, dtb, alogt, dtbt, cos, sin, tm):
    b, l, _ = h3d.shape
    tok = lambda w: pl.BlockSpec((None, tm, w), lambda i, j: (i, j, 0))
    tab = pl.BlockSpec((tm, DK), lambda i, j: (j, 0))
    sds = lambda w, dt=bf16: jax.ShapeDtypeStruct((b, l, w), dt)
    res = lambda *shape: _resident(shape, layer)
    return pl.pallas_call(
        _proj_body, grid=(b, l // tm),
        in_specs=[tok(D_MODEL), res(1, D_MODEL), res(D_MODEL, _MAIN_W), res(D_MODEL, LANES),
                  res(SUBLANES, D_MODEL), res(CONV_W, CONV_DIM), res(1, LANES),
                  res(1, LANES), res(SUBLANES, 1), res(SUBLANES, 1), tab, tab],
        out_specs=[tok(QK_W), tok(QK_W), tok(V_W), tok(V_W), tok(QK_W), tok(QK_W), tok(V_W), tok(V_W),
                   tok(LANES), pl.BlockSpec((None, SUBLANES, tm), lambda i, j: (i, 0, j)),
                   pl.BlockSpec((None, SUBLANES, CONV_DIM), lambda i, j: (i, 0, 0))],
        out_shape=[sds(QK_W), sds(QK_W), sds(V_W), sds(V_W), sds(QK_W), sds(QK_W), sds(V_W), sds(V_W),
                   sds(LANES, f32), jax.ShapeDtypeStruct((b, SUBLANES, l), f32),
                   jax.ShapeDtypeStruct((b, SUBLANES, CONV_DIM), f32)],
        scratch_shapes=[pltpu.VMEM((SUBLANES, CONV_DIM), f32)],
        compiler_params=_params("parallel", "arbitrary"), name="proj",
    )(h3d, gpre, wm, wab, wabt, cw, alog, dtb, alogt, dtbt, cos, sin)


def _ret_steps(q_ref, k_ref, v_ref, sg_ref, o_ref, s_ref, chunk):
    c = chunk
    row, col = _iota((c, c), 0), _iota((c, c), 1)
    dist = (row - col).astype(f32)
    pos = _iota((c, 1), 0).astype(f32)
    per_head = {}

    def head_consts(h):
        if h not in per_head:
            lg = LOG_GAMMA[h]
            per_head[h] = (jnp.exp(jnp.where(row >= col, dist * lg, -jnp.inf)), jnp.exp((pos + 1.0) * lg),
                           jnp.exp((c - 1.0 - pos) * lg))
        return per_head[h]

    def step(h, t, b):
        decay, q_scale, k_scale = head_consts(h)
        rows = slice(t * c, (t + 1) * c)
        qh = q_ref[b, rows, h * DK:(h + 1) * DK]
        kh = k_ref[b, rows, h * DK:(h + 1) * DK]
        vh = v_ref[b, rows, h * DV:(h + 1) * DV]
        attn = lax.dot_general(qh, kh, NT, preferred_element_type=f32) * decay
        q_dec = (qh.astype(f32) * q_scale).astype(bf16)
        k_dec = (kh.astype(f32) * k_scale).astype(bf16)
        s = s_ref[b, h]
        o = _dot(q_dec, s.astype(bf16)) + _dot(attn.astype(bf16), vh)
        s_ref[b, h] = s * math.exp(c * LOG_GAMMA[h]) + lax.dot_general(k_dec, vh, TN, preferred_element_type=f32)
        o_ref[b, rows, h * DV:(h + 1) * DV] = (
            sg_ref[b, rows, h * DV:(h + 1) * DV].astype(f32) * _rms(o)).astype(bf16)

    return [functools.partial(step, h, t, b) for h in range(N_HEADS) for t in range(q_ref.shape[1] // c)
            for b in range(q_ref.shape[0])]


def _gdn_steps(q_ref, k_ref, v_ref, sz_ref, gcol_ref, grow_ref, nw_ref, o_ref, s_ref):
    c = CHUNK
    hc = N_HEADS * c
    row, lane = _iota((c, hc), 0), _iota((c, hc), 1)
    lane_head, col = lane // c, lane % c
    lower, strict = row >= col, row > col
    head_mask = [(lane_head == h).astype(bf16) for h in range(N_HEADS)]
    block_diag = lambda x: jnp.concatenate([x * m for m in head_mask], axis=0)
    spread = lambda cols: functools.reduce(
        lambda acc, h: jnp.where(lane_head == h, cols[h], acc), range(N_HEADS - 1), cols[N_HEADS - 1])
    cum_rows = (_iota((c, c), 0) >= _iota((c, c), 1)).astype(f32)
    cum_cols = (_iota((c, hc), 0) <= (_iota((c, hc), 1) % c)).astype(f32)
    own_row = _iota((SUBLANES, hc), 0) == _iota((SUBLANES, hc), 1) // c
    zeros_k = jnp.zeros((c, DK), bf16)
    nw = nw_ref[...]

    items = [(b, t) for t in range(q_ref.shape[1] // c) for b in range(q_ref.shape[0])]
    chunks = range(len(items))
    tok = lambda t: slice(t * c, (t + 1) * c)

    def chunk_setup(b, t):
        rows = tok(t)
        k_heads = [k_ref[b, rows, h * DK:(h + 1) * DK] for h in range(N_HEADS)]
        k_bd = jnp.concatenate([jnp.concatenate([k_heads[h] if g == h else zeros_k for g in range(N_HEADS)], axis=1)
                                for h in range(N_HEADS)], axis=0)
        qk = lax.dot_general(jnp.concatenate([q_ref[b, rows, :], k_ref[b, rows, :]], axis=0), k_bd, NT,
                             preferred_element_type=f32)
        gcol = gcol_ref[b, rows, :]
        cum = jnp.dot(cum_rows, gcol, preferred_element_type=f32, precision=HIGHEST)
        b_cols = [cum[:, h:h + 1] for h in range(N_HEADS)]
        beta_cols = [gcol[:, N_HEADS + h:N_HEADS + h + 1] for h in range(N_HEADS)]
        b_row = jnp.sum(jnp.where(own_row, jnp.dot(grow_ref[b, t], cum_cols, preferred_element_type=f32,
                                                   precision=HIGHEST), 0.0), axis=0, keepdims=True)
        decay = jnp.exp(jnp.where(lower, spread(b_cols) - b_row, -jnp.inf))
        x = jnp.where(strict, qk[c:] * spread(beta_cols) * decay, 0.0) * -1.0
        return (qk[:c] * decay, x, jnp.concatenate(b_cols, axis=0), jnp.concatenate(beta_cols, axis=0))

    attns, xs, b_colv, betav = zip(*[chunk_setup(b, t) for b, t in items])
    ns = xs
    x_bds = [block_diag(x.astype(bf16)) for x in xs]
    for _ in range(int(math.log2(c)) - 1):
        xs = [_dot(x.astype(bf16), x_bd) for x, x_bd in zip(xs, x_bds)]
        x_bds = [block_diag(x.astype(bf16)) for x in xs]
        ns = [n + x + _dot(n.astype(bf16), x_bd) for n, x, x_bd in zip(ns, xs, x_bds)]

    def chunk_wy(i):
        b, rows = items[i][0], tok(items[i][1])
        b_col, beta = b_colv[i], betav[i]
        kst = jnp.concatenate([k_ref[b, rows, h * DK:(h + 1) * DK] for h in range(N_HEADS)], axis=0)
        qst = jnp.concatenate([q_ref[b, rows, h * DK:(h + 1) * DK] for h in range(N_HEADS)], axis=0)
        vst = jnp.concatenate([v_ref[b, rows, h * DV:(h + 1) * DV] for h in range(N_HEADS)], axis=0).astype(f32)
        n_bd = block_diag(ns[i].astype(bf16))
        kf = kst.astype(f32)
        e_b = jnp.exp(b_col)
        ru, rw = vst * beta, kf * (beta * e_b)
        u = ru + _dot(n_bd, ru.astype(bf16))
        w = (rw + _dot(n_bd, rw.astype(bf16))).astype(bf16)
        q_dec = (qst.astype(f32) * e_b).astype(bf16)
        return u, w, q_dec, kf, block_diag(attns[i].astype(bf16))

    wy = [chunk_wy(i) for i in chunks]
    for i in chunks:
        b, rows = items[i][0], tok(items[i][1])
        u, w, q_dec, kf, attn_bd = wy[i]
        b_col = b_colv[i]
        v_new, o_inter = [], []
        for h in range(N_HEADS):
            hs = slice(h * c, (h + 1) * c)
            r = _dot(jnp.concatenate([w[hs], q_dec[hs]], axis=0), s_ref[b, h].astype(bf16))
            v_new.append(u[hs] - r[:c])
            o_inter.append(r[c:])
        vb = jnp.concatenate(v_new, axis=0).astype(bf16)
        o = jnp.concatenate(o_inter, axis=0) + _dot(attn_bd, vb)
        for h in range(N_HEADS):
            hs = slice(h * c, (h + 1) * c)
            b_last = b_col[(h + 1) * c - 1:(h + 1) * c]
            k_dec = (kf[hs] * jnp.exp(b_last - b_col[hs])).astype(bf16)
            s_ref[b, h] = (s_ref[b, h] * jnp.exp(b_last)
                           + lax.dot_general(k_dec, vb[hs], TN, preferred_element_type=f32))
            o_ref[b, rows, h * DV:(h + 1) * DV] = (
                _rms(o[hs], nw) * sz_ref[b, rows, h * DV:(h + 1) * DV].astype(f32)).astype(bf16)


def _scans_body(rq_ref, rk_ref, rv_ref, rg_ref, gq_ref, gk_ref, gv_ref, gz_ref, gcol_ref, grow_ref, nw_ref,
                oret_ref, sret_ref, ogdn_ref, sgdn_ref, ret_state, gdn_state, *, ret_chunk):
    j = pl.program_id(1)

    @pl.when(j == 0)
    def _():
        ret_state[...] = jnp.zeros_like(ret_state)
        gdn_state[...] = jnp.zeros_like(gdn_state)

    for step in _ret_steps(rq_ref, rk_ref, rv_ref, rg_ref, oret_ref, ret_state, ret_chunk):
        step()
    _gdn_steps(gq_ref, gk_ref, gv_ref, gz_ref, gcol_ref, grow_ref, nw_ref, ogdn_ref, gdn_state)

    @pl.when(j == pl.num_programs(1) - 1)
    def _():
        sret_ref[...] = ret_state[...]
        sgdn_ref[...] = gdn_state[...]


def _scans(layer, rq, rk, rv, rg, gq, gk, gv, gz, gcol, grow4, nw, tile, nb, ret_chunk):
    b, l, _ = gq.shape
    assert b % nb == 0 and tile % ret_chunk == 0
    tok = lambda w: pl.BlockSpec((nb, tile, w), lambda i, j: (i, j, 0))
    st = pl.BlockSpec((nb, N_HEADS, DK, DV), lambda i, j: (i, 0, 0, 0))
    o_sds = jax.ShapeDtypeStruct((b, l, V_W), bf16)
    s_sds = jax.ShapeDtypeStruct((b, N_HEADS, DK, DV), f32)
    state = pltpu.VMEM((nb, N_HEADS, DK, DV), f32)
    return pl.pallas_call(
        functools.partial(_scans_body, ret_chunk=ret_chunk), grid=(b // nb, l // tile),
        in_specs=[tok(QK_W), tok(QK_W), tok(V_W), tok(V_W), tok(QK_W), tok(QK_W), tok(V_W), tok(V_W), tok(LANES),
                  pl.BlockSpec((nb, tile // CHUNK, SUBLANES, CHUNK), lambda i, j: (i, j, 0, 0)),
                  _resident((1, DV), layer)],
        out_specs=[tok(V_W), st, tok(V_W), st], out_shape=[o_sds, s_sds, o_sds, s_sds],
        scratch_shapes=[state, state],
        compiler_params=_params("parallel", "arbitrary"), name="scans",
    )(rq, rk, rv, rg, gq, gk, gv, gz, gcol, grow4, nw)


def _softmax(s):
    e = jnp.exp(s - jnp.max(s, axis=-1, keepdims=True))
    return e / jnp.sum(e, axis=-1, keepdims=True)


def _merge_tail(x, xn, o_ret, o_gdn, mem_proj, wg_ref, wr_ref, wd_ref, wo_ref, gpost):
    gates = jax.nn.sigmoid(_dot(xn, wg_ref[...]))
    merged = (gates[:, :D_MODEL] * _dot(o_ret, wr_ref[...])
              + gates[:, D_MODEL:2 * D_MODEL] * _dot(o_gdn, wd_ref[...])
              + gates[:, 2 * D_MODEL:] * mem_proj)
    y = _dot(merged.astype(bf16), wo_ref[...])
    return x + _rms(y, gpost)


def _merge_body(h_ref, gpre_ref, wq_ref, wg_ref, mk_ref, mv_ref, oret_ref, ogdn_ref,
                wr_ref, wd_ref, wm_ref, wo_ref, gpost_ref, o_ref):
    x = h_ref[...]
    xn = _rms(x, gpre_ref[...]).astype(bf16)
    mq = _dot(xn, wq_ref[...]).astype(bf16)
    mem_proj = None
    for h in range(N_HEADS):
        sl = slice(h * HD_X, (h + 1) * HD_X)
        s = lax.dot_general(mq[:, sl], mk_ref[:, sl], NT, preferred_element_type=f32) * HD_X ** -0.5
        o = _dot(_softmax(s).astype(bf16), mv_ref[:, sl])
        part = _dot(o.astype(bf16), wm_ref[sl, :])
        mem_proj = part if mem_proj is None else mem_proj + part
    o_ref[...] = _merge_tail(x, xn, oret_ref[...], ogdn_ref[...], mem_proj, wg_ref, wr_ref, wd_ref, wo_ref,
                             gpost_ref[...])


def _merge(layer, h3d, gpre, wq, wg, mk, mv, o_ret, o_gdn, wr, wd, wm, wo, gpost, tm):
    b, l, _ = h3d.shape
    tok = pl.BlockSpec((None, tm, D_MODEL), lambda i, j: (i, j, 0))
    mem = pl.BlockSpec((None, None, N_MEM, D_MODEL), lambda i, j: (layer, i, 0, 0))
    sq = _resident((D_MODEL, D_MODEL), layer)
    return pl.pallas_call(
        _merge_body, grid=(b, l // tm),
        in_specs=[tok, _resident((1, D_MODEL), layer), _packed_mq(layer), _packed_gates(layer), mem, mem,
                  tok, tok, sq, sq, sq, sq, _resident((1, D_MODEL), layer)],
        out_specs=tok, out_shape=jax.ShapeDtypeStruct((b, l, D_MODEL), f32),
        compiler_params=_params("parallel", "parallel"), name="merge",
    )(h3d, gpre, wq, wg, mk, mv, o_ret, o_gdn, wr, wd, wm, wo, gpost)


def _sproj_body(h_ref, gpre_ref, wm_ref, wab_ref, wq_ref, cw_ref, alog_ref, dtb_ref, cos_ref, sin_ref, buf_ref,
                rq_ref, rk_ref, rv_ref, rg_ref, gq_ref, gk_ref, gv_ref, gz_ref, gcol_ref, mq_ref, nbuf_ref):
    xn = _rms(h_ref[...], gpre_ref[...]).astype(bf16)
    p = _dot(xn, wm_ref[...])
    cos, sin = cos_ref[...], sin_ref[...]
    for h in range(N_HEADS):
        sl = slice(h * DK, (h + 1) * DK)
        rq_ref[:, sl] = _rotary(p[:, _RQ + h * DK:_RQ + (h + 1) * DK], cos, sin)
        rk_ref[:, sl] = _rotary(p[:, _RK + h * DK:_RK + (h + 1) * DK], cos, sin) * DK ** -0.5
    rv_ref[...] = p[:, _RV:_RG]
    rg_ref[...] = _silu(p[:, _RG:_GQKV])
    gz_ref[...] = _silu(p[:, _GZ:_MAIN_W])
    x = p[:, _GQKV:_GZ]
    cw = cw_ref[...]
    conv = x * cw[CONV_W - 1:CONV_W]
    for s in range(CONV_W - 1):
        prev = buf_ref[:, s * CONV_DIM:(s + 1) * CONV_DIM]
        conv = conv + prev * cw[s:s + 1]
        if s > 0:
            nbuf_ref[:, (s - 1) * CONV_DIM:s * CONV_DIM] = prev
    nbuf_ref[:, (CONV_W - 2) * CONV_DIM:] = x
    qs, ks, v = _gdn_qkv(conv)
    for h in range(N_HEADS):
        gq_ref[:, h * DK:(h + 1) * DK] = qs[h]
        gk_ref[:, h * DK:(h + 1) * DK] = ks[h]
    gv_ref[...] = v
    ab = _dot(xn, wab_ref[...])
    gcol_ref[...] = _decay_beta(ab, alog_ref[...], dtb_ref[...], _iota(ab.shape, 1) < N_HEADS)
    mq_ref[...] = _dot(xn, wq_ref[...])


def _sample_proj(layer, h2d, gpre, wm, wab, wq, cw, alog, dtb, cos, sin, buf2d):
    n = h2d.shape[0]
    sds = lambda w: jax.ShapeDtypeStruct((n, w), f32)
    res = lambda *shape: _resident(shape, layer)
    widths = (QK_W, QK_W, V_W, V_W, QK_W, QK_W, V_W, V_W, LANES, D_MODEL, (CONV_W - 1) * CONV_DIM)
    return pl.pallas_call(
        _sproj_body, grid=(1,),
        in_specs=[_resident((n, D_MODEL)), res(1, D_MODEL), res(D_MODEL, _MAIN_W), res(D_MODEL, LANES),
                  _packed_mq(layer), res(CONV_W, CONV_DIM), res(1, LANES), res(1, LANES),
                  _resident((1, DK)), _resident((1, DK)), _resident((n, (CONV_W - 1) * CONV_DIM))],
        out_specs=[pl.BlockSpec((n, w), lambda i: (0, 0)) for w in widths],
        out_shape=[sds(w) for w in widths],
        compiler_params=_params("arbitrary"), name="sample_proj",
    )(h2d, gpre, wm, wab, wq, cw, alog, dtb, cos, sin, buf2d)


def _sstate_body(cols_ref, rv_ref, gv_ref, gcol_ref, sret_ref, sgdn_ref, nret_ref, ngdn_ref, oret_ref, ogdn_ref):
    vecs = cols_ref[...].T
    col = lambda j, v, h: vecs[:, (j * 4 + v) * N_HEADS + h:(j * 4 + v) * N_HEADS + h + 1]
    for j in range(rv_ref.shape[0]):
        for h in range(N_HEADS):
            vs = slice(h * DV, (h + 1) * DV)
            q, k = col(j, 0, h), col(j, 1, h)
            s_new = sret_ref[j, h] * math.exp(LOG_GAMMA[h]) + k * rv_ref[j:j + 1, vs]
            nret_ref[j, h] = s_new
            oret_ref[j:j + 1, vs] = jnp.sum(q * s_new, axis=0, keepdims=True)
            q, k = col(j, 2, h), col(j, 3, h)
            eg = jnp.exp(gcol_ref[j:j + 1, h:h + 1])
            beta = gcol_ref[j:j + 1, N_HEADS + h:N_HEADS + h + 1]
            s = sgdn_ref[j, h]
            ks = jnp.sum(k * s, axis=0, keepdims=True)
            v_new = beta * (gv_ref[j:j + 1, vs] - eg * ks)
            s_new = s * eg + k * v_new
            ngdn_ref[j, h] = s_new
            ogdn_ref[j:j + 1, vs] = jnp.sum(q * s_new, axis=0, keepdims=True)


def _sattn_body(mq_ref, k_ref, v_ref, o_ref):
    for j in range(mq_ref.shape[0]):
        s = jnp.sum(k_ref[j] * mq_ref[j][None], axis=-1, keepdims=True) * HD_X ** -0.5
        e = jnp.exp(s - jnp.max(s, axis=0, keepdims=True))
        p = e / jnp.sum(e, axis=0, keepdims=True)
        o_ref[j] = jnp.sum(p * v_ref[j], axis=0)


def _smerge_body(h_ref, gpre_ref, wg_ref, oret_ref, rg_ref, ogdn_ref, gz_ref, nw_ref, omem_ref,
                 wr_ref, wd_ref, wm_ref, wo_ref, gpost_ref, o_ref):
    x = h_ref[...]
    xn = _rms(x, gpre_ref[...]).astype(bf16)
    nw = nw_ref[...]
    rets, gdns = [], []
    for h in range(N_HEADS):
        vs = slice(h * DV, (h + 1) * DV)
        rets.append(rg_ref[:, vs] * _rms(oret_ref[:, vs]))
        gdns.append(_rms(ogdn_ref[:, vs], nw) * gz_ref[:, vs])
    o_ret = jnp.concatenate(rets, axis=-1).astype(bf16)
    o_gdn = jnp.concatenate(gdns, axis=-1).astype(bf16)
    mem_proj = _dot(omem_ref[...].astype(bf16), wm_ref[...])
    o_ref[...] = _merge_tail(x, xn, o_ret, o_gdn, mem_proj, wg_ref, wr_ref, wd_ref, wo_ref, gpost_ref[...])


def _sample_merge(layer, h2d, gpre, wg, o_ret, rg, o_gdn, gz, nw, o_mem, wr, wd, wm, wo, gpost):
    n = h2d.shape[0]
    act = _resident((n, D_MODEL))
    sq = _resident((D_MODEL, D_MODEL), layer)
    return pl.pallas_call(
        _smerge_body, grid=(1,),
        in_specs=[act, _resident((1, D_MODEL), layer), _packed_gates(layer), act, act, act, act,
                  _resident((1, DV), layer), act, sq, sq, sq, sq, _resident((1, D_MODEL), layer)],
        out_specs=pl.BlockSpec((n, D_MODEL), lambda i: (0, 0)), out_shape=jax.ShapeDtypeStruct(h2d.shape, f32),
        compiler_params=_params("arbitrary"), name="sample_merge",
    )(h2d, gpre, wg, o_ret, rg, o_gdn, gz, nw, o_mem, wr, wd, wm, wo, gpost)


def _rope_tables(pos):
    half = DK // 2
    inv_freq = ROPE_BASE ** (-jnp.arange(half, dtype=f32) / half)
    ang = pos.astype(f32)[:, None] * inv_freq[None, :]
    cos, sin = jnp.cos(ang), jnp.sin(ang)
    return jnp.concatenate([cos, cos], axis=-1), jnp.concatenate([-sin, sin], axis=-1)


def _row_tile(n, want):
    t = min(n, want)
    assert n % t == 0, (n, t)
    return t


def kernel(x_prompt, x_sample, mem_prompt, state_ret, state_gdn, state_conv, cache_mem_k, cache_mem_v, norm_ffn1_pre, norm_ffn1_post, ffn1_w_in, ffn1_w_out, norm_mix_pre, norm_mix_post, w_in, gdn_conv_w, gdn_a_log, gdn_dt_bias, gdn_norm, norm_mem, w_mem_k, w_mem_v, w_branch_ret, w_branch_gdn, w_branch_mem, w_out, norm_ffn2_pre, norm_ffn2_post, ffn2_w_in, ffn2_w_out):
    bp, sp, _ = x_prompt.shape
    ns, ss, _ = x_sample.shape
    depth = w_in.shape[0]
    assert ss == 1 and sp % CHUNK == 0
    tm_ffn = _row_tile(bp * sp, 512)
    tm_seq = _row_tile(sp, 512)
    tm_proj = _row_tile(sp, 256)

    cos_p, sin_p = _rope_tables(jnp.arange(sp))
    cos_s, sin_s = _rope_tables(PAST_LEN + jnp.arange(ss))
    row = lambda v: v.reshape(depth, 1, -1).astype(f32)
    pad_lanes = lambda v: jnp.pad(v.astype(f32), ((0, 0), (0, LANES - v.shape[1]))).reshape(depth, 1, LANES)
    pad_rows = lambda v: jnp.pad(v.astype(f32), ((0, 0), (0, SUBLANES - v.shape[1]))).reshape(depth, SUBLANES, 1)
    w_in_t = jnp.swapaxes(w_in, 1, 2)
    w_main = _w_in_cols(w_in_t, 0, _MAIN_W)
    w_mq = _w_in_cols(w_in_t, _AB_END, D_MODEL)
    w_gates = _w_in_cols(w_in_t, _MQ_END, 3 * D_MODEL)
    w_ab_row = w_in_t[:, _MAIN_W:_AB_END].astype(bf16)
    w_ab_col = jnp.pad(jnp.swapaxes(w_ab_row, 1, 2), ((0, 0), (0, 0), (0, LANES - 2 * N_HEADS)))
    f1_in, f1_out, f2_in, f2_out = (w.astype(bf16) for w in (ffn1_w_in, ffn1_w_out, ffn2_w_in, ffn2_w_out))
    wr, wd, wm, wo = (w.astype(bf16) for w in (w_branch_ret, w_branch_gdn, w_branch_mem, w_out))
    cw = gdn_conv_w.astype(f32)
    alog, dtb = pad_lanes(gdn_a_log), pad_lanes(gdn_dt_bias)
    alog_t, dtb_t = pad_rows(gdn_a_log), pad_rows(gdn_dt_bias)
    nw = row(gdn_norm)
    g_f1pre, g_f1post, g_f2pre, g_f2post = (row(g) for g in (norm_ffn1_pre, norm_ffn1_post, norm_ffn2_pre,
                                                              norm_ffn2_post))
    g_mpre, g_mpost = row(norm_mix_pre), row(norm_mix_post)

    mk, mv, mk_b, mv_b = _memkv(mem_prompt, row(norm_mem), w_mem_k.astype(bf16), w_mem_v.astype(bf16))
    hp = x_prompt
    hs = x_sample.reshape(ns, D_MODEL)
    outs = [[] for _ in range(4)]
    new_states = None
    scan_seqs = 2 if bp % 2 == 0 else 1
    scan_tile = _row_tile(sp, 256)
    ret_chunk = scan_tile
    hosted_per_step = ns // (2 * (bp * sp // tm_ffn))
    assert hosted_per_step * 2 * (bp * sp // tm_ffn) == ns
    for l in range(depth):
        h1s = _ffn(l, hs, g_f1pre, f1_in, f1_out, g_f1post, ns)
        s_rq, s_rk, s_rv, s_rg, s_gq, s_gk, s_gv, s_gz, s_gcol, mq, nbuf = _sample_proj(
            l, h1s, g_mpre, w_main, w_ab_col, w_mq, cw, alog, dtb, cos_s, sin_s,
            state_conv[l].reshape(ns, (CONV_W - 1) * CONV_DIM))
        outs[3].append(nbuf.reshape(ns, CONV_W - 1, CONV_DIM))
        groups = ns // hosted_per_step
        by_step = lambda a: a.reshape((groups, hosted_per_step) + a.shape[1:])
        cols = jnp.stack([s_rq, s_rk, s_gq, s_gk], axis=1).reshape(groups, hosted_per_step * 4 * N_HEADS, DK)
        host = lambda part: (part, 2, hosted_per_step, by_step(mq.reshape(ns, N_HEADS, HD_X)), cache_mem_k,
                             cache_mem_v, cols, by_step(s_rv), by_step(s_gv), by_step(s_gcol), state_ret,
                             state_gdn, new_states)

        h1, o_mem_a, n_ret, n_gdn, so_ret_a, so_gdn_a = _ffn(
            l, hp.reshape(bp * sp, D_MODEL), g_f1pre, f1_in, f1_out, g_f1post, tm_ffn, hosted=host(0))
        new_states = (n_ret, n_gdn)
        h1 = h1.reshape(bp, sp, D_MODEL)
        rq, rk, rv, rg, gq, gk, gv, gz, gcol, grow, tail = _proj(
            l, h1, g_mpre, w_main, w_ab_col, w_ab_row, cw, alog, dtb, alog_t, dtb_t, cos_p, sin_p, tm_proj)
        grow4 = grow.reshape(bp, SUBLANES, sp // CHUNK, CHUNK).transpose(0, 2, 1, 3)
        o_ret, s_ret, o_gdn, s_gdn = _scans(l, rq, rk, rv, rg, gq, gk, gv, gz, gcol, grow4, nw, scan_tile,
                                            scan_seqs, ret_chunk)
        h2 = _merge(l, h1, g_mpre, w_mq, w_gates, mk_b, mv_b, o_ret, o_gdn, wr, wd, wm, wo, g_mpost, tm_seq)
        hp, o_mem_b, n_ret, n_gdn, so_ret_b, so_gdn_b = _ffn(
            l, h2.reshape(bp * sp, D_MODEL), g_f2pre, f2_in, f2_out, g_f2post, tm_ffn, hosted=host(1))
        new_states = (n_ret, n_gdn)
        hp = hp.reshape(bp, sp, D_MODEL)
        outs[0].append(s_ret)
        outs[1].append(s_gdn)
        outs[2].append(tail[:, SUBLANES - (CONV_W - 1):])

        join = lambda a, b: jnp.concatenate([a, b], axis=0).reshape(ns, -1)
        o_mem, so_ret, so_gdn = join(o_mem_a, o_mem_b), join(so_ret_a, so_ret_b), join(so_gdn_a, so_gdn_b)
        h2s = _sample_merge(l, h1s, g_mpre, w_gates, so_ret, s_rg, so_gdn, s_gz, nw, o_mem, wr, wd, wm, wo, g_mpost)
        hs = _ffn(l, h2s, g_f2pre, f2_in, f2_out, g_f2post, ns)

    ret_p, gdn_p, conv_p, conv_s = (jnp.stack(o) for o in outs)
    return (hp, hs.reshape(ns, ss, D_MODEL), ret_p, gdn_p, conv_p, mk, mv) + new_states + (conv_s,)
```

```python
import functools
import math

import jax
import jax.numpy as jnp
from jax import lax
from jax.experimental import pallas as pl
from jax.experimental.pallas import tpu as pltpu

f32, bf16 = jnp.float32, jnp.bfloat16

D_MODEL = 1024
N_HEADS = 4
DK = 128
DV = 256
QK_W = N_HEADS * DK
V_W = N_HEADS * DV
CONV_W = 4
CONV_DIM = 2 * QK_W + V_W
HD_X = 256
N_MEM = 256
D_FF = 2816
EPS = 1e-6
ROPE_BASE = 10000.0
PAST_LEN = 16384
CHUNK = 64
LANES = 128
SUBLANES = 8
VMEM_LIMIT = 56 * 1024 * 1024
LOG_GAMMA = tuple(math.log1p(-2.0 ** (-5.0 - h)) for h in range(N_HEADS))
HIGHEST = lax.Precision.HIGHEST
NT = (((1,), (1,)), ((), ()))
TN = (((0,), (0,)), ((), ()))


def _params(*sem):
    return pltpu.CompilerParams(dimension_semantics=sem, vmem_limit_bytes=VMEM_LIMIT)


def _resident(shape, layer=None, col_block=0):
    if layer is None:
        return pl.BlockSpec(shape, lambda *_: (0,) * len(shape), pipeline_mode=pl.Buffered(1))
    index = (layer,) + (0,) * (len(shape) - 1) + (col_block,)
    return pl.BlockSpec((None,) + shape, lambda *_: index, pipeline_mode=pl.Buffered(1))


def _rms(x, gain=None):
    y = x * lax.rsqrt(jnp.mean(x * x, axis=-1, keepdims=True) + EPS)
    return y if gain is None else y * gain


def _silu(x):
    return x * jax.nn.sigmoid(x)


def _softplus(x):
    return jnp.maximum(x, 0.0) + jnp.log1p(jnp.exp(-jnp.abs(x)))


def _dot(a, b):
    return jnp.dot(a, b, preferred_element_type=f32)


def _iota(shape, dim):
    return lax.broadcasted_iota(jnp.int32, shape, dim)


def _ffn_body(x_ref, gpre_ref, wgu_ref, wo_ref, gpost_ref, o_ref):
    x = x_ref[...]
    xn = _rms(x, gpre_ref[...]).astype(bf16)
    h = _dot(xn, wgu_ref[...])
    act = (_silu(h[:, :D_FF]) * h[:, D_FF:]).astype(bf16)
    y = _dot(act, wo_ref[...])
    o_ref[...] = x + 0.5 * _rms(y, gpost_ref[...])


def _ffn_hosting_body(x_ref, gpre_ref, wgu_ref, wo_ref, gpost_ref, mq_ref, k_ref, v_ref,
                      cols_ref, rv_ref, gv_ref, gcol_ref, sret_ref, sgdn_ref, *refs):
    o_ref, omem_ref, nret_ref, ngdn_ref, oret_ref, ogdn_ref = refs[-6:]
    _sstate_body(cols_ref, rv_ref, gv_ref, gcol_ref, sret_ref, sgdn_ref, nret_ref, ngdn_ref, oret_ref, ogdn_ref)
    _sattn_body(mq_ref, k_ref, v_ref, omem_ref)
    _ffn_body(x_ref, gpre_ref, wgu_ref, wo_ref, gpost_ref, o_ref)


def _ffn(layer, x2d, gpre, wgu, wo, gpost, tm, hosted=None):
    t = x2d.shape[0]
    steps = t // tm
    row = pl.BlockSpec((tm, D_MODEL), lambda i: (i, 0))
    in_specs = [row, _resident((1, D_MODEL), layer), _resident((D_MODEL, 2 * D_FF), layer),
                _resident((D_FF, D_MODEL), layer), _resident((1, D_MODEL), layer)]
    out_row = jax.ShapeDtypeStruct((t, D_MODEL), f32)
    if hosted is None:
        return pl.pallas_call(
            _ffn_body, grid=(steps,), in_specs=in_specs, out_specs=row, out_shape=out_row,
            compiler_params=_params("parallel"), name="ffn",
        )(x2d, gpre, wgu, wo, gpost)
    part, parts, ps, mq, mem_k, mem_v, cols, rv, gv, gcol, s_ret, s_gdn, prev = hosted
    depth, n = s_ret.shape[:2]
    assert ps * parts * steps == n, (n, parts, steps, ps)
    first = part * steps
    grp = lambda *shape: pl.BlockSpec((None,) + shape, lambda i: (first + i,) + (0,) * len(shape))
    out_grp = lambda *shape: pl.BlockSpec((None,) + shape, lambda i: (i,) + (0,) * len(shape))
    mem = pl.BlockSpec((None, ps, N_MEM, N_HEADS, HD_X), lambda i: (layer, first + i, 0, 0, 0))
    st = pl.BlockSpec((None, ps, N_HEADS, DK, DV), lambda i: (layer, first + i, 0, 0, 0))
    st_sds = jax.ShapeDtypeStruct((depth, n, N_HEADS, DK, DV), f32)
    grp_sds = lambda *shape: jax.ShapeDtypeStruct((steps,) + shape, f32)
    prev = [] if prev is None else list(prev)
    prev_specs = [pl.BlockSpec(memory_space=pl.ANY)] * len(prev)
    aliases = {14 + i: 2 + i for i in range(len(prev))}
    return pl.pallas_call(
        _ffn_hosting_body, grid=(steps,),
        in_specs=in_specs + [grp(ps, N_HEADS, HD_X), mem, mem,
                             grp(ps * 4 * N_HEADS, DK),
                             grp(ps, V_W), grp(ps, V_W), grp(ps, LANES), st, st] + prev_specs,
        out_specs=[row, out_grp(ps, N_HEADS, HD_X), st, st, out_grp(ps, V_W), out_grp(ps, V_W)],
        out_shape=[out_row, grp_sds(ps, N_HEADS, HD_X), st_sds, st_sds, grp_sds(ps, V_W), grp_sds(ps, V_W)],
        input_output_aliases=aliases, compiler_params=_params("parallel"), name="ffn_hosting",
    )(x2d, gpre, wgu, wo, gpost, mq, mem_k, mem_v, cols, rv, gv, gcol, s_ret, s_gdn, *prev)


def _memkv_body(m_ref, g_ref, wk_ref, wv_ref, k_ref, v_ref, kb_ref, vb_ref):
    nb = m_ref.shape[0]
    mn = _rms(m_ref[...].reshape(nb * N_MEM, D_MODEL), g_ref[...]).astype(bf16)
    k = _dot(mn, wk_ref[...])
    v = _dot(mn, wv_ref[...])
    for b in range(nb):
        rows = slice(b * N_MEM, (b + 1) * N_MEM)
        for h in range(N_HEADS):
            k_ref[b, :, h, :] = k[rows, h * HD_X:(h + 1) * HD_X]
            v_ref[b, :, h, :] = v[rows, h * HD_X:(h + 1) * HD_X]
        kb_ref[b] = k[rows].astype(bf16)
        vb_ref[b] = v[rows].astype(bf16)


def _memkv(mem, g, wk, wv):
    b = mem.shape[0]
    depth = wk.shape[0]
    nb = 4 if b % 4 == 0 else 1
    per_layer = lambda shape: pl.BlockSpec((None,) + shape, lambda l, i: (l,) + (0,) * len(shape))
    out5 = pl.BlockSpec((None, nb, N_MEM, N_HEADS, HD_X), lambda l, i: (l, i, 0, 0, 0))
    outb = pl.BlockSpec((None, nb, N_MEM, D_MODEL), lambda l, i: (l, i, 0, 0))
    sds5 = jax.ShapeDtypeStruct((depth, b, N_MEM, N_HEADS, HD_X), f32)
    sdsb = jax.ShapeDtypeStruct((depth, b, N_MEM, D_MODEL), bf16)
    return pl.pallas_call(
        _memkv_body, grid=(depth, b // nb),
        in_specs=[pl.BlockSpec((nb, N_MEM, D_MODEL), lambda l, i: (i, 0, 0)), per_layer((1, D_MODEL)),
                  per_layer((D_MODEL, D_MODEL)), per_layer((D_MODEL, D_MODEL))],
        out_specs=[out5, out5, outb, outb], out_shape=[sds5, sds5, sdsb, sdsb],
        compiler_params=_params("parallel", "parallel"), name="memkv",
    )(mem, g, wk, wv)


_RQ, _RK, _RV, _RG, _GQKV, _GZ, _MAIN_W = 0, 512, 1024, 2048, 3072, 5120, 6144
_AB_END = _MAIN_W + 2 * N_HEADS
_MQ_END = _AB_END + D_MODEL


def _packed_mq(layer):
    return _resident((D_MODEL, D_MODEL), layer)


def _packed_gates(layer):
    return _resident((D_MODEL, 3 * D_MODEL), layer)


def _w_in_cols_body(wt_ref, o_ref):
    o_ref[...] = wt_ref[0].T.astype(bf16)


def _w_in_cols(w_in_t, first, width, blk=1024):
    depth = w_in_t.shape[0]
    assert first % SUBLANES == 0 and width % blk == 0
    return pl.pallas_call(
        _w_in_cols_body, grid=(depth, width // blk),
        in_specs=[pl.BlockSpec((pl.Element(1), pl.Element(blk), pl.Element(D_MODEL)),
                               lambda l, c: (l, pl.multiple_of(first + c * blk, SUBLANES), 0))],
        out_specs=pl.BlockSpec((None, D_MODEL, blk), lambda l, c: (l, 0, c)),
        out_shape=jax.ShapeDtypeStruct((depth, D_MODEL, width), bf16),
        compiler_params=_params("parallel", "parallel"), name="w_in_cols",
    )(w_in_t)


def _packed_views(packed_ref):
    bounds = (_RQ, _RK, _RV, _RG, _GQKV, _GQKV + QK_W, _GQKV + 2 * QK_W, _GZ, _MAIN_W)
    lead = (slice(None),) * (len(packed_ref.shape) - 1)
    return [packed_ref.at[lead + (slice(lo, hi),)] for lo, hi in zip(bounds[:-1], bounds[1:])]


def _rotary(x, cos, sin_signed):
    return x * cos + pltpu.roll(x, DK // 2, axis=1) * sin_signed


def _decay_beta(ab, a_log, dt_bias, is_decay):
    g = -jnp.exp(a_log) * _softplus(ab + dt_bias)
    return jnp.where(is_decay, g, jax.nn.sigmoid(ab))


def _gdn_qkv(conv):
    a = _silu(conv)
    qs, ks = [], []
    for h in range(N_HEADS):
        q = a[:, h * DK:(h + 1) * DK]
        k = a[:, QK_W + h * DK:QK_W + (h + 1) * DK]
        qs.append(q * (lax.rsqrt(jnp.sum(q * q, axis=-1, keepdims=True) + EPS) * DK ** -0.5))
        ks.append(k * lax.rsqrt(jnp.sum(k * k, axis=-1, keepdims=True) + EPS))
    return qs, ks, a[:, 2 * QK_W:]


def _proj_body(h_ref, gpre_ref, wm_ref, wab_ref, wabt_ref, cw_ref, alog_ref, dtb_ref, alogt_ref, dtbt_ref,
               cos_ref, sin_ref, packed_ref, gcol_ref, grow_ref, tail_ref, carry_ref):
    j = pl.program_id(1)
    tm = h_ref.shape[0]
    rq_ref, rk_ref, rv_ref, rg_ref, gq_ref, gk_ref, gv_ref, gz_ref = _packed_views(packed_ref)

    @pl.when(j == 0)
    def _():
        carry_ref[...] = jnp.zeros_like(carry_ref)

    xn = _rms(h_ref[...], gpre_ref[...]).astype(bf16)
    cos, sin = cos_ref[...], sin_ref[...]

    def ret_qk(p):
        for h in range(N_HEADS):
            sl = slice(h * DK, (h + 1) * DK)
            rq_ref[:, sl] = _rotary(p[:, h * DK:(h + 1) * DK], cos, sin).astype(bf16)
            rk_ref[:, sl] = (_rotary(p[:, QK_W + h * DK:QK_W + (h + 1) * DK], cos, sin) * DK ** -0.5).astype(bf16)

    def ret_v(p):
        rv_ref[...] = p.astype(bf16)

    def ret_gate(p):
        rg_ref[...] = _silu(p).astype(bf16)

    def gdn_z(p):
        gz_ref[...] = _silu(p).astype(bf16)

    def conv_silu(x, cols):
        carry = carry_ref[:, cols]
        cw = cw_ref[:, cols]
        conv = x * cw[CONV_W - 1:CONV_W]
        conv_top = x[:SUBLANES] * cw[CONV_W - 1:CONV_W]
        first = _iota(conv_top.shape, 0)
        for s in range(1, CONV_W):
            w = cw[CONV_W - 1 - s:CONV_W - s]
            xs = pltpu.roll(x, s, axis=0)
            conv = conv + xs * w
            conv_top = conv_top + jnp.where(first < s, pltpu.roll(carry, s, axis=0), xs[:SUBLANES]) * w
        carry_ref[:, cols] = x[tm - SUBLANES:]
        tail_ref[:, cols] = x[tm - SUBLANES:]
        return (slice(None), _silu(conv)), (slice(0, SUBLANES), _silu(conv_top))

    def gdn_qk(p):
        for rows, a in conv_silu(p, slice(0, 2 * QK_W)):
            for h in range(N_HEADS):
                q = a[:, h * DK:(h + 1) * DK]
                k = a[:, QK_W + h * DK:QK_W + (h + 1) * DK]
                q = q * (lax.rsqrt(jnp.sum(q * q, axis=-1, keepdims=True) + EPS) * DK ** -0.5)
                k = k * lax.rsqrt(jnp.sum(k * k, axis=-1, keepdims=True) + EPS)
                gq_ref[rows, h * DK:(h + 1) * DK] = q.astype(bf16)
                gk_ref[rows, h * DK:(h + 1) * DK] = k.astype(bf16)

    def gdn_v(p):
        for rows, a in conv_silu(p, slice(2 * QK_W, CONV_DIM)):
            gv_ref[rows, :] = a.astype(bf16)

    groups = ((_RQ, _RV, ret_qk), (_RV, _RG, ret_v), (_RG, _GQKV, ret_gate), (_GQKV, _GQKV + 2 * QK_W, gdn_qk),
              (_GQKV + 2 * QK_W, _GZ, gdn_v), (_GZ, _MAIN_W, gdn_z))
    pending = None
    for lo, hi, epilogue in groups:
        p = _dot(xn, wm_ref[:, lo:hi])
        if pending is not None:
            pending()
        pending = functools.partial(epilogue, p)
    pending()

    ab = _dot(xn, wab_ref[...])
    gcol_ref[...] = _decay_beta(ab, alog_ref[...], dtb_ref[...], _iota(ab.shape, 1) < N_HEADS)
    abt = lax.dot_general(wabt_ref[...], xn, NT, preferred_element_type=f32)
    grow_ref[...] = _decay_beta(abt, alogt_ref[...], dtbt_ref[...], _iota(abt.shape, 0) < N_HEADS)


def _proj(layer, h3d, gpre, wm, wab, wabt, cw, alog, dtb, alogt, dtbt, cos, sin, tm):
    b, l, _ = h3d.shape
    tok = lambda w: pl.BlockSpec((None, tm, w), lambda i, j: (i, j, 0))
    tab = pl.BlockSpec((tm, DK), lambda i, j: (j, 0))
    sds = lambda w, dt=bf16: jax.ShapeDtypeStruct((b, l, w), dt)
    res = lambda *shape: _resident(shape, layer)
    return pl.pallas_call(
        _proj_body, grid=(b, l // tm),
        in_specs=[tok(D_MODEL), res(1, D_MODEL), res(D_MODEL, _MAIN_W), res(D_MODEL, LANES),
                  res(SUBLANES, D_MODEL), res(CONV_W, CONV_DIM), res(1, LANES),
                  res(1, LANES), res(SUBLANES, 1), res(SUBLANES, 1), tab, tab],
        out_specs=[tok(_MAIN_W), tok(LANES), pl.BlockSpec((None, SUBLANES, tm), lambda i, j: (i, 0, j)),
                   pl.BlockSpec((None, SUBLANES, CONV_DIM), lambda i, j: (i, 0, 0))],
        out_shape=[sds(_MAIN_W), sds(LANES, f32), jax.ShapeDtypeStruct((b, SUBLANES, l), f32),
                   jax.ShapeDtypeStruct((b, SUBLANES, CONV_DIM), f32)],
        scratch_shapes=[pltpu.VMEM((SUBLANES, CONV_DIM), f32)],
        compiler_params=_params("parallel", "arbitrary"), name="proj",
    )(h3d, gpre, wm, wab, wabt, cw, alog, dtb, alogt, dtbt, cos, sin)


def _ret_steps(q_ref, k_ref, v_ref, sg_ref, o_ref, s_ref, chunk):
    c = chunk
    row, col = _iota((c, c), 0), _iota((c, c), 1)
    dist = (row - col).astype(f32)
    pos = _iota((c, 1), 0).astype(f32)
    per_head = {}

    def head_consts(h):
        if h not in per_head:
            lg = LOG_GAMMA[h]
            per_head[h] = (jnp.exp(jnp.where(row >= col, dist * lg, -jnp.inf)), jnp.exp((pos + 1.0) * lg),
                           jnp.exp((c - 1.0 - pos) * lg))
        return per_head[h]

    def step(h, t, b):
        decay, q_scale, k_scale = head_consts(h)
        rows = slice(t * c, (t + 1) * c)
        qh = q_ref[b, rows, h * DK:(h + 1) * DK]
        kh = k_ref[b, rows, h * DK:(h + 1) * DK]
        vh = v_ref[b, rows, h * DV:(h + 1) * DV]
        attn = lax.dot_general(qh, kh, NT, preferred_element_type=f32) * decay
        q_dec = (qh.astype(f32) * q_scale).astype(bf16)
        k_dec = (kh.astype(f32) * k_scale).astype(bf16)
        s = s_ref[b, h]
        o = _dot(q_dec, s.astype(bf16)) + _dot(attn.astype(bf16), vh)
        s_ref[b, h] = s * math.exp(c * LOG_GAMMA[h]) + lax.dot_general(k_dec, vh, TN, preferred_element_type=f32)
        o_ref[b, rows, h * DV:(h + 1) * DV] = (
            sg_ref[b, rows, h * DV:(h + 1) * DV].astype(f32) * _rms(o)).astype(bf16)

    return [functools.partial(step, h, t, b) for h in range(N_HEADS) for t in range(q_ref.shape[1] // c)
            for b in range(q_ref.shape[0])]


def _gdn_steps(q_ref, k_ref, v_ref, sz_ref, gcol_ref, grow_ref, nw_ref, o_ref, s_ref):
    c = CHUNK
    hc = N_HEADS * c
    row, lane = _iota((c, hc), 0), _iota((c, hc), 1)
    lane_head, col = lane // c, lane % c
    lower, strict = row >= col, row > col
    head_mask = [(lane_head == h).astype(bf16) for h in range(N_HEADS)]
    block_diag = lambda x: jnp.concatenate([x * m for m in head_mask], axis=0)
    spread = lambda cols: functools.reduce(
        lambda acc, h: jnp.where(lane_head == h, cols[h], acc), range(N_HEADS - 1), cols[N_HEADS - 1])
    cum_rows = (_iota((c, c), 0) >= _iota((c, c), 1)).astype(f32)
    cum_cols = (_iota((c, hc), 0) <= (_iota((c, hc), 1) % c)).astype(f32)
    own_row = _iota((SUBLANES, hc), 0) == _iota((SUBLANES, hc), 1) // c
    zeros_k = jnp.zeros((c, DK), bf16)
    nw = nw_ref[...]

    items = [(b, t) for t in range(q_ref.shape[1] // c) for b in range(q_ref.shape[0])]
    chunks = range(len(items))
    tok = lambda t: slice(t * c, (t + 1) * c)

    def chunk_setup(b, t):
        rows = tok(t)
        k_heads = [k_ref[b, rows, h * DK:(h + 1) * DK] for h in range(N_HEADS)]
        k_bd = jnp.concatenate([jnp.concatenate([k_heads[h] if g == h else zeros_k for g in range(N_HEADS)], axis=1)
                                for h in range(N_HEADS)], axis=0)
        qk = lax.dot_general(jnp.concatenate([q_ref[b, rows, :], k_ref[b, rows, :]], axis=0), k_bd, NT,
                             preferred_element_type=f32)
        gcol = gcol_ref[b, rows, :]
        cum = jnp.dot(cum_rows, gcol, preferred_element_type=f32, precision=HIGHEST)
        b_cols = [cum[:, h:h + 1] for h in range(N_HEADS)]
        beta_cols = [gcol[:, N_HEADS + h:N_HEADS + h + 1] for h in range(N_HEADS)]
        b_row = jnp.sum(jnp.where(own_row, jnp.dot(grow_ref[b, t], cum_cols, preferred_element_type=f32,
                                                   precision=HIGHEST), 0.0), axis=0, keepdims=True)
        decay = jnp.exp(jnp.where(lower, spread(b_cols) - b_row, -jnp.inf))
        x = jnp.where(strict, qk[c:] * spread(beta_cols) * decay, 0.0) * -1.0
        return (qk[:c] * decay, x, jnp.concatenate(b_cols, axis=0), jnp.concatenate(beta_cols, axis=0))

    attns, xs, b_colv, betav = zip(*[chunk_setup(b, t) for b, t in items])
    ns = xs
    x_bds = [block_diag(x.astype(bf16)) for x in xs]
    for _ in range(int(math.log2(c)) - 1):
        xs = [_dot(x.astype(bf16), x_bd) for x, x_bd in zip(xs, x_bds)]
        x_bds = [block_diag(x.astype(bf16)) for x in xs]
        ns = [n + x + _dot(n.astype(bf16), x_bd) for n, x, x_bd in zip(ns, xs, x_bds)]

    def chunk_wy(i):
        b, rows = items[i][0], tok(items[i][1])
        b_col, beta = b_colv[i], betav[i]
        kst = jnp.concatenate([k_ref[b, rows, h * DK:(h + 1) * DK] for h in range(N_HEADS)], axis=0)
        qst = jnp.concatenate([q_ref[b, rows, h * DK:(h + 1) * DK] for h in range(N_HEADS)], axis=0)
        vst = jnp.concatenate([v_ref[b, rows, h * DV:(h + 1) * DV] for h in range(N_HEADS)], axis=0).astype(f32)
        n_bd = block_diag(ns[i].astype(bf16))
        kf = kst.astype(f32)
        e_b = jnp.exp(b_col)
        ru, rw = vst * beta, kf * (beta * e_b)
        u = ru + _dot(n_bd, ru.astype(bf16))
        w = (rw + _dot(n_bd, rw.astype(bf16))).astype(bf16)
        q_dec = (qst.astype(f32) * e_b).astype(bf16)
        return u, w, q_dec, kf, block_diag(attns[i].astype(bf16))

    wy = [chunk_wy(i) for i in chunks]
    for i in chunks:
        b, rows = items[i][0], tok(items[i][1])
        u, w, q_dec, kf, attn_bd = wy[i]
        b_col = b_colv[i]
        v_new, o_inter = [], []
        for h in range(N_HEADS):
            hs = slice(h * c, (h + 1) * c)
            r = _dot(jnp.concatenate([w[hs], q_dec[hs]], axis=0), s_ref[b, h].astype(bf16))
            v_new.append(u[hs] - r[:c])
            o_inter.append(r[c:])
        vb = jnp.concatenate(v_new, axis=0).astype(bf16)
        o = jnp.concatenate(o_inter, axis=0) + _dot(attn_bd, vb)
        for h in range(N_HEADS):
            hs = slice(h * c, (h + 1) * c)
            b_last = b_col[(h + 1) * c - 1:(h + 1) * c]
            k_dec = (kf[hs] * jnp.exp(b_last - b_col[hs])).astype(bf16)
            s_ref[b, h] = (s_ref[b, h] * jnp.exp(b_last)
                           + lax.dot_general(k_dec, vb[hs], TN, preferred_element_type=f32))
            o_ref[b, rows, h * DV:(h + 1) * DV] = (
                _rms(o[hs], nw) * sz_ref[b, rows, h * DV:(h + 1) * DV].astype(f32)).astype(bf16)


def _scans_body(packed_ref, gcol_ref, grow_ref, nw_ref, o_ref, sret_ref, sgdn_ref, ret_state, gdn_state, *,
                ret_chunk):
    j = pl.program_id(1)
    rq_ref, rk_ref, rv_ref, rg_ref, gq_ref, gk_ref, gv_ref, gz_ref = _packed_views(packed_ref)
    oret_ref, ogdn_ref = o_ref.at[:, :, :V_W], o_ref.at[:, :, V_W:]

    @pl.when(j == 0)
    def _():
        ret_state[...] = jnp.zeros_like(ret_state)
        gdn_state[...] = jnp.zeros_like(gdn_state)

    for step in _ret_steps(rq_ref, rk_ref, rv_ref, rg_ref, oret_ref, ret_state, ret_chunk):
        step()
    _gdn_steps(gq_ref, gk_ref, gv_ref, gz_ref, gcol_ref, grow_ref, nw_ref, ogdn_ref, gdn_state)

    @pl.when(j == pl.num_programs(1) - 1)
    def _():
        sret_ref[...] = ret_state[...]
        sgdn_ref[...] = gdn_state[...]


def _scans(layer, packed, gcol, grow4, nw, tile, nb, ret_chunk):
    b, l, _ = packed.shape
    assert b % nb == 0 and tile % ret_chunk == 0
    tok = lambda w: pl.BlockSpec((nb, tile, w), lambda i, j: (i, j, 0))
    st = pl.BlockSpec((nb, N_HEADS, DK, DV), lambda i, j: (i, 0, 0, 0))
    s_sds = jax.ShapeDtypeStruct((b, N_HEADS, DK, DV), f32)
    state = pltpu.VMEM((nb, N_HEADS, DK, DV), f32)
    return pl.pallas_call(
        functools.partial(_scans_body, ret_chunk=ret_chunk), grid=(b // nb, l // tile),
        in_specs=[tok(_MAIN_W), tok(LANES),
                  pl.BlockSpec((nb, tile // CHUNK, SUBLANES, CHUNK), lambda i, j: (i, j, 0, 0)),
                  _resident((1, DV), layer)],
        out_specs=[tok(2 * V_W), st, st],
        out_shape=[jax.ShapeDtypeStruct((b, l, 2 * V_W), bf16), s_sds, s_sds],
        scratch_shapes=[state, state],
        compiler_params=_params("parallel", "arbitrary"), name="scans",
    )(packed, gcol, grow4, nw)


def _softmax(s):
    e = jnp.exp(s - jnp.max(s, axis=-1, keepdims=True))
    return e / jnp.sum(e, axis=-1, keepdims=True)


def _merge_tail(x, xn, o_ret, o_gdn, mem_proj, wg_ref, wr_ref, wd_ref, wo_ref, gpost):
    gates = jax.nn.sigmoid(_dot(xn, wg_ref[...]))
    merged = (gates[:, :D_MODEL] * _dot(o_ret, wr_ref[...])
              + gates[:, D_MODEL:2 * D_MODEL] * _dot(o_gdn, wd_ref[...])
              + gates[:, 2 * D_MODEL:] * mem_proj)
    y = _dot(merged.astype(bf16), wo_ref[...])
    return x + _rms(y, gpost)


def _merge_body(h_ref, gpre_ref, wq_ref, wg_ref, mk_ref, mv_ref, branches_ref,
                wr_ref, wd_ref, wm_ref, wo_ref, gpost_ref, o_ref):
    oret_ref, ogdn_ref = branches_ref.at[:, :V_W], branches_ref.at[:, V_W:]
    x = h_ref[...]
    xn = _rms(x, gpre_ref[...]).astype(bf16)
    mq = _dot(xn, wq_ref[...]).astype(bf16)
    mem_proj = None
    for h in range(N_HEADS):
        sl = slice(h * HD_X, (h + 1) * HD_X)
        s = lax.dot_general(mq[:, sl], mk_ref[:, sl], NT, preferred_element_type=f32) * HD_X ** -0.5
        o = _dot(_softmax(s).astype(bf16), mv_ref[:, sl])
        part = _dot(o.astype(bf16), wm_ref[sl, :])
        mem_proj = part if mem_proj is None else mem_proj + part
    o_ref[...] = _merge_tail(x, xn, oret_ref[...], ogdn_ref[...], mem_proj, wg_ref, wr_ref, wd_ref, wo_ref,
                             gpost_ref[...])


def _merge(layer, h3d, gpre, wq, wg, mk, mv, branches, wr, wd, wm, wo, gpost, tm):
    b, l, _ = h3d.shape
    tok = lambda w: pl.BlockSpec((None, tm, w), lambda i, j: (i, j, 0))
    mem = pl.BlockSpec((None, None, N_MEM, D_MODEL), lambda i, j: (layer, i, 0, 0))
    sq = _resident((D_MODEL, D_MODEL), layer)
    return pl.pallas_call(
        _merge_body, grid=(b, l // tm),
        in_specs=[tok(D_MODEL), _resident((1, D_MODEL), layer), _packed_mq(layer), _packed_gates(layer), mem, mem,
                  tok(2 * V_W), sq, sq, sq, sq, _resident((1, D_MODEL), layer)],
        out_specs=tok(D_MODEL), out_shape=jax.ShapeDtypeStruct((b, l, D_MODEL), f32),
        compiler_params=_params("parallel", "parallel"), name="merge",
    )(h3d, gpre, wq, wg, mk, mv, branches, wr, wd, wm, wo, gpost)


def _sproj_body(h_ref, gpre_ref, wm_ref, wab_ref, wq_ref, cw_ref, alog_ref, dtb_ref, cos_ref, sin_ref, buf_ref,
                rq_ref, rk_ref, rv_ref, rg_ref, gq_ref, gk_ref, gv_ref, gz_ref, gcol_ref, mq_ref, nbuf_ref):
    xn = _rms(h_ref[...], gpre_ref[...]).astype(bf16)
    p = _dot(xn, wm_ref[...])
    cos, sin = cos_ref[...], sin_ref[...]
    for h in range(N_HEADS):
        sl = slice(h * DK, (h + 1) * DK)
        rq_ref[:, sl] = _rotary(p[:, _RQ + h * DK:_RQ + (h + 1) * DK], cos, sin)
        rk_ref[:, sl] = _rotary(p[:, _RK + h * DK:_RK + (h + 1) * DK], cos, sin) * DK ** -0.5
    rv_ref[...] = p[:, _RV:_RG]
    rg_ref[...] = _silu(p[:, _RG:_GQKV])
    gz_ref[...] = _silu(p[:, _GZ:_MAIN_W])
    x = p[:, _GQKV:_GZ]
    cw = cw_ref[...]
    conv = x * cw[CONV_W - 1:CONV_W]
    for s in range(CONV_W - 1):
        prev = buf_ref[:, s * CONV_DIM:(s + 1) * CONV_DIM]
        conv = conv + prev * cw[s:s + 1]
        if s > 0:
            nbuf_ref[:, (s - 1) * CONV_DIM:s * CONV_DIM] = prev
    nbuf_ref[:, (CONV_W - 2) * CONV_DIM:] = x
    qs, ks, v = _gdn_qkv(conv)
    for h in range(N_HEADS):
        gq_ref[:, h * DK:(h + 1) * DK] = qs[h]
        gk_ref[:, h * DK:(h + 1) * DK] = ks[h]
    gv_ref[...] = v
    ab = _dot(xn, wab_ref[...])
    gcol_ref[...] = _decay_beta(ab, alog_ref[...], dtb_ref[...], _iota(ab.shape, 1) < N_HEADS)
    mq_ref[...] = _dot(xn, wq_ref[...])


def _sample_proj(layer, h2d, gpre, wm, wab, wq, cw, alog, dtb, cos, sin, buf2d):
    n = h2d.shape[0]
    sds = lambda w: jax.ShapeDtypeStruct((n, w), f32)
    res = lambda *shape: _resident(shape, layer)
    widths = (QK_W, QK_W, V_W, V_W, QK_W, QK_W, V_W, V_W, LANES, D_MODEL, (CONV_W - 1) * CONV_DIM)
    return pl.pallas_call(
        _sproj_body, grid=(1,),
        in_specs=[_resident((n, D_MODEL)), res(1, D_MODEL), res(D_MODEL, _MAIN_W), res(D_MODEL, LANES),
                  _packed_mq(layer), res(CONV_W, CONV_DIM), res(1, LANES), res(1, LANES),
                  _resident((1, DK)), _resident((1, DK)), _resident((n, (CONV_W - 1) * CONV_DIM))],
        out_specs=[pl.BlockSpec((n, w), lambda i: (0, 0)) for w in widths],
        out_shape=[sds(w) for w in widths],
        compiler_params=_params("arbitrary"), name="sample_proj",
    )(h2d, gpre, wm, wab, wq, cw, alog, dtb, cos, sin, buf2d)


def _sstate_body(cols_ref, rv_ref, gv_ref, gcol_ref, sret_ref, sgdn_ref, nret_ref, ngdn_ref, oret_ref, ogdn_ref):
    vecs = cols_ref[...].T
    col = lambda j, v, h: vecs[:, (j * 4 + v) * N_HEADS + h:(j * 4 + v) * N_HEADS + h + 1]
    for j in range(rv_ref.shape[0]):
        for h in range(N_HEADS):
            vs = slice(h * DV, (h + 1) * DV)
            q, k = col(j, 0, h), col(j, 1, h)
            s_new = sret_ref[j, h] * math.exp(LOG_GAMMA[h]) + k * rv_ref[j:j + 1, vs]
            nret_ref[j, h] = s_new
            oret_ref[j:j + 1, vs] = jnp.sum(q * s_new, axis=0, keepdims=True)
            q, k = col(j, 2, h), col(j, 3, h)
            eg = jnp.exp(gcol_ref[j:j + 1, h:h + 1])
            beta = gcol_ref[j:j + 1, N_HEADS + h:N_HEADS + h + 1]
            s = sgdn_ref[j, h]
            ks = jnp.sum(k * s, axis=0, keepdims=True)
            v_new = beta * (gv_ref[j:j + 1, vs] - eg * ks)
            s_new = s * eg + k * v_new
            ngdn_ref[j, h] = s_new
            ogdn_ref[j:j + 1, vs] = jnp.sum(q * s_new, axis=0, keepdims=True)


def _sattn_body(mq_ref, k_ref, v_ref, o_ref):
    for j in range(mq_ref.shape[0]):
        s = jnp.sum(k_ref[j] * mq_ref[j][None], axis=-1, keepdims=True) * HD_X ** -0.5
        e = jnp.exp(s - jnp.max(s, axis=0, keepdims=True))
        p = e / jnp.sum(e, axis=0, keepdims=True)
        o_ref[j] = jnp.sum(p * v_ref[j], axis=0)


def _smerge_body(h_ref, gpre_ref, wg_ref, oret_ref, rg_ref, ogdn_ref, gz_ref, nw_ref, omem_ref,
                 wr_ref, wd_ref, wm_ref, wo_ref, gpost_ref, o_ref):
    x = h_ref[...]
    xn = _rms(x, gpre_ref[...]).astype(bf16)
    nw = nw_ref[...]
    rets, gdns = [], []
    for h in range(N_HEADS):
        vs = slice(h * DV, (h + 1) * DV)
        rets.append(rg_ref[:, vs] * _rms(oret_ref[:, vs]))
        gdns.append(_rms(ogdn_ref[:, vs], nw) * gz_ref[:, vs])
    o_ret = jnp.concatenate(rets, axis=-1).astype(bf16)
    o_gdn = jnp.concatenate(gdns, axis=-1).astype(bf16)
    mem_proj = _dot(omem_ref[...].astype(bf16), wm_ref[...])
    o_ref[...] = _merge_tail(x, xn, o_ret, o_gdn, mem_proj, wg_ref, wr_ref, wd_ref, wo_ref, gpost_ref[...])


def _sample_merge(layer, h2d, gpre, wg, o_ret, rg, o_gdn, gz, nw, o_mem, wr, wd, wm, wo, gpost):
    n = h2d.shape[0]
    act = _resident((n, D_MODEL))
    sq = _resident((D_MODEL, D_MODEL), layer)
    return pl.pallas_call(
        _smerge_body, grid=(1,),
        in_specs=[act, _resident((1, D_MODEL), layer), _packed_gates(layer), act, act, act, act,
                  _resident((1, DV), layer), act, sq, sq, sq, sq, _resident((1, D_MODEL), layer)],
        out_specs=pl.BlockSpec((n, D_MODEL), lambda i: (0, 0)), out_shape=jax.ShapeDtypeStruct(h2d.shape, f32),
        compiler_params=_params("arbitrary"), name="sample_merge",
    )(h2d, gpre, wg, o_ret, rg, o_gdn, gz, nw, o_mem, wr, wd, wm, wo, gpost)


def _rope_tables(pos):
    half = DK // 2
    inv_freq = ROPE_BASE ** (-jnp.arange(half, dtype=f32) / half)
    ang = pos.astype(f32)[:, None] * inv_freq[None, :]
    cos, sin = jnp.cos(ang), jnp.sin(ang)
    return jnp.concatenate([cos, cos], axis=-1), jnp.concatenate([-sin, sin], axis=-1)


def _row_tile(n, want):
    t = min(n, want)
    assert n % t == 0, (n, t)
    return t


def kernel(x_prompt, x_sample, mem_prompt, state_ret, state_gdn, state_conv, cache_mem_k, cache_mem_v, norm_ffn1_pre, norm_ffn1_post, ffn1_w_in, ffn1_w_out, norm_mix_pre, norm_mix_post, w_in, gdn_conv_w, gdn_a_log, gdn_dt_bias, gdn_norm, norm_mem, w_mem_k, w_mem_v, w_branch_ret, w_branch_gdn, w_branch_mem, w_out, norm_ffn2_pre, norm_ffn2_post, ffn2_w_in, ffn2_w_out):
    bp, sp, _ = x_prompt.shape
    ns, ss, _ = x_sample.shape
    depth = w_in.shape[0]
    assert ss == 1 and sp % CHUNK == 0
    tm_ffn = _row_tile(bp * sp, 512)
    tm_seq = _row_tile(sp, 512)
    tm_proj = _row_tile(sp, 256)

    cos_p, sin_p = _rope_tables(jnp.arange(sp))
    cos_s, sin_s = _rope_tables(PAST_LEN + jnp.arange(ss))
    row = lambda v: v.reshape(depth, 1, -1).astype(f32)
    pad_lanes = lambda v: jnp.pad(v.astype(f32), ((0, 0), (0, LANES - v.shape[1]))).reshape(depth, 1, LANES)
    pad_rows = lambda v: jnp.pad(v.astype(f32), ((0, 0), (0, SUBLANES - v.shape[1]))).reshape(depth, SUBLANES, 1)
    w_in_t = jnp.swapaxes(w_in, 1, 2)
    w_main = _w_in_cols(w_in_t, 0, _MAIN_W)
    w_mq = _w_in_cols(w_in_t, _AB_END, D_MODEL)
    w_gates = _w_in_cols(w_in_t, _MQ_END, 3 * D_MODEL)
    w_ab_row = w_in_t[:, _MAIN_W:_AB_END].astype(bf16)
    w_ab_col = jnp.pad(jnp.swapaxes(w_ab_row, 1, 2), ((0, 0), (0, 0), (0, LANES - 2 * N_HEADS)))
    f1_in, f1_out, f2_in, f2_out = (w.astype(bf16) for w in (ffn1_w_in, ffn1_w_out, ffn2_w_in, ffn2_w_out))
    wr, wd, wm, wo = (w.astype(bf16) for w in (w_branch_ret, w_branch_gdn, w_branch_mem, w_out))
    cw = gdn_conv_w.astype(f32)
    alog, dtb = pad_lanes(gdn_a_log), pad_lanes(gdn_dt_bias)
    alog_t, dtb_t = pad_rows(gdn_a_log), pad_rows(gdn_dt_bias)
    nw = row(gdn_norm)
    g_f1pre, g_f1post, g_f2pre, g_f2post = (row(g) for g in (norm_ffn1_pre, norm_ffn1_post, norm_ffn2_pre,
                                                              norm_ffn2_post))
    g_mpre, g_mpost = row(norm_mix_pre), row(norm_mix_post)

    mk, mv, mk_b, mv_b = _memkv(mem_prompt, row(norm_mem), w_mem_k.astype(bf16), w_mem_v.astype(bf16))
    hp = x_prompt
    hs = x_sample.reshape(ns, D_MODEL)
    outs = [[] for _ in range(4)]
    new_states = None
    scan_seqs = 2 if bp % 2 == 0 else 1
    scan_tile = _row_tile(sp, 256)
    ret_chunk = scan_tile
    hosted_per_step = ns // (2 * (bp * sp // tm_ffn))
    assert hosted_per_step * 2 * (bp * sp // tm_ffn) == ns
    for l in range(depth):
        h1s = _ffn(l, hs, g_f1pre, f1_in, f1_out, g_f1post, ns)
        s_rq, s_rk, s_rv, s_rg, s_gq, s_gk, s_gv, s_gz, s_gcol, mq, nbuf = _sample_proj(
            l, h1s, g_mpre, w_main, w_ab_col, w_mq, cw, alog, dtb, cos_s, sin_s,
            state_conv[l].reshape(ns, (CONV_W - 1) * CONV_DIM))
        outs[3].append(nbuf.reshape(ns, CONV_W - 1, CONV_DIM))
        groups = ns // hosted_per_step
        by_step = lambda a: a.reshape((groups, hosted_per_step) + a.shape[1:])
        cols = jnp.stack([s_rq, s_rk, s_gq, s_gk], axis=1).reshape(groups, hosted_per_step * 4 * N_HEADS, DK)
        host = lambda part: (part, 2, hosted_per_step, by_step(mq.reshape(ns, N_HEADS, HD_X)), cache_mem_k,
                             cache_mem_v, cols, by_step(s_rv), by_step(s_gv), by_step(s_gcol), state_ret,
                             state_gdn, new_states)

        h1, o_mem_a, n_ret, n_gdn, so_ret_a, so_gdn_a = _ffn(
            l, hp.reshape(bp * sp, D_MODEL), g_f1pre, f1_in, f1_out, g_f1post, tm_ffn, hosted=host(0))
        new_states = (n_ret, n_gdn)
        h1 = h1.reshape(bp, sp, D_MODEL)
        packed, gcol, grow, tail = _proj(
            l, h1, g_mpre, w_main, w_ab_col, w_ab_row, cw, alog, dtb, alog_t, dtb_t, cos_p, sin_p, tm_proj)
        grow4 = grow.reshape(bp, SUBLANES, sp // CHUNK, CHUNK).transpose(0, 2, 1, 3)
        branches, s_ret, s_gdn = _scans(l, packed, gcol, grow4, nw, scan_tile, scan_seqs, ret_chunk)
        h2 = _merge(l, h1, g_mpre, w_mq, w_gates, mk_b, mv_b, branches, wr, wd, wm, wo, g_mpost, tm_seq)
        hp, o_mem_b, n_ret, n_gdn, so_ret_b, so_gdn_b = _ffn(
            l, h2.reshape(bp * sp, D_MODEL), g_f2pre, f2_in, f2_out, g_f2post, tm_ffn, hosted=host(1))
        new_states = (n_ret, n_gdn)
        hp = hp.reshape(bp, sp, D_MODEL)
        outs[0].append(s_ret)
        outs[1].append(s_gdn)
        outs[2].append(tail[:, SUBLANES - (CONV_W - 1):])

        join = lambda a, b: jnp.concatenate([a, b], axis=0).reshape(ns, -1)
        o_mem, so_ret, so_gdn = join(o_mem_a, o_mem_b), join(so_ret_a, so_ret_b), join(so_gdn_a, so_gdn_b)
        h2s = _sample_merge(l, h1s, g_mpre, w_gates, so_ret, s_rg, so_gdn, s_gz, nw, o_mem, wr, wd, wm, wo, g_mpost)
        hs = _ffn(l, h2s, g_f2pre, f2_in, f2_out, g_f2post, ns)

    ret_p, gdn_p, conv_p, conv_s = (jnp.stack(o) for o in outs)
    return (hp, hs.reshape(ns, ss, D_MODEL), ret_p, gdn_p, conv_p, mk, mv) + new_states + (conv_s,)
```

```python
import functools
import math

import jax
import jax.numpy as jnp
from jax import lax
from jax.experimental import pallas as pl
from jax.experimental.pallas import tpu as pltpu

f32, bf16 = jnp.float32, jnp.bfloat16

D_MODEL = 1024
N_HEADS = 4
DK = 128
DV = 256
QK_W = N_HEADS * DK
V_W = N_HEADS * DV
CONV_W = 4
CONV_DIM = 2 * QK_W + V_W
HD_X = 256
N_MEM = 256
D_FF = 2816
EPS = 1e-6
ROPE_BASE = 10000.0
PAST_LEN = 16384
CHUNK = 64
LANES = 128
SUBLANES = 8
VMEM_LIMIT = 56 * 1024 * 1024
LOG_GAMMA = tuple(math.log1p(-2.0 ** (-5.0 - h)) for h in range(N_HEADS))
HIGHEST = lax.Precision.HIGHEST
NT = (((1,), (1,)), ((), ()))
TN = (((0,), (0,)), ((), ()))


def _params(*sem):
    return pltpu.CompilerParams(dimension_semantics=sem, vmem_limit_bytes=VMEM_LIMIT)


def _resident(shape, layer=None, col_block=0):
    if layer is None:
        return pl.BlockSpec(shape, lambda *_: (0,) * len(shape), pipeline_mode=pl.Buffered(1))
    index = (layer,) + (0,) * (len(shape) - 1) + (col_block,)
    return pl.BlockSpec((None,) + shape, lambda *_: index, pipeline_mode=pl.Buffered(1))


def _rms(x, gain=None):
    y = x * lax.rsqrt(jnp.mean(x * x, axis=-1, keepdims=True) + EPS)
    return y if gain is None else y * gain


def _silu(x):
    return x * jax.nn.sigmoid(x)


def _softplus(x):
    return jnp.maximum(x, 0.0) + jnp.log1p(jnp.exp(-jnp.abs(x)))


def _dot(a, b):
    return jnp.dot(a, b, preferred_element_type=f32)


def _iota(shape, dim):
    return lax.broadcasted_iota(jnp.int32, shape, dim)


def _ffn_body(x_ref, gpre_ref, wgu_ref, wo_ref, gpost_ref, o_ref, row_parts=1):
    part = x_ref.shape[0] // row_parts
    for i in range(row_parts):
        rows = slice(i * part, (i + 1) * part)
        x = x_ref[rows, :]
        xn = _rms(x, gpre_ref[...]).astype(bf16)
        h = _dot(xn, wgu_ref[...])
        act = (_silu(h[:, :D_FF]) * h[:, D_FF:]).astype(bf16)
        y = _dot(act, wo_ref[...])
        o_ref[rows, :] = x + 0.5 * _rms(y, gpost_ref[...])


def _ffn_hosting_body(x_ref, gpre_ref, wgu_ref, wo_ref, gpost_ref, mq_ref, k_ref, v_ref,
                      cols_ref, rv_ref, gv_ref, gcol_ref, sret_ref, sgdn_ref, *refs):
    o_ref, omem_ref, nret_ref, ngdn_ref, oret_ref, ogdn_ref = refs[-6:]
    _sstate_body(cols_ref, rv_ref, gv_ref, gcol_ref, sret_ref, sgdn_ref, nret_ref, ngdn_ref, oret_ref, ogdn_ref)
    _sattn_body(mq_ref, k_ref, v_ref, omem_ref)
    _ffn_body(x_ref, gpre_ref, wgu_ref, wo_ref, gpost_ref, o_ref, row_parts=2)


def _ffn(layer, x2d, gpre, wgu, wo, gpost, tm, hosted=None):
    t = x2d.shape[0]
    steps = t // tm
    row = pl.BlockSpec((tm, D_MODEL), lambda i: (i, 0))
    in_specs = [row, _resident((1, D_MODEL), layer), _resident((D_MODEL, 2 * D_FF), layer),
                _resident((D_FF, D_MODEL), layer), _resident((1, D_MODEL), layer)]
    out_row = jax.ShapeDtypeStruct((t, D_MODEL), f32)
    if hosted is None:
        return pl.pallas_call(
            _ffn_body, grid=(steps,), in_specs=in_specs, out_specs=row, out_shape=out_row,
            compiler_params=_params("parallel"), name="ffn",
        )(x2d, gpre, wgu, wo, gpost)
    part, parts, ps, mq, mem_k, mem_v, cols, rv, gv, gcol, s_ret, s_gdn, prev = hosted
    depth, n = s_ret.shape[:2]
    assert ps * parts * steps == n, (n, parts, steps, ps)
    first = part * steps
    grp = lambda *shape: pl.BlockSpec((None,) + shape, lambda i: (first + i,) + (0,) * len(shape))
    out_grp = lambda *shape: pl.BlockSpec((None,) + shape, lambda i: (i,) + (0,) * len(shape))
    mem = pl.BlockSpec((None, ps, N_MEM, N_HEADS, HD_X), lambda i: (layer, first + i, 0, 0, 0))
    st = pl.BlockSpec((None, ps, N_HEADS, DK, DV), lambda i: (layer, first + i, 0, 0, 0))
    st_sds = jax.ShapeDtypeStruct((depth, n, N_HEADS, DK, DV), f32)
    grp_sds = lambda *shape: jax.ShapeDtypeStruct((steps,) + shape, f32)
    prev = [] if prev is None else list(prev)
    prev_specs = [pl.BlockSpec(memory_space=pl.ANY)] * len(prev)
    aliases = {14 + i: 2 + i for i in range(len(prev))}
    return pl.pallas_call(
        _ffn_hosting_body, grid=(steps,),
        in_specs=in_specs + [grp(ps, N_HEADS, HD_X), mem, mem,
                             grp(ps * 4 * N_HEADS, DK),
                             grp(ps, V_W), grp(ps, V_W), grp(ps, LANES), st, st] + prev_specs,
        out_specs=[row, out_grp(ps, N_HEADS, HD_X), st, st, out_grp(ps, V_W), out_grp(ps, V_W)],
        out_shape=[out_row, grp_sds(ps, N_HEADS, HD_X), st_sds, st_sds, grp_sds(ps, V_W), grp_sds(ps, V_W)],
        input_output_aliases=aliases, compiler_params=_params("parallel"), name="ffn_hosting",
    )(x2d, gpre, wgu, wo, gpost, mq, mem_k, mem_v, cols, rv, gv, gcol, s_ret, s_gdn, *prev)


def _memkv_body(m_ref, g_ref, wk_ref, wv_ref, k_ref, v_ref, kb_ref, vb_ref):
    nb = m_ref.shape[0]
    mn = _rms(m_ref[...].reshape(nb * N_MEM, D_MODEL), g_ref[...]).astype(bf16)
    k = _dot(mn, wk_ref[...])
    v = _dot(mn, wv_ref[...])
    for b in range(nb):
        rows = slice(b * N_MEM, (b + 1) * N_MEM)
        for h in range(N_HEADS):
            k_ref[b, :, h, :] = k[rows, h * HD_X:(h + 1) * HD_X]
            v_ref[b, :, h, :] = v[rows, h * HD_X:(h + 1) * HD_X]
        kb_ref[b] = k[rows].astype(bf16)
        vb_ref[b] = v[rows].astype(bf16)


def _memkv(mem, g, wk, wv):
    b = mem.shape[0]
    depth = wk.shape[0]
    nb = 4 if b % 4 == 0 else 1
    per_layer = lambda shape: pl.BlockSpec((None,) + shape, lambda l, i: (l,) + (0,) * len(shape))
    out5 = pl.BlockSpec((None, nb, N_MEM, N_HEADS, HD_X), lambda l, i: (l, i, 0, 0, 0))
    outb = pl.BlockSpec((None, nb, N_MEM, D_MODEL), lambda l, i: (l, i, 0, 0))
    sds5 = jax.ShapeDtypeStruct((depth, b, N_MEM, N_HEADS, HD_X), f32)
    sdsb = jax.ShapeDtypeStruct((depth, b, N_MEM, D_MODEL), bf16)
    return pl.pallas_call(
        _memkv_body, grid=(depth, b // nb),
        in_specs=[pl.BlockSpec((nb, N_MEM, D_MODEL), lambda l, i: (i, 0, 0)), per_layer((1, D_MODEL)),
                  per_layer((D_MODEL, D_MODEL)), per_layer((D_MODEL, D_MODEL))],
        out_specs=[out5, out5, outb, outb], out_shape=[sds5, sds5, sdsb, sdsb],
        compiler_params=_params("parallel", "parallel"), name="memkv",
    )(mem, g, wk, wv)


_RQ, _RK, _RV, _RG, _GQKV, _GZ, _MAIN_W = 0, 512, 1024, 2048, 3072, 5120, 6144
_AB_END = _MAIN_W + 2 * N_HEADS
_MQ_END = _AB_END + D_MODEL


def _packed_mq(layer):
    return _resident((D_MODEL, D_MODEL), layer)


def _packed_gates(layer):
    return _resident((D_MODEL, 3 * D_MODEL), layer)


def _w_in_cols_body(wt_ref, o_ref):
    o_ref[...] = wt_ref[0].T.astype(bf16)


def _w_in_cols(w_in_t, first, width, blk=1024):
    depth = w_in_t.shape[0]
    assert first % SUBLANES == 0 and width % blk == 0
    return pl.pallas_call(
        _w_in_cols_body, grid=(depth, width // blk),
        in_specs=[pl.BlockSpec((pl.Element(1), pl.Element(blk), pl.Element(D_MODEL)),
                               lambda l, c: (l, pl.multiple_of(first + c * blk, SUBLANES), 0))],
        out_specs=pl.BlockSpec((None, D_MODEL, blk), lambda l, c: (l, 0, c)),
        out_shape=jax.ShapeDtypeStruct((depth, D_MODEL, width), bf16),
        compiler_params=_params("parallel", "parallel"), name="w_in_cols",
    )(w_in_t)


def _rotary(x, cos, sin_signed):
    return x * cos + pltpu.roll(x, DK // 2, axis=1) * sin_signed


def _decay_beta(ab, a_log, dt_bias, is_decay):
    g = -jnp.exp(a_log) * _softplus(ab + dt_bias)
    return jnp.where(is_decay, g, jax.nn.sigmoid(ab))


def _gdn_qkv(conv):
    a = _silu(conv)
    qs, ks = [], []
    for h in range(N_HEADS):
        q = a[:, h * DK:(h + 1) * DK]
        k = a[:, QK_W + h * DK:QK_W + (h + 1) * DK]
        qs.append(q * (lax.rsqrt(jnp.sum(q * q, axis=-1, keepdims=True) + EPS) * DK ** -0.5))
        ks.append(k * lax.rsqrt(jnp.sum(k * k, axis=-1, keepdims=True) + EPS))
    return qs, ks, a[:, 2 * QK_W:]


def _proj_body(h_ref, gpre_ref, wm_ref, wab_ref, wabt_ref, cw_ref, alog_ref, dtb_ref, alogt_ref, dtbt_ref,
               cos_ref, sin_ref,
               rq_ref, rk_ref, rv_ref, rg_ref, gq_ref, gk_ref, gv_ref, gz_ref, gcol_ref, grow_ref, tail_ref,
               carry_ref):
    j = pl.program_id(1)
    tm = h_ref.shape[0]

    @pl.when(j == 0)
    def _():
        carry_ref[...] = jnp.zeros_like(carry_ref)

    xn = _rms(h_ref[...], gpre_ref[...]).astype(bf16)
    cos, sin = cos_ref[...], sin_ref[...]

    def ret_qk(p):
        for h in range(N_HEADS):
            sl = slice(h * DK, (h + 1) * DK)
            rq_ref[:, sl] = _rotary(p[:, h * DK:(h + 1) * DK], cos, sin).astype(bf16)
            rk_ref[:, sl] = (_rotary(p[:, QK_W + h * DK:QK_W + (h + 1) * DK], cos, sin) * DK ** -0.5).astype(bf16)

    def ret_v(p):
        rv_ref[...] = p.astype(bf16)

    def ret_gate(p):
        rg_ref[...] = _silu(p).astype(bf16)

    def gdn_z(p):
        gz_ref[...] = _silu(p).astype(bf16)

    def conv_silu(x, cols):
        carry = carry_ref[:, cols]
        cw = cw_ref[:, cols]
        conv = x * cw[CONV_W - 1:CONV_W]
        conv_top = x[:SUBLANES] * cw[CONV_W - 1:CONV_W]
        first = _iota(conv_top.shape, 0)
        for s in range(1, CONV_W):
            w = cw[CONV_W - 1 - s:CONV_W - s]
            xs = pltpu.roll(x, s, axis=0)
            conv = conv + xs * w
            conv_top = conv_top + jnp.where(first < s, pltpu.roll(carry, s, axis=0), xs[:SUBLANES]) * w
        carry_ref[:, cols] = x[tm - SUBLANES:]
        tail_ref[:, cols] = x[tm - SUBLANES:]
        return (slice(None), _silu(conv)), (slice(0, SUBLANES), _silu(conv_top))

    def gdn_qk(p):
        for rows, a in conv_silu(p, slice(0, 2 * QK_W)):
            for h in range(N_HEADS):
                q = a[:, h * DK:(h + 1) * DK]
                k = a[:, QK_W + h * DK:QK_W + (h + 1) * DK]
                q = q * (lax.rsqrt(jnp.sum(q * q, axis=-1, keepdims=True) + EPS) * DK ** -0.5)
                k = k * lax.rsqrt(jnp.sum(k * k, axis=-1, keepdims=True) + EPS)
                gq_ref[rows, h * DK:(h + 1) * DK] = q.astype(bf16)
                gk_ref[rows, h * DK:(h + 1) * DK] = k.astype(bf16)

    def gdn_v(p):
        for rows, a in conv_silu(p, slice(2 * QK_W, CONV_DIM)):
            gv_ref[rows, :] = a.astype(bf16)

    groups = ((_RQ, _RV, ret_qk), (_RV, _RG, ret_v), (_RG, _GQKV, ret_gate), (_GQKV, _GQKV + 2 * QK_W, gdn_qk),
              (_GQKV + 2 * QK_W, _GZ, gdn_v), (_GZ, _MAIN_W, gdn_z))
    pending = None
    for lo, hi, epilogue in groups:
        p = _dot(xn, wm_ref[:, lo:hi])
        if pending is not None:
            pending()
        pending = functools.partial(epilogue, p)
    pending()

    ab = _dot(xn, wab_ref[...])
    gcol_ref[...] = _decay_beta(ab, alog_ref[...], dtb_ref[...], _iota(ab.shape, 1) < N_HEADS)
    abt = lax.dot_general(wabt_ref[...], xn, NT, preferred_element_type=f32)
    grow_ref[...] = _decay_beta(abt, alogt_ref[...], dtbt_ref[...], _iota(abt.shape, 0) < N_HEADS)


def _proj(layer, h3d, gpre, wm, wab, wabt, cw, alog, dtb, alogt, dtbt, cos, sin, tm):
    b, l, _ = h3d.shape
    tok = lambda w: pl.BlockSpec((None, tm, w), lambda i, j: (i, j, 0))
    tab = pl.BlockSpec((tm, DK), lambda i, j: (j, 0))
    sds = lambda w, dt=bf16: jax.ShapeDtypeStruct((b, l, w), dt)
    res = lambda *shape: _resident(shape, layer)
    return pl.pallas_call(
        _proj_body, grid=(b, l // tm),
        in_specs=[tok(D_MODEL), res(1, D_MODEL), res(D_MODEL, _MAIN_W), res(D_MODEL, LANES),
                  res(SUBLANES, D_MODEL), res(CONV_W, CONV_DIM), res(1, LANES),
                  res(1, LANES), res(SUBLANES, 1), res(SUBLANES, 1), tab, tab],
        out_specs=[tok(QK_W), tok(QK_W), tok(V_W), tok(V_W), tok(QK_W), tok(QK_W), tok(V_W), tok(V_W),
                   tok(LANES), pl.BlockSpec((None, SUBLANES, tm), lambda i, j: (i, 0, j)),
                   pl.BlockSpec((None, SUBLANES, CONV_DIM), lambda i, j: (i, 0, 0))],
        out_shape=[sds(QK_W), sds(QK_W), sds(V_W), sds(V_W), sds(QK_W), sds(QK_W), sds(V_W), sds(V_W),
                   sds(LANES, f32), jax.ShapeDtypeStruct((b, SUBLANES, l), f32),
                   jax.ShapeDtypeStruct((b, SUBLANES, CONV_DIM), f32)],
        scratch_shapes=[pltpu.VMEM((SUBLANES, CONV_DIM), f32)],
        compiler_params=_params("parallel", "arbitrary"), name="proj",
    )(h3d, gpre, wm, wab, wabt, cw, alog, dtb, alogt, dtbt, cos, sin)


def _ret_steps(q_ref, k_ref, v_ref, sg_ref, o_ref, s_ref, chunk):
    c = chunk
    row, col = _iota((c, c), 0), _iota((c, c), 1)
    dist = (row - col).astype(f32)
    pos = _iota((c, 1), 0).astype(f32)
    per_head = {}

    def head_consts(h):
        if h not in per_head:
            lg = LOG_GAMMA[h]
            per_head[h] = (jnp.exp(jnp.where(row >= col, dist * lg, -jnp.inf)), jnp.exp((pos + 1.0) * lg),
                           jnp.exp((c - 1.0 - pos) * lg))
        return per_head[h]

    def step(h, t, b):
        decay, q_scale, k_scale = head_consts(h)
        rows = slice(t * c, (t + 1) * c)
        qh = q_ref[b, rows, h * DK:(h + 1) * DK]
        kh = k_ref[b, rows, h * DK:(h + 1) * DK]
        vh = v_ref[b, rows, h * DV:(h + 1) * DV]
        attn = lax.dot_general(qh, kh, NT, preferred_element_type=f32) * decay
        q_dec = (qh.astype(f32) * q_scale).astype(bf16)
        k_dec = (kh.astype(f32) * k_scale).astype(bf16)
        s = s_ref[b, h]
        o = _dot(q_dec, s.astype(bf16)) + _dot(attn.astype(bf16), vh)
        s_ref[b, h] = s * math.exp(c * LOG_GAMMA[h]) + lax.dot_general(k_dec, vh, TN, preferred_element_type=f32)
        o_ref[b, rows, h * DV:(h + 1) * DV] = (
            sg_ref[b, rows, h * DV:(h + 1) * DV].astype(f32) * _rms(o)).astype(bf16)

    return [functools.partial(step, h, t, b) for h in range(N_HEADS) for t in range(q_ref.shape[1] // c)
            for b in range(q_ref.shape[0])]


def _gdn_steps(q_ref, k_ref, v_ref, sz_ref, gcol_ref, grow_ref, nw_ref, o_ref, s_ref):
    c = CHUNK
    hc = N_HEADS * c
    row, lane = _iota((c, hc), 0), _iota((c, hc), 1)
    lane_head, col = lane // c, lane % c
    lower, strict = row >= col, row > col
    head_mask = [(lane_head == h).astype(bf16) for h in range(N_HEADS)]
    block_diag = lambda x: jnp.concatenate([x * m for m in head_mask], axis=0)
    spread = lambda cols: functools.reduce(
        lambda acc, h: jnp.where(lane_head == h, cols[h], acc), range(N_HEADS - 1), cols[N_HEADS - 1])
    cum_rows = (_iota((c, c), 0) >= _iota((c, c), 1)).astype(f32)
    cum_cols = (_iota((c, hc), 0) <= (_iota((c, hc), 1) % c)).astype(f32)
    own_row = _iota((SUBLANES, hc), 0) == _iota((SUBLANES, hc), 1) // c
    zeros_k = jnp.zeros((c, DK), bf16)
    nw = nw_ref[...]

    items = [(b, t) for t in range(q_ref.shape[1] // c) for b in range(q_ref.shape[0])]
    chunks = range(len(items))
    tok = lambda t: slice(t * c, (t + 1) * c)

    def chunk_setup(b, t):
        rows = tok(t)
        k_heads = [k_ref[b, rows, h * DK:(h + 1) * DK] for h in range(N_HEADS)]
        k_bd = jnp.concatenate([jnp.concatenate([k_heads[h] if g == h else zeros_k for g in range(N_HEADS)], axis=1)
                                for h in range(N_HEADS)], axis=0)
        qk = lax.dot_general(jnp.concatenate([q_ref[b, rows, :], k_ref[b, rows, :]], axis=0), k_bd, NT,
                             preferred_element_type=f32)
        gcol = gcol_ref[b, rows, :]
        cum = jnp.dot(cum_rows, gcol, preferred_element_type=f32, precision=HIGHEST)
        b_cols = [cum[:, h:h + 1] for h in range(N_HEADS)]
        beta_cols = [gcol[:, N_HEADS + h:N_HEADS + h + 1] for h in range(N_HEADS)]
        b_row = jnp.sum(jnp.where(own_row, jnp.dot(grow_ref[b, t], cum_cols, preferred_element_type=f32,
                                                   precision=HIGHEST), 0.0), axis=0, keepdims=True)
        decay = jnp.exp(jnp.where(lower, spread(b_cols) - b_row, -jnp.inf))
        x = jnp.where(strict, qk[c:] * spread(beta_cols) * decay, 0.0) * -1.0
        return (qk[:c] * decay, x, jnp.concatenate(b_cols, axis=0), jnp.concatenate(beta_cols, axis=0))

    attns, xs, b_colv, betav = zip(*[chunk_setup(b, t) for b, t in items])
    ns = xs
    x_bds = [block_diag(x.astype(bf16)) for x in xs]
    for _ in range(int(math.log2(c)) - 1):
        xs = [_dot(x.astype(bf16), x_bd) for x, x_bd in zip(xs, x_bds)]
        x_bds = [block_diag(x.astype(bf16)) for x in xs]
        ns = [n + x + _dot(n.astype(bf16), x_bd) for n, x, x_bd in zip(ns, xs, x_bds)]

    def chunk_wy(i):
        b, rows = items[i][0], tok(items[i][1])
        b_col, beta = b_colv[i], betav[i]
        kst = jnp.concatenate([k_ref[b, rows, h * DK:(h + 1) * DK] for h in range(N_HEADS)], axis=0)
        qst = jnp.concatenate([q_ref[b, rows, h * DK:(h + 1) * DK] for h in range(N_HEADS)], axis=0)
        vst = jnp.concatenate([v_ref[b, rows, h * DV:(h + 1) * DV] for h in range(N_HEADS)], axis=0).astype(f32)
        n_bd = block_diag(ns[i].astype(bf16))
        kf = kst.astype(f32)
        e_b = jnp.exp(b_col)
        ru, rw = vst * beta, kf * (beta * e_b)
        u = ru + _dot(n_bd, ru.astype(bf16))
        w = (rw + _dot(n_bd, rw.astype(bf16))).astype(bf16)
        q_dec = (qst.astype(f32) * e_b).astype(bf16)
        return u, w, q_dec, kf, block_diag(attns[i].astype(bf16))

    wy = [chunk_wy(i) for i in chunks]
    for i in chunks:
        b, rows = items[i][0], tok(items[i][1])
        u, w, q_dec, kf, attn_bd = wy[i]
        b_col = b_colv[i]
        v_new, o_inter = [], []
        for h in range(N_HEADS):
            hs = slice(h * c, (h + 1) * c)
            r = _dot(jnp.concatenate([w[hs], q_dec[hs]], axis=0), s_ref[b, h].astype(bf16))
            v_new.append(u[hs] - r[:c])
            o_inter.append(r[c:])
        vb = jnp.concatenate(v_new, axis=0).astype(bf16)
        o = jnp.concatenate(o_inter, axis=0) + _dot(attn_bd, vb)
        for h in range(N_HEADS):
            hs = slice(h * c, (h + 1) * c)
            b_last = b_col[(h + 1) * c - 1:(h + 1) * c]
            k_dec = (kf[hs] * jnp.exp(b_last - b_col[hs])).astype(bf16)
            s_ref[b, h] = (s_ref[b, h] * jnp.exp(b_last)
                           + lax.dot_general(k_dec, vb[hs], TN, preferred_element_type=f32))
            o_ref[b, rows, h * DV:(h + 1) * DV] = (
                _rms(o[hs], nw) * sz_ref[b, rows, h * DV:(h + 1) * DV].astype(f32)).astype(bf16)


def _scans_body(rq_ref, rk_ref, rv_ref, rg_ref, gq_ref, gk_ref, gv_ref, gz_ref, gcol_ref, grow_ref, nw_ref,
                oret_ref, sret_ref, ogdn_ref, sgdn_ref, ret_state, gdn_state, *, ret_chunk):
    j = pl.program_id(1)

    @pl.when(j == 0)
    def _():
        ret_state[...] = jnp.zeros_like(ret_state)
        gdn_state[...] = jnp.zeros_like(gdn_state)

    for step in _ret_steps(rq_ref, rk_ref, rv_ref, rg_ref, oret_ref, ret_state, ret_chunk):
        step()
    _gdn_steps(gq_ref, gk_ref, gv_ref, gz_ref, gcol_ref, grow_ref, nw_ref, ogdn_ref, gdn_state)

    @pl.when(j == pl.num_programs(1) - 1)
    def _():
        sret_ref[...] = ret_state[...]
        sgdn_ref[...] = gdn_state[...]


def _scans(layer, rq, rk, rv, rg, gq, gk, gv, gz, gcol, grow4, nw, tile, nb, ret_chunk):
    b, l, _ = gq.shape
    assert b % nb == 0 and tile % ret_chunk == 0
    tok = lambda w: pl.BlockSpec((nb, tile, w), lambda i, j: (i, j, 0))
    st = pl.BlockSpec((nb, N_HEADS, DK, DV), lambda i, j: (i, 0, 0, 0))
    o_sds = jax.ShapeDtypeStruct((b, l, V_W), bf16)
    s_sds = jax.ShapeDtypeStruct((b, N_HEADS, DK, DV), f32)
    state = pltpu.VMEM((nb, N_HEADS, DK, DV), f32)
    return pl.pallas_call(
        functools.partial(_scans_body, ret_chunk=ret_chunk), grid=(b // nb, l // tile),
        in_specs=[tok(QK_W), tok(QK_W), tok(V_W), tok(V_W), tok(QK_W), tok(QK_W), tok(V_W), tok(V_W), tok(LANES),
                  pl.BlockSpec((nb, tile // CHUNK, SUBLANES, CHUNK), lambda i, j: (i, j, 0, 0)),
                  _resident((1, DV), layer)],
        out_specs=[tok(V_W), st, tok(V_W), st], out_shape=[o_sds, s_sds, o_sds, s_sds],
        scratch_shapes=[state, state],
        compiler_params=_params("parallel", "arbitrary"), name="scans",
    )(rq, rk, rv, rg, gq, gk, gv, gz, gcol, grow4, nw)


def _softmax(s):
    e = jnp.exp(s - jnp.max(s, axis=-1, keepdims=True))
    return e / jnp.sum(e, axis=-1, keepdims=True)


def _merge_tail(x, xn, o_ret, o_gdn, mem_proj, wg_ref, wr_ref, wd_ref, wo_ref, gpost):
    gates = jax.nn.sigmoid(_dot(xn, wg_ref[...]))
    merged = (gates[:, :D_MODEL] * _dot(o_ret, wr_ref[...])
              + gates[:, D_MODEL:2 * D_MODEL] * _dot(o_gdn, wd_ref[...])
              + gates[:, 2 * D_MODEL:] * mem_proj)
    y = _dot(merged.astype(bf16), wo_ref[...])
    return x + _rms(y, gpost)


def _merge_body(h_ref, gpre_ref, wq_ref, wg_ref, mk_ref, mv_ref, oret_ref, ogdn_ref,
                wr_ref, wd_ref, wm_ref, wo_ref, gpost_ref, o_ref):
    x = h_ref[...]
    xn = _rms(x, gpre_ref[...]).astype(bf16)
    mq = _dot(xn, wq_ref[...]).astype(bf16)
    mem_proj = None
    for h in range(N_HEADS):
        sl = slice(h * HD_X, (h + 1) * HD_X)
        s = lax.dot_general(mq[:, sl], mk_ref[:, sl], NT, preferred_element_type=f32) * HD_X ** -0.5
        o = _dot(_softmax(s).astype(bf16), mv_ref[:, sl])
        part = _dot(o.astype(bf16), wm_ref[sl, :])
        mem_proj = part if mem_proj is None else mem_proj + part
    o_ref[...] = _merge_tail(x, xn, oret_ref[...], ogdn_ref[...], mem_proj, wg_ref, wr_ref, wd_ref, wo_ref,
                             gpost_ref[...])


def _merge(layer, h3d, gpre, wq, wg, mk, mv, o_ret, o_gdn, wr, wd, wm, wo, gpost, tm):
    b, l, _ = h3d.shape
    tok = pl.BlockSpec((None, tm, D_MODEL), lambda i, j: (i, j, 0))
    mem = pl.BlockSpec((None, None, N_MEM, D_MODEL), lambda i, j: (layer, i, 0, 0))
    sq = _resident((D_MODEL, D_MODEL), layer)
    return pl.pallas_call(
        _merge_body, grid=(b, l // tm),
        in_specs=[tok, _resident((1, D_MODEL), layer), _packed_mq(layer), _packed_gates(layer), mem, mem,
                  tok, tok, sq, sq, sq, sq, _resident((1, D_MODEL), layer)],
        out_specs=tok, out_shape=jax.ShapeDtypeStruct((b, l, D_MODEL), f32),
        compiler_params=_params("parallel", "parallel"), name="merge",
    )(h3d, gpre, wq, wg, mk, mv, o_ret, o_gdn, wr, wd, wm, wo, gpost)


def _sproj_body(h_ref, gpre_ref, wm_ref, wab_ref, wq_ref, cw_ref, alog_ref, dtb_ref, cos_ref, sin_ref, buf_ref,
                rq_ref, rk_ref, rv_ref, rg_ref, gq_ref, gk_ref, gv_ref, gz_ref, gcol_ref, mq_ref, nbuf_ref):
    xn = _rms(h_ref[...], gpre_ref[...]).astype(bf16)
    p = _dot(xn, wm_ref[...])
    cos, sin = cos_ref[...], sin_ref[...]
    for h in range(N_HEADS):
        sl = slice(h * DK, (h + 1) * DK)
        rq_ref[:, sl] = _rotary(p[:, _RQ + h * DK:_RQ + (h + 1) * DK], cos, sin)
        rk_ref[:, sl] = _rotary(p[:, _RK + h * DK:_RK + (h + 1) * DK], cos, sin) * DK ** -0.5
    rv_ref[...] = p[:, _RV:_RG]
    rg_ref[...] = _silu(p[:, _RG:_GQKV])
    gz_ref[...] = _silu(p[:, _GZ:_MAIN_W])
    x = p[:, _GQKV:_GZ]
    cw = cw_ref[...]
    conv = x * cw[CONV_W - 1:CONV_W]
    for s in range(CONV_W - 1):
        prev = buf_ref[:, s * CONV_DIM:(s + 1) * CONV_DIM]
        conv = conv + prev * cw[s:s + 1]
        if s > 0:
            nbuf_ref[:, (s - 1) * CONV_DIM:s * CONV_DIM] = prev
    nbuf_ref[:, (CONV_W - 2) * CONV_DIM:] = x
    qs, ks, v = _gdn_qkv(conv)
    for h in range(N_HEADS):
        gq_ref[:, h * DK:(h + 1) * DK] = qs[h]
        gk_ref[:, h * DK:(h + 1) * DK] = ks[h]
    gv_ref[...] = v
    ab = _dot(xn, wab_ref[...])
    gcol_ref[...] = _decay_beta(ab, alog_ref[...], dtb_ref[...], _iota(ab.shape, 1) < N_HEADS)
    mq_ref[...] = _dot(xn, wq_ref[...])


def _sample_proj(layer, h2d, gpre, wm, wab, wq, cw, alog, dtb, cos, sin, buf2d):
    n = h2d.shape[0]
    sds = lambda w: jax.ShapeDtypeStruct((n, w), f32)
    res = lambda *shape: _resident(shape, layer)
    widths = (QK_W, QK_W, V_W, V_W, QK_W, QK_W, V_W, V_W, LANES, D_MODEL, (CONV_W - 1) * CONV_DIM)
    return pl.pallas_call(
        _sproj_body, grid=(1,),
        in_specs=[_resident((n, D_MODEL)), res(1, D_MODEL), res(D_MODEL, _MAIN_W), res(D_MODEL, LANES),
                  _packed_mq(layer), res(CONV_W, CONV_DIM), res(1, LANES), res(1, LANES),
                  _resident((1, DK)), _resident((1, DK)), _resident((n, (CONV_W - 1) * CONV_DIM))],
        out_specs=[pl.BlockSpec((n, w), lambda i: (0, 0)) for w in widths],
        out_shape=[sds(w) for w in widths],
        compiler_params=_params("arbitrary"), name="sample_proj",
    )(h2d, gpre, wm, wab, wq, cw, alog, dtb, cos, sin, buf2d)


def _sstate_body(cols_ref, rv_ref, gv_ref, gcol_ref, sret_ref, sgdn_ref, nret_ref, ngdn_ref, oret_ref, ogdn_ref):
    vecs = cols_ref[...].T
    col = lambda j, v, h: vecs[:, (j * 4 + v) * N_HEADS + h:(j * 4 + v) * N_HEADS + h + 1]
    for j in range(rv_ref.shape[0]):
        for h in range(N_HEADS):
            vs = slice(h * DV, (h + 1) * DV)
            q, k = col(j, 0, h), col(j, 1, h)
            s_new = sret_ref[j, h] * math.exp(LOG_GAMMA[h]) + k * rv_ref[j:j + 1, vs]
            nret_ref[j, h] = s_new
            oret_ref[j:j + 1, vs] = jnp.sum(q * s_new, axis=0, keepdims=True)
            q, k = col(j, 2, h), col(j, 3, h)
            eg = jnp.exp(gcol_ref[j:j + 1, h:h + 1])
            beta = gcol_ref[j:j + 1, N_HEADS + h:N_HEADS + h + 1]
            s = sgdn_ref[j, h]
            ks = jnp.sum(k * s, axis=0, keepdims=True)
            v_new = beta * (gv_ref[j:j + 1, vs] - eg * ks)
            s_new = s * eg + k * v_new
            ngdn_ref[j, h] = s_new
            ogdn_ref[j:j + 1, vs] = jnp.sum(q * s_new, axis=0, keepdims=True)


def _sattn_body(mq_ref, k_ref, v_ref, o_ref):
    for j in range(mq_ref.shape[0]):
        s = jnp.sum(k_ref[j] * mq_ref[j][None], axis=-1, keepdims=True) * HD_X ** -0.5
        e = jnp.exp(s - jnp.max(s, axis=0, keepdims=True))
        p = e / jnp.sum(e, axis=0, keepdims=True)
        o_ref[j] = jnp.sum(p * v_ref[j], axis=0)


def _smerge_body(h_ref, gpre_ref, wg_ref, oret_ref, rg_ref, ogdn_ref, gz_ref, nw_ref, omem_ref,
                 wr_ref, wd_ref, wm_ref, wo_ref, gpost_ref, o_ref):
    x = h_ref[...]
    xn = _rms(x, gpre_ref[...]).astype(bf16)
    nw = nw_ref[...]
    rets, gdns = [], []
    for h in range(N_HEADS):
        vs = slice(h * DV, (h + 1) * DV)
        rets.append(rg_ref[:, vs] * _rms(oret_ref[:, vs]))
        gdns.append(_rms(ogdn_ref[:, vs], nw) * gz_ref[:, vs])
    o_ret = jnp.concatenate(rets, axis=-1).astype(bf16)
    o_gdn = jnp.concatenate(gdns, axis=-1).astype(bf16)
    mem_proj = _dot(omem_ref[...].astype(bf16), wm_ref[...])
    o_ref[...] = _merge_tail(x, xn, o_ret, o_gdn, mem_proj, wg_ref, wr_ref, wd_ref, wo_ref, gpost_ref[...])


def _sample_merge(layer, h2d, gpre, wg, o_ret, rg, o_gdn, gz, nw, o_mem, wr, wd, wm, wo, gpost):
    n = h2d.shape[0]
    act = _resident((n, D_MODEL))
    sq = _resident((D_MODEL, D_MODEL), layer)
    return pl.pallas_call(
        _smerge_body, grid=(1,),
        in_specs=[act, _resident((1, D_MODEL), layer), _packed_gates(layer), act, act, act, act,
                  _resident((1, DV), layer), act, sq, sq, sq, sq, _resident((1, D_MODEL), layer)],
        out_specs=pl.BlockSpec((n, D_MODEL), lambda i: (0, 0)), out_shape=jax.ShapeDtypeStruct(h2d.shape, f32),
        compiler_params=_params("arbitrary"), name="sample_merge",
    )(h2d, gpre, wg, o_ret, rg, o_gdn, gz, nw, o_mem, wr, wd, wm, wo, gpost)


def _rope_tables(pos):
    half = DK // 2
    inv_freq = ROPE_BASE ** (-jnp.arange(half, dtype=f32) / half)
    ang = pos.astype(f32)[:, None] * inv_freq[None, :]
    cos, sin = jnp.cos(ang), jnp.sin(ang)
    return jnp.concatenate([cos, cos], axis=-1), jnp.concatenate([-sin, sin], axis=-1)


def _row_tile(n, want):
    t = min(n, want)
    assert n % t == 0, (n, t)
    return t


def kernel(x_prompt, x_sample, mem_prompt, state_ret, state_gdn, state_conv, cache_mem_k, cache_mem_v, norm_ffn1_pre, norm_ffn1_post, ffn1_w_in, ffn1_w_out, norm_mix_pre, norm_mix_post, w_in, gdn_conv_w, gdn_a_log, gdn_dt_bias, gdn_norm, norm_mem, w_mem_k, w_mem_v, w_branch_ret, w_branch_gdn, w_branch_mem, w_out, norm_ffn2_pre, norm_ffn2_post, ffn2_w_in, ffn2_w_out):
    bp, sp, _ = x_prompt.shape
    ns, ss, _ = x_sample.shape
    depth = w_in.shape[0]
    assert ss == 1 and sp % CHUNK == 0
    tm_ffn = _row_tile(bp * sp, 512)
    tm_seq = _row_tile(sp, 512)
    tm_proj = _row_tile(sp, 256)

    cos_p, sin_p = _rope_tables(jnp.arange(sp))
    cos_s, sin_s = _rope_tables(PAST_LEN + jnp.arange(ss))
    row = lambda v: v.reshape(depth, 1, -1).astype(f32)
    pad_lanes = lambda v: jnp.pad(v.astype(f32), ((0, 0), (0, LANES - v.shape[1]))).reshape(depth, 1, LANES)
    pad_rows = lambda v: jnp.pad(v.astype(f32), ((0, 0), (0, SUBLANES - v.shape[1]))).reshape(depth, SUBLANES, 1)
    w_in_t = jnp.swapaxes(w_in, 1, 2)
    w_main = _w_in_cols(w_in_t, 0, _MAIN_W)
    w_mq = _w_in_cols(w_in_t, _AB_END, D_MODEL)
    w_gates = _w_in_cols(w_in_t, _MQ_END, 3 * D_MODEL)
    w_ab_row = w_in_t[:, _MAIN_W:_AB_END].astype(bf16)
    w_ab_col = jnp.pad(jnp.swapaxes(w_ab_row, 1, 2), ((0, 0), (0, 0), (0, LANES - 2 * N_HEADS)))
    f1_in, f1_out, f2_in, f2_out = (w.astype(bf16) for w in (ffn1_w_in, ffn1_w_out, ffn2_w_in, ffn2_w_out))
    wr, wd, wm, wo = (w.astype(bf16) for w in (w_branch_ret, w_branch_gdn, w_branch_mem, w_out))
    cw = gdn_conv_w.astype(f32)
    alog, dtb = pad_lanes(gdn_a_log), pad_lanes(gdn_dt_bias)
    alog_t, dtb_t = pad_rows(gdn_a_log), pad_rows(gdn_dt_bias)
    nw = row(gdn_norm)
    g_f1pre, g_f1post, g_f2pre, g_f2post = (row(g) for g in (norm_ffn1_pre, norm_ffn1_post, norm_ffn2_pre,
                                                              norm_ffn2_post))
    g_mpre, g_mpost = row(norm_mix_pre), row(norm_mix_post)

    mk, mv, mk_b, mv_b = _memkv(mem_prompt, row(norm_mem), w_mem_k.astype(bf16), w_mem_v.astype(bf16))
    hp = x_prompt
    hs = x_sample.reshape(ns, D_MODEL)
    outs = [[] for _ in range(4)]
    new_states = None
    scan_seqs = 2 if bp % 2 == 0 else 1
    scan_tile = _row_tile(sp, 256)
    ret_chunk = scan_tile
    hosted_per_step = ns // (2 * (bp * sp // tm_ffn))
    assert hosted_per_step * 2 * (bp * sp // tm_ffn) == ns
    for l in range(depth):
        h1s = _ffn(l, hs, g_f1pre, f1_in, f1_out, g_f1post, ns)
        s_rq, s_rk, s_rv, s_rg, s_gq, s_gk, s_gv, s_gz, s_gcol, mq, nbuf = _sample_proj(
            l, h1s, g_mpre, w_main, w_ab_col, w_mq, cw, alog, dtb, cos_s, sin_s,
            state_conv[l].reshape(ns, (CONV_W - 1) * CONV_DIM))
        outs[3].append(nbuf.reshape(ns, CONV_W - 1, CONV_DIM))
        groups = ns // hosted_per_step
        by_step = lambda a: a.reshape((groups, hosted_per_step) + a.shape[1:])
        cols = jnp.stack([s_rq, s_rk, s_gq, s_gk], axis=1).reshape(groups, hosted_per_step * 4 * N_HEADS, DK)
        host = lambda part: (part, 2, hosted_per_step, by_step(mq.reshape(ns, N_HEADS, HD_X)), cache_mem_k,
                             cache_mem_v, cols, by_step(s_rv), by_step(s_gv), by_step(s_gcol), state_ret,
                             state_gdn, new_states)

        h1, o_mem_a, n_ret, n_gdn, so_ret_a, so_gdn_a = _ffn(
            l, hp.reshape(bp * sp, D_MODEL), g_f1pre, f1_in, f1_out, g_f1post, tm_ffn, hosted=host(0))
        new_states = (n_ret, n_gdn)
        h1 = h1.reshape(bp, sp, D_MODEL)
        rq, rk, rv, rg, gq, gk, gv, gz, gcol, grow, tail = _proj(
            l, h1, g_mpre, w_main, w_ab_col, w_ab_row, cw, alog, dtb, alog_t, dtb_t, cos_p, sin_p, tm_proj)
        grow4 = grow.reshape(bp, SUBLANES, sp // CHUNK, CHUNK).transpose(0, 2, 1, 3)
        o_ret, s_ret, o_gdn, s_gdn = _scans(l, rq, rk, rv, rg, gq, gk, gv, gz, gcol, grow4, nw, scan_tile,
                                            scan_seqs, ret_chunk)
        h2 = _merge(l, h1, g_mpre, w_mq, w_gates, mk_b, mv_b, o_ret, o_gdn, wr, wd, wm, wo, g_mpost, tm_seq)
        hp, o_mem_b, n_ret, n_gdn, so_ret_b, so_gdn_b = _ffn(
            l, h2.reshape(bp * sp, D_MODEL), g_f2pre, f2_in, f2_out, g_f2post, tm_ffn, hosted=host(1))
        new_states = (n_ret, n_gdn)
        hp = hp.reshape(bp, sp, D_MODEL)
        outs[0].append(s_ret)
        outs[1].append(s_gdn)
        outs[2].append(tail[:, SUBLANES - (CONV_W - 1):])

        join = lambda a, b: jnp.concatenate([a, b], axis=0).reshape(ns, -1)
        o_mem, so_ret, so_gdn = join(o_mem_a, o_mem_b), join(so_ret_a, so_ret_b), join(so_gdn_a, so_gdn_b)
        h2s = _sample_merge(l, h1s, g_mpre, w_gates, so_ret, s_rg, so_gdn, s_gz, nw, o_mem, wr, wd, wm, wo, g_mpost)
        hs = _ffn(l, h2s, g_f2pre, f2_in, f2_out, g_f2post, ns)

    ret_p, gdn_p, conv_p, conv_s = (jnp.stack(o) for o in outs)
    return (hp, hs.reshape(ns, ss, D_MODEL), ret_p, gdn_p, conv_p, mk, mv) + new_states + (conv_s,)
```

```python
import functools
import math

import jax
import jax.numpy as jnp
from jax import lax
from jax.experimental import pallas as pl
from jax.experimental.pallas import tpu as pltpu

f32, bf16 = jnp.float32, jnp.bfloat16

D_MODEL = 1024
N_HEADS = 4
DK = 128
DV = 256
QK_W = N_HEADS * DK
V_W = N_HEADS * DV
CONV_W = 4
CONV_DIM = 2 * QK_W + V_W
HD_X = 256
N_MEM = 256
D_FF = 2816
EPS = 1e-6
ROPE_BASE = 10000.0
PAST_LEN = 16384
CHUNK = 64
LANES = 128
SUBLANES = 8
VMEM_LIMIT = 56 * 1024 * 1024
LOG_GAMMA = tuple(math.log1p(-2.0 ** (-5.0 - h)) for h in range(N_HEADS))
HIGHEST = lax.Precision.HIGHEST
NT = (((1,), (1,)), ((), ()))
TN = (((0,), (0,)), ((), ()))


def _params(*sem):
    return pltpu.CompilerParams(dimension_semantics=sem, vmem_limit_bytes=VMEM_LIMIT)


def _resident(shape, layer=None, col_block=0):
    if layer is None:
        return pl.BlockSpec(shape, lambda *_: (0,) * len(shape), pipeline_mode=pl.Buffered(1))
    index = (layer,) + (0,) * (len(shape) - 1) + (col_block,)
    return pl.BlockSpec((None,) + shape, lambda *_: index, pipeline_mode=pl.Buffered(1))


def _rms(x, gain=None):
    y = x * lax.rsqrt(jnp.mean(x * x, axis=-1, keepdims=True) + EPS)
    return y if gain is None else y * gain


def _silu(x):
    return x * jax.nn.sigmoid(x)


def _softplus(x):
    return jnp.maximum(x, 0.0) + jnp.log1p(jnp.exp(-jnp.abs(x)))


def _dot(a, b):
    return jnp.dot(a, b, preferred_element_type=f32)


def _iota(shape, dim):
    return lax.broadcasted_iota(jnp.int32, shape, dim)


def _ffn_body(x_ref, gpre_ref, wgu_ref, wo_ref, gpost_ref, o_ref, row_parts=1):
    part = x_ref.shape[0] // row_parts
    for i in range(row_parts):
        rows = slice(i * part, (i + 1) * part)
        x = x_ref[rows, :]
        xn = _rms(x, gpre_ref[...]).astype(bf16)
        h = _dot(xn, wgu_ref[...])
        act = (_silu(h[:, :D_FF]) * h[:, D_FF:]).astype(bf16)
        y = _dot(act, wo_ref[...])
        o_ref[rows, :] = x + 0.5 * _rms(y, gpost_ref[...])


def _ffn_hosting_body(x_ref, gpre_ref, wgu_ref, wo_ref, gpost_ref, mq_ref, k_ref, v_ref,
                      cols_ref, rv_ref, gv_ref, gcol_ref, sret_ref, sgdn_ref, *refs):
    o_ref, omem_ref, nret_ref, ngdn_ref, oret_ref, ogdn_ref = refs[-6:]
    _sstate_body(cols_ref, rv_ref, gv_ref, gcol_ref, sret_ref, sgdn_ref, nret_ref, ngdn_ref, oret_ref, ogdn_ref)
    _sattn_body(mq_ref, k_ref, v_ref, omem_ref)
    _ffn_body(x_ref, gpre_ref, wgu_ref, wo_ref, gpost_ref, o_ref, row_parts=2)


def _ffn(layer, x2d, gpre, wgu, wo, gpost, tm, hosted=None):
    t = x2d.shape[0]
    steps = t // tm
    row = pl.BlockSpec((tm, D_MODEL), lambda i: (i, 0))
    in_specs = [row, _resident((1, D_MODEL), layer), _resident((D_MODEL, 2 * D_FF), layer),
                _resident((D_FF, D_MODEL), layer), _resident((1, D_MODEL), layer)]
    out_row = jax.ShapeDtypeStruct((t, D_MODEL), f32)
    if hosted is None:
        return pl.pallas_call(
            _ffn_body, grid=(steps,), in_specs=in_specs, out_specs=row, out_shape=out_row,
            compiler_params=_params("parallel"), name="ffn",
        )(x2d, gpre, wgu, wo, gpost)
    part, parts, ps, mq, mem_k, mem_v, cols, rv, gv, gcol, s_ret, s_gdn, prev = hosted
    depth, n = s_ret.shape[:2]
    assert ps * parts * steps == n, (n, parts, steps, ps)
    first = part * steps
    grp = lambda *shape: pl.BlockSpec((None,) + shape, lambda i: (first + i,) + (0,) * len(shape))
    out_grp = lambda *shape: pl.BlockSpec((None,) + shape, lambda i: (i,) + (0,) * len(shape))
    mem = pl.BlockSpec((None, ps, N_MEM, N_HEADS, HD_X), lambda i: (layer, first + i, 0, 0, 0))
    st = pl.BlockSpec((None, ps, N_HEADS, DK, DV), lambda i: (layer, first + i, 0, 0, 0))
    st_sds = jax.ShapeDtypeStruct((depth, n, N_HEADS, DK, DV), f32)
    grp_sds = lambda *shape: jax.ShapeDtypeStruct((steps,) + shape, f32)
    prev = [] if prev is None else list(prev)
    prev_specs = [pl.BlockSpec(memory_space=pl.ANY)] * len(prev)
    aliases = {14 + i: 2 + i for i in range(len(prev))}
    return pl.pallas_call(
        _ffn_hosting_body, grid=(steps,),
        in_specs=in_specs + [grp(ps, N_HEADS, HD_X), mem, mem,
                             grp(ps * 4 * N_HEADS, DK),
                             grp(ps, V_W), grp(ps, V_W), grp(ps, LANES), st, st] + prev_specs,
        out_specs=[row, out_grp(ps, N_HEADS, HD_X), st, st, out_grp(ps, V_W), out_grp(ps, V_W)],
        out_shape=[out_row, grp_sds(ps, N_HEADS, HD_X), st_sds, st_sds, grp_sds(ps, V_W), grp_sds(ps, V_W)],
        input_output_aliases=aliases, compiler_params=_params("parallel"), name="ffn_hosting",
    )(x2d, gpre, wgu, wo, gpost, mq, mem_k, mem_v, cols, rv, gv, gcol, s_ret, s_gdn, *prev)


def _memkv_body(m_ref, g_ref, wk_ref, wv_ref, k_ref, v_ref, kb_ref, vb_ref):
    nb = m_ref.shape[0]
    mn = _rms(m_ref[...].reshape(nb * N_MEM, D_MODEL), g_ref[...]).astype(bf16)
    k = _dot(mn, wk_ref[...])
    v = _dot(mn, wv_ref[...])
    for b in range(nb):
        rows = slice(b * N_MEM, (b + 1) * N_MEM)
        for h in range(N_HEADS):
            k_ref[b, :, h, :] = k[rows, h * HD_X:(h + 1) * HD_X]
            v_ref[b, :, h, :] = v[rows, h * HD_X:(h + 1) * HD_X]
        kb_ref[b] = k[rows].astype(bf16)
        vb_ref[b] = v[rows].astype(bf16)


def _memkv(mem, g, wk, wv):
    b = mem.shape[0]
    depth = wk.shape[0]
    nb = 4 if b % 4 == 0 else 1
    per_layer = lambda shape: pl.BlockSpec((None,) + shape, lambda l, i: (l,) + (0,) * len(shape))
    out5 = pl.BlockSpec((None, nb, N_MEM, N_HEADS, HD_X), lambda l, i: (l, i, 0, 0, 0))
    outb = pl.BlockSpec((None, nb, N_MEM, D_MODEL), lambda l, i: (l, i, 0, 0))
    sds5 = jax.ShapeDtypeStruct((depth, b, N_MEM, N_HEADS, HD_X), f32)
    sdsb = jax.ShapeDtypeStruct((depth, b, N_MEM, D_MODEL), bf16)
    return pl.pallas_call(
        _memkv_body, grid=(depth, b // nb),
        in_specs=[pl.BlockSpec((nb, N_MEM, D_MODEL), lambda l, i: (i, 0, 0)), per_layer((1, D_MODEL)),
                  per_layer((D_MODEL, D_MODEL)), per_layer((D_MODEL, D_MODEL))],
        out_specs=[out5, out5, outb, outb], out_shape=[sds5, sds5, sdsb, sdsb],
        compiler_params=_params("parallel", "parallel"), name="memkv",
    )(mem, g, wk, wv)


_RQ, _RK, _RV, _RG, _GQKV, _GZ, _MAIN_W = 0, 512, 1024, 2048, 3072, 5120, 6144
_AB_END = _MAIN_W + 2 * N_HEADS
_MQ_END = _AB_END + D_MODEL


def _packed_mq(layer):
    return _resident((D_MODEL, D_MODEL), layer)


def _packed_gates(layer):
    return _resident((D_MODEL, 3 * D_MODEL), layer)


def _w_in_cols_body(wt_ref, o_ref):
    o_ref[...] = wt_ref[0].T.astype(bf16)


def _w_in_cols(w_in_t, first, width, blk=1024):
    depth = w_in_t.shape[0]
    assert first % SUBLANES == 0 and width % blk == 0
    return pl.pallas_call(
        _w_in_cols_body, grid=(depth, width // blk),
        in_specs=[pl.BlockSpec((pl.Element(1), pl.Element(blk), pl.Element(D_MODEL)),
                               lambda l, c: (l, pl.multiple_of(first + c * blk, SUBLANES), 0))],
        out_specs=pl.BlockSpec((None, D_MODEL, blk), lambda l, c: (l, 0, c)),
        out_shape=jax.ShapeDtypeStruct((depth, D_MODEL, width), bf16),
        compiler_params=_params("parallel", "parallel"), name="w_in_cols",
    )(w_in_t)


def _rotary(x, cos, sin_signed):
    return x * cos + pltpu.roll(x, DK // 2, axis=1) * sin_signed


def _decay_beta(ab, a_log, dt_bias, is_decay):
    g = -jnp.exp(a_log) * _softplus(ab + dt_bias)
    return jnp.where(is_decay, g, jax.nn.sigmoid(ab))


def _gdn_qkv(conv):
    a = _silu(conv)
    qs, ks = [], []
    for h in range(N_HEADS):
        q = a[:, h * DK:(h + 1) * DK]
        k = a[:, QK_W + h * DK:QK_W + (h + 1) * DK]
        qs.append(q * (lax.rsqrt(jnp.sum(q * q, axis=-1, keepdims=True) + EPS) * DK ** -0.5))
        ks.append(k * lax.rsqrt(jnp.sum(k * k, axis=-1, keepdims=True) + EPS))
    return qs, ks, a[:, 2 * QK_W:]


def _proj_body(h_ref, gpre_ref, wm_ref, wab_ref, wabt_ref, cw_ref, alog_ref, dtb_ref, alogt_ref, dtbt_ref,
               cos_ref, sin_ref,
               rq_ref, rk_ref, rv_ref, rg_ref, gq_ref, gk_ref, gv_ref, gz_ref, gcol_ref, grow_ref, tail_ref,
               carry_ref):
    j = pl.program_id(1)
    tm = h_ref.shape[0]

    @pl.when(j == 0)
    def _():
        carry_ref[...] = jnp.zeros_like(carry_ref)

    xn = _rms(h_ref[...], gpre_ref[...]).astype(bf16)
    cos, sin = cos_ref[...], sin_ref[...]

    def ret_qk(p):
        for h in range(N_HEADS):
            sl = slice(h * DK, (h + 1) * DK)
            rq_ref[:, sl] = _rotary(p[:, h * DK:(h + 1) * DK], cos, sin).astype(bf16)
            rk_ref[:, sl] = (_rotary(p[:, QK_W + h * DK:QK_W + (h + 1) * DK], cos, sin) * DK ** -0.5).astype(bf16)

    def ret_v(p):
        rv_ref[...] = p.astype(bf16)

    def ret_gate(p):
        rg_ref[...] = _silu(p).astype(bf16)

    def gdn_z(p):
        gz_ref[...] = _silu(p).astype(bf16)

    def conv_silu(x, cols):
        carry = carry_ref[:, cols]
        cw = cw_ref[:, cols]
        conv = x * cw[CONV_W - 1:CONV_W]
        conv_top = x[:SUBLANES] * cw[CONV_W - 1:CONV_W]
        first = _iota(conv_top.shape, 0)
        for s in range(1, CONV_W):
            w = cw[CONV_W - 1 - s:CONV_W - s]
            xs = pltpu.roll(x, s, axis=0)
            conv = conv + xs * w
            conv_top = conv_top + jnp.where(first < s, pltpu.roll(carry, s, axis=0), xs[:SUBLANES]) * w
        carry_ref[:, cols] = x[tm - SUBLANES:]
        tail_ref[:, cols] = x[tm - SUBLANES:]
        return (slice(None), _silu(conv)), (slice(0, SUBLANES), _silu(conv_top))

    def gdn_qk(p):
        for rows, a in conv_silu(p, slice(0, 2 * QK_W)):
            for h in range(N_HEADS):
                q = a[:, h * DK:(h + 1) * DK]
                k = a[:, QK_W + h * DK:QK_W + (h + 1) * DK]
                q = q * (lax.rsqrt(jnp.sum(q * q, axis=-1, keepdims=True) + EPS) * DK ** -0.5)
                k = k * lax.rsqrt(jnp.sum(k * k, axis=-1, keepdims=True) + EPS)
                gq_ref[rows, h * DK:(h + 1) * DK] = q.astype(bf16)
                gk_ref[rows, h * DK:(h + 1) * DK] = k.astype(bf16)

    def gdn_v(p):
        for rows, a in conv_silu(p, slice(2 * QK_W, CONV_DIM)):
            gv_ref[rows, :] = a.astype(bf16)

    groups = ((_RV, _RG, ret_v), (_RG, _GQKV, ret_gate), (_GZ, _MAIN_W, gdn_z), (_RQ, _RV, ret_qk),
              (_GQKV + 2 * QK_W, _GZ, gdn_v), (_GQKV, _GQKV + 2 * QK_W, gdn_qk))
    pending = None
    for lo, hi, epilogue in groups:
        p = _dot(xn, wm_ref[:, lo:hi])
        if pending is not None:
            pending()
        pending = functools.partial(epilogue, p)
    pending()

    ab = _dot(xn, wab_ref[...])
    gcol_ref[...] = _decay_beta(ab, alog_ref[...], dtb_ref[...], _iota(ab.shape, 1) < N_HEADS)
    abt = lax.dot_general(wabt_ref[...], xn, NT, preferred_element_type=f32)
    grow_ref[...] = _decay_beta(abt, alogt_ref[...], dtbt_ref[...], _iota(abt.shape, 0) < N_HEADS)


def _proj(layer, h3d, gpre, wm, wab, wabt, cw, alog, dtb, alogt, dtbt, cos, sin, tm):
    b, l, _ = h3d.shape
    tok = lambda w: pl.BlockSpec((None, tm, w), lambda i, j: (i, j, 0))
    tab = pl.BlockSpec((tm, DK), lambda i, j: (j, 0))
    sds = lambda w, dt=bf16: jax.ShapeDtypeStruct((b, l, w), dt)
    res = lambda *shape: _resident(shape, layer)
    return pl.pallas_call(
        _proj_body, grid=(b, l // tm),
        in_specs=[tok(D_MODEL), res(1, D_MODEL), res(D_MODEL, _MAIN_W), res(D_MODEL, LANES),
                  res(SUBLANES, D_MODEL), res(CONV_W, CONV_DIM), res(1, LANES),
                  res(1, LANES), res(SUBLANES, 1), res(SUBLANES, 1), tab, tab],
        out_specs=[tok(QK_W), tok(QK_W), tok(V_W), tok(V_W), tok(QK_W), tok(QK_W), tok(V_W), tok(V_W),
                   tok(LANES), pl.BlockSpec((None, SUBLANES, tm), lambda i, j: (i, 0, j)),
                   pl.BlockSpec((None, SUBLANES, CONV_DIM), lambda i, j: (i, 0, 0))],
        out_shape=[sds(QK_W), sds(QK_W), sds(V_W), sds(V_W), sds(QK_W), sds(QK_W), sds(V_W), sds(V_W),
                   sds(LANES, f32), jax.ShapeDtypeStruct((b, SUBLANES, l), f32),
                   jax.ShapeDtypeStruct((b, SUBLANES, CONV_DIM), f32)],
        scratch_shapes=[pltpu.VMEM((SUBLANES, CONV_DIM), f32)],
        compiler_params=_params("parallel", "arbitrary"), name="proj",
    )(h3d, gpre, wm, wab, wabt, cw, alog, dtb, alogt, dtbt, cos, sin)


def _ret_steps(q_ref, k_ref, v_ref, sg_ref, o_ref, s_ref, chunk):
    c = chunk
    row, col = _iota((c, c), 0), _iota((c, c), 1)
    dist = (row - col).astype(f32)
    pos = _iota((c, 1), 0).astype(f32)
    per_head = {}

    def head_consts(h):
        if h not in per_head:
            lg = LOG_GAMMA[h]
            per_head[h] = (jnp.exp(jnp.where(row >= col, dist * lg, -jnp.inf)), jnp.exp((pos + 1.0) * lg),
                           jnp.exp((c - 1.0 - pos) * lg))
        return per_head[h]

    def step(h, t, b):
        decay, q_scale, k_scale = head_consts(h)
        rows = slice(t * c, (t + 1) * c)
        qh = q_ref[b, rows, h * DK:(h + 1) * DK]
        kh = k_ref[b, rows, h * DK:(h + 1) * DK]
        vh = v_ref[b, rows, h * DV:(h + 1) * DV]
        attn = lax.dot_general(qh, kh, NT, preferred_element_type=f32) * decay
        q_dec = (qh.astype(f32) * q_scale).astype(bf16)
        k_dec = (kh.astype(f32) * k_scale).astype(bf16)
        s = s_ref[b, h]
        o = _dot(q_dec, s.astype(bf16)) + _dot(attn.astype(bf16), vh)
        s_ref[b, h] = s * math.exp(c * LOG_GAMMA[h]) + lax.dot_general(k_dec, vh, TN, preferred_element_type=f32)
        o_ref[b, rows, h * DV:(h + 1) * DV] = (
            sg_ref[b, rows, h * DV:(h + 1) * DV].astype(f32) * _rms(o)).astype(bf16)

    return [functools.partial(step, h, t, b) for h in range(N_HEADS) for t in range(q_ref.shape[1] // c)
            for b in range(q_ref.shape[0])]


def _gdn_steps(q_ref, k_ref, v_ref, sz_ref, gcol_ref, grow_ref, nw_ref, o_ref, s_ref):
    c = CHUNK
    hc = N_HEADS * c
    row, lane = _iota((c, hc), 0), _iota((c, hc), 1)
    lane_head, col = lane // c, lane % c
    lower, strict = row >= col, row > col
    head_mask = [(lane_head == h).astype(bf16) for h in range(N_HEADS)]
    block_diag = lambda x: jnp.concatenate([x * m for m in head_mask], axis=0)
    spread = lambda cols: functools.reduce(
        lambda acc, h: jnp.where(lane_head == h, cols[h], acc), range(N_HEADS - 1), cols[N_HEADS - 1])
    cum_rows = (_iota((c, c), 0) >= _iota((c, c), 1)).astype(f32)
    cum_cols = (_iota((c, hc), 0) <= (_iota((c, hc), 1) % c)).astype(f32)
    own_row = _iota((SUBLANES, hc), 0) == _iota((SUBLANES, hc), 1) // c
    zeros_k = jnp.zeros((c, DK), bf16)
    nw = nw_ref[...]

    items = [(b, t) for t in range(q_ref.shape[1] // c) for b in range(q_ref.shape[0])]
    chunks = range(len(items))
    tok = lambda t: slice(t * c, (t + 1) * c)

    def chunk_setup(b, t):
        rows = tok(t)
        k_heads = [k_ref[b, rows, h * DK:(h + 1) * DK] for h in range(N_HEADS)]
        k_bd = jnp.concatenate([jnp.concatenate([k_heads[h] if g == h else zeros_k for g in range(N_HEADS)], axis=1)
                                for h in range(N_HEADS)], axis=0)
        qk = lax.dot_general(jnp.concatenate([q_ref[b, rows, :], k_ref[b, rows, :]], axis=0), k_bd, NT,
                             preferred_element_type=f32)
        gcol = gcol_ref[b, rows, :]
        cum = jnp.dot(cum_rows, gcol, preferred_element_type=f32, precision=HIGHEST)
        b_cols = [cum[:, h:h + 1] for h in range(N_HEADS)]
        beta_cols = [gcol[:, N_HEADS + h:N_HEADS + h + 1] for h in range(N_HEADS)]
        b_row = jnp.sum(jnp.where(own_row, jnp.dot(grow_ref[b, t], cum_cols, preferred_element_type=f32,
                                                   precision=HIGHEST), 0.0), axis=0, keepdims=True)
        decay = jnp.exp(jnp.where(lower, spread(b_cols) - b_row, -jnp.inf))
        x = jnp.where(strict, qk[c:] * spread(beta_cols) * decay, 0.0) * -1.0
        return (qk[:c] * decay, x, jnp.concatenate(b_cols, axis=0), jnp.concatenate(beta_cols, axis=0))

    attns, xs, b_colv, betav = zip(*[chunk_setup(b, t) for b, t in items])
    ns = xs
    x_bds = [block_diag(x.astype(bf16)) for x in xs]
    for _ in range(int(math.log2(c)) - 1):
        xs = [_dot(x.astype(bf16), x_bd) for x, x_bd in zip(xs, x_bds)]
        x_bds = [block_diag(x.astype(bf16)) for x in xs]
        ns = [n + x + _dot(n.astype(bf16), x_bd) for n, x, x_bd in zip(ns, xs, x_bds)]

    def chunk_wy(i):
        b, rows = items[i][0], tok(items[i][1])
        b_col, beta = b_colv[i], betav[i]
        kst = jnp.concatenate([k_ref[b, rows, h * DK:(h + 1) * DK] for h in range(N_HEADS)], axis=0)
        qst = jnp.concatenate([q_ref[b, rows, h * DK:(h + 1) * DK] for h in range(N_HEADS)], axis=0)
        vst = jnp.concatenate([v_ref[b, rows, h * DV:(h + 1) * DV] for h in range(N_HEADS)], axis=0).astype(f32)
        n_bd = block_diag(ns[i].astype(bf16))
        kf = kst.astype(f32)
        e_b = jnp.exp(b_col)
        ru, rw = vst * beta, kf * (beta * e_b)
        u = ru + _dot(n_bd, ru.astype(bf16))
        w = (rw + _dot(n_bd, rw.astype(bf16))).astype(bf16)
        q_dec = (qst.astype(f32) * e_b).astype(bf16)
        return u, w, q_dec, kf, block_diag(attns[i].astype(bf16))

    wy = [chunk_wy(i) for i in chunks]
    for i in chunks:
        b, rows = items[i][0], tok(items[i][1])
        u, w, q_dec, kf, attn_bd = wy[i]
        b_col = b_colv[i]
        v_new, o_inter = [], []
        for h in range(N_HEADS):
            hs = slice(h * c, (h + 1) * c)
            r = _dot(jnp.concatenate([w[hs], q_dec[hs]], axis=0), s_ref[b, h].astype(bf16))
            v_new.append(u[hs] - r[:c])
            o_inter.append(r[c:])
        vb = jnp.concatenate(v_new, axis=0).astype(bf16)
        o = jnp.concatenate(o_inter, axis=0) + _dot(attn_bd, vb)
        for h in range(N_HEADS):
            hs = slice(h * c, (h + 1) * c)
            b_last = b_col[(h + 1) * c - 1:(h + 1) * c]
            k_dec = (kf[hs] * jnp.exp(b_last - b_col[hs])).astype(bf16)
            s_ref[b, h] = (s_ref[b, h] * jnp.exp(b_last)
                           + lax.dot_general(k_dec, vb[hs], TN, preferred_element_type=f32))
            o_ref[b, rows, h * DV:(h + 1) * DV] = (
                _rms(o[hs], nw) * sz_ref[b, rows, h * DV:(h + 1) * DV].astype(f32)).astype(bf16)


def _scans_body(rq_ref, rk_ref, rv_ref, rg_ref, gq_ref, gk_ref, gv_ref, gz_ref, gcol_ref, grow_ref, nw_ref,
                oret_ref, sret_ref, ogdn_ref, sgdn_ref, ret_state, gdn_state, *, ret_chunk):
    j = pl.program_id(1)

    @pl.when(j == 0)
    def _():
        ret_state[...] = jnp.zeros_like(ret_state)
        gdn_state[...] = jnp.zeros_like(gdn_state)

    for step in _ret_steps(rq_ref, rk_ref, rv_ref, rg_ref, oret_ref, ret_state, ret_chunk):
        step()
    _gdn_steps(gq_ref, gk_ref, gv_ref, gz_ref, gcol_ref, grow_ref, nw_ref, ogdn_ref, gdn_state)

    @pl.when(j == pl.num_programs(1) - 1)
    def _():
        sret_ref[...] = ret_state[...]
        sgdn_ref[...] = gdn_state[...]


def _scans(layer, rq, rk, rv, rg, gq, gk, gv, gz, gcol, grow4, nw, tile, nb, ret_chunk):
    b, l, _ = gq.shape
    assert b % nb == 0 and tile % ret_chunk == 0
    tok = lambda w: pl.BlockSpec((nb, tile, w), lambda i, j: (i, j, 0))
    st = pl.BlockSpec((nb, N_HEADS, DK, DV), lambda i, j: (i, 0, 0, 0))
    o_sds = jax.ShapeDtypeStruct((b, l, V_W), bf16)
    s_sds = jax.ShapeDtypeStruct((b, N_HEADS, DK, DV), f32)
    state = pltpu.VMEM((nb, N_HEADS, DK, DV), f32)
    return pl.pallas_call(
        functools.partial(_scans_body, ret_chunk=ret_chunk), grid=(b // nb, l // tile),
        in_specs=[tok(QK_W), tok(QK_W), tok(V_W), tok(V_W), tok(QK_W), tok(QK_W), tok(V_W), tok(V_W), tok(LANES),
                  pl.BlockSpec((nb, tile // CHUNK, SUBLANES, CHUNK), lambda i, j: (i, j, 0, 0)),
                  _resident((1, DV), layer)],
        out_specs=[tok(V_W), st, tok(V_W), st], out_shape=[o_sds, s_sds, o_sds, s_sds],
        scratch_shapes=[state, state],
        compiler_params=_params("parallel", "arbitrary"), name="scans",
    )(rq, rk, rv, rg, gq, gk, gv, gz, gcol, grow4, nw)


def _softmax(s):
    e = jnp.exp(s - jnp.max(s, axis=-1, keepdims=True))
    return e / jnp.sum(e, axis=-1, keepdims=True)


def _merge_tail(x, xn, o_ret, o_gdn, mem_proj, wg_ref, wr_ref, wd_ref, wo_ref, gpost):
    gates = jax.nn.sigmoid(_dot(xn, wg_ref[...]))
    merged = (gates[:, :D_MODEL] * _dot(o_ret, wr_ref[...])
              + gates[:, D_MODEL:2 * D_MODEL] * _dot(o_gdn, wd_ref[...])
              + gates[:, 2 * D_MODEL:] * mem_proj)
    y = _dot(merged.astype(bf16), wo_ref[...])
    return x + _rms(y, gpost)


def _merge_body(h_ref, gpre_ref, wq_ref, wg_ref, mk_ref, mv_ref, oret_ref, ogdn_ref,
                wr_ref, wd_ref, wm_ref, wo_ref, gpost_ref, o_ref):
    x = h_ref[...]
    xn = _rms(x, gpre_ref[...]).astype(bf16)
    mq = _dot(xn, wq_ref[...]).astype(bf16)
    mem_proj = None
    for h in range(N_HEADS):
        sl = slice(h * HD_X, (h + 1) * HD_X)
        s = lax.dot_general(mq[:, sl], mk_ref[:, sl], NT, preferred_element_type=f32) * HD_X ** -0.5
        o = _dot(_softmax(s).astype(bf16), mv_ref[:, sl])
        part = _dot(o.astype(bf16), wm_ref[sl, :])
        mem_proj = part if mem_proj is None else mem_proj + part
    o_ref[...] = _merge_tail(x, xn, oret_ref[...], ogdn_ref[...], mem_proj, wg_ref, wr_ref, wd_ref, wo_ref,
                             gpost_ref[...])


def _merge(layer, h3d, gpre, wq, wg, mk, mv, o_ret, o_gdn, wr, wd, wm, wo, gpost, tm):
    b, l, _ = h3d.shape
    tok = pl.BlockSpec((None, tm, D_MODEL), lambda i, j: (i, j, 0))
    mem = pl.BlockSpec((None, None, N_MEM, D_MODEL), lambda i, j: (layer, i, 0, 0))
    sq = _resident((D_MODEL, D_MODEL), layer)
    return pl.pallas_call(
        _merge_body, grid=(b, l // tm),
        in_specs=[tok, _resident((1, D_MODEL), layer), _packed_mq(layer), _packed_gates(layer), mem, mem,
                  tok, tok, sq, sq, sq, sq, _resident((1, D_MODEL), layer)],
        out_specs=tok, out_shape=jax.ShapeDtypeStruct((b, l, D_MODEL), f32),
        compiler_params=_params("parallel", "parallel"), name="merge",
    )(h3d, gpre, wq, wg, mk, mv, o_ret, o_gdn, wr, wd, wm, wo, gpost)


def _sproj_body(h_ref, gpre_ref, wm_ref, wab_ref, wq_ref, cw_ref, alog_ref, dtb_ref, cos_ref, sin_ref, buf_ref,
                rq_ref, rk_ref, rv_ref, rg_ref, gq_ref, gk_ref, gv_ref, gz_ref, gcol_ref, mq_ref, nbuf_ref):
    xn = _rms(h_ref[...], gpre_ref[...]).astype(bf16)
    p = _dot(xn, wm_ref[...])
    cos, sin = cos_ref[...], sin_ref[...]
    for h in range(N_HEADS):
        sl = slice(h * DK, (h + 1) * DK)
        rq_ref[:, sl] = _rotary(p[:, _RQ + h * DK:_RQ + (h + 1) * DK], cos, sin)
        rk_ref[:, sl] = _rotary(p[:, _RK + h * DK:_RK + (h + 1) * DK], cos, sin) * DK ** -0.5
    rv_ref[...] = p[:, _RV:_RG]
    rg_ref[...] = _silu(p[:, _RG:_GQKV])
    gz_ref[...] = _silu(p[:, _GZ:_MAIN_W])
    x = p[:, _GQKV:_GZ]
    cw = cw_ref[...]
    conv = x * cw[CONV_W - 1:CONV_W]
    for s in range(CONV_W - 1):
        prev = buf_ref[:, s * CONV_DIM:(s + 1) * CONV_DIM]
        conv = conv + prev * cw[s:s + 1]
        if s > 0:
            nbuf_ref[:, (s - 1) * CONV_DIM:s * CONV_DIM] = prev
    nbuf_ref[:, (CONV_W - 2) * CONV_DIM:] = x
    qs, ks, v = _gdn_qkv(conv)
    for h in range(N_HEADS):
        gq_ref[:, h * DK:(h + 1) * DK] = qs[h]
        gk_ref[:, h * DK:(h + 1) * DK] = ks[h]
    gv_ref[...] = v
    ab = _dot(xn, wab_ref[...])
    gcol_ref[...] = _decay_beta(ab, alog_ref[...], dtb_ref[...], _iota(ab.shape, 1) < N_HEADS)
    mq_ref[...] = _dot(xn, wq_ref[...])


def _sample_pre_body(x_ref, g1_ref, wgu_ref, wof_ref, g2_ref, *refs):
    proj_in, h1_ref, proj_out = refs[:10], refs[10], refs[11:]
    _ffn_body(x_ref, g1_ref, wgu_ref, wof_ref, g2_ref, h1_ref)
    _sproj_body(h1_ref, *proj_in, *proj_out)


def _sample_pre(layer, x2d, g1, wgu, wof, g2, gpre, wm, wab, wq, cw, alog, dtb, cos, sin, buf2d):
    n = x2d.shape[0]
    sds = lambda w: jax.ShapeDtypeStruct((n, w), f32)
    res = lambda *shape: _resident(shape, layer)
    widths = (D_MODEL, QK_W, QK_W, V_W, V_W, QK_W, QK_W, V_W, V_W, LANES, D_MODEL, (CONV_W - 1) * CONV_DIM)
    return pl.pallas_call(
        _sample_pre_body, grid=(1,),
        in_specs=[_resident((n, D_MODEL)), res(1, D_MODEL), res(D_MODEL, 2 * D_FF), res(D_FF, D_MODEL), res(1, D_MODEL),
                  res(1, D_MODEL), res(D_MODEL, _MAIN_W), res(D_MODEL, LANES),
                  _packed_mq(layer), res(CONV_W, CONV_DIM), res(1, LANES), res(1, LANES),
                  _resident((1, DK)), _resident((1, DK)), _resident((n, (CONV_W - 1) * CONV_DIM))],
        out_specs=[pl.BlockSpec((n, w), lambda i: (0, 0)) for w in widths],
        out_shape=[sds(w) for w in widths],
        compiler_params=_params("arbitrary"), name="sample_pre",
    )(x2d, g1, wgu, wof, g2, gpre, wm, wab, wq, cw, alog, dtb, cos, sin, buf2d)


def _sstate_body(cols_ref, rv_ref, gv_ref, gcol_ref, sret_ref, sgdn_ref, nret_ref, ngdn_ref, oret_ref, ogdn_ref):
    vecs = cols_ref[...].T
    col = lambda j, v, h: vecs[:, (j * 4 + v) * N_HEADS + h:(j * 4 + v) * N_HEADS + h + 1]
    for j in range(rv_ref.shape[0]):
        for h in range(N_HEADS):
            vs = slice(h * DV, (h + 1) * DV)
            q, k = col(j, 0, h), col(j, 1, h)
            s_new = sret_ref[j, h] * math.exp(LOG_GAMMA[h]) + k * rv_ref[j:j + 1, vs]
            nret_ref[j, h] = s_new
            oret_ref[j:j + 1, vs] = jnp.sum(q * s_new, axis=0, keepdims=True)
            q, k = col(j, 2, h), col(j, 3, h)
            eg = jnp.exp(gcol_ref[j:j + 1, h:h + 1])
            beta = gcol_ref[j:j + 1, N_HEADS + h:N_HEADS + h + 1]
            s = sgdn_ref[j, h]
            ks = jnp.sum(k * s, axis=0, keepdims=True)
            v_new = beta * (gv_ref[j:j + 1, vs] - eg * ks)
            s_new = s * eg + k * v_new
            ngdn_ref[j, h] = s_new
            ogdn_ref[j:j + 1, vs] = jnp.sum(q * s_new, axis=0, keepdims=True)


def _sattn_body(mq_ref, k_ref, v_ref, o_ref):
    for j in range(mq_ref.shape[0]):
        s = jnp.sum(k_ref[j] * mq_ref[j][None], axis=-1, keepdims=True) * HD_X ** -0.5
        e = jnp.exp(s - jnp.max(s, axis=0, keepdims=True))
        p = e / jnp.sum(e, axis=0, keepdims=True)
        o_ref[j] = jnp.sum(p * v_ref[j], axis=0)


def _smerge_body(h_ref, gpre_ref, wg_ref, oret_ref, rg_ref, ogdn_ref, gz_ref, nw_ref, omem_ref,
                 wr_ref, wd_ref, wm_ref, wo_ref, gpost_ref, o_ref):
    x = h_ref[...]
    xn = _rms(x, gpre_ref[...]).astype(bf16)
    nw = nw_ref[...]
    rets, gdns = [], []
    for h in range(N_HEADS):
        vs = slice(h * DV, (h + 1) * DV)
        rets.append(rg_ref[:, vs] * _rms(oret_ref[:, vs]))
        gdns.append(_rms(ogdn_ref[:, vs], nw) * gz_ref[:, vs])
    o_ret = jnp.concatenate(rets, axis=-1).astype(bf16)
    o_gdn = jnp.concatenate(gdns, axis=-1).astype(bf16)
    mem_proj = _dot(omem_ref[...].astype(bf16), wm_ref[...])
    o_ref[...] = _merge_tail(x, xn, o_ret, o_gdn, mem_proj, wg_ref, wr_ref, wd_ref, wo_ref, gpost_ref[...])


def _sample_post_body(*refs):
    merge_in, (g1_ref, wgu_ref, wof_ref, g2_ref, o_ref, h2_ref) = refs[:14], refs[14:]
    _smerge_body(*merge_in, h2_ref)
    _ffn_body(h2_ref, g1_ref, wgu_ref, wof_ref, g2_ref, o_ref)


def _sample_post(layer, h2d, gpre, wg, o_ret, rg, o_gdn, gz, nw, o_mem, wr, wd, wm, wo, gpost, g1, wgu, wof, g2):
    n = h2d.shape[0]
    act = _resident((n, D_MODEL))
    sq = _resident((D_MODEL, D_MODEL), layer)
    vec = _resident((1, D_MODEL), layer)
    return pl.pallas_call(
        _sample_post_body, grid=(1,),
        in_specs=[act, vec, _packed_gates(layer), act, act, act, act,
                  _resident((1, DV), layer), act, sq, sq, sq, sq, vec,
                  vec, _resident((D_MODEL, 2 * D_FF), layer), _resident((D_FF, D_MODEL), layer), vec],
        out_specs=pl.BlockSpec((n, D_MODEL), lambda i: (0, 0)), out_shape=jax.ShapeDtypeStruct(h2d.shape, f32),
        scratch_shapes=[pltpu.VMEM((n, D_MODEL), f32)],
        compiler_params=_params("arbitrary"), name="sample_post",
    )(h2d, gpre, wg, o_ret, rg, o_gdn, gz, nw, o_mem, wr, wd, wm, wo, gpost, g1, wgu, wof, g2)


def _rope_tables(pos):
    half = DK // 2
    inv_freq = ROPE_BASE ** (-jnp.arange(half, dtype=f32) / half)
    ang = pos.astype(f32)[:, None] * inv_freq[None, :]
    cos, sin = jnp.cos(ang), jnp.sin(ang)
    return jnp.concatenate([cos, cos], axis=-1), jnp.concatenate([-sin, sin], axis=-1)


def _row_tile(n, want):
    t = min(n, want)
    assert n % t == 0, (n, t)
    return t


def kernel(x_prompt, x_sample, mem_prompt, state_ret, state_gdn, state_conv, cache_mem_k, cache_mem_v, norm_ffn1_pre, norm_ffn1_post, ffn1_w_in, ffn1_w_out, norm_mix_pre, norm_mix_post, w_in, gdn_conv_w, gdn_a_log, gdn_dt_bias, gdn_norm, norm_mem, w_mem_k, w_mem_v, w_branch_ret, w_branch_gdn, w_branch_mem, w_out, norm_ffn2_pre, norm_ffn2_post, ffn2_w_in, ffn2_w_out):
    bp, sp, _ = x_prompt.shape
    ns, ss, _ = x_sample.shape
    depth = w_in.shape[0]
    assert ss == 1 and sp % CHUNK == 0
    tm_ffn = _row_tile(bp * sp, 512)
    tm_seq = _row_tile(sp, 512)
    tm_proj = _row_tile(sp, 256)

    cos_p, sin_p = _rope_tables(jnp.arange(sp))
    cos_s, sin_s = _rope_tables(PAST_LEN + jnp.arange(ss))
    row = lambda v: v.reshape(depth, 1, -1).astype(f32)
    pad_lanes = lambda v: jnp.pad(v.astype(f32), ((0, 0), (0, LANES - v.shape[1]))).reshape(depth, 1, LANES)
    pad_rows = lambda v: jnp.pad(v.astype(f32), ((0, 0), (0, SUBLANES - v.shape[1]))).reshape(depth, SUBLANES, 1)
    w_in_t = jnp.swapaxes(w_in, 1, 2)
    w_main = _w_in_cols(w_in_t, 0, _MAIN_W)
    w_mq = _w_in_cols(w_in_t, _AB_END, D_MODEL)
    w_gates = _w_in_cols(w_in_t, _MQ_END, 3 * D_MODEL)
    w_ab_row = w_in_t[:, _MAIN_W:_AB_END].astype(bf16)
    w_ab_col = jnp.pad(jnp.swapaxes(w_ab_row, 1, 2), ((0, 0), (0, 0), (0, LANES - 2 * N_HEADS)))
    f1_in, f1_out, f2_in, f2_out = (w.astype(bf16) for w in (ffn1_w_in, ffn1_w_out, ffn2_w_in, ffn2_w_out))
    wr, wd, wm, wo = (w.astype(bf16) for w in (w_branch_ret, w_branch_gdn, w_branch_mem, w_out))
    cw = gdn_conv_w.astype(f32)
    alog, dtb = pad_lanes(gdn_a_log), pad_lanes(gdn_dt_bias)
    alog_t, dtb_t = pad_rows(gdn_a_log), pad_rows(gdn_dt_bias)
    nw = row(gdn_norm)
    g_f1pre, g_f1post, g_f2pre, g_f2post = (row(g) for g in (norm_ffn1_pre, norm_ffn1_post, norm_ffn2_pre,
                                                              norm_ffn2_post))
    g_mpre, g_mpost = row(norm_mix_pre), row(norm_mix_post)

    mk, mv, mk_b, mv_b = _memkv(mem_prompt, row(norm_mem), w_mem_k.astype(bf16), w_mem_v.astype(bf16))
    hp = x_prompt
    hs = x_sample.reshape(ns, D_MODEL)
    outs = [[] for _ in range(4)]
    new_states = None
    scan_seqs = 2 if bp % 2 == 0 else 1
    scan_tile = _row_tile(sp, 256)
    ret_chunk = scan_tile
    hosted_per_step = ns // (2 * (bp * sp // tm_ffn))
    assert hosted_per_step * 2 * (bp * sp // tm_ffn) == ns
    for l in range(depth):
        h1s, s_rq, s_rk, s_rv, s_rg, s_gq, s_gk, s_gv, s_gz, s_gcol, mq, nbuf = _sample_pre(
            l, hs, g_f1pre, f1_in, f1_out, g_f1post, g_mpre, w_main, w_ab_col, w_mq, cw, alog, dtb, cos_s, sin_s,
            state_conv[l].reshape(ns, (CONV_W - 1) * CONV_DIM))
        outs[3].append(nbuf.reshape(ns, CONV_W - 1, CONV_DIM))
        groups = ns // hosted_per_step
        by_step = lambda a: a.reshape((groups, hosted_per_step) + a.shape[1:])
        cols = jnp.stack([s_rq, s_rk, s_gq, s_gk], axis=1).reshape(groups, hosted_per_step * 4 * N_HEADS, DK)
        host = lambda part: (part, 2, hosted_per_step, by_step(mq.reshape(ns, N_HEADS, HD_X)), cache_mem_k,
                             cache_mem_v, cols, by_step(s_rv), by_step(s_gv), by_step(s_gcol), state_ret,
                             state_gdn, new_states)

        h1, o_mem_a, n_ret, n_gdn, so_ret_a, so_gdn_a = _ffn(
            l, hp.reshape(bp * sp, D_MODEL), g_f1pre, f1_in, f1_out, g_f1post, tm_ffn, hosted=host(0))
        new_states = (n_ret, n_gdn)
        h1 = h1.reshape(bp, sp, D_MODEL)
        rq, rk, rv, rg, gq, gk, gv, gz, gcol, grow, tail = _proj(
            l, h1, g_mpre, w_main, w_ab_col, w_ab_row, cw, alog, dtb, alog_t, dtb_t, cos_p, sin_p, tm_proj)
        grow4 = grow.reshape(bp, SUBLANES, sp // CHUNK, CHUNK).transpose(0, 2, 1, 3)
        o_ret, s_ret, o_gdn, s_gdn = _scans(l, rq, rk, rv, rg, gq, gk, gv, gz, gcol, grow4, nw, scan_tile,
                                            scan_seqs, ret_chunk)
        h2 = _merge(l, h1, g_mpre, w_mq, w_gates, mk_b, mv_b, o_ret, o_gdn, wr, wd, wm, wo, g_mpost, tm_seq)
        hp, o_mem_b, n_ret, n_gdn, so_ret_b, so_gdn_b = _ffn(
            l, h2.reshape(bp * sp, D_MODEL), g_f2pre, f2_in, f2_out, g_f2post, tm_ffn, hosted=host(1))
        new_states = (n_ret, n_gdn)
        hp = hp.reshape(bp, sp, D_MODEL)
        outs[0].append(s_ret)
        outs[1].append(s_gdn)
        outs[2].append(tail[:, SUBLANES - (CONV_W - 1):])

        join = lambda a, b: jnp.concatenate([a, b], axis=0).reshape(ns, -1)
        o_mem, so_ret, so_gdn = join(o_mem_a, o_mem_b), join(so_ret_a, so_ret_b), join(so_gdn_a, so_gdn_b)
        hs = _sample_post(l, h1s, g_mpre, w_gates, so_ret, s_rg, so_gdn, s_gz, nw, o_mem, wr, wd, wm, wo, g_mpost,
                          g_f2pre, f2_in, f2_out, g_f2post)

    ret_p, gdn_p, conv_p, conv_s = (jnp.stack(o) for o in outs)
    return (hp, hs.reshape(ns, ss, D_MODEL), ret_p, gdn_p, conv_p, mk, mv) + new_states + (conv_s,)
```

```python
import functools
import math

import jax
import jax.numpy as jnp
from jax import lax
from jax.experimental import pallas as pl
from jax.experimental.pallas import tpu as pltpu

f32, bf16 = jnp.float32, jnp.bfloat16

D_MODEL = 1024
N_HEADS = 4
DK = 128
DV = 256
QK_W = N_HEADS * DK
V_W = N_HEADS * DV
CONV_W = 4
CONV_DIM = 2 * QK_W + V_W
HD_X = 256
N_MEM = 256
D_FF = 2816
EPS = 1e-6
ROPE_BASE = 10000.0
PAST_LEN = 16384
CHUNK = 64
LANES = 128
SUBLANES = 8
VMEM_LIMIT = 56 * 1024 * 1024
LOG_GAMMA = tuple(math.log1p(-2.0 ** (-5.0 - h)) for h in range(N_HEADS))
HIGHEST = lax.Precision.HIGHEST
NT = (((1,), (1,)), ((), ()))
TN = (((0,), (0,)), ((), ()))


def _params(*sem):
    return pltpu.CompilerParams(dimension_semantics=sem, vmem_limit_bytes=VMEM_LIMIT)


def _resident(shape, layer=None, col_block=0):
    if layer is None:
        return pl.BlockSpec(shape, lambda *_: (0,) * len(shape), pipeline_mode=pl.Buffered(1))
    index = (layer,) + (0,) * (len(shape) - 1) + (col_block,)
    return pl.BlockSpec((None,) + shape, lambda *_: index, pipeline_mode=pl.Buffered(1))


def _rms(x, gain=None):
    y = x * lax.rsqrt(jnp.mean(x * x, axis=-1, keepdims=True) + EPS)
    return y if gain is None else y * gain


def _silu(x):
    return x * jax.nn.sigmoid(x)


def _softplus(x):
    return jnp.maximum(x, 0.0) + jnp.log1p(jnp.exp(-jnp.abs(x)))


def _dot(a, b):
    return jnp.dot(a, b, preferred_element_type=f32)


def _iota(shape, dim):
    return lax.broadcasted_iota(jnp.int32, shape, dim)


def _ffn_body(x_ref, gpre_ref, wgu_ref, wo_ref, gpost_ref, o_ref, row_parts=1):
    part = x_ref.shape[0] // row_parts
    for i in range(row_parts):
        rows = slice(i * part, (i + 1) * part)
        x = x_ref[rows, :]
        xn = _rms(x, gpre_ref[...]).astype(bf16)
        h = _dot(xn, wgu_ref[...])
        act = (_silu(h[:, :D_FF]) * h[:, D_FF:]).astype(bf16)
        y = _dot(act, wo_ref[...])
        o_ref[rows, :] = x + 0.5 * _rms(y, gpost_ref[...])


def _ffn_hosting_body(x_ref, gpre_ref, wgu_ref, wo_ref, gpost_ref, mq_ref, k_ref, v_ref,
                      cols_ref, rv_ref, gv_ref, gcol_ref, sret_ref, sgdn_ref, *refs):
    o_ref, omem_ref, nret_ref, ngdn_ref, oret_ref, ogdn_ref = refs[-6:]
    _sstate_body(cols_ref, rv_ref, gv_ref, gcol_ref, sret_ref, sgdn_ref, nret_ref, ngdn_ref, oret_ref, ogdn_ref)
    _sattn_body(mq_ref, k_ref, v_ref, omem_ref)
    _ffn_body(x_ref, gpre_ref, wgu_ref, wo_ref, gpost_ref, o_ref, row_parts=2)


def _ffn(layer, x2d, gpre, wgu, wo, gpost, tm, hosted=None):
    t = x2d.shape[0]
    steps = t // tm
    row = pl.BlockSpec((tm, D_MODEL), lambda i: (i, 0))
    in_specs = [row, _resident((1, D_MODEL), layer), _resident((D_MODEL, 2 * D_FF), layer),
                _resident((D_FF, D_MODEL), layer), _resident((1, D_MODEL), layer)]
    out_row = jax.ShapeDtypeStruct((t, D_MODEL), f32)
    if hosted is None:
        return pl.pallas_call(
            _ffn_body, grid=(steps,), in_specs=in_specs, out_specs=row, out_shape=out_row,
            compiler_params=_params("parallel"), name="ffn",
        )(x2d, gpre, wgu, wo, gpost)
    part, parts, ps, mq, mem_k, mem_v, cols, rv, gv, gcol, s_ret, s_gdn, prev = hosted
    depth, n = s_ret.shape[:2]
    assert ps * parts * steps == n, (n, parts, steps, ps)
    first = part * steps
    grp = lambda *shape: pl.BlockSpec((None,) + shape, lambda i: (first + i,) + (0,) * len(shape))
    out_grp = lambda *shape: pl.BlockSpec((None,) + shape, lambda i: (i,) + (0,) * len(shape))
    mem = pl.BlockSpec((None, ps, N_MEM, N_HEADS, HD_X), lambda i: (layer, first + i, 0, 0, 0))
    st = pl.BlockSpec((None, ps, N_HEADS, DK, DV), lambda i: (layer, first + i, 0, 0, 0))
    st_sds = jax.ShapeDtypeStruct((depth, n, N_HEADS, DK, DV), f32)
    grp_sds = lambda *shape: jax.ShapeDtypeStruct((steps,) + shape, f32)
    prev = [] if prev is None else list(prev)
    prev_specs = [pl.BlockSpec(memory_space=pl.ANY)] * len(prev)
    aliases = {14 + i: 2 + i for i in range(len(prev))}
    return pl.pallas_call(
        _ffn_hosting_body, grid=(steps,),
        in_specs=in_specs + [grp(ps, N_HEADS, HD_X), mem, mem,
                             grp(ps * 4 * N_HEADS, DK),
                             grp(ps, V_W), grp(ps, V_W), grp(ps, LANES), st, st] + prev_specs,
        out_specs=[row, out_grp(ps, N_HEADS, HD_X), st, st, out_grp(ps, V_W), out_grp(ps, V_W)],
        out_shape=[out_row, grp_sds(ps, N_HEADS, HD_X), st_sds, st_sds, grp_sds(ps, V_W), grp_sds(ps, V_W)],
        input_output_aliases=aliases, compiler_params=_params("parallel"), name="ffn_hosting",
    )(x2d, gpre, wgu, wo, gpost, mq, mem_k, mem_v, cols, rv, gv, gcol, s_ret, s_gdn, *prev)


def _memkv_body(m_ref, g_ref, wk_ref, wv_ref, k_ref, v_ref, kb_ref, vb_ref):
    nb = m_ref.shape[0]
    mn = _rms(m_ref[...].reshape(nb * N_MEM, D_MODEL), g_ref[...]).astype(bf16)
    k = _dot(mn, wk_ref[...])
    v = _dot(mn, wv_ref[...])
    for b in range(nb):
        rows = slice(b * N_MEM, (b + 1) * N_MEM)
        for h in range(N_HEADS):
            k_ref[b, :, h, :] = k[rows, h * HD_X:(h + 1) * HD_X]
            v_ref[b, :, h, :] = v[rows, h * HD_X:(h + 1) * HD_X]
        kb_ref[b] = k[rows].astype(bf16)
        vb_ref[b] = v[rows].astype(bf16)


def _memkv(mem, g, wk, wv):
    b = mem.shape[0]
    depth = wk.shape[0]
    nb = 4 if b % 4 == 0 else 1
    per_layer = lambda shape: pl.BlockSpec((None,) + shape, lambda l, i: (l,) + (0,) * len(shape))
    out5 = pl.BlockSpec((None, nb, N_MEM, N_HEADS, HD_X), lambda l, i: (l, i, 0, 0, 0))
    outb = pl.BlockSpec((None, nb, N_MEM, D_MODEL), lambda l, i: (l, i, 0, 0))
    sds5 = jax.ShapeDtypeStruct((depth, b, N_MEM, N_HEADS, HD_X), f32)
    sdsb = jax.ShapeDtypeStruct((depth, b, N_MEM, D_MODEL), bf16)
    return pl.pallas_call(
        _memkv_body, grid=(depth, b // nb),
        in_specs=[pl.BlockSpec((nb, N_MEM, D_MODEL), lambda l, i: (i, 0, 0)), per_layer((1, D_MODEL)),
                  per_layer((D_MODEL, D_MODEL)), per_layer((D_MODEL, D_MODEL))],
        out_specs=[out5, out5, outb, outb], out_shape=[sds5, sds5, sdsb, sdsb],
        compiler_params=_params("parallel", "parallel"), name="memkv",
    )(mem, g, wk, wv)


_RQ, _RK, _RV, _RG, _GQKV, _GZ, _MAIN_W = 0, 512, 1024, 2048, 3072, 5120, 6144
_AB_END = _MAIN_W + 2 * N_HEADS
_MQ_END = _AB_END + D_MODEL


def _packed_mq(layer):
    return _resident((D_MODEL, D_MODEL), layer)


def _packed_gates(layer):
    return _resident((D_MODEL, 3 * D_MODEL), layer)


def _w_in_cols_body(wt_ref, o_ref):
    o_ref[...] = wt_ref[0].T.astype(bf16)


def _w_in_cols(w_in_t, first, width, blk=1024):
    depth = w_in_t.shape[0]
    assert first % SUBLANES == 0 and width % blk == 0
    return pl.pallas_call(
        _w_in_cols_body, grid=(depth, width // blk),
        in_specs=[pl.BlockSpec((pl.Element(1), pl.Element(blk), pl.Element(D_MODEL)),
                               lambda l, c: (l, pl.multiple_of(first + c * blk, SUBLANES), 0))],
        out_specs=pl.BlockSpec((None, D_MODEL, blk), lambda l, c: (l, 0, c)),
        out_shape=jax.ShapeDtypeStruct((depth, D_MODEL, width), bf16),
        compiler_params=_params("parallel", "parallel"), name="w_in_cols",
    )(w_in_t)


def _rotary(x, cos, sin_signed):
    return x * cos + pltpu.roll(x, DK // 2, axis=1) * sin_signed


def _decay_beta(ab, a_log, dt_bias, is_decay):
    g = -jnp.exp(a_log) * _softplus(ab + dt_bias)
    return jnp.where(is_decay, g, jax.nn.sigmoid(ab))


def _gdn_qkv(conv):
    a = _silu(conv)
    qs, ks = [], []
    for h in range(N_HEADS):
        q = a[:, h * DK:(h + 1) * DK]
        k = a[:, QK_W + h * DK:QK_W + (h + 1) * DK]
        qs.append(q * (lax.rsqrt(jnp.sum(q * q, axis=-1, keepdims=True) + EPS) * DK ** -0.5))
        ks.append(k * lax.rsqrt(jnp.sum(k * k, axis=-1, keepdims=True) + EPS))
    return qs, ks, a[:, 2 * QK_W:]


def _proj_body(h_ref, gpre_ref, wm_ref, wab_ref, wabt_ref, cw_ref, alog_ref, dtb_ref, alogt_ref, dtbt_ref,
               cos_ref, sin_ref,
               rq_ref, rk_ref, rv_ref, rg_ref, gq_ref, gk_ref, gv_ref, gz_ref, gcol_ref, grow_ref, tail_ref,
               carry_ref):
    j = pl.program_id(1)
    tm = h_ref.shape[0]

    @pl.when(j == 0)
    def _():
        carry_ref[...] = jnp.zeros_like(carry_ref)

    xn = _rms(h_ref[...], gpre_ref[...]).astype(bf16)
    cos, sin = cos_ref[...], sin_ref[...]

    def ret_qk(p):
        for h in range(N_HEADS):
            sl = slice(h * DK, (h + 1) * DK)
            rq_ref[:, sl] = _rotary(p[:, h * DK:(h + 1) * DK], cos, sin).astype(bf16)
            rk_ref[:, sl] = (_rotary(p[:, QK_W + h * DK:QK_W + (h + 1) * DK], cos, sin) * DK ** -0.5).astype(bf16)

    def ret_v(p):
        rv_ref[...] = p.astype(bf16)

    def ret_gate(p):
        rg_ref[...] = _silu(p).astype(bf16)

    def gdn_z(p):
        gz_ref[...] = _silu(p).astype(bf16)

    def conv_silu(x, cols):
        carry = carry_ref[:, cols]
        cw = cw_ref[:, cols]
        conv = x * cw[CONV_W - 1:CONV_W]
        conv_top = x[:SUBLANES] * cw[CONV_W - 1:CONV_W]
        first = _iota(conv_top.shape, 0)
        for s in range(1, CONV_W):
            w = cw[CONV_W - 1 - s:CONV_W - s]
            xs = pltpu.roll(x, s, axis=0)
            conv = conv + xs * w
            conv_top = conv_top + jnp.where(first < s, pltpu.roll(carry, s, axis=0), xs[:SUBLANES]) * w
        carry_ref[:, cols] = x[tm - SUBLANES:]
        tail_ref[:, cols] = x[tm - SUBLANES:]
        return (slice(None), _silu(conv)), (slice(0, SUBLANES), _silu(conv_top))

    def gdn_qk(p):
        for rows, a in conv_silu(p, slice(0, 2 * QK_W)):
            for h in range(N_HEADS):
                q = a[:, h * DK:(h + 1) * DK]
                k = a[:, QK_W + h * DK:QK_W + (h + 1) * DK]
                q = q * (lax.rsqrt(jnp.sum(q * q, axis=-1, keepdims=True) + EPS) * DK ** -0.5)
                k = k * lax.rsqrt(jnp.sum(k * k, axis=-1, keepdims=True) + EPS)
                gq_ref[rows, h * DK:(h + 1) * DK] = q.astype(bf16)
                gk_ref[rows, h * DK:(h + 1) * DK] = k.astype(bf16)

    def gdn_v(p):
        for rows, a in conv_silu(p, slice(2 * QK_W, CONV_DIM)):
            gv_ref[rows, :] = a.astype(bf16)

    groups = ((_RV, _RG, ret_v), (_RG, _GQKV, ret_gate), (_GZ, _MAIN_W, gdn_z), (_RQ, _RV, ret_qk),
              (_GQKV + 2 * QK_W, _GZ, gdn_v), (_GQKV, _GQKV + 2 * QK_W, gdn_qk))
    pending = None
    for lo, hi, epilogue in groups:
        p = _dot(xn, wm_ref[:, lo:hi])
        if pending is not None:
            pending()
        pending = functools.partial(epilogue, p)
    pending()

    ab = _dot(xn, wab_ref[...])
    gcol_ref[...] = _decay_beta(ab, alog_ref[...], dtb_ref[...], _iota(ab.shape, 1) < N_HEADS)
    abt = lax.dot_general(wabt_ref[...], xn, NT, preferred_element_type=f32)
    grow_ref[...] = _decay_beta(abt, alogt_ref[...], dtbt_ref[...], _iota(abt.shape, 0) < N_HEADS)


def _proj(layer, h3d, gpre, wm, wab, wabt, cw, alog, dtb, alogt, dtbt, cos, sin, tm):
    b, l, _ = h3d.shape
    tok = lambda w: pl.BlockSpec((None, tm, w), lambda i, j: (i, j, 0))
    tab = pl.BlockSpec((tm, DK), lambda i, j: (j, 0))
    sds = lambda w, dt=bf16: jax.ShapeDtypeStruct((b, l, w), dt)
    res = lambda *shape: _resident(shape, layer)
    return pl.pallas_call(
        _proj_body, grid=(b, l // tm),
        in_specs=[tok(D_MODEL), res(1, D_MODEL), res(D_MODEL, _MAIN_W), res(D_MODEL, LANES),
                  res(SUBLANES, D_MODEL), res(CONV_W, CONV_DIM), res(1, LANES),
                  res(1, LANES), res(SUBLANES, 1), res(SUBLANES, 1), tab, tab],
        out_specs=[tok(QK_W), tok(QK_W), tok(V_W), tok(V_W), tok(QK_W), tok(QK_W), tok(V_W), tok(V_W),
                   tok(LANES), pl.BlockSpec((None, SUBLANES, tm), lambda i, j: (i, 0, j)),
                   pl.BlockSpec((None, SUBLANES, CONV_DIM), lambda i, j: (i, 0, 0))],
        out_shape=[sds(QK_W), sds(QK_W), sds(V_W), sds(V_W), sds(QK_W), sds(QK_W), sds(V_W), sds(V_W),
                   sds(LANES, f32), jax.ShapeDtypeStruct((b, SUBLANES, l), f32),
                   jax.ShapeDtypeStruct((b, SUBLANES, CONV_DIM), f32)],
        scratch_shapes=[pltpu.VMEM((SUBLANES, CONV_DIM), f32)],
        compiler_params=_params("parallel", "arbitrary"), name="proj",
    )(h3d, gpre, wm, wab, wabt, cw, alog, dtb, alogt, dtbt, cos, sin)


def _ret_steps(q_ref, k_ref, v_ref, sg_ref, o_ref, s_ref, chunk):
    c = chunk
    row, col = _iota((c, c), 0), _iota((c, c), 1)
    dist = (row - col).astype(f32)
    pos = _iota((c, 1), 0).astype(f32)
    per_head = {}

    def head_consts(h):
        if h not in per_head:
            lg = LOG_GAMMA[h]
            per_head[h] = (jnp.exp(jnp.where(row >= col, dist * lg, -jnp.inf)), jnp.exp((pos + 1.0) * lg),
                           jnp.exp((c - 1.0 - pos) * lg))
        return per_head[h]

    def step(h, t, b):
        decay, q_scale, k_scale = head_consts(h)
        rows = slice(t * c, (t + 1) * c)
        qh = q_ref[b, rows, h * DK:(h + 1) * DK]
        kh = k_ref[b, rows, h * DK:(h + 1) * DK]
        vh = v_ref[b, rows, h * DV:(h + 1) * DV]
        attn = lax.dot_general(qh, kh, NT, preferred_element_type=f32) * decay
        q_dec = (qh.astype(f32) * q_scale).astype(bf16)
        k_dec = (kh.astype(f32) * k_scale).astype(bf16)
        s = s_ref[b, h]
        o = _dot(q_dec, s.astype(bf16)) + _dot(attn.astype(bf16), vh)
        s_ref[b, h] = s * math.exp(c * LOG_GAMMA[h]) + lax.dot_general(k_dec, vh, TN, preferred_element_type=f32)
        o_ref[b, rows, h * DV:(h + 1) * DV] = (
            sg_ref[b, rows, h * DV:(h + 1) * DV].astype(f32) * _rms(o)).astype(bf16)

    return [functools.partial(step, h, t, b) for h in range(N_HEADS) for t in range(q_ref.shape[1] // c)
            for b in range(q_ref.shape[0])]


def _gdn_steps(q_ref, k_ref, v_ref, sz_ref, gcol_ref, grow_ref, nw_ref, o_ref, s_ref):
    c = CHUNK
    hc = N_HEADS * c
    row, lane = _iota((c, hc), 0), _iota((c, hc), 1)
    lane_head, col = lane // c, lane % c
    lower, strict = row >= col, row > col
    head_mask = [(lane_head == h).astype(bf16) for h in range(N_HEADS)]
    block_diag = lambda x: jnp.concatenate([x * m for m in head_mask], axis=0)
    spread = lambda cols: functools.reduce(
        lambda acc, h: jnp.where(lane_head == h, cols[h], acc), range(N_HEADS - 1), cols[N_HEADS - 1])
    cum_rows = (_iota((c, c), 0) >= _iota((c, c), 1)).astype(f32)
    cum_cols = (_iota((c, hc), 0) <= (_iota((c, hc), 1) % c)).astype(f32)
    own_row = _iota((SUBLANES, hc), 0) == _iota((SUBLANES, hc), 1) // c
    zeros_k = jnp.zeros((c, DK), bf16)
    nw = nw_ref[...]

    items = [(b, t) for t in range(q_ref.shape[1] // c) for b in range(q_ref.shape[0])]
    chunks = range(len(items))
    tok = lambda t: slice(t * c, (t + 1) * c)

    def chunk_setup(b, t):
        rows = tok(t)
        k_heads = [k_ref[b, rows, h * DK:(h + 1) * DK] for h in range(N_HEADS)]
        k_bd = jnp.concatenate([jnp.concatenate([k_heads[h] if g == h else zeros_k for g in range(N_HEADS)], axis=1)
                                for h in range(N_HEADS)], axis=0)
        qk = lax.dot_general(jnp.concatenate([q_ref[b, rows, :], k_ref[b, rows, :]], axis=0), k_bd, NT,
                             preferred_element_type=f32)
        gcol = gcol_ref[b, rows, :]
        cum = jnp.dot(cum_rows, gcol, preferred_element_type=f32, precision=HIGHEST)
        b_cols = [cum[:, h:h + 1] for h in range(N_HEADS)]
        beta_cols = [gcol[:, N_HEADS + h:N_HEADS + h + 1] for h in range(N_HEADS)]
        b_row = jnp.sum(jnp.where(own_row, jnp.dot(grow_ref[b, t], cum_cols, preferred_element_type=f32,
                                                   precision=HIGHEST), 0.0), axis=0, keepdims=True)
        decay = jnp.exp(jnp.where(lower, spread(b_cols) - b_row, -jnp.inf))
        x = jnp.where(strict, qk[c:] * spread(beta_cols) * decay, 0.0) * -1.0
        return (qk[:c] * decay, x, jnp.concatenate(b_cols, axis=0), jnp.concatenate(beta_cols, axis=0))

    attns, xs, b_colv, betav = zip(*[chunk_setup(b, t) for b, t in items])
    ns = xs
    x_bds = [block_diag(x.astype(bf16)) for x in xs]
    for _ in range(int(math.log2(c)) - 1):
        xs = [_dot(x.astype(bf16), x_bd) for x, x_bd in zip(xs, x_bds)]
        x_bds = [block_diag(x.astype(bf16)) for x in xs]
        ns = [n + x + _dot(n.astype(bf16), x_bd) for n, x, x_bd in zip(ns, xs, x_bds)]

    def chunk_wy(i):
        b, rows = items[i][0], tok(items[i][1])
        b_col, beta = b_colv[i], betav[i]
        kst = jnp.concatenate([k_ref[b, rows, h * DK:(h + 1) * DK] for h in range(N_HEADS)], axis=0)
        qst = jnp.concatenate([q_ref[b, rows, h * DK:(h + 1) * DK] for h in range(N_HEADS)], axis=0)
        vst = jnp.concatenate([v_ref[b, rows, h * DV:(h + 1) * DV] for h in range(N_HEADS)], axis=0).astype(f32)
        n_bd = block_diag(ns[i].astype(bf16))
        kf = kst.astype(f32)
        e_b = jnp.exp(b_col)
        ru, rw = vst * beta, kf * (beta * e_b)
        u = ru + _dot(n_bd, ru.astype(bf16))
        w = (rw + _dot(n_bd, rw.astype(bf16))).astype(bf16)
        q_dec = (qst.astype(f32) * e_b).astype(bf16)
        return u, w, q_dec, kf, block_diag(attns[i].astype(bf16))

    wy = [chunk_wy(i) for i in chunks]
    for i in chunks:
        b, rows = items[i][0], tok(items[i][1])
        u, w, q_dec, kf, attn_bd = wy[i]
        b_col = b_colv[i]
        v_new, o_inter = [], []
        for h in range(N_HEADS):
            hs = slice(h * c, (h + 1) * c)
            r = _dot(jnp.concatenate([w[hs], q_dec[hs]], axis=0), s_ref[b, h].astype(bf16))
            v_new.append(u[hs] - r[:c])
            o_inter.append(r[c:])
        vb = jnp.concatenate(v_new, axis=0).astype(bf16)
        o = jnp.concatenate(o_inter, axis=0) + _dot(attn_bd, vb)
        for h in range(N_HEADS):
            hs = slice(h * c, (h + 1) * c)
            b_last = b_col[(h + 1) * c - 1:(h + 1) * c]
            k_dec = (kf[hs] * jnp.exp(b_last - b_col[hs])).astype(bf16)
            s_ref[b, h] = (s_ref[b, h] * jnp.exp(b_last)
                           + lax.dot_general(k_dec, vb[hs], TN, preferred_element_type=f32))
            o_ref[b, rows, h * DV:(h + 1) * DV] = (
                _rms(o[hs], nw) * sz_ref[b, rows, h * DV:(h + 1) * DV].astype(f32)).astype(bf16)


def _scans_body(rq_ref, rk_ref, rv_ref, rg_ref, gq_ref, gk_ref, gv_ref, gz_ref, gcol_ref, grow_ref, nw_ref,
                oret_ref, sret_ref, ogdn_ref, sgdn_ref, ret_state, gdn_state, *, ret_chunk):
    j = pl.program_id(1)

    @pl.when(j == 0)
    def _():
        ret_state[...] = jnp.zeros_like(ret_state)
        gdn_state[...] = jnp.zeros_like(gdn_state)

    _gdn_steps(gq_ref, gk_ref, gv_ref, gz_ref, gcol_ref, grow_ref, nw_ref, ogdn_ref, gdn_state)
    for step in _ret_steps(rq_ref, rk_ref, rv_ref, rg_ref, oret_ref, ret_state, ret_chunk):
        step()

    @pl.when(j == pl.num_programs(1) - 1)
    def _():
        sret_ref[...] = ret_state[...]
        sgdn_ref[...] = gdn_state[...]


def _scans(layer, rq, rk, rv, rg, gq, gk, gv, gz, gcol, grow4, nw, tile, nb, ret_chunk):
    b, l, _ = gq.shape
    assert b % nb == 0 and tile % ret_chunk == 0
    tok = lambda w: pl.BlockSpec((nb, tile, w), lambda i, j: (i, j, 0))
    st = pl.BlockSpec((nb, N_HEADS, DK, DV), lambda i, j: (i, 0, 0, 0))
    o_sds = jax.ShapeDtypeStruct((b, l, V_W), bf16)
    s_sds = jax.ShapeDtypeStruct((b, N_HEADS, DK, DV), f32)
    state = pltpu.VMEM((nb, N_HEADS, DK, DV), f32)
    return pl.pallas_call(
        functools.partial(_scans_body, ret_chunk=ret_chunk), grid=(b // nb, l // tile),
        in_specs=[tok(QK_W), tok(QK_W), tok(V_W), tok(V_W), tok(QK_W), tok(QK_W), tok(V_W), tok(V_W), tok(LANES),
                  pl.BlockSpec((nb, tile // CHUNK, SUBLANES, CHUNK), lambda i, j: (i, j, 0, 0)),
                  _resident((1, DV), layer)],
        out_specs=[tok(V_W), st, tok(V_W), st], out_shape=[o_sds, s_sds, o_sds, s_sds],
        scratch_shapes=[state, state],
        compiler_params=_params("parallel", "arbitrary"), name="scans",
    )(rq, rk, rv, rg, gq, gk, gv, gz, gcol, grow4, nw)


def _softmax(s):
    e = jnp.exp(s - jnp.max(s, axis=-1, keepdims=True))
    return e / jnp.sum(e, axis=-1, keepdims=True)


def _merge_tail(x, xn, o_ret, o_gdn, mem_proj, wg_ref, wr_ref, wd_ref, wo_ref, gpost):
    gates = jax.nn.sigmoid(_dot(xn, wg_ref[...]))
    merged = (gates[:, :D_MODEL] * _dot(o_ret, wr_ref[...])
              + gates[:, D_MODEL:2 * D_MODEL] * _dot(o_gdn, wd_ref[...])
              + gates[:, 2 * D_MODEL:] * mem_proj)
    y = _dot(merged.astype(bf16), wo_ref[...])
    return x + _rms(y, gpost)


def _merge_body(h_ref, gpre_ref, wq_ref, wg_ref, mk_ref, mv_ref, oret_ref, ogdn_ref,
                wr_ref, wd_ref, wm_ref, wo_ref, gpost_ref, o_ref):
    x = h_ref[...]
    xn = _rms(x, gpre_ref[...]).astype(bf16)
    gate = lambda i: jax.nn.sigmoid(_dot(xn, wg_ref[:, i * D_MODEL:(i + 1) * D_MODEL]))
    merged = gate(0) * _dot(oret_ref[...], wr_ref[...]) + gate(1) * _dot(ogdn_ref[...], wd_ref[...])
    mq = _dot(xn, wq_ref[...]).astype(bf16)
    mem_proj = None
    for h in range(N_HEADS):
        sl = slice(h * HD_X, (h + 1) * HD_X)
        s = lax.dot_general(mq[:, sl], mk_ref[:, sl], NT, preferred_element_type=f32) * HD_X ** -0.5
        o = _dot(_softmax(s).astype(bf16), mv_ref[:, sl])
        part = _dot(o.astype(bf16), wm_ref[sl, :])
        mem_proj = part if mem_proj is None else mem_proj + part
    merged = merged + gate(2) * mem_proj
    y = _dot(merged.astype(bf16), wo_ref[...])
    o_ref[...] = x + _rms(y, gpost_ref[...])


def _merge(layer, h3d, gpre, wq, wg, mk, mv, o_ret, o_gdn, wr, wd, wm, wo, gpost, tm):
    b, l, _ = h3d.shape
    tok = pl.BlockSpec((None, tm, D_MODEL), lambda i, j: (i, j, 0))
    mem = pl.BlockSpec((None, None, N_MEM, D_MODEL), lambda i, j: (layer, i, 0, 0))
    sq = _resident((D_MODEL, D_MODEL), layer)
    return pl.pallas_call(
        _merge_body, grid=(b, l // tm),
        in_specs=[tok, _resident((1, D_MODEL), layer), _packed_mq(layer), _packed_gates(layer), mem, mem,
                  tok, tok, sq, sq, sq, sq, _resident((1, D_MODEL), layer)],
        out_specs=tok, out_shape=jax.ShapeDtypeStruct((b, l, D_MODEL), f32),
        compiler_params=_params("parallel", "parallel"), name="merge",
    )(h3d, gpre, wq, wg, mk, mv, o_ret, o_gdn, wr, wd, wm, wo, gpost)


def _sproj_body(h_ref, gpre_ref, wm_ref, wab_ref, wq_ref, cw_ref, alog_ref, dtb_ref, cos_ref, sin_ref, buf_ref,
                rq_ref, rk_ref, rv_ref, rg_ref, gq_ref, gk_ref, gv_ref, gz_ref, gcol_ref, mq_ref, nbuf_ref):
    xn = _rms(h_ref[...], gpre_ref[...]).astype(bf16)
    p = _dot(xn, wm_ref[...])
    cos, sin = cos_ref[...], sin_ref[...]
    for h in range(N_HEADS):
        sl = slice(h * DK, (h + 1) * DK)
        rq_ref[:, sl] = _rotary(p[:, _RQ + h * DK:_RQ + (h + 1) * DK], cos, sin)
        rk_ref[:, sl] = _rotary(p[:, _RK + h * DK:_RK + (h + 1) * DK], cos, sin) * DK ** -0.5
    rv_ref[...] = p[:, _RV:_RG]
    rg_ref[...] = _silu(p[:, _RG:_GQKV])
    gz_ref[...] = _silu(p[:, _GZ:_MAIN_W])
    x = p[:, _GQKV:_GZ]
    cw = cw_ref[...]
    conv = x * cw[CONV_W - 1:CONV_W]
    for s in range(CONV_W - 1):
        prev = buf_ref[:, s * CONV_DIM:(s + 1) * CONV_DIM]
        conv = conv + prev * cw[s:s + 1]
        if s > 0:
            nbuf_ref[:, (s - 1) * CONV_DIM:s * CONV_DIM] = prev
    nbuf_ref[:, (CONV_W - 2) * CONV_DIM:] = x
    qs, ks, v = _gdn_qkv(conv)
    for h in range(N_HEADS):
        gq_ref[:, h * DK:(h + 1) * DK] = qs[h]
        gk_ref[:, h * DK:(h + 1) * DK] = ks[h]
    gv_ref[...] = v
    ab = _dot(xn, wab_ref[...])
    gcol_ref[...] = _decay_beta(ab, alog_ref[...], dtb_ref[...], _iota(ab.shape, 1) < N_HEADS)
    mq_ref[...] = _dot(xn, wq_ref[...])


def _sample_pre_body(x_ref, g1_ref, wgu_ref, wof_ref, g2_ref, *refs):
    proj_in, h1_ref, proj_out = refs[:10], refs[10], refs[11:]
    _ffn_body(x_ref, g1_ref, wgu_ref, wof_ref, g2_ref, h1_ref)
    _sproj_body(h1_ref, *proj_in, *proj_out)


def _sample_pre(layer, x2d, g1, wgu, wof, g2, gpre, wm, wab, wq, cw, alog, dtb, cos, sin, buf2d):
    n = x2d.shape[0]
    sds = lambda w: jax.ShapeDtypeStruct((n, w), f32)
    res = lambda *shape: _resident(shape, layer)
    widths = (D_MODEL, QK_W, QK_W, V_W, V_W, QK_W, QK_W, V_W, V_W, LANES, D_MODEL, (CONV_W - 1) * CONV_DIM)
    return pl.pallas_call(
        _sample_pre_body, grid=(1,),
        in_specs=[_resident((n, D_MODEL)), res(1, D_MODEL), res(D_MODEL, 2 * D_FF), res(D_FF, D_MODEL), res(1, D_MODEL),
                  res(1, D_MODEL), res(D_MODEL, _MAIN_W), res(D_MODEL, LANES),
                  _packed_mq(layer), res(CONV_W, CONV_DIM), res(1, LANES), res(1, LANES),
                  _resident((1, DK)), _resident((1, DK)), _resident((n, (CONV_W - 1) * CONV_DIM))],
        out_specs=[pl.BlockSpec((n, w), lambda i: (0, 0)) for w in widths],
        out_shape=[sds(w) for w in widths],
        compiler_params=_params("arbitrary"), name="sample_pre",
    )(x2d, g1, wgu, wof, g2, gpre, wm, wab, wq, cw, alog, dtb, cos, sin, buf2d)


def _sstate_body(cols_ref, rv_ref, gv_ref, gcol_ref, sret_ref, sgdn_ref, nret_ref, ngdn_ref, oret_ref, ogdn_ref):
    vecs = cols_ref[...].T
    col = lambda j, v, h: vecs[:, (j * 4 + v) * N_HEADS + h:(j * 4 + v) * N_HEADS + h + 1]
    for j in range(rv_ref.shape[0]):
        for h in range(N_HEADS):
            vs = slice(h * DV, (h + 1) * DV)
            q, k = col(j, 0, h), col(j, 1, h)
            s_new = sret_ref[j, h] * math.exp(LOG_GAMMA[h]) + k * rv_ref[j:j + 1, vs]
            nret_ref[j, h] = s_new
            oret_ref[j:j + 1, vs] = jnp.sum(q * s_new, axis=0, keepdims=True)
            q, k = col(j, 2, h), col(j, 3, h)
            eg = jnp.exp(gcol_ref[j:j + 1, h:h + 1])
            beta = gcol_ref[j:j + 1, N_HEADS + h:N_HEADS + h + 1]
            s = sgdn_ref[j, h]
            ks = jnp.sum(k * s, axis=0, keepdims=True)
            v_new = beta * (gv_ref[j:j + 1, vs] - eg * ks)
            s_new = s * eg + k * v_new
            ngdn_ref[j, h] = s_new
            ogdn_ref[j:j + 1, vs] = jnp.sum(q * s_new, axis=0, keepdims=True)


def _sattn_body(mq_ref, k_ref, v_ref, o_ref):
    for j in range(mq_ref.shape[0]):
        s = jnp.sum(k_ref[j] * mq_ref[j][None], axis=-1, keepdims=True) * HD_X ** -0.5
        e = jnp.exp(s - jnp.max(s, axis=0, keepdims=True))
        p = e / jnp.sum(e, axis=0, keepdims=True)
        o_ref[j] = jnp.sum(p * v_ref[j], axis=0)


def _smerge_body(h_ref, gpre_ref, wg_ref, oret_ref, rg_ref, ogdn_ref, gz_ref, nw_ref, omem_ref,
                 wr_ref, wd_ref, wm_ref, wo_ref, gpost_ref, o_ref):
    x = h_ref[...]
    xn = _rms(x, gpre_ref[...]).astype(bf16)
    nw = nw_ref[...]
    rets, gdns = [], []
    for h in range(N_HEADS):
        vs = slice(h * DV, (h + 1) * DV)
        rets.append(rg_ref[:, vs] * _rms(oret_ref[:, vs]))
        gdns.append(_rms(ogdn_ref[:, vs], nw) * gz_ref[:, vs])
    o_ret = jnp.concatenate(rets, axis=-1).astype(bf16)
    o_gdn = jnp.concatenate(gdns, axis=-1).astype(bf16)
    mem_proj = _dot(omem_ref[...].astype(bf16), wm_ref[...])
    o_ref[...] = _merge_tail(x, xn, o_ret, o_gdn, mem_proj, wg_ref, wr_ref, wd_ref, wo_ref, gpost_ref[...])


def _sample_post_body(*refs):
    merge_in, (g1_ref, wgu_ref, wof_ref, g2_ref, o_ref, h2_ref) = refs[:14], refs[14:]
    _smerge_body(*merge_in, h2_ref)
    _ffn_body(h2_ref, g1_ref, wgu_ref, wof_ref, g2_ref, o_ref)


def _sample_post(layer, h2d, gpre, wg, o_ret, rg, o_gdn, gz, nw, o_mem, wr, wd, wm, wo, gpost, g1, wgu, wof, g2):
    n = h2d.shape[0]
    act = _resident((n, D_MODEL))
    sq = _resident((D_MODEL, D_MODEL), layer)
    vec = _resident((1, D_MODEL), layer)
    return pl.pallas_call(
        _sample_post_body, grid=(1,),
        in_specs=[act, vec, _packed_gates(layer), act, act, act, act,
                  _resident((1, DV), layer), act, sq, sq, sq, sq, vec,
                  vec, _resident((D_MODEL, 2 * D_FF), layer), _resident((D_FF, D_MODEL), layer), vec],
        out_specs=pl.BlockSpec((n, D_MODEL), lambda i: (0, 0)), out_shape=jax.ShapeDtypeStruct(h2d.shape, f32),
        scratch_shapes=[pltpu.VMEM((n, D_MODEL), f32)],
        compiler_params=_params("arbitrary"), name="sample_post",
    )(h2d, gpre, wg, o_ret, rg, o_gdn, gz, nw, o_mem, wr, wd, wm, wo, gpost, g1, wgu, wof, g2)


def _rope_tables(pos):
    half = DK // 2
    inv_freq = ROPE_BASE ** (-jnp.arange(half, dtype=f32) / half)
    ang = pos.astype(f32)[:, None] * inv_freq[None, :]
    cos, sin = jnp.cos(ang), jnp.sin(ang)
    return jnp.concatenate([cos, cos], axis=-1), jnp.concatenate([-sin, sin], axis=-1)


def _row_tile(n, want):
    t = min(n, want)
    assert n % t == 0, (n, t)
    return t


def kernel(x_prompt, x_sample, mem_prompt, state_ret, state_gdn, state_conv, cache_mem_k, cache_mem_v, norm_ffn1_pre, norm_ffn1_post, ffn1_w_in, ffn1_w_out, norm_mix_pre, norm_mix_post, w_in, gdn_conv_w, gdn_a_log, gdn_dt_bias, gdn_norm, norm_mem, w_mem_k, w_mem_v, w_branch_ret, w_branch_gdn, w_branch_mem, w_out, norm_ffn2_pre, norm_ffn2_post, ffn2_w_in, ffn2_w_out):
    bp, sp, _ = x_prompt.shape
    ns, ss, _ = x_sample.shape
    depth = w_in.shape[0]
    assert ss == 1 and sp % CHUNK == 0
    tm_ffn = _row_tile(bp * sp, 512)
    tm_seq = _row_tile(sp, 512)
    tm_proj = _row_tile(sp, 256)

    cos_p, sin_p = _rope_tables(jnp.arange(sp))
    cos_s, sin_s = _rope_tables(PAST_LEN + jnp.arange(ss))
    row = lambda v: v.reshape(depth, 1, -1).astype(f32)
    pad_lanes = lambda v: jnp.pad(v.astype(f32), ((0, 0), (0, LANES - v.shape[1]))).reshape(depth, 1, LANES)
    pad_rows = lambda v: jnp.pad(v.astype(f32), ((0, 0), (0, SUBLANES - v.shape[1]))).reshape(depth, SUBLANES, 1)
    w_in_t = jnp.swapaxes(w_in, 1, 2)
    w_main = _w_in_cols(w_in_t, 0, _MAIN_W)
    w_mq = _w_in_cols(w_in_t, _AB_END, D_MODEL)
    w_gates = _w_in_cols(w_in_t, _MQ_END, 3 * D_MODEL)
    w_ab_row = w_in_t[:, _MAIN_W:_AB_END].astype(bf16)
    w_ab_col = jnp.pad(jnp.swapaxes(w_ab_row, 1, 2), ((0, 0), (0, 0), (0, LANES - 2 * N_HEADS)))
    f1_in, f1_out, f2_in, f2_out = (w.astype(bf16) for w in (ffn1_w_in, ffn1_w_out, ffn2_w_in, ffn2_w_out))
    wr, wd, wm, wo = (w.astype(bf16) for w in (w_branch_ret, w_branch_gdn, w_branch_mem, w_out))
    cw = gdn_conv_w.astype(f32)
    alog, dtb = pad_lanes(gdn_a_log), pad_lanes(gdn_dt_bias)
    alog_t, dtb_t = pad_rows(gdn_a_log), pad_rows(gdn_dt_bias)
    nw = row(gdn_norm)
    g_f1pre, g_f1post, g_f2pre, g_f2post = (row(g) for g in (norm_ffn1_pre, norm_ffn1_post, norm_ffn2_pre,
                                                              norm_ffn2_post))
    g_mpre, g_mpost = row(norm_mix_pre), row(norm_mix_post)

    mk, mv, mk_b, mv_b = _memkv(mem_prompt, row(norm_mem), w_mem_k.astype(bf16), w_mem_v.astype(bf16))
    hp = x_prompt
    hs = x_sample.reshape(ns, D_MODEL)
    outs = [[] for _ in range(4)]
    new_states = None
    scan_seqs = 2 if bp % 2 == 0 else 1
    scan_tile = _row_tile(sp, 256)
    ret_chunk = scan_tile
    hosted_per_step = ns // (2 * (bp * sp // tm_ffn))
    assert hosted_per_step * 2 * (bp * sp // tm_ffn) == ns
    for l in range(depth):
        h1s, s_rq, s_rk, s_rv, s_rg, s_gq, s_gk, s_gv, s_gz, s_gcol, mq, nbuf = _sample_pre(
            l, hs, g_f1pre, f1_in, f1_out, g_f1post, g_mpre, w_main, w_ab_col, w_mq, cw, alog, dtb, cos_s, sin_s,
            state_conv[l].reshape(ns, (CONV_W - 1) * CONV_DIM))
        outs[3].append(nbuf.reshape(ns, CONV_W - 1, CONV_DIM))
        groups = ns // hosted_per_step
        by_step = lambda a: a.reshape((groups, hosted_per_step) + a.shape[1:])
        cols = jnp.stack([s_rq, s_rk, s_gq, s_gk], axis=1).reshape(groups, hosted_per_step * 4 * N_HEADS, DK)
        host = lambda part: (part, 2, hosted_per_step, by_step(mq.reshape(ns, N_HEADS, HD_X)), cache_mem_k,
                             cache_mem_v, cols, by_step(s_rv), by_step(s_gv), by_step(s_gcol), state_ret,
                             state_gdn, new_states)

        h1, o_mem_a, n_ret, n_gdn, so_ret_a, so_gdn_a = _ffn(
            l, hp.reshape(bp * sp, D_MODEL), g_f1pre, f1_in, f1_out, g_f1post, tm_ffn, hosted=host(0))
        new_states = (n_ret, n_gdn)
        h1 = h1.reshape(bp, sp, D_MODEL)
        rq, rk, rv, rg, gq, gk, gv, gz, gcol, grow, tail = _proj(
            l, h1, g_mpre, w_main, w_ab_col, w_ab_row, cw, alog, dtb, alog_t, dtb_t, cos_p, sin_p, tm_proj)
        grow4 = grow.reshape(bp, SUBLANES, sp // CHUNK, CHUNK).transpose(0, 2, 1, 3)
        o_ret, s_ret, o_gdn, s_gdn = _scans(l, rq, rk, rv, rg, gq, gk, gv, gz, gcol, grow4, nw, scan_tile,
                                            scan_seqs, ret_chunk)
        h2 = _merge(l, h1, g_mpre, w_mq, w_gates, mk_b, mv_b, o_ret, o_gdn, wr, wd, wm, wo, g_mpost, tm_seq)
        hp, o_mem_b, n_ret, n_gdn, so_ret_b, so_gdn_b = _ffn(
            l, h2.reshape(bp * sp, D_MODEL), g_f2pre, f2_in, f2_out, g_f2post, tm_ffn, hosted=host(1))
        new_states = (n_ret, n_gdn)
        hp = hp.reshape(bp, sp, D_MODEL)
        outs[0].append(s_ret)
        outs[1].append(s_gdn)
        outs[2].append(tail[:, SUBLANES - (CONV_W - 1):])

        join = lambda a, b: jnp.concatenate([a, b], axis=0).reshape(ns, -1)
        o_mem, so_ret, so_gdn = join(o_mem_a, o_mem_b), join(so_ret_a, so_ret_b), join(so_gdn_a, so_gdn_b)
        hs = _sample_post(l, h1s, g_mpre, w_gates, so_ret, s_rg, so_gdn, s_gz, nw, o_mem, wr, wd, wm, wo, g_mpost,
                          g_f2pre, f2_in, f2_out, g_f2post)

    ret_p, gdn_p, conv_p, conv_s = (jnp.stack(o) for o in outs)
    return (hp, hs.reshape(ns, ss, D_MODEL), ret_p, gdn_p, conv_p, mk, mv) + new_states + (conv_s,)
```

```python
import functools
import math

import jax
import jax.numpy as jnp
from jax import lax
from jax.experimental import pallas as pl
from jax.experimental.pallas import tpu as pltpu

f32, bf16 = jnp.float32, jnp.bfloat16

D_MODEL = 1024
N_HEADS = 4
DK = 128
DV = 256
QK_W = N_HEADS * DK
V_W = N_HEADS * DV
CONV_W = 4
CONV_DIM = 2 * QK_W + V_W
HD_X = 256
N_MEM = 256
D_FF = 2816
EPS = 1e-6
ROPE_BASE = 10000.0
PAST_LEN = 16384
CHUNK = 64
LANES = 128
SUBLANES = 8
VMEM_LIMIT = 56 * 1024 * 1024
LOG_GAMMA = tuple(math.log1p(-2.0 ** (-5.0 - h)) for h in range(N_HEADS))
HIGHEST = lax.Precision.HIGHEST
NT = (((1,), (1,)), ((), ()))
TN = (((0,), (0,)), ((), ()))


def _params(*sem):
    return pltpu.CompilerParams(dimension_semantics=sem, vmem_limit_bytes=VMEM_LIMIT)


def _resident(shape, layer=None, col_block=0):
    if layer is None:
        return pl.BlockSpec(shape, lambda *_: (0,) * len(shape), pipeline_mode=pl.Buffered(1))
    index = (layer,) + (0,) * (len(shape) - 1) + (col_block,)
    return pl.BlockSpec((None,) + shape, lambda *_: index, pipeline_mode=pl.Buffered(1))


def _rms(x, gain=None):
    y = x * lax.rsqrt(jnp.mean(x * x, axis=-1, keepdims=True) + EPS)
    return y if gain is None else y * gain


def _silu(x):
    return x * jax.nn.sigmoid(x)


def _softplus(x):
    return jnp.maximum(x, 0.0) + jnp.log1p(jnp.exp(-jnp.abs(x)))


def _dot(a, b):
    return jnp.dot(a, b, preferred_element_type=f32)


def _iota(shape, dim):
    return lax.broadcasted_iota(jnp.int32, shape, dim)


def _ffn_body(x_ref, gpre_ref, wgu_ref, wo_ref, gpost_ref, o_ref, row_parts=1):
    part = x_ref.shape[0] // row_parts
    for i in range(row_parts):
        rows = slice(i * part, (i + 1) * part)
        x = x_ref[rows, :]
        xn = _rms(x, gpre_ref[...]).astype(bf16)
        h = _dot(xn, wgu_ref[...])
        act = (_silu(h[:, :D_FF]) * h[:, D_FF:]).astype(bf16)
        y = _dot(act, wo_ref[...])
        o_ref[rows, :] = x + 0.5 * _rms(y, gpost_ref[...])


def _ffn_hosting_body(x_ref, gpre_ref, wgu_ref, wo_ref, gpost_ref, mq_ref, k_ref, v_ref,
                      cols_ref, rv_ref, gv_ref, gcol_ref, sret_ref, sgdn_ref, *refs):
    o_ref, omem_ref, nret_ref, ngdn_ref, oret_ref, ogdn_ref = refs[-6:]
    _sstate_body(cols_ref, rv_ref, gv_ref, gcol_ref, sret_ref, sgdn_ref, nret_ref, ngdn_ref, oret_ref, ogdn_ref)
    _sattn_body(mq_ref, k_ref, v_ref, omem_ref)
    _ffn_body(x_ref, gpre_ref, wgu_ref, wo_ref, gpost_ref, o_ref, row_parts=2)


def _ffn(layer, x2d, gpre, wgu, wo, gpost, tm, hosted=None):
    t = x2d.shape[0]
    steps = t // tm
    row = pl.BlockSpec((tm, D_MODEL), lambda i: (i, 0))
    in_specs = [row, _resident((1, D_MODEL), layer), _resident((D_MODEL, 2 * D_FF), layer),
                _resident((D_FF, D_MODEL), layer), _resident((1, D_MODEL), layer)]
    out_row = jax.ShapeDtypeStruct((t, D_MODEL), f32)
    if hosted is None:
        return pl.pallas_call(
            _ffn_body, grid=(steps,), in_specs=in_specs, out_specs=row, out_shape=out_row,
            compiler_params=_params("parallel"), name="ffn",
        )(x2d, gpre, wgu, wo, gpost)
    part, parts, ps, mq, mem_k, mem_v, cols, rv, gv, gcol, s_ret, s_gdn, prev = hosted
    depth, n = s_ret.shape[:2]
    assert ps * parts * steps == n, (n, parts, steps, ps)
    first = part * steps
    grp = lambda *shape: pl.BlockSpec((None,) + shape, lambda i: (first + i,) + (0,) * len(shape))
    out_grp = lambda *shape: pl.BlockSpec((None,) + shape, lambda i: (i,) + (0,) * len(shape))
    mem = pl.BlockSpec((None, ps, N_MEM, N_HEADS, HD_X), lambda i: (layer, first + i, 0, 0, 0))
    st = pl.BlockSpec((None, ps, N_HEADS, DK, DV), lambda i: (layer, first + i, 0, 0, 0))
    st_sds = jax.ShapeDtypeStruct((depth, n, N_HEADS, DK, DV), f32)
    grp_sds = lambda *shape: jax.ShapeDtypeStruct((steps,) + shape, f32)
    prev = [] if prev is None else list(prev)
    prev_specs = [pl.BlockSpec(memory_space=pl.ANY)] * len(prev)
    aliases = {14 + i: 2 + i for i in range(len(prev))}
    return pl.pallas_call(
        _ffn_hosting_body, grid=(steps,),
        in_specs=in_specs + [grp(ps, N_HEADS, HD_X), mem, mem,
                             grp(ps * 4 * N_HEADS, DK),
                             grp(ps, V_W), grp(ps, V_W), grp(ps, LANES), st, st] + prev_specs,
        out_specs=[row, out_grp(ps, N_HEADS, HD_X), st, st, out_grp(ps, V_W), out_grp(ps, V_W)],
        out_shape=[out_row, grp_sds(ps, N_HEADS, HD_X), st_sds, st_sds, grp_sds(ps, V_W), grp_sds(ps, V_W)],
        input_output_aliases=aliases, compiler_params=_params("parallel"), name="ffn_hosting",
    )(x2d, gpre, wgu, wo, gpost, mq, mem_k, mem_v, cols, rv, gv, gcol, s_ret, s_gdn, *prev)


def _memkv_body(m_ref, g_ref, wk_ref, wv_ref, k_ref, v_ref, kb_ref, vb_ref):
    nb = m_ref.shape[0]
    mn = _rms(m_ref[...].reshape(nb * N_MEM, D_MODEL), g_ref[...]).astype(bf16)
    k = _dot(mn, wk_ref[...])
    v = _dot(mn, wv_ref[...])
    for b in range(nb):
        rows = slice(b * N_MEM, (b + 1) * N_MEM)
        for h in range(N_HEADS):
            k_ref[b, :, h, :] = k[rows, h * HD_X:(h + 1) * HD_X]
            v_ref[b, :, h, :] = v[rows, h * HD_X:(h + 1) * HD_X]
        kb_ref[b] = k[rows].astype(bf16)
        vb_ref[b] = v[rows].astype(bf16)


def _memkv(mem, g, wk, wv):
    b = mem.shape[0]
    depth = wk.shape[0]
    nb = 4 if b % 4 == 0 else 1
    per_layer = lambda shape: pl.BlockSpec((None,) + shape, lambda l, i: (l,) + (0,) * len(shape))
    out5 = pl.BlockSpec((None, nb, N_MEM, N_HEADS, HD_X), lambda l, i: (l, i, 0, 0, 0))
    outb = pl.BlockSpec((None, nb, N_MEM, D_MODEL), lambda l, i: (l, i, 0, 0))
    sds5 = jax.ShapeDtypeStruct((depth, b, N_MEM, N_HEADS, HD_X), f32)
    sdsb = jax.ShapeDtypeStruct((depth, b, N_MEM, D_MODEL), bf16)
    return pl.pallas_call(
        _memkv_body, grid=(depth, b // nb),
        in_specs=[pl.BlockSpec((nb, N_MEM, D_MODEL), lambda l, i: (i, 0, 0)), per_layer((1, D_MODEL)),
                  per_layer((D_MODEL, D_MODEL)), per_layer((D_MODEL, D_MODEL))],
        out_specs=[out5, out5, outb, outb], out_shape=[sds5, sds5, sdsb, sdsb],
        compiler_params=_params("parallel", "parallel"), name="memkv",
    )(mem, g, wk, wv)


_RQ, _RK, _RV, _RG, _GQKV, _GZ, _MAIN_W = 0, 512, 1024, 2048, 3072, 5120, 6144
_AB_END = _MAIN_W + 2 * N_HEADS
_MQ_END = _AB_END + D_MODEL


def _packed_mq(layer):
    return _resident((D_MODEL, D_MODEL), layer)


def _packed_gates(layer):
    return _resident((D_MODEL, 3 * D_MODEL), layer)


def _w_in_cols_body(wt_ref, o_ref):
    o_ref[...] = wt_ref[0].T.astype(bf16)


def _w_in_cols(w_in_t, first, width, blk=1024):
    depth = w_in_t.shape[0]
    assert first % SUBLANES == 0 and width % blk == 0
    return pl.pallas_call(
        _w_in_cols_body, grid=(depth, width // blk),
        in_specs=[pl.BlockSpec((pl.Element(1), pl.Element(blk), pl.Element(D_MODEL)),
                               lambda l, c: (l, pl.multiple_of(first + c * blk, SUBLANES), 0))],
        out_specs=pl.BlockSpec((None, D_MODEL, blk), lambda l, c: (l, 0, c)),
        out_shape=jax.ShapeDtypeStruct((depth, D_MODEL, width), bf16),
        compiler_params=_params("parallel", "parallel"), name="w_in_cols",
    )(w_in_t)


def _rotary(x, cos, sin_signed):
    return x * cos + pltpu.roll(x, DK // 2, axis=1) * sin_signed


def _decay_beta(ab, a_log, dt_bias, is_decay):
    g = -jnp.exp(a_log) * _softplus(ab + dt_bias)
    return jnp.where(is_decay, g, jax.nn.sigmoid(ab))


def _gdn_qkv(conv):
    a = _silu(conv)
    qs, ks = [], []
    for h in range(N_HEADS):
        q = a[:, h * DK:(h + 1) * DK]
        k = a[:, QK_W + h * DK:QK_W + (h + 1) * DK]
        qs.append(q * (lax.rsqrt(jnp.sum(q * q, axis=-1, keepdims=True) + EPS) * DK ** -0.5))
        ks.append(k * lax.rsqrt(jnp.sum(k * k, axis=-1, keepdims=True) + EPS))
    return qs, ks, a[:, 2 * QK_W:]


def _proj_body(h_ref, gpre_ref, wm_ref, wab_ref, wabt_ref, cw_ref, alog_ref, dtb_ref, alogt_ref, dtbt_ref,
               cos_ref, sin_ref,
               rq_ref, rk_ref, rv_ref, rg_ref, gq_ref, gk_ref, gv_ref, gz_ref, gcol_ref, grow_ref, tail_ref,
               carry_ref):
    j = pl.program_id(1)
    tm = h_ref.shape[0]

    @pl.when(j == 0)
    def _():
        carry_ref[...] = jnp.zeros_like(carry_ref)

    xn = _rms(h_ref[...], gpre_ref[...]).astype(bf16)
    cos, sin = cos_ref[...], sin_ref[...]

    def ret_qk(p):
        for h in range(N_HEADS):
            sl = slice(h * DK, (h + 1) * DK)
            rq_ref[:, sl] = _rotary(p[:, h * DK:(h + 1) * DK], cos, sin).astype(bf16)
            rk_ref[:, sl] = (_rotary(p[:, QK_W + h * DK:QK_W + (h + 1) * DK], cos, sin) * DK ** -0.5).astype(bf16)

    def ret_v(p):
        rv_ref[...] = p.astype(bf16)

    def ret_gate(p):
        rg_ref[...] = _silu(p).astype(bf16)

    def gdn_z(p):
        gz_ref[...] = _silu(p).astype(bf16)

    def conv_silu(x, cols):
        carry = carry_ref[:, cols]
        cw = cw_ref[:, cols]
        conv = x * cw[CONV_W - 1:CONV_W]
        conv_top = x[:SUBLANES] * cw[CONV_W - 1:CONV_W]
        first = _iota(conv_top.shape, 0)
        for s in range(1, CONV_W):
            w = cw[CONV_W - 1 - s:CONV_W - s]
            xs = pltpu.roll(x, s, axis=0)
            conv = conv + xs * w
            conv_top = conv_top + jnp.where(first < s, pltpu.roll(carry, s, axis=0), xs[:SUBLANES]) * w
        carry_ref[:, cols] = x[tm - SUBLANES:]
        tail_ref[:, cols] = x[tm - SUBLANES:]
        return (slice(None), _silu(conv)), (slice(0, SUBLANES), _silu(conv_top))

    def gdn_qk(p):
        for rows, a in conv_silu(p, slice(0, 2 * QK_W)):
            for h in range(N_HEADS):
                q = a[:, h * DK:(h + 1) * DK]
                k = a[:, QK_W + h * DK:QK_W + (h + 1) * DK]
                q = q * (lax.rsqrt(jnp.sum(q * q, axis=-1, keepdims=True) + EPS) * DK ** -0.5)
                k = k * lax.rsqrt(jnp.sum(k * k, axis=-1, keepdims=True) + EPS)
                gq_ref[rows, h * DK:(h + 1) * DK] = q.astype(bf16)
                gk_ref[rows, h * DK:(h + 1) * DK] = k.astype(bf16)

    def gdn_v(p):
        for rows, a in conv_silu(p, slice(2 * QK_W, CONV_DIM)):
            gv_ref[rows, :] = a.astype(bf16)

    groups = ((_RV, _RG, ret_v), (_RG, _GQKV, ret_gate), (_GZ, _MAIN_W, gdn_z), (_RQ, _RV, ret_qk),
              (_GQKV + 2 * QK_W, _GZ, gdn_v), (_GQKV, _GQKV + 2 * QK_W, gdn_qk))
    pending = None
    for lo, hi, epilogue in groups:
        p = _dot(xn, wm_ref[:, lo:hi])
        if pending is not None:
            pending()
        pending = functools.partial(epilogue, p)
    pending()

    ab = _dot(xn, wab_ref[...])
    gcol_ref[...] = _decay_beta(ab, alog_ref[...], dtb_ref[...], _iota(ab.shape, 1) < N_HEADS)
    abt = lax.dot_general(wabt_ref[...], xn, NT, preferred_element_type=f32)
    grow_ref[...] = _decay_beta(abt, alogt_ref[...], dtbt_ref[...], _iota(abt.shape, 0) < N_HEADS)


def _proj(layer, h3d, gpre, wm, wab, wabt, cw, alog, dtb, alogt, dtbt, cos, sin, tm):
    b, l, _ = h3d.shape
    tok = lambda w: pl.BlockSpec((None, tm, w), lambda i, j: (i, j, 0))
    tab = pl.BlockSpec((tm, DK), lambda i, j: (j, 0))
    sds = lambda w, dt=bf16: jax.ShapeDtypeStruct((b, l, w), dt)
    res = lambda *shape: _resident(shape, layer)
    return pl.pallas_call(
        _proj_body, grid=(b, l // tm),
        in_specs=[tok(D_MODEL), res(1, D_MODEL), res(D_MODEL, _MAIN_W), res(D_MODEL, LANES),
                  res(SUBLANES, D_MODEL), res(CONV_W, CONV_DIM), res(1, LANES),
                  res(1, LANES), res(SUBLANES, 1), res(SUBLANES, 1), tab, tab],
        out_specs=[tok(QK_W), tok(QK_W), tok(V_W), tok(V_W), tok(QK_W), tok(QK_W), tok(V_W), tok(V_W),
                   tok(LANES), pl.BlockSpec((None, SUBLANES, tm), lambda i, j: (i, 0, j)),
                   pl.BlockSpec((None, SUBLANES, CONV_DIM), lambda i, j: (i, 0, 0))],
        out_shape=[sds(QK_W), sds(QK_W), sds(V_W), sds(V_W), sds(QK_W), sds(QK_W), sds(V_W), sds(V_W),
                   sds(LANES, f32), jax.ShapeDtypeStruct((b, SUBLANES, l), f32),
                   jax.ShapeDtypeStruct((b, SUBLANES, CONV_DIM), f32)],
        scratch_shapes=[pltpu.VMEM((SUBLANES, CONV_DIM), f32)],
        compiler_params=_params("parallel", "arbitrary"), name="proj",
    )(h3d, gpre, wm, wab, wabt, cw, alog, dtb, alogt, dtbt, cos, sin)


def _ret_steps(q_ref, k_ref, v_ref, sg_ref, o_ref, s_ref, chunk):
    c = chunk
    row, col = _iota((c, c), 0), _iota((c, c), 1)
    dist = (row - col).astype(f32)
    pos = _iota((c, 1), 0).astype(f32)
    per_head = {}

    def head_consts(h):
        if h not in per_head:
            lg = LOG_GAMMA[h]
            per_head[h] = (jnp.exp(jnp.where(row >= col, dist * lg, -jnp.inf)), jnp.exp((pos + 1.0) * lg),
                           jnp.exp((c - 1.0 - pos) * lg))
        return per_head[h]

    def step(h, t, b):
        decay, q_scale, k_scale = head_consts(h)
        rows = slice(t * c, (t + 1) * c)
        qh = q_ref[b, rows, h * DK:(h + 1) * DK]
        kh = k_ref[b, rows, h * DK:(h + 1) * DK]
        vh = v_ref[b, rows, h * DV:(h + 1) * DV]
        attn = lax.dot_general(qh, kh, NT, preferred_element_type=f32) * decay
        q_dec = (qh.astype(f32) * q_scale).astype(bf16)
        k_dec = (kh.astype(f32) * k_scale).astype(bf16)
        s = s_ref[b, h]
        o = _dot(q_dec, s.astype(bf16)) + _dot(attn.astype(bf16), vh)
        s_ref[b, h] = s * math.exp(c * LOG_GAMMA[h]) + lax.dot_general(k_dec, vh, TN, preferred_element_type=f32)
        o_ref[b, rows, h * DV:(h + 1) * DV] = (
            sg_ref[b, rows, h * DV:(h + 1) * DV].astype(f32) * _rms(o)).astype(bf16)

    return [functools.partial(step, h, t, b) for h in range(N_HEADS) for t in range(q_ref.shape[1] // c)
            for b in range(q_ref.shape[0])]


def _gdn_steps(q_ref, k_ref, v_ref, sz_ref, gcol_ref, grow_ref, nw_ref, o_ref, s_ref):
    c = CHUNK
    hc = N_HEADS * c
    row, lane = _iota((c, hc), 0), _iota((c, hc), 1)
    lane_head, col = lane // c, lane % c
    lower, strict = row >= col, row > col
    head_mask = [(lane_head == h).astype(bf16) for h in range(N_HEADS)]
    block_diag = lambda x: jnp.concatenate([x * m for m in head_mask], axis=0)
    spread = lambda cols: functools.reduce(
        lambda acc, h: jnp.where(lane_head == h, cols[h], acc), range(N_HEADS - 1), cols[N_HEADS - 1])
    cum_rows = (_iota((c, c), 0) >= _iota((c, c), 1)).astype(f32)
    cum_cols = (_iota((c, hc), 0) <= (_iota((c, hc), 1) % c)).astype(f32)
    own_row = _iota((SUBLANES, hc), 0) == _iota((SUBLANES, hc), 1) // c
    zeros_k = jnp.zeros((c, DK), bf16)
    nw = nw_ref[...]

    items = [(b, t) for t in range(q_ref.shape[1] // c) for b in range(q_ref.shape[0])]
    chunks = range(len(items))
    tok = lambda t: slice(t * c, (t + 1) * c)

    def chunk_setup(b, t):
        rows = tok(t)
        k_heads = [k_ref[b, rows, h * DK:(h + 1) * DK] for h in range(N_HEADS)]
        k_bd = jnp.concatenate([jnp.concatenate([k_heads[h] if g == h else zeros_k for g in range(N_HEADS)], axis=1)
                                for h in range(N_HEADS)], axis=0)
        qk = lax.dot_general(jnp.concatenate([q_ref[b, rows, :], k_ref[b, rows, :]], axis=0), k_bd, NT,
                             preferred_element_type=f32)
        gcol = gcol_ref[b, rows, :]
        cum = jnp.dot(cum_rows, gcol, preferred_element_type=f32, precision=HIGHEST)
        b_cols = [cum[:, h:h + 1] for h in range(N_HEADS)]
        beta_cols = [gcol[:, N_HEADS + h:N_HEADS + h + 1] for h in range(N_HEADS)]
        b_row = jnp.sum(jnp.where(own_row, jnp.dot(grow_ref[b, t], cum_cols, preferred_element_type=f32,
                                                   precision=HIGHEST), 0.0), axis=0, keepdims=True)
        decay = jnp.exp(jnp.where(lower, spread(b_cols) - b_row, -jnp.inf))
        x = jnp.where(strict, qk[c:] * spread(beta_cols) * decay, 0.0) * -1.0
        return (qk[:c] * decay, x, jnp.concatenate(b_cols, axis=0), jnp.concatenate(beta_cols, axis=0))

    attns, xs, b_colv, betav = zip(*[chunk_setup(b, t) for b, t in items])
    ns = xs
    x_bds = [block_diag(x.astype(bf16)) for x in xs]
    for _ in range(int(math.log2(c)) - 1):
        xs = [_dot(x.astype(bf16), x_bd) for x, x_bd in zip(xs, x_bds)]
        x_bds = [block_diag(x.astype(bf16)) for x in xs]
        ns = [n + x + _dot(n.astype(bf16), x_bd) for n, x, x_bd in zip(ns, xs, x_bds)]

    def chunk_wy(i):
        b, rows = items[i][0], tok(items[i][1])
        b_col, beta = b_colv[i], betav[i]
        kst = jnp.concatenate([k_ref[b, rows, h * DK:(h + 1) * DK] for h in range(N_HEADS)], axis=0)
        qst = jnp.concatenate([q_ref[b, rows, h * DK:(h + 1) * DK] for h in range(N_HEADS)], axis=0)
        vst = jnp.concatenate([v_ref[b, rows, h * DV:(h + 1) * DV] for h in range(N_HEADS)], axis=0).astype(f32)
        n_bd = block_diag(ns[i].astype(bf16))
        kf = kst.astype(f32)
        e_b = jnp.exp(b_col)
        ru, rw = vst * beta, kf * (beta * e_b)
        u = ru + _dot(n_bd, ru.astype(bf16))
        w = (rw + _dot(n_bd, rw.astype(bf16))).astype(bf16)
        q_dec = (qst.astype(f32) * e_b).astype(bf16)
        return u, w, q_dec, kf, block_diag(attns[i].astype(bf16))

    wy = [chunk_wy(i) for i in chunks]
    for i in chunks:
        b, rows = items[i][0], tok(items[i][1])
        u, w, q_dec, kf, attn_bd = wy[i]
        b_col = b_colv[i]
        v_new, o_inter = [], []
        for h in range(N_HEADS):
            hs = slice(h * c, (h + 1) * c)
            r = _dot(jnp.concatenate([w[hs], q_dec[hs]], axis=0), s_ref[b, h].astype(bf16))
            v_new.append(u[hs] - r[:c])
            o_inter.append(r[c:])
        vb = jnp.concatenate(v_new, axis=0).astype(bf16)
        o = jnp.concatenate(o_inter, axis=0) + _dot(attn_bd, vb)
        for h in range(N_HEADS):
            hs = slice(h * c, (h + 1) * c)
            b_last = b_col[(h + 1) * c - 1:(h + 1) * c]
            k_dec = (kf[hs] * jnp.exp(b_last - b_col[hs])).astype(bf16)
            s_ref[b, h] = (s_ref[b, h] * jnp.exp(b_last)
                           + lax.dot_general(k_dec, vb[hs], TN, preferred_element_type=f32))
            o_ref[b, rows, h * DV:(h + 1) * DV] = (
                _rms(o[hs], nw) * sz_ref[b, rows, h * DV:(h + 1) * DV].astype(f32)).astype(bf16)


def _scans_body(rq_ref, rk_ref, rv_ref, rg_ref, gq_ref, gk_ref, gv_ref, gz_ref, gcol_ref, grow_ref, nw_ref,
                oret_ref, sret_ref, ogdn_ref, sgdn_ref, ret_state, gdn_state, *, ret_chunk):
    j = pl.program_id(1)

    @pl.when(j == 0)
    def _():
        ret_state[...] = jnp.zeros_like(ret_state)
        gdn_state[...] = jnp.zeros_like(gdn_state)

    for step in _ret_steps(rq_ref, rk_ref, rv_ref, rg_ref, oret_ref, ret_state, ret_chunk):
        step()
    _gdn_steps(gq_ref, gk_ref, gv_ref, gz_ref, gcol_ref, grow_ref, nw_ref, ogdn_ref, gdn_state)

    @pl.when(j == pl.num_programs(1) - 1)
    def _():
        sret_ref[...] = ret_state[...]
        sgdn_ref[...] = gdn_state[...]


def _scans(layer, rq, rk, rv, rg, gq, gk, gv, gz, gcol, grow4, nw, tile, nb, ret_chunk):
    b, l, _ = gq.shape
    assert b % nb == 0 and tile % ret_chunk == 0
    tok = lambda w: pl.BlockSpec((nb, tile, w), lambda i, j: (i, j, 0))
    st = pl.BlockSpec((nb, N_HEADS, DK, DV), lambda i, j: (i, 0, 0, 0))
    o_sds = jax.ShapeDtypeStruct((b, l, V_W), bf16)
    s_sds = jax.ShapeDtypeStruct((b, N_HEADS, DK, DV), f32)
    state = pltpu.VMEM((nb, N_HEADS, DK, DV), f32)
    return pl.pallas_call(
        functools.partial(_scans_body, ret_chunk=ret_chunk), grid=(b // nb, l // tile),
        in_specs=[tok(QK_W), tok(QK_W), tok(V_W), tok(V_W), tok(QK_W), tok(QK_W), tok(V_W), tok(V_W), tok(LANES),
                  pl.BlockSpec((nb, tile // CHUNK, SUBLANES, CHUNK), lambda i, j: (i, j, 0, 0)),
                  _resident((1, DV), layer)],
        out_specs=[tok(V_W), st, tok(V_W), st], out_shape=[o_sds, s_sds, o_sds, s_sds],
        scratch_shapes=[state, state],
        compiler_params=_params("parallel", "arbitrary"), name="scans",
    )(rq, rk, rv, rg, gq, gk, gv, gz, gcol, grow4, nw)


def _softmax(s):
    e = jnp.exp(s - jnp.max(s, axis=-1, keepdims=True))
    return e / jnp.sum(e, axis=-1, keepdims=True)


def _merge_tail(x, xn, o_ret, o_gdn, mem_proj, wg_ref, wr_ref, wd_ref, wo_ref, gpost):
    gates = jax.nn.sigmoid(_dot(xn, wg_ref[...]))
    merged = (gates[:, :D_MODEL] * _dot(o_ret, wr_ref[...])
              + gates[:, D_MODEL:2 * D_MODEL] * _dot(o_gdn, wd_ref[...])
              + gates[:, 2 * D_MODEL:] * mem_proj)
    y = _dot(merged.astype(bf16), wo_ref[...])
    return x + _rms(y, gpost)


def _merge_body(h_ref, gpre_ref, wq_ref, wg_ref, mk_ref, mv_ref, oret_ref, ogdn_ref,
                wr_ref, wd_ref, wm_ref, wo_ref, gpost_ref, o_ref):
    x = h_ref[...]
    xn = _rms(x, gpre_ref[...]).astype(bf16)
    gate = lambda i: jax.nn.sigmoid(_dot(xn, wg_ref[:, i * D_MODEL:(i + 1) * D_MODEL]))
    merged = gate(0) * _dot(oret_ref[...], wr_ref[...]) + gate(1) * _dot(ogdn_ref[...], wd_ref[...])
    mq = _dot(xn, wq_ref[...]).astype(bf16)
    mem_proj = None
    for h in range(N_HEADS):
        sl = slice(h * HD_X, (h + 1) * HD_X)
        s = lax.dot_general(mq[:, sl], mk_ref[:, sl], NT, preferred_element_type=f32) * HD_X ** -0.5
        o = _dot(_softmax(s).astype(bf16), mv_ref[:, sl])
        part = _dot(o.astype(bf16), wm_ref[sl, :])
        mem_proj = part if mem_proj is None else mem_proj + part
    merged = merged + gate(2) * mem_proj
    y = _dot(merged.astype(bf16), wo_ref[...])
    o_ref[...] = x + _rms(y, gpost_ref[...])


def _merge(layer, h3d, gpre, wq, wg, mk, mv, o_ret, o_gdn, wr, wd, wm, wo, gpost, tm):
    b, l, _ = h3d.shape
    tok = pl.BlockSpec((None, tm, D_MODEL), lambda i, j: (i, j, 0))
    mem = pl.BlockSpec((None, None, N_MEM, D_MODEL), lambda i, j: (layer, i, 0, 0))
    sq = _resident((D_MODEL, D_MODEL), layer)
    return pl.pallas_call(
        _merge_body, grid=(b, l // tm),
        in_specs=[tok, _resident((1, D_MODEL), layer), _packed_mq(layer), _packed_gates(layer), mem, mem,
                  tok, tok, sq, sq, sq, sq, _resident((1, D_MODEL), layer)],
        out_specs=tok, out_shape=jax.ShapeDtypeStruct((b, l, D_MODEL), f32),
        compiler_params=_params("parallel", "parallel"), name="merge",
    )(h3d, gpre, wq, wg, mk, mv, o_ret, o_gdn, wr, wd, wm, wo, gpost)


def _sproj_body(h_ref, gpre_ref, wm_ref, wab_ref, wq_ref, cw_ref, alog_ref, dtb_ref, cos_ref, sin_ref, buf_ref,
                rq_ref, rk_ref, rv_ref, rg_ref, gq_ref, gk_ref, gv_ref, gz_ref, gcol_ref, mq_ref, nbuf_ref):
    xn = _rms(h_ref[...], gpre_ref[...]).astype(bf16)
    p = _dot(xn, wm_ref[...])
    cos, sin = cos_ref[...], sin_ref[...]
    for h in range(N_HEADS):
        sl = slice(h * DK, (h + 1) * DK)
        rq_ref[:, sl] = _rotary(p[:, _RQ + h * DK:_RQ + (h + 1) * DK], cos, sin)
        rk_ref[:, sl] = _rotary(p[:, _RK + h * DK:_RK + (h + 1) * DK], cos, sin) * DK ** -0.5
    rv_ref[...] = p[:, _RV:_RG]
    rg_ref[...] = _silu(p[:, _RG:_GQKV])
    gz_ref[...] = _silu(p[:, _GZ:_MAIN_W])
    x = p[:, _GQKV:_GZ]
    cw = cw_ref[...]
    conv = x * cw[CONV_W - 1:CONV_W]
    for s in range(CONV_W - 1):
        prev = buf_ref[:, s * CONV_DIM:(s + 1) * CONV_DIM]
        conv = conv + prev * cw[s:s + 1]
        if s > 0:
            nbuf_ref[:, (s - 1) * CONV_DIM:s * CONV_DIM] = prev
    nbuf_ref[:, (CONV_W - 2) * CONV_DIM:] = x
    qs, ks, v = _gdn_qkv(conv)
    for h in range(N_HEADS):
        gq_ref[:, h * DK:(h + 1) * DK] = qs[h]
        gk_ref[:, h * DK:(h + 1) * DK] = ks[h]
    gv_ref[...] = v
    ab = _dot(xn, wab_ref[...])
    gcol_ref[...] = _decay_beta(ab, alog_ref[...], dtb_ref[...], _iota(ab.shape, 1) < N_HEADS)
    mq_ref[...] = _dot(xn, wq_ref[...])


class _StreamedWeight:
    def __init__(self, hbm_ref, layer, vmem_ref, sem):
        self._copy = pltpu.make_async_copy(hbm_ref.at[layer], vmem_ref, sem)
        self._vmem = vmem_ref
        self._arrived = False
        self._copy.start()

    def __getitem__(self, idx):
        if not self._arrived:
            self._copy.wait()
            self._arrived = True
        return self._vmem[idx]


_HBM = pl.BlockSpec(memory_space=pl.ANY)


def _sample_pre_body(x_ref, g1_ref, wgu_hbm, wof_hbm, g2_ref, gpre_ref, wm_hbm, wab_ref, wq_hbm, *refs, layer):
    proj_in, h1_ref, proj_out, (wgu_v, wof_v, wm_v, wq_v, sems) = refs[:6], refs[6], refs[7:18], refs[18:]
    wgu, wof, wm, wq = (_StreamedWeight(src, layer, dst, sems.at[i]) for i, (src, dst) in enumerate(
        ((wgu_hbm, wgu_v), (wof_hbm, wof_v), (wm_hbm, wm_v), (wq_hbm, wq_v))))
    _ffn_body(x_ref, g1_ref, wgu, wof, g2_ref, h1_ref)
    _sproj_body(h1_ref, gpre_ref, wm, wab_ref, wq, *proj_in, *proj_out)


def _sample_pre(layer, x2d, g1, wgu, wof, g2, gpre, wm, wab, wq, cw, alog, dtb, cos, sin, buf2d):
    n = x2d.shape[0]
    sds = lambda w: jax.ShapeDtypeStruct((n, w), f32)
    res = lambda *shape: _resident(shape, layer)
    widths = (D_MODEL, QK_W, QK_W, V_W, V_W, QK_W, QK_W, V_W, V_W, LANES, D_MODEL, (CONV_W - 1) * CONV_DIM)
    return pl.pallas_call(
        functools.partial(_sample_pre_body, layer=layer), grid=(1,),
        in_specs=[_resident((n, D_MODEL)), res(1, D_MODEL), _HBM, _HBM, res(1, D_MODEL),
                  res(1, D_MODEL), _HBM, res(D_MODEL, LANES),
                  _HBM, res(CONV_W, CONV_DIM), res(1, LANES), res(1, LANES),
                  _resident((1, DK)), _resident((1, DK)), _resident((n, (CONV_W - 1) * CONV_DIM))],
        out_specs=[pl.BlockSpec((n, w), lambda i: (0, 0)) for w in widths],
        out_shape=[sds(w) for w in widths],
        scratch_shapes=[pltpu.VMEM((D_MODEL, 2 * D_FF), bf16), pltpu.VMEM((D_FF, D_MODEL), bf16),
                        pltpu.VMEM((D_MODEL, _MAIN_W), bf16), pltpu.VMEM((D_MODEL, D_MODEL), bf16),
                        pltpu.SemaphoreType.DMA((4,))],
        compiler_params=_params("arbitrary"), name="sample_pre",
    )(x2d, g1, wgu, wof, g2, gpre, wm, wab, wq, cw, alog, dtb, cos, sin, buf2d)


def _sstate_body(cols_ref, rv_ref, gv_ref, gcol_ref, sret_ref, sgdn_ref, nret_ref, ngdn_ref, oret_ref, ogdn_ref):
    vecs = cols_ref[...].T
    col = lambda j, v, h: vecs[:, (j * 4 + v) * N_HEADS + h:(j * 4 + v) * N_HEADS + h + 1]
    for j in range(rv_ref.shape[0]):
        for h in range(N_HEADS):
            vs = slice(h * DV, (h + 1) * DV)
            q, k = col(j, 0, h), col(j, 1, h)
            s_new = sret_ref[j, h] * math.exp(LOG_GAMMA[h]) + k * rv_ref[j:j + 1, vs]
            nret_ref[j, h] = s_new
            oret_ref[j:j + 1, vs] = jnp.sum(q * s_new, axis=0, keepdims=True)
            q, k = col(j, 2, h), col(j, 3, h)
            eg = jnp.exp(gcol_ref[j:j + 1, h:h + 1])
            beta = gcol_ref[j:j + 1, N_HEADS + h:N_HEADS + h + 1]
            s = sgdn_ref[j, h]
            ks = jnp.sum(k * s, axis=0, keepdims=True)
            v_new = beta * (gv_ref[j:j + 1, vs] - eg * ks)
            s_new = s * eg + k * v_new
            ngdn_ref[j, h] = s_new
            ogdn_ref[j:j + 1, vs] = jnp.sum(q * s_new, axis=0, keepdims=True)


def _sattn_body(mq_ref, k_ref, v_ref, o_ref):
    for j in range(mq_ref.shape[0]):
        s = jnp.sum(k_ref[j] * mq_ref[j][None], axis=-1, keepdims=True) * HD_X ** -0.5
        e = jnp.exp(s - jnp.max(s, axis=0, keepdims=True))
        p = e / jnp.sum(e, axis=0, keepdims=True)
        o_ref[j] = jnp.sum(p * v_ref[j], axis=0)


def _smerge_body(h_ref, gpre_ref, wg_ref, oret_ref, rg_ref, ogdn_ref, gz_ref, nw_ref, omem_ref,
                 wr_ref, wd_ref, wm_ref, wo_ref, gpost_ref, o_ref):
    x = h_ref[...]
    xn = _rms(x, gpre_ref[...]).astype(bf16)
    nw = nw_ref[...]
    rets, gdns = [], []
    for h in range(N_HEADS):
        vs = slice(h * DV, (h + 1) * DV)
        rets.append(rg_ref[:, vs] * _rms(oret_ref[:, vs]))
        gdns.append(_rms(ogdn_ref[:, vs], nw) * gz_ref[:, vs])
    o_ret = jnp.concatenate(rets, axis=-1).astype(bf16)
    o_gdn = jnp.concatenate(gdns, axis=-1).astype(bf16)
    mem_proj = _dot(omem_ref[...].astype(bf16), wm_ref[...])
    o_ref[...] = _merge_tail(x, xn, o_ret, o_gdn, mem_proj, wg_ref, wr_ref, wd_ref, wo_ref, gpost_ref[...])


def _sample_post_body(h_ref, gpre_ref, wg_hbm, oret_ref, rg_ref, ogdn_ref, gz_ref, nw_ref, omem_ref,
                      wr_hbm, wd_hbm, wm_hbm, wo_hbm, gpost_ref, g1_ref, wgu_hbm, wof_hbm, g2_ref,
                      o_ref, h2_ref, *scratch, layer):
    sems = scratch[-1]
    order = (wm_hbm, wg_hbm, wr_hbm, wd_hbm, wo_hbm, wgu_hbm, wof_hbm)
    wm, wg, wr, wd, wo, wgu, wof = (_StreamedWeight(src, layer, dst, sems.at[i])
                                    for i, (src, dst) in enumerate(zip(order, scratch[:-1])))
    _smerge_body(h_ref, gpre_ref, wg, oret_ref, rg_ref, ogdn_ref, gz_ref, nw_ref, omem_ref, wr, wd, wm, wo,
                 gpost_ref, h2_ref)
    _ffn_body(h2_ref, g1_ref, wgu, wof, g2_ref, o_ref)


def _sample_post(layer, h2d, gpre, wg, o_ret, rg, o_gdn, gz, nw, o_mem, wr, wd, wm, wo, gpost, g1, wgu, wof, g2):
    n = h2d.shape[0]
    act = _resident((n, D_MODEL))
    vec = _resident((1, D_MODEL), layer)
    sq = pltpu.VMEM((D_MODEL, D_MODEL), bf16)
    return pl.pallas_call(
        functools.partial(_sample_post_body, layer=layer), grid=(1,),
        in_specs=[act, vec, _HBM, act, act, act, act, _resident((1, DV), layer), act, _HBM, _HBM, _HBM, _HBM, vec,
                  vec, _HBM, _HBM, vec],
        out_specs=pl.BlockSpec((n, D_MODEL), lambda i: (0, 0)), out_shape=jax.ShapeDtypeStruct(h2d.shape, f32),
        scratch_shapes=[pltpu.VMEM((n, D_MODEL), f32),
                        sq, pltpu.VMEM((D_MODEL, 3 * D_MODEL), bf16), sq, sq, sq,
                        pltpu.VMEM((D_MODEL, 2 * D_FF), bf16), pltpu.VMEM((D_FF, D_MODEL), bf16),
                        pltpu.SemaphoreType.DMA((7,))],
        compiler_params=_params("arbitrary"), name="sample_post",
    )(h2d, gpre, wg, o_ret, rg, o_gdn, gz, nw, o_mem, wr, wd, wm, wo, gpost, g1, wgu, wof, g2)


def _rope_tables(pos):
    half = DK // 2
    inv_freq = ROPE_BASE ** (-jnp.arange(half, dtype=f32) / half)
    ang = pos.astype(f32)[:, None] * inv_freq[None, :]
    cos, sin = jnp.cos(ang), jnp.sin(ang)
    return jnp.concatenate([cos, cos], axis=-1), jnp.concatenate([-sin, sin], axis=-1)


def _row_tile(n, want):
    t = min(n, want)
    assert n % t == 0, (n, t)
    return t


def kernel(x_prompt, x_sample, mem_prompt, state_ret, state_gdn, state_conv, cache_mem_k, cache_mem_v, norm_ffn1_pre, norm_ffn1_post, ffn1_w_in, ffn1_w_out, norm_mix_pre, norm_mix_post, w_in, gdn_conv_w, gdn_a_log, gdn_dt_bias, gdn_norm, norm_mem, w_mem_k, w_mem_v, w_branch_ret, w_branch_gdn, w_branch_mem, w_out, norm_ffn2_pre, norm_ffn2_post, ffn2_w_in, ffn2_w_out):
    bp, sp, _ = x_prompt.shape
    ns, ss, _ = x_sample.shape
    depth = w_in.shape[0]
    assert ss == 1 and sp % CHUNK == 0
    tm_ffn = _row_tile(bp * sp, 512)
    tm_seq = _row_tile(sp, 512)
    tm_proj = _row_tile(sp, 256)

    cos_p, sin_p = _rope_tables(jnp.arange(sp))
    cos_s, sin_s = _rope_tables(PAST_LEN + jnp.arange(ss))
    row = lambda v: v.reshape(depth, 1, -1).astype(f32)
    pad_lanes = lambda v: jnp.pad(v.astype(f32), ((0, 0), (0, LANES - v.shape[1]))).reshape(depth, 1, LANES)
    pad_rows = lambda v: jnp.pad(v.astype(f32), ((0, 0), (0, SUBLANES - v.shape[1]))).reshape(depth, SUBLANES, 1)
    w_in_t = jnp.swapaxes(w_in, 1, 2)
    w_main = _w_in_cols(w_in_t, 0, _MAIN_W)
    w_mq = _w_in_cols(w_in_t, _AB_END, D_MODEL)
    w_gates = _w_in_cols(w_in_t, _MQ_END, 3 * D_MODEL)
    w_ab_row = w_in_t[:, _MAIN_W:_AB_END].astype(bf16)
    w_ab_col = jnp.pad(jnp.swapaxes(w_ab_row, 1, 2), ((0, 0), (0, 0), (0, LANES - 2 * N_HEADS)))
    f1_in, f1_out, f2_in, f2_out = (w.astype(bf16) for w in (ffn1_w_in, ffn1_w_out, ffn2_w_in, ffn2_w_out))
    wr, wd, wm, wo = (w.astype(bf16) for w in (w_branch_ret, w_branch_gdn, w_branch_mem, w_out))
    cw = gdn_conv_w.astype(f32)
    alog, dtb = pad_lanes(gdn_a_log), pad_lanes(gdn_dt_bias)
    alog_t, dtb_t = pad_rows(gdn_a_log), pad_rows(gdn_dt_bias)
    nw = row(gdn_norm)
    g_f1pre, g_f1post, g_f2pre, g_f2post = (row(g) for g in (norm_ffn1_pre, norm_ffn1_post, norm_ffn2_pre,
                                                              norm_ffn2_post))
    g_mpre, g_mpost = row(norm_mix_pre), row(norm_mix_post)

    mk, mv, mk_b, mv_b = _memkv(mem_prompt, row(norm_mem), w_mem_k.astype(bf16), w_mem_v.astype(bf16))
    hp = x_prompt
    hs = x_sample.reshape(ns, D_MODEL)
    outs = [[] for _ in range(4)]
    new_states = None
    scan_seqs = 2 if bp % 2 == 0 else 1
    scan_tile = _row_tile(sp, 256)
    ret_chunk = scan_tile
    hosted_per_step = ns // (2 * (bp * sp // tm_ffn))
    assert hosted_per_step * 2 * (bp * sp // tm_ffn) == ns
    for l in range(depth):
        h1s, s_rq, s_rk, s_rv, s_rg, s_gq, s_gk, s_gv, s_gz, s_gcol, mq, nbuf = _sample_pre(
            l, hs, g_f1pre, f1_in, f1_out, g_f1post, g_mpre, w_main, w_ab_col, w_mq, cw, alog, dtb, cos_s, sin_s,
            state_conv[l].reshape(ns, (CONV_W - 1) * CONV_DIM))
        outs[3].append(nbuf.reshape(ns, CONV_W - 1, CONV_DIM))
        groups = ns // hosted_per_step
        by_step = lambda a: a.reshape((groups, hosted_per_step) + a.shape[1:])
        cols = jnp.stack([s_rq, s_rk, s_gq, s_gk], axis=1).reshape(groups, hosted_per_step * 4 * N_HEADS, DK)
        host = lambda part: (part, 2, hosted_per_step, by_step(mq.reshape(ns, N_HEADS, HD_X)), cache_mem_k,
                             cache_mem_v, cols, by_step(s_rv), by_step(s_gv), by_step(s_gcol), state_ret,
                             state_gdn, new_states)

        h1, o_mem_a, n_ret, n_gdn, so_ret_a, so_gdn_a = _ffn(
            l, hp.reshape(bp * sp, D_MODEL), g_f1pre, f1_in, f1_out, g_f1post, tm_ffn, hosted=host(0))
        new_states = (n_ret, n_gdn)
        h1 = h1.reshape(bp, sp, D_MODEL)
        rq, rk, rv, rg, gq, gk, gv, gz, gcol, grow, tail = _proj(
            l, h1, g_mpre, w_main, w_ab_col, w_ab_row, cw, alog, dtb, alog_t, dtb_t, cos_p, sin_p, tm_proj)
        grow4 = grow.reshape(bp, SUBLANES, sp // CHUNK, CHUNK).transpose(0, 2, 1, 3)
        o_ret, s_ret, o_gdn, s_gdn = _scans(l, rq, rk, rv, rg, gq, gk, gv, gz, gcol, grow4, nw, scan_tile,
                                            scan_seqs, ret_chunk)
        h2 = _merge(l, h1, g_mpre, w_mq, w_gates, mk_b, mv_b, o_ret, o_gdn, wr, wd, wm, wo, g_mpost, tm_seq)
        hp, o_mem_b, n_ret, n_gdn, so_ret_b, so_gdn_b = _ffn(
            l, h2.reshape(bp * sp, D_MODEL), g_f2pre, f2_in, f2_out, g_f2post, tm_ffn, hosted=host(1))
        new_states = (n_ret, n_gdn)
        hp = hp.reshape(bp, sp, D_MODEL)
        outs[0].append(s_ret)
        outs[1].append(s_gdn)
        outs[2].append(tail[:, SUBLANES - (CONV_W - 1):])

        join = lambda a, b: jnp.concatenate([a, b], axis=0).reshape(ns, -1)
        o_mem, so_ret, so_gdn = join(o_mem_a, o_mem_b), join(so_ret_a, so_ret_b), join(so_gdn_a, so_gdn_b)
        hs = _sample_post(l, h1s, g_mpre, w_gates, so_ret, s_rg, so_gdn, s_gz, nw, o_mem, wr, wd, wm, wo, g_mpost,
                          g_f2pre, f2_in, f2_out, g_f2post)

    ret_p, gdn_p, conv_p, conv_s = (jnp.stack(o) for o in outs)
    return (hp, hs.reshape(ns, ss, D_MODEL), ret_p, gdn_p, conv_p, mk, mv) + new_states + (conv_s,)
```

```python
import functools
import math

import jax
import jax.numpy as jnp
from jax import lax
from jax.experimental import pallas as pl
from jax.experimental.pallas import tpu as pltpu

f32, bf16 = jnp.float32, jnp.bfloat16

D_MODEL = 1024
N_HEADS = 4
DK = 128
DV = 256
QK_W = N_HEADS * DK
V_W = N_HEADS * DV
CONV_W = 4
CONV_DIM = 2 * QK_W + V_W
HD_X = 256
N_MEM = 256
D_FF = 2816
EPS = 1e-6
ROPE_BASE = 10000.0
PAST_LEN = 16384
CHUNK = 64
LANES = 128
SUBLANES = 8
VMEM_LIMIT = 56 * 1024 * 1024
LOG_GAMMA = tuple(math.log1p(-2.0 ** (-5.0 - h)) for h in range(N_HEADS))
HIGHEST = lax.Precision.HIGHEST
NT = (((1,), (1,)), ((), ()))
TN = (((0,), (0,)), ((), ()))


def _params(*sem):
    return pltpu.CompilerParams(dimension_semantics=sem, vmem_limit_bytes=VMEM_LIMIT)


def _resident(shape, layer=None, col_block=0):
    if layer is None:
        return pl.BlockSpec(shape, lambda *_: (0,) * len(shape), pipeline_mode=pl.Buffered(1))
    index = (layer,) + (0,) * (len(shape) - 1) + (col_block,)
    return pl.BlockSpec((None,) + shape, lambda *_: index, pipeline_mode=pl.Buffered(1))


def _rms(x, gain=None):
    y = x * lax.rsqrt(jnp.mean(x * x, axis=-1, keepdims=True) + EPS)
    return y if gain is None else y * gain


def _silu(x):
    return x * jax.nn.sigmoid(x)


def _softplus(x):
    return jnp.maximum(x, 0.0) + jnp.log1p(jnp.exp(-jnp.abs(x)))


def _dot(a, b):
    return jnp.dot(a, b, preferred_element_type=f32)


def _iota(shape, dim):
    return lax.broadcasted_iota(jnp.int32, shape, dim)


def _ffn_body(x_ref, gpre_ref, wgu_ref, wo_ref, gpost_ref, o_ref, row_parts=1):
    part = x_ref.shape[0] // row_parts
    for i in range(row_parts):
        rows = slice(i * part, (i + 1) * part)
        x = x_ref[rows, :]
        xn = _rms(x, gpre_ref[...]).astype(bf16)
        h = _dot(xn, wgu_ref[...])
        act = (_silu(h[:, :D_FF]) * h[:, D_FF:]).astype(bf16)
        y = _dot(act, wo_ref[...])
        o_ref[rows, :] = x + 0.5 * _rms(y, gpost_ref[...])


def _ffn_hosting_body(x_ref, gpre_ref, wgu_ref, wo_ref, gpost_ref, mq_ref, k_ref, v_ref,
                      cols_ref, rv_ref, gv_ref, gcol_ref, sret_ref, sgdn_ref, *refs):
    o_ref, omem_ref, nret_ref, ngdn_ref, oret_ref, ogdn_ref = refs[-6:]
    _sstate_body(cols_ref, rv_ref, gv_ref, gcol_ref, sret_ref, sgdn_ref, nret_ref, ngdn_ref, oret_ref, ogdn_ref)
    _sattn_body(mq_ref, k_ref, v_ref, omem_ref)
    _ffn_body(x_ref, gpre_ref, wgu_ref, wo_ref, gpost_ref, o_ref, row_parts=2)


def _ffn(layer, x2d, gpre, wgu, wo, gpost, tm, hosted=None):
    t = x2d.shape[0]
    steps = t // tm
    row = pl.BlockSpec((tm, D_MODEL), lambda i: (i, 0))
    in_specs = [row, _resident((1, D_MODEL), layer), _resident((D_MODEL, 2 * D_FF), layer),
                _resident((D_FF, D_MODEL), layer), _resident((1, D_MODEL), layer)]
    out_row = jax.ShapeDtypeStruct((t, D_MODEL), f32)
    if hosted is None:
        return pl.pallas_call(
            _ffn_body, grid=(steps,), in_specs=in_specs, out_specs=row, out_shape=out_row,
            compiler_params=_params("parallel"), name="ffn",
        )(x2d, gpre, wgu, wo, gpost)
    part, parts, ps, mq, mem_k, mem_v, cols, rv, gv, gcol, s_ret, s_gdn, prev = hosted
    depth, n = s_ret.shape[:2]
    assert ps * parts * steps == n, (n, parts, steps, ps)
    first = part * steps
    grp = lambda *shape: pl.BlockSpec((None,) + shape, lambda i: (first + i,) + (0,) * len(shape))
    out_grp = lambda *shape: pl.BlockSpec((None,) + shape, lambda i: (i,) + (0,) * len(shape))
    mem = pl.BlockSpec((None, ps, N_MEM, N_HEADS, HD_X), lambda i: (layer, first + i, 0, 0, 0))
    st = pl.BlockSpec((None, ps, N_HEADS, DK, DV), lambda i: (layer, first + i, 0, 0, 0))
    st_sds = jax.ShapeDtypeStruct((depth, n, N_HEADS, DK, DV), f32)
    grp_sds = lambda *shape: jax.ShapeDtypeStruct((steps,) + shape, f32)
    prev = [] if prev is None else list(prev)
    prev_specs = [pl.BlockSpec(memory_space=pl.ANY)] * len(prev)
    aliases = {14 + i: 2 + i for i in range(len(prev))}
    return pl.pallas_call(
        _ffn_hosting_body, grid=(steps,),
        in_specs=in_specs + [grp(ps, N_HEADS, HD_X), mem, mem,
                             grp(ps * 4 * N_HEADS, DK),
                             grp(ps, V_W), grp(ps, V_W), grp(ps, LANES), st, st] + prev_specs,
        out_specs=[row, out_grp(ps, N_HEADS, HD_X), st, st, out_grp(ps, V_W), out_grp(ps, V_W)],
        out_shape=[out_row, grp_sds(ps, N_HEADS, HD_X), st_sds, st_sds, grp_sds(ps, V_W), grp_sds(ps, V_W)],
        input_output_aliases=aliases, compiler_params=_params("parallel"), name="ffn_hosting",
    )(x2d, gpre, wgu, wo, gpost, mq, mem_k, mem_v, cols, rv, gv, gcol, s_ret, s_gdn, *prev)


def _memkv_body(m_ref, g_ref, wk_ref, wv_ref, k_ref, v_ref, kb_ref, vb_ref):
    nb = m_ref.shape[0]
    mn = _rms(m_ref[...].reshape(nb * N_MEM, D_MODEL), g_ref[...]).astype(bf16)
    k = _dot(mn, wk_ref[...])
    v = _dot(mn, wv_ref[...])
    for b in range(nb):
        rows = slice(b * N_MEM, (b + 1) * N_MEM)
        for h in range(N_HEADS):
            k_ref[b, :, h, :] = k[rows, h * HD_X:(h + 1) * HD_X]
            v_ref[b, :, h, :] = v[rows, h * HD_X:(h + 1) * HD_X]
        kb_ref[b] = k[rows].astype(bf16)
        vb_ref[b] = v[rows].astype(bf16)


def _memkv(mem, g, wk, wv):
    b = mem.shape[0]
    depth = wk.shape[0]
    nb = 4 if b % 4 == 0 else 1
    per_layer = lambda shape: pl.BlockSpec((None,) + shape, lambda l, i: (l,) + (0,) * len(shape))
    out5 = pl.BlockSpec((None, nb, N_MEM, N_HEADS, HD_X), lambda l, i: (l, i, 0, 0, 0))
    outb = pl.BlockSpec((None, nb, N_MEM, D_MODEL), lambda l, i: (l, i, 0, 0))
    sds5 = jax.ShapeDtypeStruct((depth, b, N_MEM, N_HEADS, HD_X), f32)
    sdsb = jax.ShapeDtypeStruct((depth, b, N_MEM, D_MODEL), bf16)
    return pl.pallas_call(
        _memkv_body, grid=(depth, b // nb),
        in_specs=[pl.BlockSpec((nb, N_MEM, D_MODEL), lambda l, i: (i, 0, 0)), per_layer((1, D_MODEL)),
                  per_layer((D_MODEL, D_MODEL)), per_layer((D_MODEL, D_MODEL))],
        out_specs=[out5, out5, outb, outb], out_shape=[sds5, sds5, sdsb, sdsb],
        compiler_params=_params("parallel", "parallel"), name="memkv",
    )(mem, g, wk, wv)


_RQ, _RK, _RV, _RG, _GQKV, _GZ, _MAIN_W = 0, 512, 1024, 2048, 3072, 5120, 6144
_AB_END = _MAIN_W + 2 * N_HEADS
_MQ_END = _AB_END + D_MODEL


def _packed_mq(layer):
    return _resident((D_MODEL, D_MODEL), layer)


def _packed_gates(layer):
    return _resident((D_MODEL, 3 * D_MODEL), layer)


def _w_in_cols_body(wt_ref, o_ref):
    o_ref[...] = wt_ref[0].T.astype(bf16)


def _w_in_cols(w_in_t, first, width, blk=1024):
    depth = w_in_t.shape[0]
    assert first % SUBLANES == 0 and width % blk == 0
    return pl.pallas_call(
        _w_in_cols_body, grid=(depth, width // blk),
        in_specs=[pl.BlockSpec((pl.Element(1), pl.Element(blk), pl.Element(D_MODEL)),
                               lambda l, c: (l, pl.multiple_of(first + c * blk, SUBLANES), 0))],
        out_specs=pl.BlockSpec((None, D_MODEL, blk), lambda l, c: (l, 0, c)),
        out_shape=jax.ShapeDtypeStruct((depth, D_MODEL, width), bf16),
        compiler_params=_params("parallel", "parallel"), name="w_in_cols",
    )(w_in_t)


def _rotary(x, cos, sin_signed):
    return x * cos + pltpu.roll(x, DK // 2, axis=1) * sin_signed


def _decay_beta(ab, a_log, dt_bias, is_decay):
    g = -jnp.exp(a_log) * _softplus(ab + dt_bias)
    return jnp.where(is_decay, g, jax.nn.sigmoid(ab))


def _gdn_qkv(conv):
    a = _silu(conv)
    qs, ks = [], []
    for h in range(N_HEADS):
        q = a[:, h * DK:(h + 1) * DK]
        k = a[:, QK_W + h * DK:QK_W + (h + 1) * DK]
        qs.append(q * (lax.rsqrt(jnp.sum(q * q, axis=-1, keepdims=True) + EPS) * DK ** -0.5))
        ks.append(k * lax.rsqrt(jnp.sum(k * k, axis=-1, keepdims=True) + EPS))
    return qs, ks, a[:, 2 * QK_W:]


def _proj_body(h_ref, gpre_ref, wm_ref, wab_ref, wabt_ref, cw_ref, alog_ref, dtb_ref, alogt_ref, dtbt_ref,
               cos_ref, sin_ref,
               rq_ref, rk_ref, rv_ref, rg_ref, gq_ref, gk_ref, gv_ref, gz_ref, gcol_ref, grow_ref, tail_ref,
               carry_ref):
    j = pl.program_id(1)
    tm = h_ref.shape[0]

    @pl.when(j == 0)
    def _():
        carry_ref[...] = jnp.zeros_like(carry_ref)

    xn = _rms(h_ref[...], gpre_ref[...]).astype(bf16)
    cos, sin = cos_ref[...], sin_ref[...]

    def ret_qk(p):
        for h in range(N_HEADS):
            sl = slice(h * DK, (h + 1) * DK)
            rq_ref[:, sl] = _rotary(p[:, h * DK:(h + 1) * DK], cos, sin).astype(bf16)
            rk_ref[:, sl] = (_rotary(p[:, QK_W + h * DK:QK_W + (h + 1) * DK], cos, sin) * DK ** -0.5).astype(bf16)

    def ret_v(p):
        rv_ref[...] = p.astype(bf16)

    def ret_gate(p):
        rg_ref[...] = _silu(p).astype(bf16)

    def gdn_z(p):
        gz_ref[...] = _silu(p).astype(bf16)

    def conv_silu(x, cols):
        carry = carry_ref[:, cols]
        cw = cw_ref[:, cols]
        conv = x * cw[CONV_W - 1:CONV_W]
        conv_top = x[:SUBLANES] * cw[CONV_W - 1:CONV_W]
        first = _iota(conv_top.shape, 0)
        for s in range(1, CONV_W):
            w = cw[CONV_W - 1 - s:CONV_W - s]
            xs = pltpu.roll(x, s, axis=0)
            conv = conv + xs * w
            conv_top = conv_top + jnp.where(first < s, pltpu.roll(carry, s, axis=0), xs[:SUBLANES]) * w
        carry_ref[:, cols] = x[tm - SUBLANES:]
        tail_ref[:, cols] = x[tm - SUBLANES:]
        return (slice(None), _silu(conv)), (slice(0, SUBLANES), _silu(conv_top))

    def gdn_qk(p):
        for rows, a in conv_silu(p, slice(0, 2 * QK_W)):
            for h in range(N_HEADS):
                q = a[:, h * DK:(h + 1) * DK]
                k = a[:, QK_W + h * DK:QK_W + (h + 1) * DK]
                q = q * (lax.rsqrt(jnp.sum(q * q, axis=-1, keepdims=True) + EPS) * DK ** -0.5)
                k = k * lax.rsqrt(jnp.sum(k * k, axis=-1, keepdims=True) + EPS)
                gq_ref[rows, h * DK:(h + 1) * DK] = q.astype(bf16)
                gk_ref[rows, h * DK:(h + 1) * DK] = k.astype(bf16)

    def gdn_v(p):
        for rows, a in conv_silu(p, slice(2 * QK_W, CONV_DIM)):
            gv_ref[rows, :] = a.astype(bf16)

    groups = ((_RV, _RG, ret_v), (_RG, _GQKV, ret_gate), (_GZ, _MAIN_W, gdn_z), (_RQ, _RV, ret_qk),
              (_GQKV + 2 * QK_W, _GZ, gdn_v), (_GQKV, _GQKV + 2 * QK_W, gdn_qk))
    pending = None
    for lo, hi, epilogue in groups:
        p = _dot(xn, wm_ref[:, lo:hi])
        if pending is not None:
            pending()
        pending = functools.partial(epilogue, p)
    pending()

    ab = _dot(xn, wab_ref[...])
    gcol_ref[...] = _decay_beta(ab, alog_ref[...], dtb_ref[...], _iota(ab.shape, 1) < N_HEADS)
    abt = lax.dot_general(wabt_ref[...], xn, NT, preferred_element_type=f32)
    grow_ref[...] = _decay_beta(abt, alogt_ref[...], dtbt_ref[...], _iota(abt.shape, 0) < N_HEADS)


def _proj(layer, h3d, gpre, wm, wab, wabt, cw, alog, dtb, alogt, dtbt, cos, sin, tm):
    b, l, _ = h3d.shape
    tok = lambda w: pl.BlockSpec((None, tm, w), lambda i, j: (i, j, 0))
    tab = pl.BlockSpec((tm, DK), lambda i, j: (j, 0))
    sds = lambda w, dt=bf16: jax.ShapeDtypeStruct((b, l, w), dt)
    res = lambda *shape: _resident(shape, layer)
    return pl.pallas_call(
        _proj_body, grid=(b, l // tm),
        in_specs=[tok(D_MODEL), res(1, D_MODEL), res(D_MODEL, _MAIN_W), res(D_MODEL, LANES),
                  res(SUBLANES, D_MODEL), res(CONV_W, CONV_DIM), res(1, LANES),
                  res(1, LANES), res(SUBLANES, 1), res(SUBLANES, 1), tab, tab],
        out_specs=[tok(QK_W), tok(QK_W), tok(V_W), tok(V_W), tok(QK_W), tok(QK_W), tok(V_W), tok(V_W),
                   tok(LANES), pl.BlockSpec((None, SUBLANES, tm), lambda i, j: (i, 0, j)),
                   pl.BlockSpec((None, SUBLANES, CONV_DIM), lambda i, j: (i, 0, 0))],
        out_shape=[sds(QK_W), sds(QK_W), sds(V_W), sds(V_W), sds(QK_W), sds(QK_W), sds(V_W), sds(V_W),
                   sds(LANES, f32), jax.ShapeDtypeStruct((b, SUBLANES, l), f32),
                   jax.ShapeDtypeStruct((b, SUBLANES, CONV_DIM), f32)],
        scratch_shapes=[pltpu.VMEM((SUBLANES, CONV_DIM), f32)],
        compiler_params=_params("parallel", "arbitrary"), name="proj",
    )(h3d, gpre, wm, wab, wabt, cw, alog, dtb, alogt, dtbt, cos, sin)


def _ret_steps(q_ref, k_ref, v_ref, sg_ref, o_ref, s_ref, chunk):
    c = chunk
    row, col = _iota((c, c), 0), _iota((c, c), 1)
    dist = (row - col).astype(f32)
    pos = _iota((c, 1), 0).astype(f32)
    per_head = {}

    def head_consts(h):
        if h not in per_head:
            lg = LOG_GAMMA[h]
            per_head[h] = (jnp.exp(jnp.where(row >= col, dist * lg, -jnp.inf)), jnp.exp((pos + 1.0) * lg),
                           jnp.exp((c - 1.0 - pos) * lg))
        return per_head[h]

    def step(h, t, b):
        decay, q_scale, k_scale = head_consts(h)
        rows = slice(t * c, (t + 1) * c)
        qh = q_ref[b, rows, h * DK:(h + 1) * DK]
        kh = k_ref[b, rows, h * DK:(h + 1) * DK]
        vh = v_ref[b, rows, h * DV:(h + 1) * DV]
        attn = lax.dot_general(qh, kh, NT, preferred_element_type=f32) * decay
        q_dec = (qh.astype(f32) * q_scale).astype(bf16)
        k_dec = (kh.astype(f32) * k_scale).astype(bf16)
        s = s_ref[b, h]
        o = _dot(q_dec, s.astype(bf16)) + _dot(attn.astype(bf16), vh)
        s_ref[b, h] = s * math.exp(c * LOG_GAMMA[h]) + lax.dot_general(k_dec, vh, TN, preferred_element_type=f32)
        o_ref[b, rows, h * DV:(h + 1) * DV] = (
            sg_ref[b, rows, h * DV:(h + 1) * DV].astype(f32) * _rms(o)).astype(bf16)

    return [functools.partial(step, h, t, b) for h in range(N_HEADS) for t in range(q_ref.shape[1] // c)
            for b in range(q_ref.shape[0])]


def _gdn_steps(q_ref, k_ref, v_ref, sz_ref, gcol_ref, grow_ref, nw_ref, o_ref, s_ref):
    c = CHUNK
    hc = N_HEADS * c
    row, lane = _iota((c, hc), 0), _iota((c, hc), 1)
    lane_head, col = lane // c, lane % c
    lower, strict = row >= col, row > col
    head_mask = [(lane_head == h).astype(bf16) for h in range(N_HEADS)]
    block_diag = lambda x: jnp.concatenate([x * m for m in head_mask], axis=0)
    spread = lambda cols: functools.reduce(
        lambda acc, h: jnp.where(lane_head == h, cols[h], acc), range(N_HEADS - 1), cols[N_HEADS - 1])
    cum_rows = (_iota((c, c), 0) >= _iota((c, c), 1)).astype(f32)
    cum_cols = (_iota((c, hc), 0) <= (_iota((c, hc), 1) % c)).astype(f32)
    own_row = _iota((SUBLANES, hc), 0) == _iota((SUBLANES, hc), 1) // c
    zeros_k = jnp.zeros((c, DK), bf16)
    nw = nw_ref[...]

    items = [(b, t) for t in range(q_ref.shape[1] // c) for b in range(q_ref.shape[0])]
    chunks = range(len(items))
    tok = lambda t: slice(t * c, (t + 1) * c)

    def chunk_setup(b, t):
        rows = tok(t)
        k_heads = [k_ref[b, rows, h * DK:(h + 1) * DK] for h in range(N_HEADS)]
        k_bd = jnp.concatenate([jnp.concatenate([k_heads[h] if g == h else zeros_k for g in range(N_HEADS)], axis=1)
                                for h in range(N_HEADS)], axis=0)
        qk = lax.dot_general(jnp.concatenate([q_ref[b, rows, :], k_ref[b, rows, :]], axis=0), k_bd, NT,
                             preferred_element_type=f32)
        gcol = gcol_ref[b, rows, :]
        cum = jnp.dot(cum_rows, gcol, preferred_element_type=f32, precision=HIGHEST)
        b_cols = [cum[:, h:h + 1] for h in range(N_HEADS)]
        beta_cols = [gcol[:, N_HEADS + h:N_HEADS + h + 1] for h in range(N_HEADS)]
        b_row = jnp.sum(jnp.where(own_row, jnp.dot(grow_ref[b, t], cum_cols, preferred_element_type=f32,
                                                   precision=HIGHEST), 0.0), axis=0, keepdims=True)
        decay = jnp.exp(jnp.where(lower, spread(b_cols) - b_row, -jnp.inf))
        x = jnp.where(strict, qk[c:] * spread(beta_cols) * decay, 0.0) * -1.0
        return (qk[:c] * decay, x, jnp.concatenate(b_cols, axis=0), jnp.concatenate(beta_cols, axis=0))

    attns, xs, b_colv, betav = zip(*[chunk_setup(b, t) for b, t in items])
    ns = xs
    x_bds = [block_diag(x.astype(bf16)) for x in xs]
    for _ in range(int(math.log2(c)) - 1):
        xs = [_dot(x.astype(bf16), x_bd) for x, x_bd in zip(xs, x_bds)]
        x_bds = [block_diag(x.astype(bf16)) for x in xs]
        ns = [n + x + _dot(n.astype(bf16), x_bd) for n, x, x_bd in zip(ns, xs, x_bds)]

    def chunk_wy(i):
        b, rows = items[i][0], tok(items[i][1])
        b_col, beta = b_colv[i], betav[i]
        kst = jnp.concatenate([k_ref[b, rows, h * DK:(h + 1) * DK] for h in range(N_HEADS)], axis=0)
        qst = jnp.concatenate([q_ref[b, rows, h * DK:(h + 1) * DK] for h in range(N_HEADS)], axis=0)
        vst = jnp.concatenate([v_ref[b, rows, h * DV:(h + 1) * DV] for h in range(N_HEADS)], axis=0).astype(f32)
        n_bd = block_diag(ns[i].astype(bf16))
        kf = kst.astype(f32)
        e_b = jnp.exp(b_col)
        ru, rw = vst * beta, kf * (beta * e_b)
        u = ru + _dot(n_bd, ru.astype(bf16))
        w = (rw + _dot(n_bd, rw.astype(bf16))).astype(bf16)
        q_dec = (qst.astype(f32) * e_b).astype(bf16)
        return u, w, q_dec, kf, block_diag(attns[i].astype(bf16))

    wy = [chunk_wy(i) for i in chunks]
    for i in chunks:
        b, rows = items[i][0], tok(items[i][1])
        u, w, q_dec, kf, attn_bd = wy[i]
        b_col = b_colv[i]
        v_new, o_inter = [], []
        for h in range(N_HEADS):
            hs = slice(h * c, (h + 1) * c)
            r = _dot(jnp.concatenate([w[hs], q_dec[hs]], axis=0), s_ref[b, h].astype(bf16))
            v_new.append(u[hs] - r[:c])
            o_inter.append(r[c:])
        vb = jnp.concatenate(v_new, axis=0).astype(bf16)
        o = jnp.concatenate(o_inter, axis=0) + _dot(attn_bd, vb)
        for h in range(N_HEADS):
            hs = slice(h * c, (h + 1) * c)
            b_last = b_col[(h + 1) * c - 1:(h + 1) * c]
            k_dec = (kf[hs] * jnp.exp(b_last - b_col[hs])).astype(bf16)
            s_ref[b, h] = (s_ref[b, h] * jnp.exp(b_last)
                           + lax.dot_general(k_dec, vb[hs], TN, preferred_element_type=f32))
            o_ref[b, rows, h * DV:(h + 1) * DV] = (
                _rms(o[hs], nw) * sz_ref[b, rows, h * DV:(h + 1) * DV].astype(f32)).astype(bf16)


def _scans_body(rq_ref, rk_ref, rv_ref, rg_ref, gq_ref, gk_ref, gv_ref, gz_ref, gcol_ref, grow_ref, nw_ref,
                oret_ref, sret_ref, ogdn_ref, sgdn_ref, ret_state, gdn_state, *, ret_chunk):
    j = pl.program_id(1)

    @pl.when(j == 0)
    def _():
        ret_state[...] = jnp.zeros_like(ret_state)
        gdn_state[...] = jnp.zeros_like(gdn_state)

    for step in _ret_steps(rq_ref, rk_ref, rv_ref, rg_ref, oret_ref, ret_state, ret_chunk):
        step()
    _gdn_steps(gq_ref, gk_ref, gv_ref, gz_ref, gcol_ref, grow_ref, nw_ref, ogdn_ref, gdn_state)

    @pl.when(j == pl.num_programs(1) - 1)
    def _():
        sret_ref[...] = ret_state[...]
        sgdn_ref[...] = gdn_state[...]


def _scans(layer, rq, rk, rv, rg, gq, gk, gv, gz, gcol, grow4, nw, tile, nb, ret_chunk):
    b, l, _ = gq.shape
    assert b % nb == 0 and tile % ret_chunk == 0
    tok = lambda w: pl.BlockSpec((nb, tile, w), lambda i, j: (i, j, 0))
    st = pl.BlockSpec((nb, N_HEADS, DK, DV), lambda i, j: (i, 0, 0, 0))
    o_sds = jax.ShapeDtypeStruct((b, l, V_W), bf16)
    s_sds = jax.ShapeDtypeStruct((b, N_HEADS, DK, DV), f32)
    state = pltpu.VMEM((nb, N_HEADS, DK, DV), f32)
    return pl.pallas_call(
        functools.partial(_scans_body, ret_chunk=ret_chunk), grid=(b // nb, l // tile),
        in_specs=[tok(QK_W), tok(QK_W), tok(V_W), tok(V_W), tok(QK_W), tok(QK_W), tok(V_W), tok(V_W), tok(LANES),
                  pl.BlockSpec((nb, tile // CHUNK, SUBLANES, CHUNK), lambda i, j: (i, j, 0, 0)),
                  _resident((1, DV), layer)],
        out_specs=[tok(V_W), st, tok(V_W), st], out_shape=[o_sds, s_sds, o_sds, s_sds],
        scratch_shapes=[state, state],
        compiler_params=_params("parallel", "arbitrary"), name="scans",
    )(rq, rk, rv, rg, gq, gk, gv, gz, gcol, grow4, nw)


def _softmax(s):
    e = jnp.exp(s - jnp.max(s, axis=-1, keepdims=True))
    return e / jnp.sum(e, axis=-1, keepdims=True)


def _merge_tail(x, xn, o_ret, o_gdn, mem_proj, wg_ref, wr_ref, wd_ref, wo_ref, gpost):
    gates = jax.nn.sigmoid(_dot(xn, wg_ref[...]))
    merged = (gates[:, :D_MODEL] * _dot(o_ret, wr_ref[...])
              + gates[:, D_MODEL:2 * D_MODEL] * _dot(o_gdn, wd_ref[...])
              + gates[:, 2 * D_MODEL:] * mem_proj)
    y = _dot(merged.astype(bf16), wo_ref[...])
    return x + _rms(y, gpost)


def _merge_body(h_ref, gpre_ref, wq_ref, wg_ref, mk_ref, mv_ref, oret_ref, ogdn_ref,
                wr_ref, wd_ref, wm_ref, wo_ref, gpost_ref, o_ref):
    x = h_ref[...]
    xn = _rms(x, gpre_ref[...]).astype(bf16)
    gate = lambda i: jax.nn.sigmoid(_dot(xn, wg_ref[:, i * D_MODEL:(i + 1) * D_MODEL]))
    merged = gate(0) * _dot(oret_ref[...], wr_ref[...]) + gate(1) * _dot(ogdn_ref[...], wd_ref[...])
    mq = _dot(xn, wq_ref[...]).astype(bf16)
    mem_proj = None
    for h in range(N_HEADS):
        sl = slice(h * HD_X, (h + 1) * HD_X)
        s = lax.dot_general(mq[:, sl], mk_ref[:, sl], NT, preferred_element_type=f32) * HD_X ** -0.5
        o = _dot(_softmax(s).astype(bf16), mv_ref[:, sl])
        part = _dot(o.astype(bf16), wm_ref[sl, :])
        mem_proj = part if mem_proj is None else mem_proj + part
    merged = merged + gate(2) * mem_proj
    y = _dot(merged.astype(bf16), wo_ref[...])
    o_ref[...] = x + _rms(y, gpost_ref[...])


def _merge(layer, h3d, gpre, wq, wg, mk, mv, o_ret, o_gdn, wr, wd, wm, wo, gpost, tm):
    b, l, _ = h3d.shape
    tok = pl.BlockSpec((None, tm, D_MODEL), lambda i, j: (i, j, 0))
    mem = pl.BlockSpec((None, None, N_MEM, D_MODEL), lambda i, j: (layer, i, 0, 0))
    sq = _resident((D_MODEL, D_MODEL), layer)
    return pl.pallas_call(
        _merge_body, grid=(b, l // tm),
        in_specs=[tok, _resident((1, D_MODEL), layer), _packed_mq(layer), _packed_gates(layer), mem, mem,
                  tok, tok, sq, sq, sq, sq, _resident((1, D_MODEL), layer)],
        out_specs=tok, out_shape=jax.ShapeDtypeStruct((b, l, D_MODEL), f32),
        compiler_params=_params("parallel", "parallel"), name="merge",
    )(h3d, gpre, wq, wg, mk, mv, o_ret, o_gdn, wr, wd, wm, wo, gpost)


def _sproj_body(h_ref, gpre_ref, wm_ref, wab_ref, wq_ref, cw_ref, alog_ref, dtb_ref, cos_ref, sin_ref, buf_ref,
                rq_ref, rk_ref, rv_ref, rg_ref, gq_ref, gk_ref, gv_ref, gz_ref, gcol_ref, mq_ref, nbuf_ref):
    xn = _rms(h_ref[...], gpre_ref[...]).astype(bf16)
    p = _dot(xn, wm_ref[...])
    cos, sin = cos_ref[...], sin_ref[...]
    for h in range(N_HEADS):
        sl = slice(h * DK, (h + 1) * DK)
        rq_ref[:, sl] = _rotary(p[:, _RQ + h * DK:_RQ + (h + 1) * DK], cos, sin)
        rk_ref[:, sl] = _rotary(p[:, _RK + h * DK:_RK + (h + 1) * DK], cos, sin) * DK ** -0.5
    rv_ref[...] = p[:, _RV:_RG]
    rg_ref[...] = _silu(p[:, _RG:_GQKV])
    gz_ref[...] = _silu(p[:, _GZ:_MAIN_W])
    x = p[:, _GQKV:_GZ]
    cw = cw_ref[...]
    conv = x * cw[CONV_W - 1:CONV_W]
    for s in range(CONV_W - 1):
        prev = buf_ref[:, s * CONV_DIM:(s + 1) * CONV_DIM]
        conv = conv + prev * cw[s:s + 1]
        if s > 0:
            nbuf_ref[:, (s - 1) * CONV_DIM:s * CONV_DIM] = prev
    nbuf_ref[:, (CONV_W - 2) * CONV_DIM:] = x
    qs, ks, v = _gdn_qkv(conv)
    for h in range(N_HEADS):
        gq_ref[:, h * DK:(h + 1) * DK] = qs[h]
        gk_ref[:, h * DK:(h + 1) * DK] = ks[h]
    gv_ref[...] = v
    ab = _dot(xn, wab_ref[...])
    gcol_ref[...] = _decay_beta(ab, alog_ref[...], dtb_ref[...], _iota(ab.shape, 1) < N_HEADS)
    mq_ref[...] = _dot(xn, wq_ref[...])


class _StreamedWeight:
    def __init__(self, hbm_ref, layer, vmem_ref, sem, priority=0):
        self._copy = pltpu.make_async_copy(hbm_ref.at[layer], vmem_ref, sem)
        self._vmem = vmem_ref
        self._arrived = False
        self._copy.start(priority=priority)

    def __getitem__(self, idx):
        if not self._arrived:
            self._copy.wait()
            self._arrived = True
        return self._vmem[idx]


_HBM = pl.BlockSpec(memory_space=pl.ANY)


def _sample_pre_body(x_ref, g1_ref, wgu_hbm, wof_hbm, g2_ref, gpre_ref, wm_hbm, wab_ref, wq_hbm, *refs, layer):
    proj_in, h1_ref, proj_out, (wgu_v, wof_v, wm_v, wq_v, sems) = refs[:6], refs[6], refs[7:18], refs[18:]
    wgu, wof, wm, wq = (_StreamedWeight(src, layer, dst, sems.at[i], i % 2) for i, (src, dst) in enumerate(
        ((wgu_hbm, wgu_v), (wof_hbm, wof_v), (wm_hbm, wm_v), (wq_hbm, wq_v))))
    _ffn_body(x_ref, g1_ref, wgu, wof, g2_ref, h1_ref)
    _sproj_body(h1_ref, gpre_ref, wm, wab_ref, wq, *proj_in, *proj_out)


def _sample_pre(layer, x2d, g1, wgu, wof, g2, gpre, wm, wab, wq, cw, alog, dtb, cos, sin, buf2d):
    n = x2d.shape[0]
    sds = lambda w: jax.ShapeDtypeStruct((n, w), f32)
    res = lambda *shape: _resident(shape, layer)
    widths = (D_MODEL, QK_W, QK_W, V_W, V_W, QK_W, QK_W, V_W, V_W, LANES, D_MODEL, (CONV_W - 1) * CONV_DIM)
    return pl.pallas_call(
        functools.partial(_sample_pre_body, layer=layer), grid=(1,),
        in_specs=[_resident((n, D_MODEL)), res(1, D_MODEL), _HBM, _HBM, res(1, D_MODEL),
                  res(1, D_MODEL), _HBM, res(D_MODEL, LANES),
                  _HBM, res(CONV_W, CONV_DIM), res(1, LANES), res(1, LANES),
                  _resident((1, DK)), _resident((1, DK)), _resident((n, (CONV_W - 1) * CONV_DIM))],
        out_specs=[pl.BlockSpec((n, w), lambda i: (0, 0)) for w in widths],
        out_shape=[sds(w) for w in widths],
        scratch_shapes=[pltpu.VMEM((D_MODEL, 2 * D_FF), bf16), pltpu.VMEM((D_FF, D_MODEL), bf16),
                        pltpu.VMEM((D_MODEL, _MAIN_W), bf16), pltpu.VMEM((D_MODEL, D_MODEL), bf16),
                        pltpu.SemaphoreType.DMA((4,))],
        compiler_params=_params("arbitrary"), name="sample_pre",
    )(x2d, g1, wgu, wof, g2, gpre, wm, wab, wq, cw, alog, dtb, cos, sin, buf2d)


def _sstate_body(cols_ref, rv_ref, gv_ref, gcol_ref, sret_ref, sgdn_ref, nret_ref, ngdn_ref, oret_ref, ogdn_ref):
    vecs = cols_ref[...].T
    col = lambda j, v, h: vecs[:, (j * 4 + v) * N_HEADS + h:(j * 4 + v) * N_HEADS + h + 1]
    for j in range(rv_ref.shape[0]):
        for h in range(N_HEADS):
            vs = slice(h * DV, (h + 1) * DV)
            q, k = col(j, 0, h), col(j, 1, h)
            s_new = sret_ref[j, h] * math.exp(LOG_GAMMA[h]) + k * rv_ref[j:j + 1, vs]
            nret_ref[j, h] = s_new
            oret_ref[j:j + 1, vs] = jnp.sum(q * s_new, axis=0, keepdims=True)
            q, k = col(j, 2, h), col(j, 3, h)
            eg = jnp.exp(gcol_ref[j:j + 1, h:h + 1])
            beta = gcol_ref[j:j + 1, N_HEADS + h:N_HEADS + h + 1]
            s = sgdn_ref[j, h]
            ks = jnp.sum(k * s, axis=0, keepdims=True)
            v_new = beta * (gv_ref[j:j + 1, vs] - eg * ks)
            s_new = s * eg + k * v_new
            ngdn_ref[j, h] = s_new
            ogdn_ref[j:j + 1, vs] = jnp.sum(q * s_new, axis=0, keepdims=True)


def _sattn_body(mq_ref, k_ref, v_ref, o_ref):
    for j in range(mq_ref.shape[0]):
        s = jnp.sum(k_ref[j] * mq_ref[j][None], axis=-1, keepdims=True) * HD_X ** -0.5
        e = jnp.exp(s - jnp.max(s, axis=0, keepdims=True))
        p = e / jnp.sum(e, axis=0, keepdims=True)
        o_ref[j] = jnp.sum(p * v_ref[j], axis=0)


def _smerge_body(h_ref, gpre_ref, wg_ref, oret_ref, rg_ref, ogdn_ref, gz_ref, nw_ref, omem_ref,
                 wr_ref, wd_ref, wm_ref, wo_ref, gpost_ref, o_ref):
    x = h_ref[...]
    xn = _rms(x, gpre_ref[...]).astype(bf16)
    nw = nw_ref[...]
    rets, gdns = [], []
    for h in range(N_HEADS):
        vs = slice(h * DV, (h + 1) * DV)
        rets.append(rg_ref[:, vs] * _rms(oret_ref[:, vs]))
        gdns.append(_rms(ogdn_ref[:, vs], nw) * gz_ref[:, vs])
    o_ret = jnp.concatenate(rets, axis=-1).astype(bf16)
    o_gdn = jnp.concatenate(gdns, axis=-1).astype(bf16)
    mem_proj = _dot(omem_ref[...].astype(bf16), wm_ref[...])
    o_ref[...] = _merge_tail(x, xn, o_ret, o_gdn, mem_proj, wg_ref, wr_ref, wd_ref, wo_ref, gpost_ref[...])


def _sample_post_body(h_ref, gpre_ref, wg_hbm, oret_ref, rg_ref, ogdn_ref, gz_ref, nw_ref, omem_ref,
                      wr_hbm, wd_hbm, wm_hbm, wo_hbm, gpost_ref, g1_ref, wgu_hbm, wof_hbm, g2_ref,
                      o_ref, h2_ref, *scratch, layer):
    sems = scratch[-1]
    order = (wm_hbm, wg_hbm, wr_hbm, wd_hbm, wo_hbm, wgu_hbm, wof_hbm)
    wm, wg, wr, wd, wo, wgu, wof = (_StreamedWeight(src, layer, dst, sems.at[i], i % 2)
                                    for i, (src, dst) in enumerate(zip(order, scratch[:-1])))
    _smerge_body(h_ref, gpre_ref, wg, oret_ref, rg_ref, ogdn_ref, gz_ref, nw_ref, omem_ref, wr, wd, wm, wo,
                 gpost_ref, h2_ref)
    _ffn_body(h2_ref, g1_ref, wgu, wof, g2_ref, o_ref)


def _sample_post(layer, h2d, gpre, wg, o_ret, rg, o_gdn, gz, nw, o_mem, wr, wd, wm, wo, gpost, g1, wgu, wof, g2):
    n = h2d.shape[0]
    act = _resident((n, D_MODEL))
    vec = _resident((1, D_MODEL), layer)
    sq = pltpu.VMEM((D_MODEL, D_MODEL), bf16)
    return pl.pallas_call(
        functools.partial(_sample_post_body, layer=layer), grid=(1,),
        in_specs=[act, vec, _HBM, act, act, act, act, _resident((1, DV), layer), act, _HBM, _HBM, _HBM, _HBM, vec,
                  vec, _HBM, _HBM, vec],
        out_specs=pl.BlockSpec((n, D_MODEL), lambda i: (0, 0)), out_shape=jax.ShapeDtypeStruct(h2d.shape, f32),
        scratch_shapes=[pltpu.VMEM((n, D_MODEL), f32),
                        sq, pltpu.VMEM((D_MODEL, 3 * D_MODEL), bf16), sq, sq, sq,
                        pltpu.VMEM((D_MODEL, 2 * D_FF), bf16), pltpu.VMEM((D_FF, D_MODEL), bf16),
                        pltpu.SemaphoreType.DMA((7,))],
        compiler_params=_params("arbitrary"), name="sample_post",
    )(h2d, gpre, wg, o_ret, rg, o_gdn, gz, nw, o_mem, wr, wd, wm, wo, gpost, g1, wgu, wof, g2)


def _rope_tables(pos):
    half = DK // 2
    inv_freq = ROPE_BASE ** (-jnp.arange(half, dtype=f32) / half)
    ang = pos.astype(f32)[:, None] * inv_freq[None, :]
    cos, sin = jnp.cos(ang), jnp.sin(ang)
    return jnp.concatenate([cos, cos], axis=-1), jnp.concatenate([-sin, sin], axis=-1)


def _row_tile(n, want):
    t = min(n, want)
    assert n % t == 0, (n, t)
    return t


def kernel(x_prompt, x_sample, mem_prompt, state_ret, state_gdn, state_conv, cache_mem_k, cache_mem_v, norm_ffn1_pre, norm_ffn1_post, ffn1_w_in, ffn1_w_out, norm_mix_pre, norm_mix_post, w_in, gdn_conv_w, gdn_a_log, gdn_dt_bias, gdn_norm, norm_mem, w_mem_k, w_mem_v, w_branch_ret, w_branch_gdn, w_branch_mem, w_out, norm_ffn2_pre, norm_ffn2_post, ffn2_w_in, ffn2_w_out):
    bp, sp, _ = x_prompt.shape
    ns, ss, _ = x_sample.shape
    depth = w_in.shape[0]
    assert ss == 1 and sp % CHUNK == 0
    tm_ffn = _row_tile(bp * sp, 512)
    tm_seq = _row_tile(sp, 512)
    tm_proj = _row_tile(sp, 256)

    cos_p, sin_p = _rope_tables(jnp.arange(sp))
    cos_s, sin_s = _rope_tables(PAST_LEN + jnp.arange(ss))
    row = lambda v: v.reshape(depth, 1, -1).astype(f32)
    pad_lanes = lambda v: jnp.pad(v.astype(f32), ((0, 0), (0, LANES - v.shape[1]))).reshape(depth, 1, LANES)
    pad_rows = lambda v: jnp.pad(v.astype(f32), ((0, 0), (0, SUBLANES - v.shape[1]))).reshape(depth, SUBLANES, 1)
    w_in_t = jnp.swapaxes(w_in, 1, 2)
    w_main = _w_in_cols(w_in_t, 0, _MAIN_W)
    w_mq = _w_in_cols(w_in_t, _AB_END, D_MODEL)
    w_gates = _w_in_cols(w_in_t, _MQ_END, 3 * D_MODEL)
    w_ab_row = w_in_t[:, _MAIN_W:_AB_END].astype(bf16)
    w_ab_col = jnp.pad(jnp.swapaxes(w_ab_row, 1, 2), ((0, 0), (0, 0), (0, LANES - 2 * N_HEADS)))
    f1_in, f1_out, f2_in, f2_out = (w.astype(bf16) for w in (ffn1_w_in, ffn1_w_out, ffn2_w_in, ffn2_w_out))
    wr, wd, wm, wo = (w.astype(bf16) for w in (w_branch_ret, w_branch_gdn, w_branch_mem, w_out))
    cw = gdn_conv_w.astype(f32)
    alog, dtb = pad_lanes(gdn_a_log), pad_lanes(gdn_dt_bias)
    alog_t, dtb_t = pad_rows(gdn_a_log), pad_rows(gdn_dt_bias)
    nw = row(gdn_norm)
    g_f1pre, g_f1post, g_f2pre, g_f2post = (row(g) for g in (norm_ffn1_pre, norm_ffn1_post, norm_ffn2_pre,
                                                              norm_ffn2_post))
    g_mpre, g_mpost = row(norm_mix_pre), row(norm_mix_post)

    mk, mv, mk_b, mv_b = _memkv(mem_prompt, row(norm_mem), w_mem_k.astype(bf16), w_mem_v.astype(bf16))
    hp = x_prompt
    hs = x_sample.reshape(ns, D_MODEL)
    outs = [[] for _ in range(4)]
    new_states = None
    scan_seqs = 2 if bp % 2 == 0 else 1
    scan_tile = _row_tile(sp, 256)
    ret_chunk = scan_tile
    hosted_per_step = ns // (2 * (bp * sp // tm_ffn))
    assert hosted_per_step * 2 * (bp * sp // tm_ffn) == ns
    for l in range(depth):
        h1s, s_rq, s_rk, s_rv, s_rg, s_gq, s_gk, s_gv, s_gz, s_gcol, mq, nbuf = _sample_pre(
            l, hs, g_f1pre, f1_in, f1_out, g_f1post, g_mpre, w_main, w_ab_col, w_mq, cw, alog, dtb, cos_s, sin_s,
            state_conv[l].reshape(ns, (CONV_W - 1) * CONV_DIM))
        outs[3].append(nbuf.reshape(ns, CONV_W - 1, CONV_DIM))
        groups = ns // hosted_per_step
        by_step = lambda a: a.reshape((groups, hosted_per_step) + a.shape[1:])
        cols = jnp.stack([s_rq, s_rk, s_gq, s_gk], axis=1).reshape(groups, hosted_per_step * 4 * N_HEADS, DK)
        host = lambda part: (part, 2, hosted_per_step, by_step(mq.reshape(ns, N_HEADS, HD_X)), cache_mem_k,
                             cache_mem_v, cols, by_step(s_rv), by_step(s_gv), by_step(s_gcol), state_ret,
                             state_gdn, new_states)

        h1, o_mem_a, n_ret, n_gdn, so_ret_a, so_gdn_a = _ffn(
            l, hp.reshape(bp * sp, D_MODEL), g_f1pre, f1_in, f1_out, g_f1post, tm_ffn, hosted=host(0))
        new_states = (n_ret, n_gdn)
        h1 = h1.reshape(bp, sp, D_MODEL)
        rq, rk, rv, rg, gq, gk, gv, gz, gcol, grow, tail = _proj(
            l, h1, g_mpre, w_main, w_ab_col, w_ab_row, cw, alog, dtb, alog_t, dtb_t, cos_p, sin_p, tm_proj)
        grow4 = grow.reshape(bp, SUBLANES, sp // CHUNK, CHUNK).transpose(0, 2, 1, 3)
        o_ret, s_ret, o_gdn, s_gdn = _scans(l, rq, rk, rv, rg, gq, gk, gv, gz, gcol, grow4, nw, scan_tile,
                                            scan_seqs, ret_chunk)
        h2 = _merge(l, h1, g_mpre, w_mq, w_gates, mk_b, mv_b, o_ret, o_gdn, wr, wd, wm, wo, g_mpost, tm_seq)
        hp, o_mem_b, n_ret, n_gdn, so_ret_b, so_gdn_b = _ffn(
            l, h2.reshape(bp * sp, D_MODEL), g_f2pre, f2_in, f2_out, g_f2post, tm_ffn, hosted=host(1))
        new_states = (n_ret, n_gdn)
        hp = hp.reshape(bp, sp, D_MODEL)
        outs[0].append(s_ret)
        outs[1].append(s_gdn)
        outs[2].append(tail[:, SUBLANES - (CONV_W - 1):])

        join = lambda a, b: jnp.concatenate([a, b], axis=0).reshape(ns, -1)
        o_mem, so_ret, so_gdn = join(o_mem_a, o_mem_b), join(so_ret_a, so_ret_b), join(so_gdn_a, so_gdn_b)
        hs = _sample_post(l, h1s, g_mpre, w_gates, so_ret, s_rg, so_gdn, s_gz, nw, o_mem, wr, wd, wm, wo, g_mpost,
                          g_f2pre, f2_in, f2_out, g_f2post)

    ret_p, gdn_p, conv_p, conv_s = (jnp.stack(o) for o in outs)
    return (hp, hs.reshape(ns, ss, D_MODEL), ret_p, gdn_p, conv_p, mk, mv) + new_states + (conv_s,)
```

```python
import functools
import math

import jax
import jax.numpy as jnp
from jax import lax
from jax.experimental import pallas as pl
from jax.experimental.pallas import tpu as pltpu

f32, bf16 = jnp.float32, jnp.bfloat16

D_MODEL = 1024
N_HEADS = 4
DK = 128
DV = 256
QK_W = N_HEADS * DK
V_W = N_HEADS * DV
CONV_W = 4
CONV_DIM = 2 * QK_W + V_W
HD_X = 256
N_MEM = 256
D_FF = 2816
EPS = 1e-6
ROPE_BASE = 10000.0
PAST_LEN = 16384
CHUNK = 64
LANES = 128
SUBLANES = 8
VMEM_LIMIT = 56 * 1024 * 1024
LOG_GAMMA = tuple(math.log1p(-2.0 ** (-5.0 - h)) for h in range(N_HEADS))
HIGHEST = lax.Precision.HIGHEST
NT = (((1,), (1,)), ((), ()))
TN = (((0,), (0,)), ((), ()))


def _params(*sem):
    return pltpu.CompilerParams(dimension_semantics=sem, vmem_limit_bytes=VMEM_LIMIT)


def _resident(shape, layer=None, col_block=0):
    if layer is None:
        return pl.BlockSpec(shape, lambda *_: (0,) * len(shape), pipeline_mode=pl.Buffered(1))
    index = (layer,) + (0,) * (len(shape) - 1) + (col_block,)
    return pl.BlockSpec((None,) + shape, lambda *_: index, pipeline_mode=pl.Buffered(1))


def _rms(x, gain=None):
    y = x * lax.rsqrt(jnp.mean(x * x, axis=-1, keepdims=True) + EPS)
    return y if gain is None else y * gain


def _silu(x):
    return x * jax.nn.sigmoid(x)


def _softplus(x):
    return jnp.maximum(x, 0.0) + jnp.log1p(jnp.exp(-jnp.abs(x)))


def _dot(a, b):
    return jnp.dot(a, b, preferred_element_type=f32)


def _iota(shape, dim):
    return lax.broadcasted_iota(jnp.int32, shape, dim)


def _ffn_body(x_ref, gpre_ref, wgu_ref, wo_ref, gpost_ref, o_ref, row_parts=1):
    part = x_ref.shape[0] // row_parts
    for i in range(row_parts):
        rows = slice(i * part, (i + 1) * part)
        x = x_ref[rows, :]
        xn = _rms(x, gpre_ref[...]).astype(bf16)
        h = _dot(xn, wgu_ref[...])
        act = (_silu(h[:, :D_FF]) * h[:, D_FF:]).astype(bf16)
        y = _dot(act, wo_ref[...])
        o_ref[rows, :] = x + 0.5 * _rms(y, gpost_ref[...])


def _ffn_hosting_body(x_ref, gpre_ref, wgu_ref, wo_ref, gpost_ref, mq_ref, k_ref, v_ref,
                      cols_ref, rv_ref, gv_ref, gcol_ref, sret_ref, sgdn_ref, *refs):
    o_ref, omem_ref, nret_ref, ngdn_ref, oret_ref, ogdn_ref = refs[-6:]
    _sstate_body(cols_ref, rv_ref, gv_ref, gcol_ref, sret_ref, sgdn_ref, nret_ref, ngdn_ref, oret_ref, ogdn_ref)
    _sattn_body(mq_ref, k_ref, v_ref, omem_ref)
    _ffn_body(x_ref, gpre_ref, wgu_ref, wo_ref, gpost_ref, o_ref, row_parts=2)


def _ffn(layer, x2d, gpre, wgu, wo, gpost, tm, hosted=None):
    t = x2d.shape[0]
    steps = t // tm
    row = pl.BlockSpec((tm, D_MODEL), lambda i: (i, 0))
    in_specs = [row, _resident((1, D_MODEL), layer), _resident((D_MODEL, 2 * D_FF), layer),
                _resident((D_FF, D_MODEL), layer), _resident((1, D_MODEL), layer)]
    out_row = jax.ShapeDtypeStruct((t, D_MODEL), f32)
    if hosted is None:
        return pl.pallas_call(
            _ffn_body, grid=(steps,), in_specs=in_specs, out_specs=row, out_shape=out_row,
            compiler_params=_params("parallel"), name="ffn",
        )(x2d, gpre, wgu, wo, gpost)
    part, parts, ps, mq, mem_k, mem_v, cols, rv, gv, gcol, s_ret, s_gdn, prev = hosted
    depth, n = s_ret.shape[:2]
    assert ps * parts * steps == n, (n, parts, steps, ps)
    first = part * steps
    grp = lambda *shape: pl.BlockSpec((None,) + shape, lambda i: (first + i,) + (0,) * len(shape))
    out_grp = lambda *shape: pl.BlockSpec((None,) + shape, lambda i: (i,) + (0,) * len(shape))
    mem = pl.BlockSpec((None, ps, N_MEM, N_HEADS, HD_X), lambda i: (layer, first + i, 0, 0, 0))
    st = pl.BlockSpec((None, ps, N_HEADS, DK, DV), lambda i: (layer, first + i, 0, 0, 0))
    st_sds = jax.ShapeDtypeStruct((depth, n, N_HEADS, DK, DV), f32)
    grp_sds = lambda *shape: jax.ShapeDtypeStruct((steps,) + shape, f32)
    prev = [] if prev is None else list(prev)
    prev_specs = [pl.BlockSpec(memory_space=pl.ANY)] * len(prev)
    aliases = {14 + i: 2 + i for i in range(len(prev))}
    return pl.pallas_call(
        _ffn_hosting_body, grid=(steps,),
        in_specs=in_specs + [grp(ps, N_HEADS, HD_X), mem, mem,
                             grp(ps * 4 * N_HEADS, DK),
                             grp(ps, V_W), grp(ps, V_W), grp(ps, LANES), st, st] + prev_specs,
        out_specs=[row, out_grp(ps, N_HEADS, HD_X), st, st, out_grp(ps, V_W), out_grp(ps, V_W)],
        out_shape=[out_row, grp_sds(ps, N_HEADS, HD_X), st_sds, st_sds, grp_sds(ps, V_W), grp_sds(ps, V_W)],
        input_output_aliases=aliases, compiler_params=_params("parallel"), name="ffn_hosting",
    )(x2d, gpre, wgu, wo, gpost, mq, mem_k, mem_v, cols, rv, gv, gcol, s_ret, s_gdn, *prev)


def _memkv_body(m_ref, g_ref, wk_ref, wv_ref, k_ref, v_ref, kb_ref, vb_ref):
    nb = m_ref.shape[0]
    mn = _rms(m_ref[...].reshape(nb * N_MEM, D_MODEL), g_ref[...]).astype(bf16)
    k = _dot(mn, wk_ref[...])
    v = _dot(mn, wv_ref[...])
    for b in range(nb):
        rows = slice(b * N_MEM, (b + 1) * N_MEM)
        for h in range(N_HEADS):
            k_ref[b, :, h, :] = k[rows, h * HD_X:(h + 1) * HD_X]
            v_ref[b, :, h, :] = v[rows, h * HD_X:(h + 1) * HD_X]
        kb_ref[b] = k[rows].astype(bf16)
        vb_ref[b] = v[rows].astype(bf16)


def _memkv(mem, g, wk, wv):
    b = mem.shape[0]
    depth = wk.shape[0]
    nb = 4 if b % 4 == 0 else 1
    per_layer = lambda shape: pl.BlockSpec((None,) + shape, lambda l, i: (l,) + (0,) * len(shape))
    out5 = pl.BlockSpec((None, nb, N_MEM, N_HEADS, HD_X), lambda l, i: (l, i, 0, 0, 0))
    outb = pl.BlockSpec((None, nb, N_MEM, D_MODEL), lambda l, i: (l, i, 0, 0))
    sds5 = jax.ShapeDtypeStruct((depth, b, N_MEM, N_HEADS, HD_X), f32)
    sdsb = jax.ShapeDtypeStruct((depth, b, N_MEM, D_MODEL), bf16)
    return pl.pallas_call(
        _memkv_body, grid=(depth, b // nb),
        in_specs=[pl.BlockSpec((nb, N_MEM, D_MODEL), lambda l, i: (i, 0, 0)), per_layer((1, D_MODEL)),
                  per_layer((D_MODEL, D_MODEL)), per_layer((D_MODEL, D_MODEL))],
        out_specs=[out5, out5, outb, outb], out_shape=[sds5, sds5, sdsb, sdsb],
        compiler_params=_params("parallel", "parallel"), name="memkv",
    )(mem, g, wk, wv)


_RQ, _RK, _RV, _RG, _GQKV, _GZ, _MAIN_W = 0, 512, 1024, 2048, 3072, 5120, 6144
_AB_END = _MAIN_W + 2 * N_HEADS
_MQ_END = _AB_END + D_MODEL


def _packed_mq(layer):
    return _resident((D_MODEL, D_MODEL), layer)


def _packed_gates(layer):
    return _resident((D_MODEL, 3 * D_MODEL), layer)


def _w_in_cols_body(wt_ref, o_ref):
    o_ref[...] = wt_ref[0].T.astype(bf16)


def _w_in_cols(w_in_t, first, width, blk=1024):
    depth = w_in_t.shape[0]
    assert first % SUBLANES == 0 and width % blk == 0
    return pl.pallas_call(
        _w_in_cols_body, grid=(depth, width // blk),
        in_specs=[pl.BlockSpec((pl.Element(1), pl.Element(blk), pl.Element(D_MODEL)),
                               lambda l, c: (l, pl.multiple_of(first + c * blk, SUBLANES), 0))],
        out_specs=pl.BlockSpec((None, D_MODEL, blk), lambda l, c: (l, 0, c)),
        out_shape=jax.ShapeDtypeStruct((depth, D_MODEL, width), bf16),
        compiler_params=_params("parallel", "parallel"), name="w_in_cols",
    )(w_in_t)


def _rotary(x, cos, sin_signed):
    return x * cos + pltpu.roll(x, DK // 2, axis=1) * sin_signed


def _decay_beta(ab, a_log, dt_bias, is_decay):
    g = -jnp.exp(a_log) * _softplus(ab + dt_bias)
    return jnp.where(is_decay, g, jax.nn.sigmoid(ab))


def _gdn_qkv(conv):
    a = _silu(conv)
    qs, ks = [], []
    for h in range(N_HEADS):
        q = a[:, h * DK:(h + 1) * DK]
        k = a[:, QK_W + h * DK:QK_W + (h + 1) * DK]
        qs.append(q * (lax.rsqrt(jnp.sum(q * q, axis=-1, keepdims=True) + EPS) * DK ** -0.5))
        ks.append(k * lax.rsqrt(jnp.sum(k * k, axis=-1, keepdims=True) + EPS))
    return qs, ks, a[:, 2 * QK_W:]


def _proj_body(h_ref, gpre_ref, wm_ref, wab_ref, wabt_ref, cw_ref, alog_ref, dtb_ref, alogt_ref, dtbt_ref,
               cos_ref, sin_ref,
               rq_ref, rk_ref, rv_ref, rg_ref, gq_ref, gk_ref, gv_ref, gz_ref, gcol_ref, grow_ref, tail_ref,
               carry_ref):
    j = pl.program_id(1)
    tm = h_ref.shape[0]

    @pl.when(j == 0)
    def _():
        carry_ref[...] = jnp.zeros_like(carry_ref)

    xn = _rms(h_ref[...], gpre_ref[...]).astype(bf16)
    cos, sin = cos_ref[...], sin_ref[...]

    def ret_qk(p):
        for h in range(N_HEADS):
            sl = slice(h * DK, (h + 1) * DK)
            rq_ref[:, sl] = _rotary(p[:, h * DK:(h + 1) * DK], cos, sin).astype(bf16)
            rk_ref[:, sl] = (_rotary(p[:, QK_W + h * DK:QK_W + (h + 1) * DK], cos, sin) * DK ** -0.5).astype(bf16)

    def ret_v(p):
        rv_ref[...] = p.astype(bf16)

    def ret_gate(p):
        rg_ref[...] = _silu(p).astype(bf16)

    def gdn_z(p):
        gz_ref[...] = _silu(p).astype(bf16)

    def conv_silu(x, cols):
        carry = carry_ref[:, cols]
        cw = cw_ref[:, cols]
        conv = x * cw[CONV_W - 1:CONV_W]
        conv_top = x[:SUBLANES] * cw[CONV_W - 1:CONV_W]
        first = _iota(conv_top.shape, 0)
        for s in range(1, CONV_W):
            w = cw[CONV_W - 1 - s:CONV_W - s]
            xs = pltpu.roll(x, s, axis=0)
            conv = conv + xs * w
            conv_top = conv_top + jnp.where(first < s, pltpu.roll(carry, s, axis=0), xs[:SUBLANES]) * w
        carry_ref[:, cols] = x[tm - SUBLANES:]
        tail_ref[:, cols] = x[tm - SUBLANES:]
        return (slice(None), _silu(conv)), (slice(0, SUBLANES), _silu(conv_top))

    def gdn_qk(p):
        for rows, a in conv_silu(p, slice(0, 2 * QK_W)):
            for h in range(N_HEADS):
                q = a[:, h * DK:(h + 1) * DK]
                k = a[:, QK_W + h * DK:QK_W + (h + 1) * DK]
                q = q * (lax.rsqrt(jnp.sum(q * q, axis=-1, keepdims=True) + EPS) * DK ** -0.5)
                k = k * lax.rsqrt(jnp.sum(k * k, axis=-1, keepdims=True) + EPS)
                gq_ref[rows, h * DK:(h + 1) * DK] = q.astype(bf16)
                gk_ref[rows, h * DK:(h + 1) * DK] = k.astype(bf16)

    def gdn_v(p):
        for rows, a in conv_silu(p, slice(2 * QK_W, CONV_DIM)):
            gv_ref[rows, :] = a.astype(bf16)

    groups = ((_RV, _RG, ret_v), (_RG, _GQKV, ret_gate), (_GZ, _MAIN_W, gdn_z), (_RQ, _RV, ret_qk),
              (_GQKV + 2 * QK_W, _GZ, gdn_v), (_GQKV, _GQKV + 2 * QK_W, gdn_qk))
    pending = None
    for lo, hi, epilogue in groups:
        p = _dot(xn, wm_ref[:, lo:hi])
        if pending is not None:
            pending()
        pending = functools.partial(epilogue, p)
    pending()

    ab = _dot(xn, wab_ref[...])
    gcol_ref[...] = _decay_beta(ab, alog_ref[...], dtb_ref[...], _iota(ab.shape, 1) < N_HEADS)
    abt = lax.dot_general(wabt_ref[...], xn, NT, preferred_element_type=f32)
    grow_ref[...] = _decay_beta(abt, alogt_ref[...], dtbt_ref[...], _iota(abt.shape, 0) < N_HEADS)


def _proj(layer, h3d, gpre, wm, wab, wabt, cw, alog, dtb, alogt, dtbt, cos, sin, tm):
    b, l, _ = h3d.shape
    tok = lambda w: pl.BlockSpec((None, tm, w), lambda i, j: (i, j, 0))
    tab = pl.BlockSpec((tm, DK), lambda i, j: (j, 0))
    sds = lambda w, dt=bf16: jax.ShapeDtypeStruct((b, l, w), dt)
    res = lambda *shape: _resident(shape, layer)
    return pl.pallas_call(
        _proj_body, grid=(b, l // tm),
        in_specs=[tok(D_MODEL), res(1, D_MODEL), res(D_MODEL, _MAIN_W), res(D_MODEL, LANES),
                  res(SUBLANES, D_MODEL), res(CONV_W, CONV_DIM), res(1, LANES),
                  res(1, LANES), res(SUBLANES, 1), res(SUBLANES, 1), tab, tab],
        out_specs=[tok(QK_W), tok(QK_W), tok(V_W), tok(V_W), tok(QK_W), tok(QK_W), tok(V_W), tok(V_W),
                   tok(LANES), pl.BlockSpec((None, SUBLANES, tm), lambda i, j: (i, 0, j)),
                   pl.BlockSpec((None, SUBLANES, CONV_DIM), lambda i, j: (i, 0, 0))],
        out_shape=[sds(QK_W), sds(QK_W), sds(V_W), sds(V_W), sds(QK_W), sds(QK_W), sds(V_W), sds(V_W),
                   sds(LANES, f32), jax.ShapeDtypeStruct((b, SUBLANES, l), f32),
                   jax.ShapeDtypeStruct((b, SUBLANES, CONV_DIM), f32)],
        scratch_shapes=[pltpu.VMEM((SUBLANES, CONV_DIM), f32)],
        compiler_params=_params("parallel", "arbitrary"), name="proj",
    )(h3d, gpre, wm, wab, wabt, cw, alog, dtb, alogt, dtbt, cos, sin)


def _ret_steps(q_ref, k_ref, v_ref, sg_ref, o_ref, s_ref, chunk):
    c = chunk
    row, col = _iota((c, c), 0), _iota((c, c), 1)
    dist = (row - col).astype(f32)
    pos = _iota((c, 1), 0).astype(f32)
    per_head = {}

    def head_consts(h):
        if h not in per_head:
            lg = LOG_GAMMA[h]
            per_head[h] = (jnp.exp(jnp.where(row >= col, dist * lg, -jnp.inf)), jnp.exp((pos + 1.0) * lg),
                           jnp.exp((c - 1.0 - pos) * lg))
        return per_head[h]

    def step(h, t, b):
        decay, q_scale, k_scale = head_consts(h)
        rows = slice(t * c, (t + 1) * c)
        qh = q_ref[b, rows, h * DK:(h + 1) * DK]
        kh = k_ref[b, rows, h * DK:(h + 1) * DK]
        vh = v_ref[b, rows, h * DV:(h + 1) * DV]
        attn = lax.dot_general(qh, kh, NT, preferred_element_type=f32) * decay
        q_dec = (qh.astype(f32) * q_scale).astype(bf16)
        k_dec = (kh.astype(f32) * k_scale).astype(bf16)
        s = s_ref[b, h]
        o = _dot(q_dec, s.astype(bf16)) + _dot(attn.astype(bf16), vh)
        s_ref[b, h] = s * math.exp(c * LOG_GAMMA[h]) + lax.dot_general(k_dec, vh, TN, preferred_element_type=f32)
        o_ref[b, rows, h * DV:(h + 1) * DV] = (
            sg_ref[b, rows, h * DV:(h + 1) * DV].astype(f32) * _rms(o)).astype(bf16)

    return [functools.partial(step, h, t, b) for h in range(N_HEADS) for t in range(q_ref.shape[1] // c)
            for b in range(q_ref.shape[0])]


def _gdn_steps(q_ref, k_ref, v_ref, sz_ref, gcol_ref, grow_ref, nw_ref, o_ref, s_ref):
    c = CHUNK
    hc = N_HEADS * c
    row, lane = _iota((c, hc), 0), _iota((c, hc), 1)
    lane_head, col = lane // c, lane % c
    lower, strict = row >= col, row > col
    head_mask = [(lane_head == h).astype(bf16) for h in range(N_HEADS)]
    block_diag = lambda x: jnp.concatenate([x * m for m in head_mask], axis=0)
    spread = lambda cols: functools.reduce(
        lambda acc, h: jnp.where(lane_head == h, cols[h], acc), range(N_HEADS - 1), cols[N_HEADS - 1])
    cum_rows = (_iota((c, c), 0) >= _iota((c, c), 1)).astype(f32)
    cum_cols = (_iota((c, hc), 0) <= (_iota((c, hc), 1) % c)).astype(f32)
    own_row = _iota((SUBLANES, hc), 0) == _iota((SUBLANES, hc), 1) // c
    zeros_k = jnp.zeros((c, DK), bf16)
    nw = nw_ref[...]

    items = [(b, t) for t in range(q_ref.shape[1] // c) for b in range(q_ref.shape[0])]
    chunks = range(len(items))
    tok = lambda t: slice(t * c, (t + 1) * c)

    def chunk_setup(b, t):
        rows = tok(t)
        k_heads = [k_ref[b, rows, h * DK:(h + 1) * DK] for h in range(N_HEADS)]
        k_bd = jnp.concatenate([jnp.concatenate([k_heads[h] if g == h else zeros_k for g in range(N_HEADS)], axis=1)
                                for h in range(N_HEADS)], axis=0)
        qk = lax.dot_general(jnp.concatenate([q_ref[b, rows, :], k_ref[b, rows, :]], axis=0), k_bd, NT,
                             preferred_element_type=f32)
        gcol = gcol_ref[b, rows, :]
        cum = jnp.dot(cum_rows, gcol, preferred_element_type=f32, precision=HIGHEST)
        b_cols = [cum[:, h:h + 1] for h in range(N_HEADS)]
        beta_cols = [gcol[:, N_HEADS + h:N_HEADS + h + 1] for h in range(N_HEADS)]
        b_row = jnp.sum(jnp.where(own_row, jnp.dot(grow_ref[b, t], cum_cols, preferred_element_type=f32,
                                                   precision=HIGHEST), 0.0), axis=0, keepdims=True)
        decay = jnp.exp(jnp.where(lower, spread(b_cols) - b_row, -jnp.inf))
        x = jnp.where(strict, qk[c:] * spread(beta_cols) * decay, 0.0) * -1.0
        return (qk[:c] * decay, x, jnp.concatenate(b_cols, axis=0), jnp.concatenate(beta_cols, axis=0))

    attns, xs, b_colv, betav = zip(*[chunk_setup(b, t) for b, t in items])
    ns = xs
    x_bds = [block_diag(x.astype(bf16)) for x in xs]
    for _ in range(int(math.log2(c)) - 1):
        xs = [_dot(x.astype(bf16), x_bd) for x, x_bd in zip(xs, x_bds)]
        x_bds = [block_diag(x.astype(bf16)) for x in xs]
        ns = [n + x + _dot(n.astype(bf16), x_bd) for n, x, x_bd in zip(ns, xs, x_bds)]

    def chunk_wy(i):
        b, rows = items[i][0], tok(items[i][1])
        b_col, beta = b_colv[i], betav[i]
        kst = jnp.concatenate([k_ref[b, rows, h * DK:(h + 1) * DK] for h in range(N_HEADS)], axis=0)
        qst = jnp.concatenate([q_ref[b, rows, h * DK:(h + 1) * DK] for h in range(N_HEADS)], axis=0)
        vst = jnp.concatenate([v_ref[b, rows, h * DV:(h + 1) * DV] for h in range(N_HEADS)], axis=0).astype(f32)
        n_bd = block_diag(ns[i].astype(bf16))
        kf = kst.astype(f32)
        e_b = jnp.exp(b_col)
        ru, rw = vst * beta, kf * (beta * e_b)
        u = ru + _dot(n_bd, ru.astype(bf16))
        w = (rw + _dot(n_bd, rw.astype(bf16))).astype(bf16)
        q_dec = (qst.astype(f32) * e_b).astype(bf16)
        return u, w, q_dec, kf, block_diag(attns[i].astype(bf16))

    wy = [chunk_wy(i) for i in chunks]
    for i in chunks:
        b, rows = items[i][0], tok(items[i][1])
        u, w, q_dec, kf, attn_bd = wy[i]
        b_col = b_colv[i]
        v_new, o_inter = [], []
        for h in range(N_HEADS):
            hs = slice(h * c, (h + 1) * c)
            r = _dot(jnp.concatenate([w[hs], q_dec[hs]], axis=0), s_ref[b, h].astype(bf16))
            v_new.append(u[hs] - r[:c])
            o_inter.append(r[c:])
        vb = jnp.concatenate(v_new, axis=0).astype(bf16)
        o = jnp.concatenate(o_inter, axis=0) + _dot(attn_bd, vb)
        for h in range(N_HEADS):
            hs = slice(h * c, (h + 1) * c)
            b_last = b_col[(h + 1) * c - 1:(h + 1) * c]
            k_dec = (kf[hs] * jnp.exp(b_last - b_col[hs])).astype(bf16)
            s_ref[b, h] = (s_ref[b, h] * jnp.exp(b_last)
                           + lax.dot_general(k_dec, vb[hs], TN, preferred_element_type=f32))
            o_ref[b, rows, h * DV:(h + 1) * DV] = (
                _rms(o[hs], nw) * sz_ref[b, rows, h * DV:(h + 1) * DV].astype(f32)).astype(bf16)


def _scans_body(rq_ref, rk_ref, rv_ref, rg_ref, gq_ref, gk_ref, gv_ref, gz_ref, gcol_ref, grow_ref, nw_ref,
                oret_ref, sret_ref, ogdn_ref, sgdn_ref, ret_state, gdn_state, *, ret_chunk):
    j = pl.program_id(1)

    @pl.when(j == 0)
    def _():
        ret_state[...] = jnp.zeros_like(ret_state)
        gdn_state[...] = jnp.zeros_like(gdn_state)

    for step in _ret_steps(rq_ref, rk_ref, rv_ref, rg_ref, oret_ref, ret_state, ret_chunk):
        step()
    _gdn_steps(gq_ref, gk_ref, gv_ref, gz_ref, gcol_ref, grow_ref, nw_ref, ogdn_ref, gdn_state)

    @pl.when(j == pl.num_programs(1) - 1)
    def _():
        sret_ref[...] = ret_state[...]
        sgdn_ref[...] = gdn_state[...]


def _scans(layer, rq, rk, rv, rg, gq, gk, gv, gz, gcol, grow4, nw, tile, nb, ret_chunk):
    b, l, _ = gq.shape
    assert b % nb == 0 and tile % ret_chunk == 0
    tok = lambda w: pl.BlockSpec((nb, tile, w), lambda i, j: (i, j, 0))
    st = pl.BlockSpec((nb, N_HEADS, DK, DV), lambda i, j: (i, 0, 0, 0))
    o_sds = jax.ShapeDtypeStruct((b, l, V_W), bf16)
    s_sds = jax.ShapeDtypeStruct((b, N_HEADS, DK, DV), f32)
    state = pltpu.VMEM((nb, N_HEADS, DK, DV), f32)
    return pl.pallas_call(
        functools.partial(_scans_body, ret_chunk=ret_chunk), grid=(b // nb, l // tile),
        in_specs=[tok(QK_W), tok(QK_W), tok(V_W), tok(V_W), tok(QK_W), tok(QK_W), tok(V_W), tok(V_W), tok(LANES),
                  pl.BlockSpec((nb, tile // CHUNK, SUBLANES, CHUNK), lambda i, j: (i, j, 0, 0)),
                  _resident((1, DV), layer)],
        out_specs=[tok(V_W), st, tok(V_W), st], out_shape=[o_sds, s_sds, o_sds, s_sds],
        scratch_shapes=[state, state],
        compiler_params=_params("parallel", "arbitrary"), name="scans",
    )(rq, rk, rv, rg, gq, gk, gv, gz, gcol, grow4, nw)


def _softmax(s):
    e = jnp.exp(s - jnp.max(s, axis=-1, keepdims=True))
    return e / jnp.sum(e, axis=-1, keepdims=True)


def _merge_tail(x, xn, o_ret, o_gdn, mem_proj, wg_ref, wr_ref, wd_ref, wo_ref, gpost):
    gates = jax.nn.sigmoid(_dot(xn, wg_ref[...]))
    merged = (gates[:, :D_MODEL] * _dot(o_ret, wr_ref[...])
              + gates[:, D_MODEL:2 * D_MODEL] * _dot(o_gdn, wd_ref[...])
              + gates[:, 2 * D_MODEL:] * mem_proj)
    y = _dot(merged.astype(bf16), wo_ref[...])
    return x + _rms(y, gpost)


def _merge_body(h_ref, gpre_ref, wq_ref, wg_ref, mk_ref, mv_ref, oret_ref, ogdn_ref,
                wr_ref, wd_ref, wm_ref, wo_ref, gpost_ref, o_ref):
    x = h_ref[...]
    xn = _rms(x, gpre_ref[...]).astype(bf16)
    gate = lambda i: jax.nn.sigmoid(_dot(xn, wg_ref[:, i * D_MODEL:(i + 1) * D_MODEL]))
    mq = _dot(xn, wq_ref[...]).astype(bf16)
    merged = gate(0) * _dot(oret_ref[...], wr_ref[...]) + gate(1) * _dot(ogdn_ref[...], wd_ref[...])
    mem_proj = None
    for h in range(N_HEADS):
        sl = slice(h * HD_X, (h + 1) * HD_X)
        s = lax.dot_general(mq[:, sl], mk_ref[:, sl], NT, preferred_element_type=f32) * HD_X ** -0.5
        o = _dot(_softmax(s).astype(bf16), mv_ref[:, sl])
        part = _dot(o.astype(bf16), wm_ref[sl, :])
        mem_proj = part if mem_proj is None else mem_proj + part
    merged = merged + gate(2) * mem_proj
    y = _dot(merged.astype(bf16), wo_ref[...])
    o_ref[...] = x + _rms(y, gpost_ref[...])


def _merge(layer, h3d, gpre, wq, wg, mk, mv, o_ret, o_gdn, wr, wd, wm, wo, gpost, tm):
    b, l, _ = h3d.shape
    tok = pl.BlockSpec((None, tm, D_MODEL), lambda i, j: (i, j, 0))
    mem = pl.BlockSpec((None, None, N_MEM, D_MODEL), lambda i, j: (layer, i, 0, 0))
    sq = _resident((D_MODEL, D_MODEL), layer)
    return pl.pallas_call(
        _merge_body, grid=(b, l // tm),
        in_specs=[tok, _resident((1, D_MODEL), layer), _packed_mq(layer), _packed_gates(layer), mem, mem,
                  tok, tok, sq, sq, sq, sq, _resident((1, D_MODEL), layer)],
        out_specs=tok, out_shape=jax.ShapeDtypeStruct((b, l, D_MODEL), f32),
        compiler_params=_params("parallel", "parallel"), name="merge",
    )(h3d, gpre, wq, wg, mk, mv, o_ret, o_gdn, wr, wd, wm, wo, gpost)


def _sproj_body(h_ref, gpre_ref, wm_ref, wab_ref, wq_ref, cw_ref, alog_ref, dtb_ref, cos_ref, sin_ref, buf_ref,
                rq_ref, rk_ref, rv_ref, rg_ref, gq_ref, gk_ref, gv_ref, gz_ref, gcol_ref, mq_ref, nbuf_ref):
    xn = _rms(h_ref[...], gpre_ref[...]).astype(bf16)
    p = _dot(xn, wm_ref[...])
    cos, sin = cos_ref[...], sin_ref[...]
    for h in range(N_HEADS):
        sl = slice(h * DK, (h + 1) * DK)
        rq_ref[:, sl] = _rotary(p[:, _RQ + h * DK:_RQ + (h + 1) * DK], cos, sin)
        rk_ref[:, sl] = _rotary(p[:, _RK + h * DK:_RK + (h + 1) * DK], cos, sin) * DK ** -0.5
    rv_ref[...] = p[:, _RV:_RG]
    rg_ref[...] = _silu(p[:, _RG:_GQKV])
    gz_ref[...] = _silu(p[:, _GZ:_MAIN_W])
    x = p[:, _GQKV:_GZ]
    cw = cw_ref[...]
    conv = x * cw[CONV_W - 1:CONV_W]
    for s in range(CONV_W - 1):
        prev = buf_ref[:, s * CONV_DIM:(s + 1) * CONV_DIM]
        conv = conv + prev * cw[s:s + 1]
        if s > 0:
            nbuf_ref[:, (s - 1) * CONV_DIM:s * CONV_DIM] = prev
    nbuf_ref[:, (CONV_W - 2) * CONV_DIM:] = x
    qs, ks, v = _gdn_qkv(conv)
    for h in range(N_HEADS):
        gq_ref[:, h * DK:(h + 1) * DK] = qs[h]
        gk_ref[:, h * DK:(h + 1) * DK] = ks[h]
    gv_ref[...] = v
    ab = _dot(xn, wab_ref[...])
    gcol_ref[...] = _decay_beta(ab, alog_ref[...], dtb_ref[...], _iota(ab.shape, 1) < N_HEADS)
    mq_ref[...] = _dot(xn, wq_ref[...])


class _StreamedWeight:
    def __init__(self, hbm_ref, layer, vmem_ref, sem):
        self._copy = pltpu.make_async_copy(hbm_ref.at[layer], vmem_ref, sem)
        self._vmem = vmem_ref
        self._arrived = False
        self._copy.start()

    def __getitem__(self, idx):
        if not self._arrived:
            self._copy.wait()
            self._arrived = True
        return self._vmem[idx]


_HBM = pl.BlockSpec(memory_space=pl.ANY)


def _sample_pre_body(x_ref, g1_ref, wgu_hbm, wof_hbm, g2_ref, gpre_ref, wm_hbm, wab_ref, wq_hbm, *refs, layer):
    proj_in, h1_ref, proj_out, (wgu_v, wof_v, wm_v, wq_v, sems) = refs[:6], refs[6], refs[7:18], refs[18:]
    wgu, wof, wm, wq = (_StreamedWeight(src, layer, dst, sems.at[i]) for i, (src, dst) in enumerate(
        ((wgu_hbm, wgu_v), (wof_hbm, wof_v), (wm_hbm, wm_v), (wq_hbm, wq_v))))
    _ffn_body(x_ref, g1_ref, wgu, wof, g2_ref, h1_ref)
    _sproj_body(h1_ref, gpre_ref, wm, wab_ref, wq, *proj_in, *proj_out)


def _sample_pre(layer, x2d, g1, wgu, wof, g2, gpre, wm, wab, wq, cw, alog, dtb, cos, sin, buf2d):
    n = x2d.shape[0]
    sds = lambda w: jax.ShapeDtypeStruct((n, w), f32)
    res = lambda *shape: _resident(shape, layer)
    widths = (D_MODEL, QK_W, QK_W, V_W, V_W, QK_W, QK_W, V_W, V_W, LANES, D_MODEL, (CONV_W - 1) * CONV_DIM)
    return pl.pallas_call(
        functools.partial(_sample_pre_body, layer=layer), grid=(1,),
        in_specs=[_resident((n, D_MODEL)), res(1, D_MODEL), _HBM, _HBM, res(1, D_MODEL),
                  res(1, D_MODEL), _HBM, res(D_MODEL, LANES),
                  _HBM, res(CONV_W, CONV_DIM), res(1, LANES), res(1, LANES),
                  _resident((1, DK)), _resident((1, DK)), _resident((n, (CONV_W - 1) * CONV_DIM))],
        out_specs=[pl.BlockSpec((n, w), lambda i: (0, 0)) for w in widths],
        out_shape=[sds(w) for w in widths],
        scratch_shapes=[pltpu.VMEM((D_MODEL, 2 * D_FF), bf16), pltpu.VMEM((D_FF, D_MODEL), bf16),
                        pltpu.VMEM((D_MODEL, _MAIN_W), bf16), pltpu.VMEM((D_MODEL, D_MODEL), bf16),
                        pltpu.SemaphoreType.DMA((4,))],
        compiler_params=_params("arbitrary"), name="sample_pre",
    )(x2d, g1, wgu, wof, g2, gpre, wm, wab, wq, cw, alog, dtb, cos, sin, buf2d)


def _sstate_body(cols_ref, rv_ref, gv_ref, gcol_ref, sret_ref, sgdn_ref, nret_ref, ngdn_ref, oret_ref, ogdn_ref):
    vecs = cols_ref[...].T
    col = lambda j, v, h: vecs[:, (j * 4 + v) * N_HEADS + h:(j * 4 + v) * N_HEADS + h + 1]
    for j in range(rv_ref.shape[0]):
        for h in range(N_HEADS):
            vs = slice(h * DV, (h + 1) * DV)
            q, k = col(j, 0, h), col(j, 1, h)
            s_new = sret_ref[j, h] * math.exp(LOG_GAMMA[h]) + k * rv_ref[j:j + 1, vs]
            nret_ref[j, h] = s_new
            oret_ref[j:j + 1, vs] = jnp.sum(q * s_new, axis=0, keepdims=True)
            q, k = col(j, 2, h), col(j, 3, h)
            eg = jnp.exp(gcol_ref[j:j + 1, h:h + 1])
            beta = gcol_ref[j:j + 1, N_HEADS + h:N_HEADS + h + 1]
            s = sgdn_ref[j, h]
            ks = jnp.sum(k * s, axis=0, keepdims=True)
            v_new = beta * (gv_ref[j:j + 1, vs] - eg * ks)
            s_new = s * eg + k * v_new
            ngdn_ref[j, h] = s_new
            ogdn_ref[j:j + 1, vs] = jnp.sum(q * s_new, axis=0, keepdims=True)


def _sattn_body(mq_ref, k_ref, v_ref, o_ref):
    for j in range(mq_ref.shape[0]):
        s = jnp.sum(k_ref[j] * mq_ref[j][None], axis=-1, keepdims=True) * HD_X ** -0.5
        e = jnp.exp(s - jnp.max(s, axis=0, keepdims=True))
        p = e / jnp.sum(e, axis=0, keepdims=True)
        o_ref[j] = jnp.sum(p * v_ref[j], axis=0)


def _smerge_body(h_ref, gpre_ref, wg_ref, oret_ref, rg_ref, ogdn_ref, gz_ref, nw_ref, omem_ref,
                 wr_ref, wd_ref, wm_ref, wo_ref, gpost_ref, o_ref):
    x = h_ref[...]
    xn = _rms(x, gpre_ref[...]).astype(bf16)
    nw = nw_ref[...]
    rets, gdns = [], []
    for h in range(N_HEADS):
        vs = slice(h * DV, (h + 1) * DV)
        rets.append(rg_ref[:, vs] * _rms(oret_ref[:, vs]))
        gdns.append(_rms(ogdn_ref[:, vs], nw) * gz_ref[:, vs])
    o_ret = jnp.concatenate(rets, axis=-1).astype(bf16)
    o_gdn = jnp.concatenate(gdns, axis=-1).astype(bf16)
    mem_proj = _dot(omem_ref[...].astype(bf16), wm_ref[...])
    o_ref[...] = _merge_tail(x, xn, o_ret, o_gdn, mem_proj, wg_ref, wr_ref, wd_ref, wo_ref, gpost_ref[...])


def _sample_post_body(h_ref, gpre_ref, wg_hbm, oret_ref, rg_ref, ogdn_ref, gz_ref, nw_ref, omem_ref,
                      wr_hbm, wd_hbm, wm_hbm, wo_hbm, gpost_ref, g1_ref, wgu_hbm, wof_hbm, g2_ref,
                      o_ref, h2_ref, *scratch, layer):
    sems = scratch[-1]
    order = (wm_hbm, wg_hbm, wr_hbm, wd_hbm, wo_hbm, wgu_hbm, wof_hbm)
    wm, wg, wr, wd, wo, wgu, wof = (_StreamedWeight(src, layer, dst, sems.at[i])
                                    for i, (src, dst) in enumerate(zip(order, scratch[:-1])))
    _smerge_body(h_ref, gpre_ref, wg, oret_ref, rg_ref, ogdn_ref, gz_ref, nw_ref, omem_ref, wr, wd, wm, wo,
                 gpost_ref, h2_ref)
    _ffn_body(h2_ref, g1_ref, wgu, wof, g2_ref, o_ref)


def _sample_post(layer, h2d, gpre, wg, o_ret, rg, o_gdn, gz, nw, o_mem, wr, wd, wm, wo, gpost, g1, wgu, wof, g2):
    n = h2d.shape[0]
    act = _resident((n, D_MODEL))
    vec = _resident((1, D_MODEL), layer)
    sq = pltpu.VMEM((D_MODEL, D_MODEL), bf16)
    return pl.pallas_call(
        functools.partial(_sample_post_body, layer=layer), grid=(1,),
        in_specs=[act, vec, _HBM, act, act, act, act, _resident((1, DV), layer), act, _HBM, _HBM, _HBM, _HBM, vec,
                  vec, _HBM, _HBM, vec],
        out_specs=pl.BlockSpec((n, D_MODEL), lambda i: (0, 0)), out_shape=jax.ShapeDtypeStruct(h2d.shape, f32),
        scratch_shapes=[pltpu.VMEM((n, D_MODEL), f32),
                        sq, pltpu.VMEM((D_MODEL, 3 * D_MODEL), bf16), sq, sq, sq,
                        pltpu.VMEM((D_MODEL, 2 * D_FF), bf16), pltpu.VMEM((D_FF, D_MODEL), bf16),
                        pltpu.SemaphoreType.DMA((7,))],
        compiler_params=_params("arbitrary"), name="sample_post",
    )(h2d, gpre, wg, o_ret, rg, o_gdn, gz, nw, o_mem, wr, wd, wm, wo, gpost, g1, wgu, wof, g2)


def _rope_tables(pos):
    half = DK // 2
    inv_freq = ROPE_BASE ** (-jnp.arange(half, dtype=f32) / half)
    ang = pos.astype(f32)[:, None] * inv_freq[None, :]
    cos, sin = jnp.cos(ang), jnp.sin(ang)
    return jnp.concatenate([cos, cos], axis=-1), jnp.concatenate([-sin, sin], axis=-1)


def _row_tile(n, want):
    t = min(n, want)
    assert n % t == 0, (n, t)
    return t


def kernel(x_prompt, x_sample, mem_prompt, state_ret, state_gdn, state_conv, cache_mem_k, cache_mem_v, norm_ffn1_pre, norm_ffn1_post, ffn1_w_in, ffn1_w_out, norm_mix_pre, norm_mix_post, w_in, gdn_conv_w, gdn_a_log, gdn_dt_bias, gdn_norm, norm_mem, w_mem_k, w_mem_v, w_branch_ret, w_branch_gdn, w_branch_mem, w_out, norm_ffn2_pre, norm_ffn2_post, ffn2_w_in, ffn2_w_out):
    bp, sp, _ = x_prompt.shape
    ns, ss, _ = x_sample.shape
    depth = w_in.shape[0]
    assert ss == 1 and sp % CHUNK == 0
    tm_ffn = _row_tile(bp * sp, 512)
    tm_seq = _row_tile(sp, 512)
    tm_proj = _row_tile(sp, 256)

    cos_p, sin_p = _rope_tables(jnp.arange(sp))
    cos_s, sin_s = _rope_tables(PAST_LEN + jnp.arange(ss))
    row = lambda v: v.reshape(depth, 1, -1).astype(f32)
    pad_lanes = lambda v: jnp.pad(v.astype(f32), ((0, 0), (0, LANES - v.shape[1]))).reshape(depth, 1, LANES)
    pad_rows = lambda v: jnp.pad(v.astype(f32), ((0, 0), (0, SUBLANES - v.shape[1]))).reshape(depth, SUBLANES, 1)
    w_in_t = jnp.swapaxes(w_in, 1, 2)
    w_main = _w_in_cols(w_in_t, 0, _MAIN_W)
    w_mq = _w_in_cols(w_in_t, _AB_END, D_MODEL)
    w_gates = _w_in_cols(w_in_t, _MQ_END, 3 * D_MODEL)
    w_ab_row = w_in_t[:, _MAIN_W:_AB_END].astype(bf16)
    w_ab_col = jnp.pad(jnp.swapaxes(w_ab_row, 1, 2), ((0, 0), (0, 0), (0, LANES - 2 * N_HEADS)))
    f1_in, f1_out, f2_in, f2_out = (w.astype(bf16) for w in (ffn1_w_in, ffn1_w_out, ffn2_w_in, ffn2_w_out))
    wr, wd, wm, wo = (w.astype(bf16) for w in (w_branch_ret, w_branch_gdn, w_branch_mem, w_out))
    cw = gdn_conv_w.astype(f32)
    alog, dtb = pad_lanes(gdn_a_log), pad_lanes(gdn_dt_bias)
    alog_t, dtb_t = pad_rows(gdn_a_log), pad_rows(gdn_dt_bias)
    nw = row(gdn_norm)
    g_f1pre, g_f1post, g_f2pre, g_f2post = (row(g) for g in (norm_ffn1_pre, norm_ffn1_post, norm_ffn2_pre,
                                                              norm_ffn2_post))
    g_mpre, g_mpost = row(norm_mix_pre), row(norm_mix_post)

    mk, mv, mk_b, mv_b = _memkv(mem_prompt, row(norm_mem), w_mem_k.astype(bf16), w_mem_v.astype(bf16))
    hp = x_prompt
    hs = x_sample.reshape(ns, D_MODEL)
    outs = [[] for _ in range(4)]
    new_states = None
    scan_seqs = 2 if bp % 2 == 0 else 1
    scan_tile = _row_tile(sp, 256)
    ret_chunk = scan_tile
    hosted_per_step = ns // (2 * (bp * sp // tm_ffn))
    assert hosted_per_step * 2 * (bp * sp // tm_ffn) == ns
    for l in range(depth):
        h1s, s_rq, s_rk, s_rv, s_rg, s_gq, s_gk, s_gv, s_gz, s_gcol, mq, nbuf = _sample_pre(
            l, hs, g_f1pre, f1_in, f1_out, g_f1post, g_mpre, w_main, w_ab_col, w_mq, cw, alog, dtb, cos_s, sin_s,
            state_conv[l].reshape(ns, (CONV_W - 1) * CONV_DIM))
        outs[3].append(nbuf.reshape(ns, CONV_W - 1, CONV_DIM))
        groups = ns // hosted_per_step
        by_step = lambda a: a.reshape((groups, hosted_per_step) + a.shape[1:])
        cols = jnp.stack([s_rq, s_rk, s_gq, s_gk], axis=1).reshape(groups, hosted_per_step * 4 * N_HEADS, DK)
        host = lambda part: (part, 2, hosted_per_step, by_step(mq.reshape(ns, N_HEADS, HD_X)), cache_mem_k,
                             cache_mem_v, cols, by_step(s_rv), by_step(s_gv), by_step(s_gcol), state_ret,
                             state_gdn, new_states)

        h1, o_mem_a, n_ret, n_gdn, so_ret_a, so_gdn_a = _ffn(
            l, hp.reshape(bp * sp, D_MODEL), g_f1pre, f1_in, f1_out, g_f1post, tm_ffn, hosted=host(0))
        new_states = (n_ret, n_gdn)
        h1 = h1.reshape(bp, sp, D_MODEL)
        rq, rk, rv, rg, gq, gk, gv, gz, gcol, grow, tail = _proj(
            l, h1, g_mpre, w_main, w_ab_col, w_ab_row, cw, alog, dtb, alog_t, dtb_t, cos_p, sin_p, tm_proj)
        grow4 = grow.reshape(bp, SUBLANES, sp // CHUNK, CHUNK).transpose(0, 2, 1, 3)
        o_ret, s_ret, o_gdn, s_gdn = _scans(l, rq, rk, rv, rg, gq, gk, gv, gz, gcol, grow4, nw, scan_tile,
                                            scan_seqs, ret_chunk)
        h2 = _merge(l, h1, g_mpre, w_mq, w_gates, mk_b, mv_b, o_ret, o_gdn, wr, wd, wm, wo, g_mpost, tm_seq)
        hp, o_mem_b, n_ret, n_gdn, so_ret_b, so_gdn_b = _ffn(
            l, h2.reshape(bp * sp, D_MODEL), g_f2pre, f2_in, f2_out, g_f2post, tm_ffn, hosted=host(1))
        new_states = (n_ret, n_gdn)
        hp = hp.reshape(bp, sp, D_MODEL)
        outs[0].append(s_ret)
        outs[1].append(s_gdn)
        outs[2].append(tail[:, SUBLANES - (CONV_W - 1):])

        join = lambda a, b: jnp.concatenate([a, b], axis=0).reshape(ns, -1)
        o_mem, so_ret, so_gdn = join(o_mem_a, o_mem_b), join(so_ret_a, so_ret_b), join(so_gdn_a, so_gdn_b)
        hs = _sample_post(l, h1s, g_mpre, w_gates, so_ret, s_rg, so_gdn, s_gz, nw, o_mem, wr, wd, wm, wo, g_mpost,
                          g_f2pre, f2_in, f2_out, g_f2post)

    ret_p, gdn_p, conv_p, conv_s = (jnp.stack(o) for o in outs)
    return (hp, hs.reshape(ns, ss, D_MODEL), ret_p, gdn_p, conv_p, mk, mv) + new_states + (conv_s,)
```
